```python
import math
import jax
import jax.numpy as jnp
from jax import lax
import numpy as np

D_MODEL = 1024
BATCH = 4
SEQ = 4096
DEPTH = 2
DEC_BATCH = 32
DEC_SEQ = 1
PAST_LEN = 8192
PAGE_SIZE = 128

N_BRANCH = 4
BRANCH_WIDTH = D_MODEL // N_BRANCH
S5_WIDTH = BRANCH_WIDTH
S5_GROUP = 16
S5_GROUPS = S5_WIDTH // S5_GROUP
S5_P = 64
GLA_HEADS = 4
GLA_DK = BRANCH_WIDTH // GLA_HEADS
GLA_DV = BRANCH_WIDTH // GLA_HEADS
GLA_RANK = 16
GLA_TAU = 16.0
GLA_CHUNK = 64
GDN_HEADS = 4
GDN_DK = BRANCH_WIDTH // GDN_HEADS
GDN_DV = BRANCH_WIDTH // GDN_HEADS
GDN_CONV = 4
GDN_CHUNK = 64
GDN_QKV = GDN_HEADS * (2 * GDN_DK + GDN_DV)
MOBA_HEADS = 4
MOBA_HD = BRANCH_WIDTH // MOBA_HEADS
MOBA_BLOCK = 256
MOBA_TOPK = 3
MOBA_QCHUNK = 32
ROT_DIM = MOBA_HD // 4
ROPE_THETA = 500000.0
D_FF = 4 * D_MODEL
EPS = 1e-6
NEG = -1e30

IN_SPLITS = (
    S5_WIDTH,
    GLA_HEADS * GLA_DK, GLA_HEADS * GLA_DK, GLA_HEADS * GLA_DV,
    GLA_RANK, GLA_HEADS * GLA_DV,
    GDN_HEADS * GDN_DK, GDN_HEADS * GDN_DK, GDN_HEADS * GDN_DV,
    GDN_HEADS, GDN_HEADS, GDN_HEADS * GDN_DV,
    MOBA_HEADS * MOBA_HD, MOBA_HEADS * MOBA_HD, MOBA_HEADS * MOBA_HD,
)
D_IN = sum(IN_SPLITS)

kernel_name = 'hybrid_s5_gla_gdn_moba_step'


def _rms(x, g):
    xf = x.astype(jnp.float32)
    y = xf * lax.rsqrt(jnp.mean(xf * xf, axis=-1, keepdims=True) + EPS)
    return (y * g.astype(jnp.float32)).astype(x.dtype)


def _l2n(x):
    return x * lax.rsqrt(jnp.sum(x * x, axis=-1, keepdims=True) + EPS)


def _rope(x, pos):
    f32 = jnp.float32
    half = ROT_DIM // 2
    inv = ROPE_THETA ** (-jnp.arange(half, dtype=f32) / half)
    ang = pos.astype(f32)[:, None] * inv[None, :]
    cos = jnp.cos(ang)[None, :, None, :]
    sin = jnp.sin(ang)[None, :, None, :]
    xr = x[..., :ROT_DIM].astype(f32)
    x1, x2 = xr[..., :half], xr[..., half:]
    rot = jnp.concatenate([x1 * cos - x2 * sin, x2 * cos + x1 * sin], axis=-1).astype(x.dtype)
    return jnp.concatenate([rot, x[..., ROT_DIM:]], axis=-1)


def _to_chunks(x, c):
    b, L = x.shape[:2]
    n = -(-L // c)
    x = jnp.pad(x, [(0, 0), (0, n * c - L)] + [(0, 0)] * (x.ndim - 2))
    x = x.reshape((b, n, c) + x.shape[2:])
    return x.transpose((1, 0, 3, 2) + tuple(range(4, x.ndim)))


def _from_chunks(o, L):
    n, b, h, c, d = o.shape
    return o.transpose(1, 0, 3, 2, 4).reshape(b, n * c, h, d)[:, :L]


def _s5(u, h0_re, h0_im, p):
    f32 = jnp.float32
    b, L, _ = u.shape
    uf = u.astype(f32)
    ug = uf.reshape(b, L, S5_GROUPS, S5_GROUP)
    dt = jnp.exp(p['s5_log_dt'].astype(f32))[:, None]
    ar = p['s5_a_re'].astype(f32)
    ai = p['s5_a_im'].astype(f32)
    mag = jnp.exp(ar * dt)
    abar_re = mag * jnp.cos(ai * dt)
    abar_im = mag * jnp.sin(ai * dt)
    den = ar * ar + ai * ai
    nr = abar_re - 1.0
    f_re = (nr * ar + abar_im * ai) / den
    f_im = (abar_im * ar - nr * ai) / den
    br = p['s5_b_re'].astype(f32)
    bi = p['s5_b_im'].astype(f32)
    bbar_re = f_re[..., None] * br - f_im[..., None] * bi
    bbar_im = f_re[..., None] * bi + f_im[..., None] * br
    x_re = jnp.einsum('blgc,gpc->blgp', ug, bbar_re)
    x_im = jnp.einsum('blgc,gpc->blgp', ug, bbar_im)
    h0r = h0_re.astype(f32)
    h0i = h0_im.astype(f32)
    x_re = x_re.at[:, 0].add(abar_re * h0r - abar_im * h0i)
    x_im = x_im.at[:, 0].add(abar_re * h0i + abar_im * h0r)
    a_re = jnp.broadcast_to(abar_re, x_re.shape)
    a_im = jnp.broadcast_to(abar_im, x_im.shape)

    def combine(e1, e2):
        a1r, a1i, b1r, b1i = e1
        a2r, a2i, b2r, b2i = e2
        return (a2r * a1r - a2i * a1i, a2r * a1i + a2i * a1r,
                a2r * b1r - a2i * b1i + b2r, a2r * b1i + a2i * b1r + b2i)

    _, _, hr, hi = lax.associative_scan(combine, (a_re, a_im, x_re, x_im), axis=1)
    y = (jnp.einsum('blgp,gcp->blgc', hr, p['s5_c_re'].astype(f32))
         - jnp.einsum('blgp,gcp->blgc', hi, p['s5_c_im'].astype(f32)))
    y = y.reshape(b, L, S5_WIDTH) + p['s5_d'].astype(f32) * uf
    y = jax.nn.gelu(y)
    y = y * jax.nn.sigmoid(y @ p['s5_w_glu'].astype(f32) + p['s5_b_glu'].astype(f32))
    return y.astype(u.dtype), hr[:, -1], hi[:, -1]


def _gla_chunked(q, k, v, log_a, s0):
    L = q.shape[1]
    c = min(GLA_CHUNK, L)
    qc, kc, vc, gc = (_to_chunks(t, c) for t in (q, k, v, log_a))
    tri = jnp.tril(jnp.ones((c, c), dtype=bool))

    def step(S, inp):
        qi, ki, vi, gi = inp
        bcum = jnp.cumsum(gi, axis=2)
        diff = bcum[:, :, :, None, :] - bcum[:, :, None, :, :]
        decay = jnp.exp(jnp.where(tri[:, :, None], diff, -jnp.inf))
        att = jnp.einsum('bhid,bhjd,bhijd->bhij', qi, ki, decay)
        o = (jnp.einsum('bhij,bhjv->bhiv', att, vi)
             + jnp.einsum('bhid,bhdv->bhiv', qi * jnp.exp(bcum), S))
        blast = bcum[:, :, -1:]
        S = (jnp.exp(blast[:, :, 0])[..., None] * S
             + jnp.einsum('bhjd,bhjv->bhdv', ki * jnp.exp(blast - bcum), vi))
        return S, o

    S, o = lax.scan(step, s0, (qc, kc, vc, gc))
    return _from_chunks(o, L), S


def _gdn_chunked(q, k, v, beta, g, s0):
    L = q.shape[1]
    c = min(GDN_CHUNK, L)
    qc, kc, vc = (_to_chunks(t, c) for t in (q, k, v))
    bc = _to_chunks(beta, c)
    gc = _to_chunks(g, c)
    incl = jnp.tril(jnp.ones((c, c), dtype=bool))
    strict = jnp.tril(jnp.ones((c, c), dtype=bool), -1)
    eye = jnp.eye(c, dtype=jnp.float32)

    def step(S, inp):
        qi, ki, vi, bi, gi = inp
        gcum = jnp.cumsum(gi, axis=-1)
        dec = jnp.exp(jnp.where(incl, gcum[..., :, None] - gcum[..., None, :], -jnp.inf))
        gam = jnp.exp(gcum)[..., None]
        kk = jnp.einsum('bhid,bhjd->bhij', ki, ki)
        m = jnp.where(strict, dec * kk * bi[..., None, :], 0.0) + eye
        rhs = vi - gam * jnp.einsum('bhid,bhdv->bhiv', ki, S)
        u = lax.linalg.triangular_solve(m, rhs, left_side=True, lower=True, unit_diagonal=True)
        qk = jnp.einsum('bhid,bhjd->bhij', qi, ki)
        o = (gam * jnp.einsum('bhid,bhdv->bhiv', qi, S)
             + jnp.einsum('bhij,bhjv->bhiv', dec * qk * bi[..., None, :], u))
        glast = gcum[..., -1:]
        w = jnp.exp(glast - gcum) * bi
        S = jnp.exp(glast)[..., None] * S + jnp.einsum('bhjd,bhjv->bhdv', ki * w[..., None], u)
        return S, o

    S, o = lax.scan(step, s0, (qc, kc, vc, bc, gc))
    return _from_chunks(o, L), S


def _moba(q, k_all, v_all, pos0):
    f32 = jnp.float32
    b, Lq, H, hd = q.shape
    T = k_all.shape[1]
    nb = -(-T // MOBA_BLOCK)
    padk = [(0, 0), (0, nb * MOBA_BLOCK - T), (0, 0), (0, 0)]
    kb = jnp.pad(k_all, padk).reshape(b, nb, MOBA_BLOCK, H, hd).transpose(0, 3, 1, 2, 4)
    vb = jnp.pad(v_all, padk).reshape(b, nb, MOBA_BLOCK, H, hd).transpose(0, 3, 1, 2, 4)
    kmean = jnp.mean(kb.astype(f32), axis=3)
    topk = min(MOBA_TOPK, nb)
    qc = min(MOBA_QCHUNK, Lq)
    nq = -(-Lq // qc)
    qt = jnp.pad(q, [(0, 0), (0, nq * qc - Lq), (0, 0), (0, 0)])
    qt = qt.reshape(b, nq, qc, H, hd).transpose(1, 0, 3, 2, 4)
    qpos = (pos0 + jnp.arange(nq * qc, dtype=jnp.int32)).reshape(nq, qc)
    bidx = jnp.arange(b)[:, None, None, None]
    hidx = jnp.arange(H)[None, :, None, None]
    boffs = jnp.arange(MOBA_BLOCK, dtype=jnp.int32)
    blk_ids = jnp.arange(nb, dtype=jnp.int32)
    scale = hd ** -0.5

    def attend(args):
        qi, pi = args
        qf = qi.astype(f32)
        own = pi // MOBA_BLOCK
        gate = jnp.einsum('bhqd,bhnd->bhqn', qf, kmean)
        gate = jnp.where(blk_ids[None, :] < own[:, None], gate, NEG)
        _, sel = lax.top_k(gate, topk)
        sel_ok = sel < own[:, None]
        own_b = jnp.broadcast_to(own[:, None], (b, H, qc, 1))
        blocks = jnp.concatenate([sel, own_b], axis=-1)
        kg = kb[bidx, hidx, blocks].astype(f32)
        vg = vb[bidx, hidx, blocks].astype(f32)
        kpos = blocks[..., None] * MOBA_BLOCK + boffs
        ok = (jnp.concatenate([sel_ok, jnp.ones_like(sel_ok[..., :1])], axis=-1)[..., None]
              & (kpos <= pi[:, None, None]))
        logits = jnp.einsum('bhqd,bhqnkd->bhqnk', qf, kg) * scale
        logits = jnp.where(ok, logits, -jnp.inf).reshape(b, H, qc, -1)
        w = jax.nn.softmax(logits, axis=-1).reshape(ok.shape)
        return jnp.einsum('bhqnk,bhqnkd->bhqd', w, vg)

    o = lax.map(attend, (qt, qpos))
    return o.transpose(1, 0, 3, 2, 4).reshape(b, nq * qc, H, hd)[:, :Lq].astype(q.dtype)


def _layer(x, pos0, past_k, past_v, s5_re0, s5_im0, gla0, gdn0, conv0, p):
    f32 = jnp.float32
    b, L, _ = x.shape
    pos = pos0 + jnp.arange(L, dtype=jnp.int32)
    h = _rms(x, p['ln1_g'])
    proj = h @ p['w_in']
    (s5_u, a_q, a_k, a_v, a_lr, a_r, d_q, d_k, d_v, d_b, d_a, d_z,
     m_q, m_k, m_v) = jnp.split(proj, np.cumsum(IN_SPLITS)[:-1].tolist(), axis=-1)

    y_a, s5_re, s5_im = _s5(s5_u, s5_re0, s5_im0, p)

    q = a_q.astype(f32).reshape(b, L, GLA_HEADS, GLA_DK) * GLA_DK ** -0.5
    k = a_k.astype(f32).reshape(b, L, GLA_HEADS, GLA_DK)
    v = a_v.astype(f32).reshape(b, L, GLA_HEADS, GLA_DV)
    log_a = jax.nn.log_sigmoid((a_lr @ p['gla_w_gate'] + p['gla_b_gate']).astype(f32)) / GLA_TAU
    o, gla_s = _gla_chunked(q, k, v, log_a.reshape(b, L, GLA_HEADS, GLA_DK), gla0.astype(f32))
    gate_r = jax.nn.silu(a_r.astype(f32)).reshape(b, L, GLA_HEADS, GLA_DV)
    y_b = (_rms(o, p['gla_norm']) * gate_r).reshape(b, L, GLA_HEADS * GLA_DV).astype(x.dtype)

    qkv_in = jnp.concatenate([d_q, d_k, d_v], axis=-1)
    qkv_pad = jnp.concatenate([conv0.astype(qkv_in.dtype), qkv_in], axis=1)
    qkv = lax.conv_general_dilated(qkv_pad, p['gdn_conv_w'][:, None, :].astype(qkv_pad.dtype),
                                   (1,), 'VALID', dimension_numbers=('NWC', 'WIO', 'NWC'),
                                   feature_group_count=GDN_QKV)
    qkv = jax.nn.silu(qkv.astype(f32))
    gq, gk, gv = jnp.split(qkv, [GDN_HEADS * GDN_DK, 2 * GDN_HEADS * GDN_DK], axis=-1)
    gq = _l2n(gq.reshape(b, L, GDN_HEADS, GDN_DK)) * GDN_DK ** -0.5
    gk = _l2n(gk.reshape(b, L, GDN_HEADS, GDN_DK))
    gv = gv.reshape(b, L, GDN_HEADS, GDN_DV)
    beta = jax.nn.sigmoid(d_b.astype(f32))
    g = -jnp.exp(p['gdn_a_log'].astype(f32)) * jax.nn.softplus(d_a.astype(f32) + p['gdn_dt_bias'].astype(f32))
    o, gdn_s = _gdn_chunked(gq, gk, gv, beta, g, gdn0.astype(f32))
    gate_z = jax.nn.silu(d_z.astype(f32)).reshape(b, L, GDN_HEADS, GDN_DV)
    y_c = (_rms(o, p['gdn_norm']) * gate_z).reshape(b, L, GDN_HEADS * GDN_DV).astype(x.dtype)
    new_conv = qkv_pad[:, -(GDN_CONV - 1):]

    mq = _rope(_rms(m_q.reshape(b, L, MOBA_HEADS, MOBA_HD), p['moba_q_norm']), pos)
    mk = _rope(_rms(m_k.reshape(b, L, MOBA_HEADS, MOBA_HD), p['moba_k_norm']), pos)
    mv = m_v.reshape(b, L, MOBA_HEADS, MOBA_HD)
    if past_k is None:
        k_all, v_all = mk, mv
    else:
        k_all = jnp.concatenate([past_k.astype(mk.dtype), mk], axis=1)
        v_all = jnp.concatenate([past_v.astype(mv.dtype), mv], axis=1)
    y_d = _moba(mq, k_all, v_all, pos0).reshape(b, L, MOBA_HEADS * MOBA_HD)

    gates = jax.nn.sigmoid((h @ p['w_gate']).astype(f32)).reshape(b, L, N_BRANCH, D_MODEL)
    merged = (gates[:, :, 0] * (y_a @ p['w_br_s5']) + gates[:, :, 1] * (y_b @ p['w_br_gla'])
              + gates[:, :, 2] * (y_c @ p['w_br_gdn']) + gates[:, :, 3] * (y_d @ p['w_br_moba']))
    x = x + merged.astype(x.dtype) @ p['w_out']

    z = _rms(x, p['ln2_g']) @ p['w_ff1']
    x = x + jnp.square(jax.nn.relu(z)) @ p['w_ff2']
    dt = x.dtype
    return x, (mk, mv, s5_re.astype(dt), s5_im.astype(dt), gla_s.astype(dt),
               gdn_s.astype(dt), new_conv.astype(dt))


def _stack_states(states):
    return [jnp.stack([s[i] for s in states]) for i in range(len(states[0]))]


def setup_inputs(seed: int = 0) -> dict:
    key = jax.random.key(seed)
    ks = iter(jax.random.split(key, 64))
    f32 = jnp.float32

    def nrm(shape, scale):
        return jax.random.normal(next(ks), shape, f32) * scale

    def gain(shape):
        return 1.0 + nrm(shape, 0.01)

    n_pages = PAST_LEN // PAGE_SIZE
    n_used = DEC_BATCH * n_pages
    n_pool = n_used + n_used // 4
    page_table = jax.random.permutation(next(ks), n_pool)[:n_used].reshape(DEC_BATCH, n_pages).astype(jnp.int32)
    gdn_dt = jnp.exp(jax.random.uniform(next(ks), (DEPTH, GDN_HEADS), f32, math.log(1e-3), math.log(1e-1)))
    return {
        'x_prompt': nrm((BATCH, SEQ, D_MODEL), 1.0),
        'x_sample': nrm((DEC_BATCH, DEC_SEQ, D_MODEL), 1.0),
        'cache_moba_k': nrm((DEPTH, n_pool, PAGE_SIZE, MOBA_HEADS, MOBA_HD), 1.0),
        'cache_moba_v': nrm((DEPTH, n_pool, PAGE_SIZE, MOBA_HEADS, MOBA_HD), 1.0),
        'page_table': page_table,
        'state_s5_re': nrm((DEPTH, DEC_BATCH, S5_GROUPS, S5_P), 0.1),
        'state_s5_im': nrm((DEPTH, DEC_BATCH, S5_GROUPS, S5_P), 0.1),
        'state_gla': nrm((DEPTH, DEC_BATCH, GLA_HEADS, GLA_DK, GLA_DV), 0.3),
        'state_gdn': nrm((DEPTH, DEC_BATCH, GDN_HEADS, GDN_DK, GDN_DV), 0.3),
        'state_gdn_conv': nrm((DEPTH, DEC_BATCH, GDN_CONV - 1, GDN_QKV), 1.0),
        'ln1_g': gain((DEPTH, D_MODEL)),
        'w_in': nrm((DEPTH, D_MODEL, D_IN), D_MODEL ** -0.5),
        's5_a_re': -0.5 + nrm((DEPTH, S5_GROUPS, S5_P), 0.01),
        's5_a_im': jnp.pi * jnp.arange(S5_P, dtype=f32) + nrm((DEPTH, S5_GROUPS, S5_P), 0.01),
        's5_log_dt': jax.random.uniform(next(ks), (DEPTH, S5_GROUPS), f32, math.log(1e-3), math.log(1e-1)),
        's5_b_re': nrm((DEPTH, S5_GROUPS, S5_P, S5_GROUP), S5_GROUP ** -0.5),
        's5_b_im': nrm((DEPTH, S5_GROUPS, S5_P, S5_GROUP), S5_GROUP ** -0.5),
        's5_c_re': nrm((DEPTH, S5_GROUPS, S5_GROUP, S5_P), S5_P ** -0.5),
        's5_c_im': nrm((DEPTH, S5_GROUPS, S5_GROUP, S5_P), S5_P ** -0.5),
        's5_d': nrm((DEPTH, S5_WIDTH), 1.0),
        's5_w_glu': nrm((DEPTH, S5_WIDTH, S5_WIDTH), S5_WIDTH ** -0.5),
        's5_b_glu': nrm((DEPTH, S5_WIDTH), 0.02),
        'gla_w_gate': nrm((DEPTH, GLA_RANK, GLA_HEADS * GLA_DK), GLA_RANK ** -0.5),
        'gla_b_gate': nrm((DEPTH, GLA_HEADS * GLA_DK), 0.1),
        'gla_norm': gain((DEPTH, GLA_DV)),
        'gdn_conv_w': nrm((DEPTH, GDN_CONV, GDN_QKV), GDN_CONV ** -0.5),
        'gdn_a_log': jnp.log(jax.random.uniform(next(ks), (DEPTH, GDN_HEADS), f32, 1.0, 16.0)),
        'gdn_dt_bias': gdn_dt + jnp.log(-jnp.expm1(-gdn_dt)),
        'gdn_norm': gain((DEPTH, GDN_DV)),
        'moba_q_norm': gain((DEPTH, MOBA_HD)),
        'moba_k_norm': gain((DEPTH, MOBA_HD)),
        'w_gate': nrm((DEPTH, D_MODEL, N_BRANCH * D_MODEL), D_MODEL ** -0.5),
        'w_br_s5': nrm((DEPTH, S5_WIDTH, D_MODEL), S5_WIDTH ** -0.5),
        'w_br_gla': nrm((DEPTH, GLA_HEADS * GLA_DV, D_MODEL), (GLA_HEADS * GLA_DV) ** -0.5),
        'w_br_gdn': nrm((DEPTH, GDN_HEADS * GDN_DV, D_MODEL), (GDN_HEADS * GDN_DV) ** -0.5),
        'w_br_moba': nrm((DEPTH, MOBA_HEADS * MOBA_HD, D_MODEL), (MOBA_HEADS * MOBA_HD) ** -0.5),
        'w_out': nrm((DEPTH, D_MODEL, D_MODEL), D_MODEL ** -0.5),
        'ln2_g': gain((DEPTH, D_MODEL)),
        'w_ff1': nrm((DEPTH, D_MODEL, D_FF), D_MODEL ** -0.5),
        'w_ff2': nrm((DEPTH, D_FF, D_MODEL), D_FF ** -0.5),
    }


def reference(x_prompt, x_sample, cache_moba_k, cache_moba_v, page_table,
              state_s5_re, state_s5_im, state_gla, state_gdn, state_gdn_conv,
              ln1_g, w_in, s5_a_re, s5_a_im, s5_log_dt, s5_b_re, s5_b_im, s5_c_re, s5_c_im,
              s5_d, s5_w_glu, s5_b_glu, gla_w_gate, gla_b_gate, gla_norm,
              gdn_conv_w, gdn_a_log, gdn_dt_bias, gdn_norm, moba_q_norm, moba_k_norm,
              w_gate, w_br_s5, w_br_gla, w_br_gdn, w_br_moba, w_out, ln2_g, w_ff1, w_ff2):
    layers = [dict(ln1_g=ln1_g[l], w_in=w_in[l], s5_a_re=s5_a_re[l], s5_a_im=s5_a_im[l],
                   s5_log_dt=s5_log_dt[l], s5_b_re=s5_b_re[l], s5_b_im=s5_b_im[l],
                   s5_c_re=s5_c_re[l], s5_c_im=s5_c_im[l], s5_d=s5_d[l], s5_w_glu=s5_w_glu[l],
                   s5_b_glu=s5_b_glu[l], gla_w_gate=gla_w_gate[l], gla_b_gate=gla_b_gate[l],
                   gla_norm=gla_norm[l], gdn_conv_w=gdn_conv_w[l], gdn_a_log=gdn_a_log[l],
                   gdn_dt_bias=gdn_dt_bias[l], gdn_norm=gdn_norm[l], moba_q_norm=moba_q_norm[l],
                   moba_k_norm=moba_k_norm[l], w_gate=w_gate[l], w_br_s5=w_br_s5[l],
                   w_br_gla=w_br_gla[l], w_br_gdn=w_br_gdn[l], w_br_moba=w_br_moba[l],
                   w_out=w_out[l], ln2_g=ln2_g[l], w_ff1=w_ff1[l], w_ff2=w_ff2[l])
              for l in range(DEPTH)]
    bp = x_prompt.shape[0]
    bs = x_sample.shape[0]
    n_pages = page_table.shape[1]
    past_len = n_pages * PAGE_SIZE
    dt = x_prompt.dtype

    xp = x_prompt
    p_states = []
    for l in range(DEPTH):
        xp, st = _layer(xp, 0, None, None,
                        jnp.zeros((bp, S5_GROUPS, S5_P), dt), jnp.zeros((bp, S5_GROUPS, S5_P), dt),
                        jnp.zeros((bp, GLA_HEADS, GLA_DK, GLA_DV), dt),
                        jnp.zeros((bp, GDN_HEADS, GDN_DK, GDN_DV), dt),
                        jnp.zeros((bp, GDN_CONV - 1, GDN_QKV), dt), layers[l])
        p_states.append(st)

    xs = x_sample
    s_states = []
    for l in range(DEPTH):
        past_k = cache_moba_k[l][page_table].reshape(bs, past_len, MOBA_HEADS, MOBA_HD)
        past_v = cache_moba_v[l][page_table].reshape(bs, past_len, MOBA_HEADS, MOBA_HD)
        xs, st = _layer(xs, past_len, past_k, past_v, state_s5_re[l], state_s5_im[l],
                        state_gla[l], state_gdn[l], state_gdn_conv[l], layers[l])
        s_states.append(st)

    p_moba_k, p_moba_v, p_s5_re, p_s5_im, p_gla, p_gdn, p_gdn_conv = _stack_states(p_states)
    s_moba_k, s_moba_v, s_s5_re, s_s5_im, s_gla, s_gdn, s_gdn_conv = _stack_states(s_states)
    y_prompt = xp
    y_sample = xs
    return (y_prompt, y_sample, p_moba_k, p_moba_v, p_s5_re, p_s5_im, p_gla, p_gdn, p_gdn_conv,
            s_moba_k, s_moba_v, s_s5_re, s_s5_im, s_gla, s_gdn, s_gdn_conv)
```

```python
import functools
import math

import jax
import jax.numpy as jnp
import numpy as np
from jax import lax
from jax.experimental import pallas as pl
from jax.experimental.pallas import tpu as pltpu

F32 = jnp.float32
BF16 = jnp.bfloat16

D_MODEL = 1024
N_BRANCH = 4
BRANCH_WIDTH = D_MODEL // N_BRANCH
HEADS = 4
HEAD_DIM = BRANCH_WIDTH // HEADS
S5_GROUP = 16
S5_GROUPS = BRANCH_WIDTH // S5_GROUP
S5_P = 64
S5_STATE = S5_GROUPS * S5_P
GLA_RANK = 16
GLA_TAU = 16.0
GDN_CONV = 4
GDN_QKV = 3 * BRANCH_WIDTH
MOBA_BLOCK = 256
MOBA_TOPK = 3
ROT_DIM = HEAD_DIM // 4
ROPE_THETA = 500000.0
PAGE_SIZE = 128
D_FF = 4 * D_MODEL
EPS = 1e-6
NEG = -1e30

LANES = 128
S5_CHUNK = 4
GLA_SUB = 16
GDN_CHUNK = 64
VMEM_LIMIT = 56 * 1024 * 1024

HIGHEST = lax.Precision.HIGHEST


def _cparams(*sem):
    return pltpu.CompilerParams(dimension_semantics=sem, vmem_limit_bytes=VMEM_LIMIT)


def _const_spec(shape):
    zeros = (0,) * len(shape)
    return pl.BlockSpec(shape, lambda *_: zeros)


def _dot(a, b):
    return jnp.dot(a, b, preferred_element_type=F32)


def _dot_nt(a, b):
    return lax.dot_general(a, b, (((1,), (1,)), ((), ())), preferred_element_type=F32)


def _dot_tn(a, b):
    return lax.dot_general(a, b, (((0,), (0,)), ((), ())), preferred_element_type=F32)


def _bdot(a, b):
    return _dot(a.astype(BF16), b.astype(BF16))


def _split2(x):
    hi = x.astype(BF16)
    lo = (x - hi.astype(F32)).astype(BF16)
    return hi, lo


def _dot_sel(x, sel):
    hi, lo = _split2(x)
    return _dot(hi, sel) + _dot(lo, sel)


def _dot_sel_exact(x, sel):
    x1 = x.astype(BF16)
    r1 = x - x1.astype(F32)
    x2 = r1.astype(BF16)
    x3 = (r1 - x2.astype(F32)).astype(BF16)
    return _dot(x1, sel) + _dot(x2, sel) + _dot(x3, sel)


def _dot3(a, b):
    ah, al = _split2(a)
    bh, bl = _split2(b)
    return _dot(ah, bh) + (_dot(ah, bl) + _dot(al, bh))


def _rms_rows(x, g):
    return x * lax.rsqrt(jnp.mean(x * x, axis=-1, keepdims=True) + EPS) * g


def _sigmoid(x):
    return 1.0 / (1.0 + jnp.exp(-x))


def _silu(x):
    return x * _sigmoid(x)


def _softplus(x):
    return jnp.maximum(x, 0.0) + jnp.log1p(jnp.exp(-jnp.abs(x)))


def _head_ones():
    r = np.arange(BRANCH_WIDTH) // HEAD_DIM
    return jnp.asarray(r[:, None] == r[None, :], BF16)


IN_OUTS = (("s5_u", 256), ("gla_q", 256), ("gla_k", 256), ("gla_v", 256), ("gla_r", 256),
           ("gla_lr", 128), ("gdn_qkv", 768), ("gdn_b", 256), ("gdn_a", 256), ("gdn_z", 256),
           ("moba_q", 256), ("moba_k", 256), ("moba_v", 256))
IN_WIDTH = sum(w for _, w in IN_OUTS)


def _regroup_w_in(w_in):
    sizes = (256, 256, 256, 256, GLA_RANK, 256, 256, 256, 256, HEADS, HEADS, 256, 256, 256, 256)
    offs = np.cumsum((0,) + sizes)
    (s5_u, a_q, a_k, a_v, a_lr, a_r, d_q, d_k, d_v, d_b, d_a, d_z, m_q, m_k, m_v) = (
        w_in[:, offs[i]:offs[i + 1]] for i in range(len(sizes)))
    lr = jnp.pad(a_lr, ((0, 0), (0, 128 - GLA_RANK)))
    cols = [s5_u, a_q, a_k, a_v, a_r, lr, d_q, d_k, d_v,
            jnp.repeat(d_b, HEAD_DIM, axis=1), jnp.repeat(d_a, HEAD_DIM, axis=1), d_z, m_q, m_k, m_v]
    return jnp.concatenate(cols, axis=1).astype(BF16)


def _inproj_kernel(x_ref, g_ref, w_ref, wg_ref, bg_ref, alog_ref, dtb_ref, *outs):
    x = x_ref[...]
    hb = _rms_rows(x, g_ref[...]).astype(BF16)
    vals = {}
    off = 0
    for name, n in IN_OUTS:
        vals[name] = _dot(hb, w_ref[:, off:off + n])
        off += n
    z = _bdot(vals["gla_lr"], wg_ref[...]) + bg_ref[...]
    vals["gla_lr"] = -_softplus(-z) * (1.0 / GLA_TAU)
    vals["gdn_b"] = _sigmoid(vals["gdn_b"])
    vals["gdn_a"] = -jnp.exp(alog_ref[...]) * _softplus(vals["gdn_a"] + dtb_ref[...])
    for (name, _), o_ref in zip(IN_OUTS, outs):
        o_ref[...] = vals[name]


def _inproj(x2d, lp, tm):
    n = x2d.shape[0]
    out_shape = []
    out_specs = []
    for name, w in IN_OUTS:
        w_out = 256 if name == "gla_lr" else w
        out_shape.append(jax.ShapeDtypeStruct((n, w_out), F32))
        out_specs.append(pl.BlockSpec((tm, w_out), lambda i: (i, 0)))
    res = pl.pallas_call(
        _inproj_kernel,
        grid=(n // tm,),
        in_specs=[pl.BlockSpec((tm, D_MODEL), lambda i: (i, 0)),
                  _const_spec((1, D_MODEL)), _const_spec((D_MODEL, IN_WIDTH)),
                  _const_spec((128, 256)), _const_spec((1, 256)),
                  _const_spec((1, 256)), _const_spec((1, 256))],
        out_specs=out_specs,
        out_shape=out_shape,
        compiler_params=_cparams("parallel"),
        name="inproj",
    )(x2d, lp["ln1_g"], lp["w_in"], lp["gla_wg"], lp["gla_bg"], lp["gdn_alog"], lp["gdn_dtb"])
    return dict(zip((nm for nm, _ in IN_OUTS), res))


def _merge_kernel(x_ref, ya_ref, yb_ref, yc_ref, yd_ref, g_ref, wgate_ref, wglu_ref, bglu_ref,
                  wbr_ref, wout_ref, o_ref):
    x = x_ref[...]
    hb = _rms_rows(x, g_ref[...]).astype(BF16)
    ya = ya_ref[...]
    ya = ya * _sigmoid(_bdot(ya, wglu_ref[...]) + bglu_ref[...])
    merged = None
    for i, y in enumerate((ya, yb_ref[...], yc_ref[...], yd_ref[...])):
        gate = _sigmoid(_dot(hb, wgate_ref[:, i * D_MODEL:(i + 1) * D_MODEL]))
        term = gate * _dot(y.astype(BF16), wbr_ref[i])
        merged = term if merged is None else merged + term
    o_ref[...] = x + _bdot(merged, wout_ref[...])


def _merge(x2d, ya, yb, yc, yd, lp, tm):
    n = x2d.shape[0]
    row = lambda w: pl.BlockSpec((tm, w), lambda i: (i, 0))
    return pl.pallas_call(
        _merge_kernel,
        grid=(n // tm,),
        in_specs=[row(D_MODEL), row(256), row(256), row(256), row(256),
                  _const_spec((1, D_MODEL)), _const_spec((D_MODEL, N_BRANCH * D_MODEL)),
                  _const_spec((256, 256)), _const_spec((1, 256)),
                  _const_spec((N_BRANCH, 256, D_MODEL)), _const_spec((D_MODEL, D_MODEL))],
        out_specs=row(D_MODEL),
        out_shape=jax.ShapeDtypeStruct((n, D_MODEL), F32),
        compiler_params=_cparams("parallel"),
        name="merge",
    )(x2d, ya, yb, yc, yd, lp["ln1_g"], lp["w_gate"], lp["s5_w_glu"], lp["s5_b_glu"],
      lp["w_br"], lp["w_out"])


def _mlp_kernel(x_ref, g_ref, w1_ref, w2_ref, o_ref):
    x = x_ref[...]
    hb = _rms_rows(x, g_ref[...]).astype(BF16)
    z = jnp.maximum(_dot(hb, w1_ref[...]), 0.0)
    o_ref[...] = x + _bdot(z * z, w2_ref[...])


def _mlp(x2d, lp, tm):
    n = x2d.shape[0]
    row = pl.BlockSpec((tm, D_MODEL), lambda i: (i, 0))
    single = pl.Buffered(1)
    return pl.pallas_call(
        _mlp_kernel,
        grid=(n // tm,),
        in_specs=[row, _const_spec((1, D_MODEL)),
                  pl.BlockSpec((D_MODEL, D_FF), lambda i: (0, 0), pipeline_mode=single),
                  pl.BlockSpec((D_FF, D_MODEL), lambda i: (0, 0), pipeline_mode=single)],
        out_specs=row,
        out_shape=jax.ShapeDtypeStruct((n, D_MODEL), F32),
        compiler_params=_cparams("parallel"),
        name="mlp",
    )(x2d, lp["ln2_g"], lp["w_ff1"], lp["w_ff2"])


def _s5_matrices(p, chunk):
    hp = dict(precision=HIGHEST)
    dt = jnp.exp(p["s5_log_dt"])[:, None]
    ar, ai = p["s5_a_re"], p["s5_a_im"]
    mag = jnp.exp(ar * dt)
    abar_re = mag * jnp.cos(ai * dt)
    abar_im = mag * jnp.sin(ai * dt)
    den = ar * ar + ai * ai
    nr = abar_re - 1.0
    f_re = (nr * ar + abar_im * ai) / den
    f_im = (abar_im * ar - nr * ai) / den
    br, bi = p["s5_b_re"], p["s5_b_im"]
    bbar_re = f_re[..., None] * br - f_im[..., None] * bi
    bbar_im = f_re[..., None] * bi + f_im[..., None] * br
    pw_re = [jnp.ones_like(abar_re)]
    pw_im = [jnp.zeros_like(abar_re)]
    for _ in range(chunk):
        r, i = pw_re[-1], pw_im[-1]
        pw_re.append(r * abar_re - i * abar_im)
        pw_im.append(r * abar_im + i * abar_re)
    pw_re = jnp.stack(pw_re)
    pw_im = jnp.stack(pw_im)
    cr, ci = p["s5_c_re"], p["s5_c_im"]
    ca_re = cr[None] * pw_re[:, :, None, :] - ci[None] * pw_im[:, :, None, :]
    ca_im = cr[None] * pw_im[:, :, None, :] + ci[None] * pw_re[:, :, None, :]
    eye = jnp.eye(S5_GROUPS, dtype=F32)
    kern = (jnp.einsum("tgop,gpi->tgoi", ca_re[:chunk], bbar_re, **hp)
            - jnp.einsum("tgop,gpi->tgoi", ca_im[:chunk], bbar_im, **hp))
    lag = np.zeros((chunk, chunk, chunk), np.float32)
    for s in range(chunk):
        for t in range(s, chunk):
            lag[s, t, t - s] = 1.0
    w_in = jnp.einsum("stk,gh,kgoi->sgitho", jnp.asarray(lag), eye, kern, **hp)
    w_in = w_in.reshape(chunk * BRANCH_WIDTH, chunk * BRANCH_WIDTH)
    rev_re = pw_re[:chunk][::-1]
    rev_im = pw_im[:chunk][::-1]
    ab_re = rev_re[..., None] * bbar_re[None] - rev_im[..., None] * bbar_im[None]
    ab_im = rev_re[..., None] * bbar_im[None] + rev_im[..., None] * bbar_re[None]
    xs_re = jnp.einsum("gh,sgpi->sgihp", eye, ab_re, **hp).reshape(chunk * BRANCH_WIDTH, S5_STATE)
    xs_im = jnp.einsum("gh,sgpi->sgihp", eye, ab_im, **hp).reshape(chunk * BRANCH_WIDTH, S5_STATE)
    w_xs = jnp.concatenate([xs_re, xs_im], axis=1)
    hy_re = jnp.einsum("gh,tgop->gptho", eye, ca_re[1:], **hp).reshape(S5_STATE, chunk * BRANCH_WIDTH)
    hy_im = jnp.einsum("gh,tgop->gptho", eye, ca_im[1:], **hp).reshape(S5_STATE, chunk * BRANCH_WIDTH)
    w_hy = jnp.concatenate([hy_re, -hy_im], axis=0)
    a_pow = jnp.stack([pw_re[chunk].reshape(-1), pw_im[chunk].reshape(-1)])
    d_row = jnp.tile(p["s5_d"], chunk)[None, :]
    return dict(w_xs=w_xs.astype(BF16), w_in=w_in.astype(BF16), w_hy=w_hy.astype(BF16),
                a_pow=a_pow, d_row=d_row)


def _gelu(y):
    c = math.sqrt(2.0 / math.pi)
    return 0.5 * y * (1.0 + jnp.tanh(c * (y + 0.044715 * (y * y * y))))


def _s5_kernel(u_ref, wxs_ref, win_ref, why_ref, apow_ref, d_ref, y_ref, hfin_ref, xs_scr, hs_scr):
    rows = u_ref.shape[0]
    u = u_ref[...]
    ub = u.astype(BF16)
    xs_scr[...] = _dot(ub, wxs_ref[...])
    a_re = apow_ref[0:1, :]
    a_im = apow_ref[1:2, :]

    def step(r, carry):
        hr, hi = carry
        hs_scr[pl.ds(r, 1), :] = jnp.concatenate([hr, hi], axis=-1)
        x = xs_scr[pl.ds(r, 1), :]
        nhr = a_re * hr - a_im * hi + x[:, :S5_STATE]
        nhi = a_re * hi + a_im * hr + x[:, S5_STATE:]
        return nhr, nhi

    zero = jnp.zeros((1, S5_STATE), F32)
    hr, hi = lax.fori_loop(0, rows, step, (zero, zero))
    hfin_ref[0] = jnp.concatenate([hr, hi], axis=-1)
    y = _dot(ub, win_ref[...]) + _dot(hs_scr[...].astype(BF16), why_ref[...]) + d_ref[...] * u
    y_ref[...] = _gelu(y)


def _s5_prompt(u, sm, batch, seq):
    c = S5_CHUNK
    rows = seq // c
    width = c * BRANCH_WIDTH
    u_rows = u.reshape(batch * rows, width)
    single = pl.Buffered(1)
    wspec = lambda shape: pl.BlockSpec(shape, lambda b: (0, 0), pipeline_mode=single)
    y, hfin = pl.pallas_call(
        _s5_kernel,
        grid=(batch,),
        in_specs=[pl.BlockSpec((rows, width), lambda b: (b, 0)),
                  wspec((width, 2 * S5_STATE)), wspec((width, width)), wspec((2 * S5_STATE, width)),
                  _const_spec((2, S5_STATE)), _const_spec((1, width))],
        out_specs=[pl.BlockSpec((rows, width), lambda b: (b, 0)),
                   pl.BlockSpec((1, 1, 2 * S5_STATE), lambda b: (b, 0, 0))],
        out_shape=[jax.ShapeDtypeStruct((batch * rows, width), F32),
                   jax.ShapeDtypeStruct((batch, 1, 2 * S5_STATE), F32)],
        scratch_shapes=[pltpu.VMEM((rows, 2 * S5_STATE), F32), pltpu.VMEM((rows, 2 * S5_STATE), F32)],
        compiler_params=_cparams("parallel"),
        name="s5_prompt",
    )(u_rows, sm["w_xs"], sm["w_in"], sm["w_hy"], sm["a_pow"], sm["d_row"])
    return y.reshape(batch * seq, BRANCH_WIDTH), hfin.reshape(batch, 2 * S5_STATE)


GLA_ROWS = 256


def _gla_kernel(q_ref, k_ref, v_ref, g_ref, r_ref, tri_ref, ones_ref, bmask_ref, gn_ref,
                y_ref, sfin_ref, st_scr, bc_scr, o_scr):
    step = pl.program_id(1)
    sub = GLA_SUB

    @pl.when(step == 0)
    def _():
        st_scr[...] = jnp.zeros_like(st_scr)

    bc_scr[...] = jnp.dot(tri_ref[...], g_ref[...], precision=HIGHEST, preferred_element_type=F32)
    ones = ones_ref[...]
    bmask = bmask_ref[...]
    row = lax.broadcasted_iota(jnp.int32, (sub, BRANCH_WIDTH), 0)

    def body(s, carry):
        r0 = pl.multiple_of(s * sub, sub)
        bc = bc_scr[pl.ds(r0, sub), :]
        q = q_ref[pl.ds(r0, sub), :] * (HEAD_DIM ** -0.5)
        k = k_ref[pl.ds(r0, sub), :]
        v = v_ref[pl.ds(r0, sub), :]
        parts = []
        for j in range(sub):
            e = jnp.exp(jnp.minimum(bc - bc[j:j + 1, :], 0.0))
            parts.append(jnp.where(row >= j, q * k[j:j + 1, :] * e, 0.0))
        att = _dot_sel(jnp.concatenate(parts, axis=0), ones)
        o = att[0:sub] * v[0:1, :]
        for j in range(1, sub):
            o = o + att[j * sub:(j + 1) * sub] * v[j:j + 1, :]
        st = st_scr[...]
        o = o + _dot_nt((q * jnp.exp(bc)).astype(BF16), st.astype(BF16))
        bl = bc[sub - 1:sub, :]
        kt = k * jnp.exp(bl - bc)
        st_scr[...] = st * jnp.exp(bl) + bmask * _dot_tn(v.astype(BF16), kt.astype(BF16))
        o_scr[pl.ds(r0, sub), :] = o
        return carry

    lax.fori_loop(0, q_ref.shape[0] // sub, body, 0)
    o = o_scr[...]
    ms = _dot_sel(o * o, ones) * (1.0 / HEAD_DIM)
    y_ref[...] = o * lax.rsqrt(ms + EPS) * gn_ref[...] * _silu(r_ref[...])

    @pl.when(step == pl.num_programs(1) - 1)
    def _():
        sfin_ref[0] = st_scr[...]


def _sub_tril(rows, sub):
    i = np.arange(rows)
    return jnp.asarray((i[:, None] // sub == i[None, :] // sub) & (i[None, :] <= i[:, None]), F32)


def _unpack_state_t(st):
    b = st.shape[0]
    st = st.reshape(b, HEADS, HEAD_DIM, HEADS, HEAD_DIM)
    diag = jnp.stack([st[:, h, :, h, :] for h in range(HEADS)], axis=1)
    return diag.transpose(0, 1, 3, 2)


def _gla_prompt(q, k, v, g, r, lp, batch, seq):
    rows = min(GLA_ROWS, seq)
    nsteps = seq // rows
    blk = pl.BlockSpec((rows, BRANCH_WIDTH), lambda b, c: (b * nsteps + c, 0))
    hm = np.arange(BRANCH_WIDTH) // HEAD_DIM
    bmask = jnp.asarray(hm[:, None] == hm[None, :], F32)
    y, sfin = pl.pallas_call(
        _gla_kernel,
        grid=(batch, nsteps),
        in_specs=[blk, blk, blk, blk, blk, _const_spec((rows, rows)),
                  _const_spec((BRANCH_WIDTH, BRANCH_WIDTH)), _const_spec((BRANCH_WIDTH, BRANCH_WIDTH)),
                  _const_spec((1, BRANCH_WIDTH))],
        out_specs=[blk, pl.BlockSpec((1, BRANCH_WIDTH, BRANCH_WIDTH), lambda b, c: (b, 0, 0))],
        out_shape=[jax.ShapeDtypeStruct((batch * seq, BRANCH_WIDTH), F32),
                   jax.ShapeDtypeStruct((batch, BRANCH_WIDTH, BRANCH_WIDTH), F32)],
        scratch_shapes=[pltpu.VMEM((BRANCH_WIDTH, BRANCH_WIDTH), F32),
                        pltpu.VMEM((rows, BRANCH_WIDTH), F32), pltpu.VMEM((rows, BRANCH_WIDTH), F32)],
        compiler_params=_cparams("parallel", "arbitrary"),
        name="gla_prompt",
    )(q, k, v, g, r, _sub_tril(rows, GLA_SUB), _head_ones(), bmask, lp["gla_norm"])
    return y, _unpack_state_t(sfin)


CONV_PAD = 8


def _unit_lower_inverse(n):
    c = n.shape[0]
    eye = (lax.broadcasted_iota(jnp.int32, (c, c), 0) == lax.broadcasted_iota(jnp.int32, (c, c), 1)).astype(F32)
    inv = eye - n
    pw = n
    for _ in range(int(math.log2(c)) - 1):
        pw = _dot3(pw, pw)
        inv = inv + _dot3(inv, pw)
    return inv


def _gdn_kernel(x_ref, b_ref, g_ref, z_ref, cw_ref, tri_ref, ones_ref, gn_ref,
                y_ref, sfin_ref, conv_ref, s_scr, buf_scr):
    step = pl.program_id(1)
    c = x_ref.shape[0]

    @pl.when(step == 0)
    def _():
        s_scr[...] = jnp.zeros_like(s_scr)
        buf_scr[0:CONV_PAD, :] = jnp.zeros((CONV_PAD, GDN_QKV), F32)

    x = x_ref[...]
    buf_scr[CONV_PAD:CONV_PAD + c, :] = x
    conv = cw_ref[GDN_CONV - 1:GDN_CONV, :] * x
    for w in range(GDN_CONV - 1):
        lag = GDN_CONV - 1 - w
        conv = conv + cw_ref[w:w + 1, :] * buf_scr[CONV_PAD - lag:CONV_PAD - lag + c, :]
    tail = buf_scr[c:c + CONV_PAD, :]
    buf_scr[0:CONV_PAD, :] = tail
    conv_ref[0] = tail
    qkv = _silu(conv)
    ones = ones_ref[...]
    q = qkv[:, 0:BRANCH_WIDTH]
    k = qkv[:, BRANCH_WIDTH:2 * BRANCH_WIDTH]
    v = qkv[:, 2 * BRANCH_WIDTH:]
    q = q * lax.rsqrt(_dot_sel(q * q, ones) + EPS) * (HEAD_DIM ** -0.5)
    k = k * lax.rsqrt(_dot_sel(k * k, ones) + EPS)
    beta = b_ref[...]
    gc = jnp.dot(tri_ref[...], g_ref[...], precision=HIGHEST, preferred_element_type=F32)
    gam = jnp.exp(gc)
    glast = gc[c - 1:c, :]
    kd = k * jnp.exp(glast - gc)
    bk = beta * gam * k
    bv = beta * v
    qg = gam * q
    ri = lax.broadcasted_iota(jnp.int32, (c, c), 0)
    ci = lax.broadcasted_iota(jnp.int32, (c, c), 1)
    z = z_ref[...]
    outs = []
    for h in range(HEADS):
        sl = slice(h * HEAD_DIM, (h + 1) * HEAD_DIM)
        gch = gc[:, sl]
        dec = jnp.where(ri >= ci, jnp.exp(jnp.minimum(gch - gch.T, 0.0)), 0.0)
        kh = k[:, sl]
        khb = kh.astype(BF16)
        kk = _dot_nt(khb, khb)
        qk = _dot_nt(q[:, sl].astype(BF16), khb)
        inv = _unit_lower_inverse(jnp.where(ri > ci, beta[:, sl] * dec * kk, 0.0))
        w_m = _dot3(inv, bk[:, sl])
        u0 = _dot3(inv, bv[:, sl])
        s_h = s_scr[h]
        s_b = s_h.astype(BF16)
        u = u0 - _dot(w_m.astype(BF16), s_b)
        ub = u.astype(BF16)
        o = _dot(qg[:, sl].astype(BF16), s_b) + _dot((dec * qk).astype(BF16), ub)
        s_scr[h] = jnp.exp(glast[:, sl]) * s_h + _dot_tn(kd[:, sl].astype(BF16), ub)
        o = o * lax.rsqrt(jnp.mean(o * o, axis=-1, keepdims=True) + EPS)
        outs.append(o * gn_ref[:, sl] * _silu(z[:, sl]))
    y_ref[...] = jnp.concatenate(outs, axis=-1)

    @pl.when(step == pl.num_programs(1) - 1)
    def _():
        sfin_ref[0] = s_scr[...]


def _gdn_prompt(x, beta, g, z, lp, batch, seq):
    c = min(GDN_CHUNK, seq)
    nsteps = seq // c
    blk = lambda w: pl.BlockSpec((c, w), lambda b, s: (b * nsteps + s, 0))
    y, sfin, conv = pl.pallas_call(
        _gdn_kernel,
        grid=(batch, nsteps),
        in_specs=[blk(GDN_QKV), blk(256), blk(256), blk(256), _const_spec((CONV_PAD, GDN_QKV)),
                  _const_spec((c, c)), _const_spec((256, 256)), _const_spec((1, 256))],
        out_specs=[blk(256),
                   pl.BlockSpec((1, HEADS, HEAD_DIM, HEAD_DIM), lambda b, s: (b, 0, 0, 0)),
                   pl.BlockSpec((1, CONV_PAD, GDN_QKV), lambda b, s: (b, 0, 0))],
        out_shape=[jax.ShapeDtypeStruct((batch * seq, 256), F32),
                   jax.ShapeDtypeStruct((batch, HEADS, HEAD_DIM, HEAD_DIM), F32),
                   jax.ShapeDtypeStruct((batch, CONV_PAD, GDN_QKV), F32)],
        scratch_shapes=[pltpu.VMEM((HEADS, HEAD_DIM, HEAD_DIM), F32),
                        pltpu.VMEM((CONV_PAD + c, GDN_QKV), F32)],
        compiler_params=_cparams("parallel", "arbitrary"),
        name="gdn_prompt",
    )(x, beta, g, z, lp["gdn_conv_w"], _sub_tril(c, c), _head_ones(), lp["gdn_norm"])
    return y, sfin, conv[:, CONV_PAD - (GDN_CONV - 1):, :]


HALF_ROT = ROT_DIM // 2


def _rope_tables(pos):
    inv = ROPE_THETA ** (-jnp.arange(HALF_ROT, dtype=F32) / HALF_ROT)
    ang = pos.astype(F32)[:, None] * inv[None, :]
    cos, sin = jnp.cos(ang), jnp.sin(ang)
    n = pos.shape[0]
    rest = HEAD_DIM - ROT_DIM
    head = lambda a, b, fill: jnp.concatenate([a, b, jnp.full((n, rest), fill, F32)], axis=1)
    zero = jnp.zeros_like(sin)
    tabs = [head(cos, cos, 1.0), head(-sin, zero, 0.0), head(zero, sin, 0.0)]
    return jnp.stack([jnp.tile(t, (1, HEADS)) for t in tabs])


def _qk_norm_rope(x, gain, tab_ref, ones):
    y = x * lax.rsqrt(_dot_sel(x * x, ones) * (1.0 / HEAD_DIM) + EPS) * gain
    up = pltpu.roll(y, BRANCH_WIDTH - HALF_ROT, 1)
    down = pltpu.roll(y, HALF_ROT, 1)
    return y * tab_ref[0] + up * tab_ref[1] + down * tab_ref[2]


def _moba_prep_kernel(q_ref, k_ref, v_ref, tab_ref, qg_ref, kg_ref, ones_ref,
                      mk_ref, qh_ref, kh_ref, vt_ref, km_ref):
    ones = ones_ref[...]
    rows = q_ref.shape[0]
    nblk = rows // MOBA_BLOCK
    mq = _qk_norm_rope(q_ref[...], qg_ref[...], tab_ref, ones) * (HEAD_DIM ** -0.5)
    mk = _qk_norm_rope(k_ref[...], kg_ref[...], tab_ref, ones)
    mk_ref[...] = mk
    kmean = jnp.mean(mk.reshape(nblk, MOBA_BLOCK, BRANCH_WIDTH), axis=1)
    vt = v_ref[...].T
    for h in range(HEADS):
        sl = slice(h * HEAD_DIM, (h + 1) * HEAD_DIM)
        qh_ref[0, h] = mq[:, sl]
        km_ref[0, h] = kmean[:, sl]
        for j in range(nblk):
            rs = slice(j * MOBA_BLOCK, (j + 1) * MOBA_BLOCK)
            kh_ref[0, h, j] = mk[rs, sl]
            vt_ref[0, h, j] = vt[sl, rs]


def _moba_prep(q, k, v, tabs, lp, batch, seq):
    rows = min(8 * MOBA_BLOCK, seq)
    nsteps = seq // rows
    nblk = rows // MOBA_BLOCK
    nb = seq // MOBA_BLOCK
    blk = pl.BlockSpec((rows, 256), lambda b, r: (b * nsteps + r, 0))
    return pl.pallas_call(
        _moba_prep_kernel,
        grid=(batch, nsteps),
        in_specs=[blk, blk, blk, pl.BlockSpec((3, rows, 256), lambda b, r: (0, r, 0)),
                  _const_spec((1, 256)), _const_spec((1, 256)), _const_spec((256, 256))],
        out_specs=[blk,
                   pl.BlockSpec((1, HEADS, rows, HEAD_DIM), lambda b, r: (b, 0, r, 0)),
                   pl.BlockSpec((1, HEADS, nblk, MOBA_BLOCK, HEAD_DIM), lambda b, r: (b, 0, r, 0, 0)),
                   pl.BlockSpec((1, HEADS, nblk, HEAD_DIM, MOBA_BLOCK), lambda b, r: (b, 0, r, 0, 0)),
                   pl.BlockSpec((1, HEADS, nblk, HEAD_DIM), lambda b, r: (b, 0, r, 0))],
        out_shape=[jax.ShapeDtypeStruct((batch * seq, 256), F32),
                   jax.ShapeDtypeStruct((batch, HEADS, seq, HEAD_DIM), F32),
                   jax.ShapeDtypeStruct((batch, HEADS, nb, MOBA_BLOCK, HEAD_DIM), F32),
                   jax.ShapeDtypeStruct((batch, HEADS, nb, HEAD_DIM, MOBA_BLOCK), F32),
                   jax.ShapeDtypeStruct((batch, HEADS, nb, HEAD_DIM), F32)],
        compiler_params=_cparams("parallel", "parallel"),
        name="moba_prep",
    )(q, k, v, tabs, lp["moba_q_norm"], lp["moba_k_norm"], _head_ones())


def _moba_attn_kernel(qh_ref, kh_ref, vt_ref, km_ref, o_ref, sel_scr):
    qb = pl.program_id(1)
    nb = km_ref.shape[2]
    blk = MOBA_BLOCK
    blk_id = lax.broadcasted_iota(jnp.int32, (nb, blk), 0)
    kpos = lax.broadcasted_iota(jnp.int32, (blk, blk), 0)
    qpos = lax.broadcasted_iota(jnp.int32, (blk, blk), 1)
    outs = []
    for h in range(HEADS):
        q = qh_ref[0, h]
        qb16 = q.astype(BF16)
        gate = lax.dot_general(km_ref[0, h], q, (((1,), (1,)), ((), ())), precision=HIGHEST,
                               preferred_element_type=F32)
        gate = jnp.where(blk_id < qb, gate, NEG)
        taken = jnp.zeros((nb, blk), jnp.bool_)
        for _ in range(min(MOBA_TOPK, nb)):
            best = jnp.max(gate, axis=0, keepdims=True)
            idx = jnp.min(jnp.where(gate == best, blk_id, nb), axis=0, keepdims=True)
            hit = blk_id == idx
            taken = jnp.logical_or(taken, hit)
            gate = jnp.where(hit, -jnp.inf, gate)
        sel_scr[...] = jnp.where(jnp.logical_and(taken, blk_id < qb), 1.0, 0.0)

        def scores(j):
            return _dot_nt(kh_ref[0, h, j].astype(BF16), qb16)

        s = jnp.where(kpos <= qpos, scores(qb), NEG)
        m = jnp.max(s, axis=0, keepdims=True)
        p = jnp.exp(s - m)
        l = jnp.sum(p, axis=0, keepdims=True)
        acc = _dot(vt_ref[0, h, qb].astype(BF16), p.astype(BF16))

        def body(j, carry):
            m, l, acc = carry
            s = jnp.where(sel_scr[pl.ds(j, 1), :] > 0.0, scores(j), NEG)
            m_new = jnp.maximum(m, jnp.max(s, axis=0, keepdims=True))
            alpha = jnp.exp(m - m_new)
            p = jnp.exp(s - m_new)
            l = alpha * l + jnp.sum(p, axis=0, keepdims=True)
            acc = alpha * acc + _dot(vt_ref[0, h, j].astype(BF16), p.astype(BF16))
            return m_new, l, acc

        m, l, acc = lax.fori_loop(0, qb, body, (m, l, acc))
        outs.append(acc / l)
    o_ref[...] = jnp.concatenate(outs, axis=0).T


def _moba_prompt(qh, kh, vt, km, batch, seq):
    nb = seq // MOBA_BLOCK
    return pl.pallas_call(
        _moba_attn_kernel,
        grid=(batch, nb),
        in_specs=[pl.BlockSpec((1, HEADS, MOBA_BLOCK, HEAD_DIM), lambda b, i: (b, 0, i, 0)),
                  pl.BlockSpec((1, HEADS, nb, MOBA_BLOCK, HEAD_DIM), lambda b, i: (b, 0, 0, 0, 0)),
                  pl.BlockSpec((1, HEADS, nb, HEAD_DIM, MOBA_BLOCK), lambda b, i: (b, 0, 0, 0, 0)),
                  pl.BlockSpec((1, HEADS, nb, HEAD_DIM), lambda b, i: (b, 0, 0, 0))],
        out_specs=pl.BlockSpec((MOBA_BLOCK, 256), lambda b, i: (b * nb + i, 0)),
        out_shape=jax.ShapeDtypeStruct((batch * seq, 256), F32),
        scratch_shapes=[pltpu.VMEM((nb, MOBA_BLOCK), F32)],
        compiler_params=_cparams("parallel", "arbitrary"),
        name="moba_prompt",
    )(qh, kh, vt, km)


def _s5_step_kernel(u_ref, h0_ref, wxs_ref, win_ref, why_ref, apow_ref, d_ref, y_ref, h_ref):
    u = u_ref[...]
    ub = u.astype(BF16)
    h0 = h0_ref[...]
    xs = _dot(ub, wxs_ref[...])
    a_re = apow_ref[0:1, :]
    a_im = apow_ref[1:2, :]
    hr0 = h0[:, :S5_STATE]
    hi0 = h0[:, S5_STATE:]
    hr = a_re * hr0 - a_im * hi0 + xs[:, :S5_STATE]
    hi = a_re * hi0 + a_im * hr0 + xs[:, S5_STATE:]
    h_ref[...] = jnp.concatenate([hr, hi], axis=-1)
    y = _dot(ub, win_ref[...]) + _dot(h0.astype(BF16), why_ref[...]) + d_ref[...] * u
    y_ref[...] = _gelu(y)


def _s5_step(u, h0, sm):
    n = u.shape[0]
    return pl.pallas_call(
        _s5_step_kernel,
        out_shape=[jax.ShapeDtypeStruct((n, BRANCH_WIDTH), F32),
                   jax.ShapeDtypeStruct((n, 2 * S5_STATE), F32)],
        compiler_params=pltpu.CompilerParams(vmem_limit_bytes=VMEM_LIMIT),
        name="s5_step",
    )(u, h0, sm["w_xs"], sm["w_in"], sm["w_hy"], sm["a_pow"], sm["d_row"])


def _expand_mats():
    idx = np.arange(HEAD_DIM * HEAD_DIM)
    rep = (np.arange(HEAD_DIM)[:, None] == idx[None, :] // HEAD_DIM)
    til = (np.arange(HEAD_DIM)[:, None] == idx[None, :] % HEAD_DIM)
    return jnp.asarray(rep, BF16), jnp.asarray(til, BF16), jnp.asarray(til.T, BF16)


def _gla_step_kernel(q_ref, k_ref, v_ref, g_ref, r_ref, s0_ref, rep_ref, til_ref, tilt_ref, gn_ref,
                     y_ref, s_ref):
    rep = rep_ref[...]
    eg = _dot_sel_exact(jnp.exp(g_ref[...]), rep)
    kr = _dot_sel_exact(k_ref[...], rep)
    qr = _dot_sel_exact(q_ref[...] * (HEAD_DIM ** -0.5), rep)
    vt = _dot_sel_exact(v_ref[...], til_ref[...])
    s = eg * s0_ref[...] + kr * vt
    s_ref[...] = s
    o = _dot_sel(qr * s, tilt_ref[...])
    y_ref[...] = _rms_rows(o, gn_ref[...]) * _silu(r_ref[...])


def _gdn_conv_step_kernel(x_ref, c0_ref, cw_ref, qkv_ref, cnew_ref):
    x = x_ref[...]
    c0 = c0_ref[...]
    conv = cw_ref[GDN_CONV - 1:GDN_CONV, :] * x
    for w in range(GDN_CONV - 1):
        conv = conv + cw_ref[w:w + 1, :] * c0[:, w * GDN_QKV:(w + 1) * GDN_QKV]
    qkv_ref[...] = _silu(conv)
    cnew_ref[...] = jnp.concatenate([c0[:, GDN_QKV:], x], axis=-1)


def _gdn_step_kernel(q_ref, k_ref, v_ref, b_ref, g_ref, z_ref, s0_ref, rep_ref, til_ref, tilt_ref,
                     gn_ref, y_ref, s_ref):
    q = q_ref[...]
    k = k_ref[...]
    q = q * lax.rsqrt(jnp.sum(q * q, axis=-1, keepdims=True) + EPS) * (HEAD_DIM ** -0.5)
    k = k * lax.rsqrt(jnp.sum(k * k, axis=-1, keepdims=True) + EPS)
    beta = b_ref[:, 0:1]
    gam = jnp.exp(g_ref[:, 0:1])
    rep = rep_ref[...]
    tilt = tilt_ref[...]
    kr = _dot_sel_exact(k, rep)
    qr = _dot_sel_exact(q, rep)
    s0 = s0_ref[...]
    ks = _dot_sel(kr * s0, tilt)
    qs = _dot_sel(qr * s0, tilt)
    u = v_ref[...] - gam * ks
    qk = jnp.sum(q * k, axis=-1, keepdims=True)
    o = gam * qs + (beta * qk) * u
    s_ref[...] = gam * s0 + kr * _dot_sel_exact(beta * u, til_ref[...])
    y_ref[...] = _rms_rows(o, gn_ref[...]) * _silu(z_ref[...])


def _whole_call(kernel, out_shape, name, *args):
    return pl.pallas_call(kernel, out_shape=out_shape, name=name,
                          compiler_params=pltpu.CompilerParams(vmem_limit_bytes=VMEM_LIMIT))(*args)


def _page_sum_kernel(c_ref, o_ref):
    o_ref[...] = jnp.sum(c_ref[...], axis=1)


def _page_sums(cache_pages):
    n = cache_pages.shape[0]
    pb = 32
    return pl.pallas_call(
        _page_sum_kernel,
        grid=(n // pb,),
        in_specs=[pl.BlockSpec((pb, PAGE_SIZE, 256), lambda i: (i, 0, 0))],
        out_specs=pl.BlockSpec((pb, 256), lambda i: (i, 0)),
        out_shape=jax.ShapeDtypeStruct((n, 256), F32),
        compiler_params=_cparams("parallel"),
        name="page_sums",
    )(cache_pages)


def _moba_select_kernel(pt_ref, q_ref, k_ref, tab_ref, qg_ref, kg_ref, ones_ref, ps_ref,
                        mq_ref, mk_ref, sel_ref, km_scr):
    ones = ones_ref[...]
    mq_ref[...] = _qk_norm_rope(q_ref[...], qg_ref[...], tab_ref, ones) * (HEAD_DIM ** -0.5)
    mk_ref[...] = _qk_norm_rope(k_ref[...], kg_ref[...], tab_ref, ones)
    nseq = q_ref.shape[0]
    nb = km_scr.shape[0]
    pages_per_block = MOBA_BLOCK // PAGE_SIZE
    blk_id = lax.broadcasted_iota(jnp.int32, (nb, 256), 0)

    def per_seq(b, carry):
        def per_block(n, c2):
            acc = ps_ref[pl.ds(pt_ref[b, pages_per_block * n], 1), :]
            for e in range(1, pages_per_block):
                acc = acc + ps_ref[pl.ds(pt_ref[b, pages_per_block * n + e], 1), :]
            km_scr[pl.ds(n, 1), :] = acc * (1.0 / MOBA_BLOCK)
            return c2

        lax.fori_loop(0, nb, per_block, 0)
        gate = _dot_sel(km_scr[...] * mq_ref[pl.ds(b, 1), :], ones)
        rows = []
        for _ in range(MOBA_TOPK):
            best = jnp.max(gate, axis=0, keepdims=True)
            idx = jnp.min(jnp.where(gate == best, blk_id, nb), axis=0, keepdims=True)
            rows.append(idx)
            gate = jnp.where(blk_id == idx, -jnp.inf, gate)
        rows.append(jnp.zeros((8 - MOBA_TOPK, 256), jnp.int32))
        sel_ref[pl.ds(b, 1)] = jnp.concatenate(rows, axis=0)[None]
        return carry

    lax.fori_loop(0, nseq, per_seq, 0)


def _moba_select(q, k, tabs, page_table, psums, layer, lp):
    nseq, npages = page_table.shape
    pool = psums.shape[0] // 2
    nb = npages * PAGE_SIZE // MOBA_BLOCK
    full = lambda shape: pl.BlockSpec(shape, lambda i, pt: (0,) * len(shape))
    grid_spec = pltpu.PrefetchScalarGridSpec(
        num_scalar_prefetch=1,
        grid=(1,),
        in_specs=[full((nseq, 256)), full((nseq, 256)), full((3, 1, 256)), full((1, 256)), full((1, 256)),
                  full((256, 256)), pl.BlockSpec((pool, 256), lambda i, pt: (layer, 0))],
        out_specs=[full((nseq, 256)), full((nseq, 256)), full((nseq, 8, 256))],
        scratch_shapes=[pltpu.VMEM((nb, 256), F32)],
    )
    return pl.pallas_call(
        _moba_select_kernel,
        grid_spec=grid_spec,
        out_shape=[jax.ShapeDtypeStruct((nseq, 256), F32), jax.ShapeDtypeStruct((nseq, 256), F32),
                   jax.ShapeDtypeStruct((nseq, 8, 256), jnp.int32)],
        compiler_params=_cparams("arbitrary"),
        name="moba_select",
    )(page_table, q, k, tabs, lp["moba_q_norm"], lp["moba_k_norm"], _head_ones(), psums)


N_SEL_PAGES = MOBA_TOPK * (MOBA_BLOCK // PAGE_SIZE)


def _moba_step_kernel(sel_ref, pt_ref, q_ref, kn_ref, vn_ref, *refs):
    kp = refs[:N_SEL_PAGES]
    vp = refs[N_SEL_PAGES:2 * N_SEL_PAGES]
    o_ref = refs[2 * N_SEL_PAGES]
    h = pl.program_id(1)
    lane = lax.broadcasted_iota(jnp.int32, (1, 256), 1)
    own = lane // HEAD_DIM == h
    qm = jnp.where(own, q_ref[0], 0.0)
    logits = [jnp.sum(r[0] * qm, axis=-1, keepdims=True) for r in kp]
    l_self = jnp.sum(kn_ref[0] * qm, axis=-1, keepdims=True)
    m = l_self
    for lg in logits:
        m = jnp.maximum(m, jnp.max(lg, axis=0, keepdims=True))
    p_self = jnp.exp(l_self - m)
    den = p_self
    num = p_self * vn_ref[0]
    for lg, r in zip(logits, vp):
        p = jnp.exp(lg - m)
        den = den + jnp.sum(p, axis=0, keepdims=True)
        num = num + jnp.sum(p * r[0], axis=0, keepdims=True)
    o = jnp.where(own, num / den, 0.0)

    @pl.when(h == 0)
    def _():
        o_ref[0] = o

    @pl.when(h != 0)
    def _():
        o_ref[0] = o_ref[0] + o


def _moba_step(q, k_new, v_new, sel, page_table, cache_k, cache_v, layer):
    nseq, npages = page_table.shape
    pool = cache_k.shape[0] // 2
    ppb = MOBA_BLOCK // PAGE_SIZE

    def page_spec(r, e):
        def index(b, h, sel_ref, pt_ref):
            blk = sel_ref[(b * HEADS + h) * MOBA_TOPK + r]
            return (layer * pool + pt_ref[b * npages + ppb * blk + e], 0, 0)
        return pl.BlockSpec((1, PAGE_SIZE, 256), index)

    row = pl.BlockSpec((1, 1, 256), lambda b, h, s, p: (b, 0, 0))
    pages = [page_spec(r, e) for r in range(MOBA_TOPK) for e in range(ppb)]
    grid_spec = pltpu.PrefetchScalarGridSpec(
        num_scalar_prefetch=2,
        grid=(nseq, HEADS),
        in_specs=[row, row, row] + pages + pages,
        out_specs=row,
    )
    r3 = lambda a: a.reshape(nseq, 1, 256)
    out = pl.pallas_call(
        _moba_step_kernel,
        grid_spec=grid_spec,
        out_shape=jax.ShapeDtypeStruct((nseq, 1, 256), F32),
        compiler_params=_cparams("parallel", "arbitrary"),
        name="moba_step",
    )(sel.reshape(-1), page_table.reshape(-1), r3(q), r3(k_new), r3(v_new),
      *([cache_k] * N_SEL_PAGES), *([cache_v] * N_SEL_PAGES))
    return out.reshape(nseq, 256)


def _layer_params(l, w):
    tile4 = lambda a: jnp.tile(a, HEADS)[None]
    rep64 = lambda a: jnp.repeat(a, HEAD_DIM)[None]
    s5p = {k: w[k][l] for k in ("s5_a_re", "s5_a_im", "s5_log_dt", "s5_b_re", "s5_b_im",
                                "s5_c_re", "s5_c_im", "s5_d")}
    return dict(
        ln1_g=w["ln1_g"][l][None], w_in=_regroup_w_in(w["w_in"][l]),
        gla_wg=jnp.pad(w["gla_w_gate"][l], ((0, 128 - GLA_RANK), (0, 0))).astype(BF16),
        gla_bg=w["gla_b_gate"][l][None],
        gdn_alog=rep64(w["gdn_a_log"][l]), gdn_dtb=rep64(w["gdn_dt_bias"][l]),
        s5_prompt=_s5_matrices(s5p, S5_CHUNK), s5_step=_s5_matrices(s5p, 1),
        s5_w_glu=w["s5_w_glu"][l].astype(BF16), s5_b_glu=w["s5_b_glu"][l][None],
        gla_norm=tile4(w["gla_norm"][l]), gla_norm_head=w["gla_norm"][l][None],
        gdn_norm=tile4(w["gdn_norm"][l]), gdn_norm_head=w["gdn_norm"][l][None],
        gdn_conv_w=jnp.pad(w["gdn_conv_w"][l], ((0, CONV_PAD - GDN_CONV), (0, 0))),
        moba_q_norm=tile4(w["moba_q_norm"][l]), moba_k_norm=tile4(w["moba_k_norm"][l]),
        w_gate=w["w_gate"][l].astype(BF16),
        w_br=jnp.stack([w["w_br_s5"][l], w["w_br_gla"][l], w["w_br_gdn"][l], w["w_br_moba"][l]]).astype(BF16),
        w_out=w["w_out"][l].astype(BF16), ln2_g=w["ln2_g"][l][None],
        w_ff1=w["w_ff1"][l].astype(BF16), w_ff2=w["w_ff2"][l].astype(BF16),
    )


PROMPT_ROWS = 512


def _prompt_layer(x2d, lp, tabs, batch, seq):
    tm = min(PROMPT_ROWS, batch * seq)
    pr = _inproj(x2d, lp, tm)
    ya, s5_fin = _s5_prompt(pr["s5_u"], lp["s5_prompt"], batch, seq)
    yb, gla_s = _gla_prompt(pr["gla_q"], pr["gla_k"], pr["gla_v"], pr["gla_lr"], pr["gla_r"], lp, batch, seq)
    yc, gdn_s, conv = _gdn_prompt(pr["gdn_qkv"], pr["gdn_b"], pr["gdn_a"], pr["gdn_z"], lp, batch, seq)
    mk, qh, kh, vt, km = _moba_prep(pr["moba_q"], pr["moba_k"], pr["moba_v"], tabs, lp, batch, seq)
    yd = _moba_prompt(qh, kh, vt, km, batch, seq)
    x1 = _merge(x2d, ya, yb, yc, yd, lp, tm)
    x2 = _mlp(x1, lp, tm)
    head4 = lambda a: a.reshape(batch, seq, HEADS, HEAD_DIM)
    s5 = lambda a: a.reshape(batch, S5_GROUPS, S5_P)
    states = (head4(mk), head4(pr["moba_v"]), s5(s5_fin[:, :S5_STATE]), s5(s5_fin[:, S5_STATE:]),
              gla_s, gdn_s, conv)
    return x2, states


def _sample_layer(x2d, lp, tabs, page_table, psums, cache_k, cache_v, layer, st):
    s5_re0, s5_im0, gla0, gdn0, conv0 = st
    n = x2d.shape[0]
    pr = _inproj(x2d, lp, n)
    rows = n * HEADS
    per_head = lambda a: a.reshape(rows, HEAD_DIM)
    flat_state = lambda a: a.reshape(rows, HEAD_DIM * HEAD_DIM)
    rep, til, tilt = _expand_mats()
    sds = jax.ShapeDtypeStruct

    h0 = jnp.concatenate([s5_re0.reshape(n, S5_STATE), s5_im0.reshape(n, S5_STATE)], axis=1)
    ya, s5_new = _s5_step(pr["s5_u"], h0, lp["s5_step"])

    yb, gla_s = _whole_call(
        _gla_step_kernel, [sds((rows, HEAD_DIM), F32), sds((rows, HEAD_DIM * HEAD_DIM), F32)], "gla_step",
        per_head(pr["gla_q"]), per_head(pr["gla_k"]), per_head(pr["gla_v"]), per_head(pr["gla_lr"]),
        per_head(pr["gla_r"]), flat_state(gla0), rep, til, tilt, lp["gla_norm_head"])

    qkv, conv_new = _whole_call(
        _gdn_conv_step_kernel, [sds((n, GDN_QKV), F32), sds((n, (GDN_CONV - 1) * GDN_QKV), F32)],
        "gdn_conv_step", pr["gdn_qkv"], conv0.reshape(n, (GDN_CONV - 1) * GDN_QKV), lp["gdn_conv_w"])
    yc, gdn_s = _whole_call(
        _gdn_step_kernel, [sds((rows, HEAD_DIM), F32), sds((rows, HEAD_DIM * HEAD_DIM), F32)], "gdn_step",
        per_head(qkv[:, :256]), per_head(qkv[:, 256:512]), per_head(qkv[:, 512:]),
        per_head(pr["gdn_b"]), per_head(pr["gdn_a"]), per_head(pr["gdn_z"]), flat_state(gdn0),
        rep, til, tilt, lp["gdn_norm_head"])

    mq, mk, sel = _moba_select(pr["moba_q"], pr["moba_k"], tabs, page_table, psums, layer, lp)
    sel = sel[:, :MOBA_TOPK, ::HEAD_DIM].transpose(0, 2, 1)
    yd = _moba_step(mq, mk, pr["moba_v"], sel, page_table, cache_k, cache_v, layer)

    x1 = _merge(x2d, ya, yb.reshape(n, 256), yc.reshape(n, 256), yd, lp, n)
    x2 = _mlp(x1, lp, n)
    head4 = lambda a: a.reshape(n, 1, HEADS, HEAD_DIM)
    s5 = lambda a: a.reshape(n, S5_GROUPS, S5_P)
    state4 = lambda a: a.reshape(n, HEADS, HEAD_DIM, HEAD_DIM)
    states = (head4(mk), head4(pr["moba_v"]), s5(s5_new[:, :S5_STATE]), s5(s5_new[:, S5_STATE:]),
              state4(gla_s), state4(gdn_s), conv_new.reshape(n, GDN_CONV - 1, GDN_QKV))
    return x2, states


def kernel(x_prompt, x_sample, cache_moba_k, cache_moba_v, page_table, state_s5_re, state_s5_im, state_gla, state_gdn, state_gdn_conv, ln1_g, w_in, s5_a_re, s5_a_im, s5_log_dt, s5_b_re, s5_b_im, s5_c_re, s5_c_im, s5_d, s5_w_glu, s5_b_glu, gla_w_gate, gla_b_gate, gla_norm, gdn_conv_w, gdn_a_log, gdn_dt_bias, gdn_norm, moba_q_norm, moba_k_norm, w_gate, w_br_s5, w_br_gla, w_br_gdn, w_br_moba, w_out, ln2_g, w_ff1, w_ff2):
    weights = dict(ln1_g=ln1_g, w_in=w_in, s5_a_re=s5_a_re, s5_a_im=s5_a_im, s5_log_dt=s5_log_dt,
                   s5_b_re=s5_b_re, s5_b_im=s5_b_im, s5_c_re=s5_c_re, s5_c_im=s5_c_im, s5_d=s5_d,
                   s5_w_glu=s5_w_glu, s5_b_glu=s5_b_glu, gla_w_gate=gla_w_gate, gla_b_gate=gla_b_gate,
                   gla_norm=gla_norm, gdn_conv_w=gdn_conv_w, gdn_a_log=gdn_a_log, gdn_dt_bias=gdn_dt_bias,
                   gdn_norm=gdn_norm, moba_q_norm=moba_q_norm, moba_k_norm=moba_k_norm, w_gate=w_gate,
                   w_br_s5=w_br_s5, w_br_gla=w_br_gla, w_br_gdn=w_br_gdn, w_br_moba=w_br_moba,
                   w_out=w_out, ln2_g=ln2_g, w_ff1=w_ff1, w_ff2=w_ff2)
    depth = ln1_g.shape[0]
    layers = [_layer_params(l, weights) for l in range(depth)]
    batch, seq, _ = x_prompt.shape
    nseq = x_sample.shape[0]
    npages = page_table.shape[1]
    past_len = npages * PAGE_SIZE

    xp = x_prompt.reshape(batch * seq, D_MODEL)
    tabs_p = _rope_tables(jnp.arange(seq, dtype=jnp.int32))
    p_states = []
    for l in range(depth):
        xp, st = _prompt_layer(xp, layers[l], tabs_p, batch, seq)
        p_states.append(st)

    pool = cache_moba_k.shape[1]
    cache_k = cache_moba_k.reshape(depth * pool, PAGE_SIZE, 256)
    cache_v = cache_moba_v.reshape(depth * pool, PAGE_SIZE, 256)
    psums = _page_sums(cache_k)
    tabs_s = _rope_tables(jnp.full((1,), past_len, jnp.int32))
    xs = x_sample.reshape(nseq, D_MODEL)
    s_states = []
    for l in range(depth):
        st0 = (state_s5_re[l], state_s5_im[l], state_gla[l], state_gdn[l], state_gdn_conv[l])
        xs, st = _sample_layer(xs, layers[l], tabs_s, page_table, psums, cache_k, cache_v, l, st0)
        s_states.append(st)

    stack = lambda states: [jnp.stack([s[i] for s in states]) for i in range(len(states[0]))]
    return (xp.reshape(batch, seq, D_MODEL), xs.reshape(nseq, 1, D_MODEL), *stack(p_states), *stack(s_states))
```

```python
import functools
import math

import jax
import jax.numpy as jnp
import numpy as np
from jax import lax
from jax.experimental import pallas as pl
from jax.experimental.pallas import tpu as pltpu

F32 = jnp.float32
BF16 = jnp.bfloat16

D_MODEL = 1024
N_BRANCH = 4
BRANCH_WIDTH = D_MODEL // N_BRANCH
HEADS = 4
HEAD_DIM = BRANCH_WIDTH // HEADS
S5_GROUP = 16
S5_GROUPS = BRANCH_WIDTH // S5_GROUP
S5_P = 64
S5_STATE = S5_GROUPS * S5_P
GLA_RANK = 16
GLA_TAU = 16.0
GDN_CONV = 4
GDN_QKV = 3 * BRANCH_WIDTH
MOBA_BLOCK = 256
MOBA_TOPK = 3
ROT_DIM = HEAD_DIM // 4
ROPE_THETA = 500000.0
PAGE_SIZE = 128
D_FF = 4 * D_MODEL
EPS = 1e-6
NEG = -1e30

LANES = 128
S5_CHUNK = 4
GLA_SUB = 16
GDN_CHUNK = 64
VMEM_LIMIT = 56 * 1024 * 1024

HIGHEST = lax.Precision.HIGHEST


def _cparams(*sem):
    return pltpu.CompilerParams(dimension_semantics=sem, vmem_limit_bytes=VMEM_LIMIT)


def _const_spec(shape):
    zeros = (0,) * len(shape)
    return pl.BlockSpec(shape, lambda *_: zeros)


def _dot(a, b):
    return jnp.dot(a, b, preferred_element_type=F32)


def _dot_nt(a, b):
    return lax.dot_general(a, b, (((1,), (1,)), ((), ())), preferred_element_type=F32)


def _dot_tn(a, b):
    return lax.dot_general(a, b, (((0,), (0,)), ((), ())), preferred_element_type=F32)


def _bdot(a, b):
    return _dot(a.astype(BF16), b.astype(BF16))


def _split2(x):
    hi = x.astype(BF16)
    lo = (x - hi.astype(F32)).astype(BF16)
    return hi, lo


def _dot_sel(x, sel):
    hi, lo = _split2(x)
    return _dot(hi, sel) + _dot(lo, sel)


def _dot_sel_exact(x, sel):
    x1 = x.astype(BF16)
    r1 = x - x1.astype(F32)
    x2 = r1.astype(BF16)
    x3 = (r1 - x2.astype(F32)).astype(BF16)
    return _dot(x1, sel) + _dot(x2, sel) + _dot(x3, sel)


def _dot3(a, b):
    ah, al = _split2(a)
    bh, bl = _split2(b)
    return _dot(ah, bh) + (_dot(ah, bl) + _dot(al, bh))


def _rms_rows(x, g):
    return x * lax.rsqrt(jnp.mean(x * x, axis=-1, keepdims=True) + EPS) * g


def _sigmoid(x):
    return 1.0 / (1.0 + jnp.exp(-x))


def _silu(x):
    return x * _sigmoid(x)


def _softplus(x):
    return jnp.maximum(x, 0.0) + jnp.log1p(jnp.exp(-jnp.abs(x)))


def _head_ones():
    r = np.arange(BRANCH_WIDTH) // HEAD_DIM
    return jnp.asarray(r[:, None] == r[None, :], BF16)


IN_OUTS = (("s5_u", 256), ("gla_q", 256), ("gla_k", 256), ("gla_v", 256), ("gla_r", 256),
           ("gla_lr", 128), ("gdn_qkv", 768), ("gdn_b", 256), ("gdn_a", 256), ("gdn_z", 256),
           ("moba_q", 256), ("moba_k", 256), ("moba_v", 256))
IN_WIDTH = sum(w for _, w in IN_OUTS)


def _regroup_w_in(w_in):
    sizes = (256, 256, 256, 256, GLA_RANK, 256, 256, 256, 256, HEADS, HEADS, 256, 256, 256, 256)
    offs = np.cumsum((0,) + sizes)
    (s5_u, a_q, a_k, a_v, a_lr, a_r, d_q, d_k, d_v, d_b, d_a, d_z, m_q, m_k, m_v) = (
        w_in[:, offs[i]:offs[i + 1]] for i in range(len(sizes)))
    lr = jnp.pad(a_lr, ((0, 0), (0, 128 - GLA_RANK)))
    cols = [s5_u, a_q, a_k, a_v, a_r, lr, d_q, d_k, d_v,
            jnp.repeat(d_b, HEAD_DIM, axis=1), jnp.repeat(d_a, HEAD_DIM, axis=1), d_z, m_q, m_k, m_v]
    return jnp.concatenate(cols, axis=1).astype(BF16)


def _inproj_kernel(x_ref, g_ref, w_ref, wg_ref, bg_ref, alog_ref, dtb_ref, *outs):
    x = x_ref[...]
    hb = _rms_rows(x, g_ref[...]).astype(BF16)
    vals = {}
    off = 0
    for name, n in IN_OUTS:
        vals[name] = _dot(hb, w_ref[:, off:off + n])
        off += n
    z = _bdot(vals["gla_lr"], wg_ref[...]) + bg_ref[...]
    vals["gla_lr"] = -_softplus(-z) * (1.0 / GLA_TAU)
    vals["gdn_b"] = _sigmoid(vals["gdn_b"])
    vals["gdn_a"] = -jnp.exp(alog_ref[...]) * _softplus(vals["gdn_a"] + dtb_ref[...])
    for (name, _), o_ref in zip(IN_OUTS, outs):
        o_ref[...] = vals[name]


def _inproj(x2d, lp, tm):
    n = x2d.shape[0]
    out_shape = []
    out_specs = []
    for name, w in IN_OUTS:
        w_out = 256 if name == "gla_lr" else w
        out_shape.append(jax.ShapeDtypeStruct((n, w_out), F32))
        out_specs.append(pl.BlockSpec((tm, w_out), lambda i: (i, 0)))
    res = pl.pallas_call(
        _inproj_kernel,
        grid=(n // tm,),
        in_specs=[pl.BlockSpec((tm, D_MODEL), lambda i: (i, 0)),
                  _const_spec((1, D_MODEL)), _const_spec((D_MODEL, IN_WIDTH)),
                  _const_spec((128, 256)), _const_spec((1, 256)),
                  _const_spec((1, 256)), _const_spec((1, 256))],
        out_specs=out_specs,
        out_shape=out_shape,
        compiler_params=_cparams("parallel"),
        name="inproj",
    )(x2d, lp["ln1_g"], lp["w_in"], lp["gla_wg"], lp["gla_bg"], lp["gdn_alog"], lp["gdn_dtb"])
    return dict(zip((nm for nm, _ in IN_OUTS), res))


def _merge_kernel(x_ref, ya_ref, yb_ref, yc_ref, yd_ref, g_ref, wgate_ref, wglu_ref, bglu_ref,
                  wbr_ref, wout_ref, o_ref):
    x = x_ref[...]
    hb = _rms_rows(x, g_ref[...]).astype(BF16)
    ya = ya_ref[...]
    ya = ya * _sigmoid(_bdot(ya, wglu_ref[...]) + bglu_ref[...])
    merged = None
    for i, y in enumerate((ya, yb_ref[...], yc_ref[...], yd_ref[...])):
        gate = _sigmoid(_dot(hb, wgate_ref[:, i * D_MODEL:(i + 1) * D_MODEL]))
        term = gate * _dot(y.astype(BF16), wbr_ref[i])
        merged = term if merged is None else merged + term
    o_ref[...] = x + _bdot(merged, wout_ref[...])


def _merge(x2d, ya, yb, yc, yd, lp, tm):
    n = x2d.shape[0]
    row = lambda w: pl.BlockSpec((tm, w), lambda i: (i, 0))
    return pl.pallas_call(
        _merge_kernel,
        grid=(n // tm,),
        in_specs=[row(D_MODEL), row(256), row(256), row(256), row(256),
                  _const_spec((1, D_MODEL)), _const_spec((D_MODEL, N_BRANCH * D_MODEL)),
                  _const_spec((256, 256)), _const_spec((1, 256)),
                  _const_spec((N_BRANCH, 256, D_MODEL)), _const_spec((D_MODEL, D_MODEL))],
        out_specs=row(D_MODEL),
        out_shape=jax.ShapeDtypeStruct((n, D_MODEL), F32),
        compiler_params=_cparams("parallel"),
        name="merge",
    )(x2d, ya, yb, yc, yd, lp["ln1_g"], lp["w_gate"], lp["s5_w_glu"], lp["s5_b_glu"],
      lp["w_br"], lp["w_out"])


def _mlp_kernel(x_ref, g_ref, w1_ref, w2_ref, o_ref):
    x = x_ref[...]
    hb = _rms_rows(x, g_ref[...]).astype(BF16)
    z = jnp.maximum(_dot(hb, w1_ref[...]), 0.0)
    o_ref[...] = x + _bdot(z * z, w2_ref[...])


def _mlp(x2d, lp, tm):
    n = x2d.shape[0]
    row = pl.BlockSpec((tm, D_MODEL), lambda i: (i, 0))
    single = pl.Buffered(1)
    return pl.pallas_call(
        _mlp_kernel,
        grid=(n // tm,),
        in_specs=[row, _const_spec((1, D_MODEL)),
                  pl.BlockSpec((D_MODEL, D_FF), lambda i: (0, 0), pipeline_mode=single),
                  pl.BlockSpec((D_FF, D_MODEL), lambda i: (0, 0), pipeline_mode=single)],
        out_specs=row,
        out_shape=jax.ShapeDtypeStruct((n, D_MODEL), F32),
        compiler_params=_cparams("parallel"),
        name="mlp",
    )(x2d, lp["ln2_g"], lp["w_ff1"], lp["w_ff2"])


def _s5_matrices(p, chunk):
    hp = dict(precision=HIGHEST)
    dt = jnp.exp(p["s5_log_dt"])[:, None]
    ar, ai = p["s5_a_re"], p["s5_a_im"]
    mag = jnp.exp(ar * dt)
    abar_re = mag * jnp.cos(ai * dt)
    abar_im = mag * jnp.sin(ai * dt)
    den = ar * ar + ai * ai
    nr = abar_re - 1.0
    f_re = (nr * ar + abar_im * ai) / den
    f_im = (abar_im * ar - nr * ai) / den
    br, bi = p["s5_b_re"], p["s5_b_im"]
    bbar_re = f_re[..., None] * br - f_im[..., None] * bi
    bbar_im = f_re[..., None] * bi + f_im[..., None] * br
    pw_re = [jnp.ones_like(abar_re)]
    pw_im = [jnp.zeros_like(abar_re)]
    for _ in range(chunk):
        r, i = pw_re[-1], pw_im[-1]
        pw_re.append(r * abar_re - i * abar_im)
        pw_im.append(r * abar_im + i * abar_re)
    pw_re = jnp.stack(pw_re)
    pw_im = jnp.stack(pw_im)
    cr, ci = p["s5_c_re"], p["s5_c_im"]
    ca_re = cr[None] * pw_re[:, :, None, :] - ci[None] * pw_im[:, :, None, :]
    ca_im = cr[None] * pw_im[:, :, None, :] + ci[None] * pw_re[:, :, None, :]
    eye = jnp.eye(S5_GROUPS, dtype=F32)
    kern = (jnp.einsum("tgop,gpi->tgoi", ca_re[:chunk], bbar_re, **hp)
            - jnp.einsum("tgop,gpi->tgoi", ca_im[:chunk], bbar_im, **hp))
    lag = np.zeros((chunk, chunk, chunk), np.float32)
    for s in range(chunk):
        for t in range(s, chunk):
            lag[s, t, t - s] = 1.0
    w_in = jnp.einsum("stk,gh,kgoi->sgitho", jnp.asarray(lag), eye, kern, **hp)
    w_in = w_in.reshape(chunk * BRANCH_WIDTH, chunk * BRANCH_WIDTH)
    rev_re = pw_re[:chunk][::-1]
    rev_im = pw_im[:chunk][::-1]
    ab_re = rev_re[..., None] * bbar_re[None] - rev_im[..., None] * bbar_im[None]
    ab_im = rev_re[..., None] * bbar_im[None] + rev_im[..., None] * bbar_re[None]
    xs_re = jnp.einsum("gh,sgpi->sgihp", eye, ab_re, **hp).reshape(chunk * BRANCH_WIDTH, S5_STATE)
    xs_im = jnp.einsum("gh,sgpi->sgihp", eye, ab_im, **hp).reshape(chunk * BRANCH_WIDTH, S5_STATE)
    w_xs = jnp.concatenate([xs_re, xs_im], axis=1)
    hy_re = jnp.einsum("gh,tgop->gptho", eye, ca_re[1:], **hp).reshape(S5_STATE, chunk * BRANCH_WIDTH)
    hy_im = jnp.einsum("gh,tgop->gptho", eye, ca_im[1:], **hp).reshape(S5_STATE, chunk * BRANCH_WIDTH)
    w_hy = jnp.concatenate([hy_re, -hy_im], axis=0)
    a_pow = jnp.stack([pw_re[chunk].reshape(-1), pw_im[chunk].reshape(-1)])
    d_row = jnp.tile(p["s5_d"], chunk)[None, :]
    return dict(w_xs=w_xs.astype(BF16), w_in=w_in.astype(BF16), w_hy=w_hy.astype(BF16),
                a_pow=a_pow, d_row=d_row)


def _gelu(y):
    c = math.sqrt(2.0 / math.pi)
    return 0.5 * y * (1.0 + jnp.tanh(c * (y + 0.044715 * (y * y * y))))


def _s5_kernel(u_ref, wxs_ref, win_ref, why_ref, apow_ref, d_ref, y_ref, hfin_ref, xs_scr, hs_scr):
    rows = u_ref.shape[0]
    u = u_ref[...]
    ub = u.astype(BF16)
    xs_scr[...] = _dot(ub, wxs_ref[...])
    a_re = apow_ref[0:1, :]
    a_im = apow_ref[1:2, :]

    def step(r, carry):
        hr, hi = carry
        hs_scr[pl.ds(r, 1), :] = jnp.concatenate([hr, hi], axis=-1)
        x = xs_scr[pl.ds(r, 1), :]
        nhr = a_re * hr - a_im * hi + x[:, :S5_STATE]
        nhi = a_re * hi + a_im * hr + x[:, S5_STATE:]
        return nhr, nhi

    zero = jnp.zeros((1, S5_STATE), F32)
    hr, hi = lax.fori_loop(0, rows, step, (zero, zero))
    hfin_ref[0] = jnp.concatenate([hr, hi], axis=-1)
    y = _dot(ub, win_ref[...]) + _dot(hs_scr[...].astype(BF16), why_ref[...]) + d_ref[...] * u
    y_ref[...] = _gelu(y)


def _s5_prompt(u, sm, batch, seq):
    c = S5_CHUNK
    rows = seq // c
    width = c * BRANCH_WIDTH
    u_rows = u.reshape(batch * rows, width)
    single = pl.Buffered(1)
    wspec = lambda shape: pl.BlockSpec(shape, lambda b: (0, 0), pipeline_mode=single)
    y, hfin = pl.pallas_call(
        _s5_kernel,
        grid=(batch,),
        in_specs=[pl.BlockSpec((rows, width), lambda b: (b, 0)),
                  wspec((width, 2 * S5_STATE)), wspec((width, width)), wspec((2 * S5_STATE, width)),
                  _const_spec((2, S5_STATE)), _const_spec((1, width))],
        out_specs=[pl.BlockSpec((rows, width), lambda b: (b, 0)),
                   pl.BlockSpec((1, 1, 2 * S5_STATE), lambda b: (b, 0, 0))],
        out_shape=[jax.ShapeDtypeStruct((batch * rows, width), F32),
                   jax.ShapeDtypeStruct((batch, 1, 2 * S5_STATE), F32)],
        scratch_shapes=[pltpu.VMEM((rows, 2 * S5_STATE), F32), pltpu.VMEM((rows, 2 * S5_STATE), F32)],
        compiler_params=_cparams("parallel"),
        name="s5_prompt",
    )(u_rows, sm["w_xs"], sm["w_in"], sm["w_hy"], sm["a_pow"], sm["d_row"])
    return y.reshape(batch * seq, BRANCH_WIDTH), hfin.reshape(batch, 2 * S5_STATE)


GLA_ROWS = 256


def _gla_kernel(q_ref, k_ref, v_ref, g_ref, r_ref, tri_ref, ones_ref, bmask_ref, gn_ref,
                y_ref, sfin_ref, st_scr, bc_scr, o_scr):
    step = pl.program_id(1)
    sub = GLA_SUB

    @pl.when(step == 0)
    def _():
        st_scr[...] = jnp.zeros_like(st_scr)

    bc_scr[...] = jnp.dot(tri_ref[...], g_ref[...], precision=HIGHEST, preferred_element_type=F32)
    ones = ones_ref[...]
    bmask = bmask_ref[...]
    row = lax.broadcasted_iota(jnp.int32, (sub, BRANCH_WIDTH), 0)

    def body(s, carry):
        r0 = pl.multiple_of(s * sub, sub)
        bc = bc_scr[pl.ds(r0, sub), :]
        q = q_ref[pl.ds(r0, sub), :] * (HEAD_DIM ** -0.5)
        k = k_ref[pl.ds(r0, sub), :]
        v = v_ref[pl.ds(r0, sub), :]
        parts = []
        for j in range(sub):
            e = jnp.exp(jnp.minimum(bc - bc[j:j + 1, :], 0.0))
            parts.append(jnp.where(row >= j, q * k[j:j + 1, :] * e, 0.0))
        att = _dot_sel(jnp.concatenate(parts, axis=0), ones)
        o = att[0:sub] * v[0:1, :]
        for j in range(1, sub):
            o = o + att[j * sub:(j + 1) * sub] * v[j:j + 1, :]
        st = st_scr[...]
        o = o + _dot_nt((q * jnp.exp(bc)).astype(BF16), st.astype(BF16))
        bl = bc[sub - 1:sub, :]
        kt = k * jnp.exp(bl - bc)
        st_scr[...] = st * jnp.exp(bl) + bmask * _dot_tn(v.astype(BF16), kt.astype(BF16))
        o_scr[pl.ds(r0, sub), :] = o
        return carry

    lax.fori_loop(0, q_ref.shape[0] // sub, body, 0)
    o = o_scr[...]
    ms = _dot_sel(o * o, ones) * (1.0 / HEAD_DIM)
    y_ref[...] = o * lax.rsqrt(ms + EPS) * gn_ref[...] * _silu(r_ref[...])

    @pl.when(step == pl.num_programs(1) - 1)
    def _():
        sfin_ref[0] = st_scr[...]


def _sub_tril(rows, sub):
    i = np.arange(rows)
    return jnp.asarray((i[:, None] // sub == i[None, :] // sub) & (i[None, :] <= i[:, None]), F32)


def _unpack_state_t(st):
    b = st.shape[0]
    st = st.reshape(b, HEADS, HEAD_DIM, HEADS, HEAD_DIM)
    diag = jnp.stack([st[:, h, :, h, :] for h in range(HEADS)], axis=1)
    return diag.transpose(0, 1, 3, 2)


def _gla_prompt(q, k, v, g, r, lp, batch, seq):
    rows = min(GLA_ROWS, seq)
    nsteps = seq // rows
    blk = pl.BlockSpec((rows, BRANCH_WIDTH), lambda b, c: (b * nsteps + c, 0))
    hm = np.arange(BRANCH_WIDTH) // HEAD_DIM
    bmask = jnp.asarray(hm[:, None] == hm[None, :], F32)
    y, sfin = pl.pallas_call(
        _gla_kernel,
        grid=(batch, nsteps),
        in_specs=[blk, blk, blk, blk, blk, _const_spec((rows, rows)),
                  _const_spec((BRANCH_WIDTH, BRANCH_WIDTH)), _const_spec((BRANCH_WIDTH, BRANCH_WIDTH)),
                  _const_spec((1, BRANCH_WIDTH))],
        out_specs=[blk, pl.BlockSpec((1, BRANCH_WIDTH, BRANCH_WIDTH), lambda b, c: (b, 0, 0))],
        out_shape=[jax.ShapeDtypeStruct((batch * seq, BRANCH_WIDTH), F32),
                   jax.ShapeDtypeStruct((batch, BRANCH_WIDTH, BRANCH_WIDTH), F32)],
        scratch_shapes=[pltpu.VMEM((BRANCH_WIDTH, BRANCH_WIDTH), F32),
                        pltpu.VMEM((rows, BRANCH_WIDTH), F32), pltpu.VMEM((rows, BRANCH_WIDTH), F32)],
        compiler_params=_cparams("parallel", "arbitrary"),
        name="gla_prompt",
    )(q, k, v, g, r, _sub_tril(rows, GLA_SUB), _head_ones(), bmask, lp["gla_norm"])
    return y, _unpack_state_t(sfin)


CONV_PAD = 8


GDN_ROWS = 256


def _block_diag(x, ones):
    return jnp.concatenate([x] * HEADS, axis=0) * ones


def _unit_lower_inverse(n, eye, ones):
    c = n.shape[0]
    inv = eye - n
    pw = n
    for _ in range(int(math.log2(c)) - 1):
        pb = pw.astype(BF16)
        pw = _dot(pb, _block_diag(pb, ones))
        inv = inv + _dot(inv.astype(BF16), _block_diag(pw.astype(BF16), ones))
    ih, il = _split2(inv)
    nh, nl = _split2(n)
    prod = _dot(jnp.concatenate([nh, nl], axis=0), _block_diag(ih, ones))
    prod = prod[:c] + prod[c:] + _dot(nh, _block_diag(il, ones))
    resid = eye - inv - prod
    return inv + _dot(ih, _block_diag(resid.astype(BF16), ones))


def _gdn_kernel(x_ref, b_ref, g_ref, z_ref, cw_ref, tri_ref, ones_ref, eye_ref, gn_ref,
                y_ref, sfin_ref, conv_ref, s_scr, buf_scr):
    step = pl.program_id(1)
    rows = x_ref.shape[0]
    c = min(GDN_CHUNK, rows)

    @pl.when(step == 0)
    def _():
        s_scr[...] = jnp.zeros_like(s_scr)
        buf_scr[0:CONV_PAD, :] = jnp.zeros((CONV_PAD, GDN_QKV), F32)

    x = x_ref[...]
    buf_scr[CONV_PAD:CONV_PAD + rows, :] = x
    conv = cw_ref[GDN_CONV - 1:GDN_CONV, :] * x
    for w in range(GDN_CONV - 1):
        lag = GDN_CONV - 1 - w
        conv = conv + cw_ref[w:w + 1, :] * buf_scr[CONV_PAD - lag:CONV_PAD - lag + rows, :]
    tail = buf_scr[rows:rows + CONV_PAD, :]
    buf_scr[0:CONV_PAD, :] = tail
    conv_ref[0] = tail
    qkv = _silu(conv)
    ones = ones_ref[...]
    onesf = ones.astype(F32)
    eye = eye_ref[...]
    q = qkv[:, 0:BRANCH_WIDTH]
    k = qkv[:, BRANCH_WIDTH:2 * BRANCH_WIDTH]
    v = qkv[:, 2 * BRANCH_WIDTH:]
    q = q * lax.rsqrt(_dot_sel(q * q, ones) + EPS) * (HEAD_DIM ** -0.5)
    k = k * lax.rsqrt(_dot_sel(k * k, ones) + EPS)
    beta = b_ref[...]
    gc = jnp.dot(tri_ref[...], g_ref[...], precision=HIGHEST, preferred_element_type=F32)
    gam = jnp.exp(gc)
    ri = lax.broadcasted_iota(jnp.int32, (c, BRANCH_WIDTH), 0)
    cj = lax.broadcasted_iota(jnp.int32, (c, BRANCH_WIDTH), 1) % HEAD_DIM

    chunks = []
    for n in range(rows // c):
        rs = slice(n * c, (n + 1) * c)
        gcc, kc, qc, bc = gc[rs], k[rs], q[rs], beta[rs]
        grow = jnp.sum(gcc * eye, axis=0, keepdims=True)
        dec = jnp.where(ri >= cj, jnp.exp(jnp.minimum(gcc - grow, 0.0)), 0.0)
        kq = _dot_nt(jnp.concatenate([kc, qc], axis=0).astype(BF16), _block_diag(kc.astype(BF16), ones))
        inv = _unit_lower_inverse(jnp.where(ri > cj, bc * dec * kq[:c], 0.0), eye, ones)
        invb = inv.astype(BF16)
        glast = gcc[c - 1:c, :]
        chunks.append(dict(
            w=_dot(invb, _block_diag((bc * gam[rs] * kc).astype(BF16), ones)),
            u0=_dot(invb, _block_diag((bc * v[rs]).astype(BF16), ones)),
            aqk=(dec * kq[c:]).astype(BF16), qg=gam[rs] * qc,
            kd=(kc * jnp.exp(glast - gcc)).astype(BF16), decay=jnp.exp(glast)))

    s = s_scr[...]
    outs = []
    for ch in chunks:
        ws = _dot(jnp.concatenate([ch["w"], ch["qg"]], axis=0).astype(BF16), s.astype(BF16))
        ub = (ch["u0"] - ws[:c]).astype(BF16)
        outs.append(ws[c:] + _dot(ch["aqk"], _block_diag(ub, ones)))
        s = ch["decay"] * s + onesf * _dot_tn(ch["kd"], ub)
    s_scr[...] = s
    o = jnp.concatenate(outs, axis=0)
    ms = _dot_sel(o * o, ones) * (1.0 / HEAD_DIM)
    y_ref[...] = o * lax.rsqrt(ms + EPS) * gn_ref[...] * _silu(z_ref[...])

    @pl.when(step == pl.num_programs(1) - 1)
    def _():
        sfin_ref[0] = s


def _unpack_state(st):
    b = st.shape[0]
    st = st.reshape(b, HEADS, HEAD_DIM, HEADS, HEAD_DIM)
    return jnp.stack([st[:, h, :, h, :] for h in range(HEADS)], axis=1)


def _gdn_prompt(x, beta, g, z, lp, batch, seq):
    rows = min(GDN_ROWS, seq)
    c = min(GDN_CHUNK, rows)
    nsteps = seq // rows
    blk = lambda w: pl.BlockSpec((rows, w), lambda b, s: (b * nsteps + s, 0))
    eye = jnp.asarray(np.tile(np.eye(c, dtype=np.float32), (1, HEADS)))
    y, sfin, conv = pl.pallas_call(
        _gdn_kernel,
        grid=(batch, nsteps),
        in_specs=[blk(GDN_QKV), blk(256), blk(256), blk(256), _const_spec((CONV_PAD, GDN_QKV)),
                  _const_spec((rows, rows)), _const_spec((256, 256)), _const_spec((c, 256)),
                  _const_spec((1, 256))],
        out_specs=[blk(256),
                   pl.BlockSpec((1, 256, 256), lambda b, s: (b, 0, 0)),
                   pl.BlockSpec((1, CONV_PAD, GDN_QKV), lambda b, s: (b, 0, 0))],
        out_shape=[jax.ShapeDtypeStruct((batch * seq, 256), F32),
                   jax.ShapeDtypeStruct((batch, 256, 256), F32),
                   jax.ShapeDtypeStruct((batch, CONV_PAD, GDN_QKV), F32)],
        scratch_shapes=[pltpu.VMEM((256, 256), F32),
                        pltpu.VMEM((CONV_PAD + rows, GDN_QKV), F32)],
        compiler_params=_cparams("parallel", "arbitrary"),
        name="gdn_prompt",
    )(x, beta, g, z, lp["gdn_conv_w"], _sub_tril(rows, c), _head_ones(), eye, lp["gdn_norm"])
    return y, _unpack_state(sfin), conv[:, CONV_PAD - (GDN_CONV - 1):, :]


HALF_ROT = ROT_DIM // 2


def _rope_tables(pos):
    inv = ROPE_THETA ** (-jnp.arange(HALF_ROT, dtype=F32) / HALF_ROT)
    ang = pos.astype(F32)[:, None] * inv[None, :]
    cos, sin = jnp.cos(ang), jnp.sin(ang)
    n = pos.shape[0]
    rest = HEAD_DIM - ROT_DIM
    head = lambda a, b, fill: jnp.concatenate([a, b, jnp.full((n, rest), fill, F32)], axis=1)
    zero = jnp.zeros_like(sin)
    tabs = [head(cos, cos, 1.0), head(-sin, zero, 0.0), head(zero, sin, 0.0)]
    return jnp.stack([jnp.tile(t, (1, HEADS)) for t in tabs])


def _qk_norm_rope(x, gain, tab_ref, ones):
    y = x * lax.rsqrt(_dot_sel(x * x, ones) * (1.0 / HEAD_DIM) + EPS) * gain
    up = pltpu.roll(y, BRANCH_WIDTH - HALF_ROT, 1)
    down = pltpu.roll(y, HALF_ROT, 1)
    return y * tab_ref[0] + up * tab_ref[1] + down * tab_ref[2]


def _moba_prep_kernel(q_ref, k_ref, v_ref, tab_ref, qg_ref, kg_ref, ones_ref,
                      mk_ref, qh_ref, kh_ref, vt_ref, km_ref):
    ones = ones_ref[...]
    rows = q_ref.shape[0]
    nblk = rows // MOBA_BLOCK
    mq = _qk_norm_rope(q_ref[...], qg_ref[...], tab_ref, ones) * (HEAD_DIM ** -0.5)
    mk = _qk_norm_rope(k_ref[...], kg_ref[...], tab_ref, ones)
    mk_ref[...] = mk
    kmean = jnp.mean(mk.reshape(nblk, MOBA_BLOCK, BRANCH_WIDTH), axis=1)
    vt = v_ref[...].T
    for h in range(HEADS):
        sl = slice(h * HEAD_DIM, (h + 1) * HEAD_DIM)
        qh_ref[0, h] = mq[:, sl]
        km_ref[0, h] = kmean[:, sl]
        for j in range(nblk):
            rs = slice(j * MOBA_BLOCK, (j + 1) * MOBA_BLOCK)
            kh_ref[0, h, j] = mk[rs, sl]
            vt_ref[0, h, j] = vt[sl, rs]


def _moba_prep(q, k, v, tabs, lp, batch, seq):
    rows = min(8 * MOBA_BLOCK, seq)
    nsteps = seq // rows
    nblk = rows // MOBA_BLOCK
    nb = seq // MOBA_BLOCK
    blk = pl.BlockSpec((rows, 256), lambda b, r: (b * nsteps + r, 0))
    return pl.pallas_call(
        _moba_prep_kernel,
        grid=(batch, nsteps),
        in_specs=[blk, blk, blk, pl.BlockSpec((3, rows, 256), lambda b, r: (0, r, 0)),
                  _const_spec((1, 256)), _const_spec((1, 256)), _const_spec((256, 256))],
        out_specs=[blk,
                   pl.BlockSpec((1, HEADS, rows, HEAD_DIM), lambda b, r: (b, 0, r, 0)),
                   pl.BlockSpec((1, HEADS, nblk, MOBA_BLOCK, HEAD_DIM), lambda b, r: (b, 0, r, 0, 0)),
                   pl.BlockSpec((1, HEADS, nblk, HEAD_DIM, MOBA_BLOCK), lambda b, r: (b, 0, r, 0, 0)),
                   pl.BlockSpec((1, HEADS, nblk, HEAD_DIM), lambda b, r: (b, 0, r, 0))],
        out_shape=[jax.ShapeDtypeStruct((batch * seq, 256), F32),
                   jax.ShapeDtypeStruct((batch, HEADS, seq, HEAD_DIM), F32),
                   jax.ShapeDtypeStruct((batch, HEADS, nb, MOBA_BLOCK, HEAD_DIM), F32),
                   jax.ShapeDtypeStruct((batch, HEADS, nb, HEAD_DIM, MOBA_BLOCK), F32),
                   jax.ShapeDtypeStruct((batch, HEADS, nb, HEAD_DIM), F32)],
        compiler_params=_cparams("parallel", "parallel"),
        name="moba_prep",
    )(q, k, v, tabs, lp["moba_q_norm"], lp["moba_k_norm"], _head_ones())


def _moba_attn_kernel(qh_ref, kh_ref, vt_ref, km_ref, o_ref, sel_scr):
    qb = pl.program_id(1)
    nb = km_ref.shape[2]
    blk = MOBA_BLOCK
    blk_id = lax.broadcasted_iota(jnp.int32, (nb, blk), 0)
    kpos = lax.broadcasted_iota(jnp.int32, (blk, blk), 0)
    qpos = lax.broadcasted_iota(jnp.int32, (blk, blk), 1)
    qs = [qh_ref[0, h].astype(BF16) for h in range(HEADS)]

    def scores(h, j):
        return _dot_nt(kh_ref[0, h, j].astype(BF16), qs[h])

    state = []
    for h in range(HEADS):
        gate = lax.dot_general(km_ref[0, h], qh_ref[0, h], (((1,), (1,)), ((), ())), precision=HIGHEST,
                               preferred_element_type=F32)
        gate = jnp.where(blk_id < qb, gate, NEG)
        taken = jnp.zeros((nb, blk), jnp.bool_)
        for _ in range(min(MOBA_TOPK, nb)):
            best = jnp.max(gate, axis=0, keepdims=True)
            idx = jnp.min(jnp.where(gate == best, blk_id, nb), axis=0, keepdims=True)
            hit = blk_id == idx
            taken = jnp.logical_or(taken, hit)
            gate = jnp.where(hit, -jnp.inf, gate)
        sel_scr[h] = jnp.where(jnp.logical_and(taken, blk_id < qb), 1.0, 0.0)
        s = jnp.where(kpos <= qpos, scores(h, qb), NEG)
        m = jnp.max(s, axis=0, keepdims=True)
        p = jnp.exp(s - m)
        l = jnp.sum(p, axis=0, keepdims=True)
        acc = _dot(vt_ref[0, h, qb].astype(BF16), p.astype(BF16))
        state.append((m, l, acc))

    def body(j, carry):
        new = []
        for h, (m, l, acc) in enumerate(carry):
            s = jnp.where(sel_scr[h, pl.ds(j, 1), :] > 0.0, scores(h, j), NEG)
            m_new = jnp.maximum(m, jnp.max(s, axis=0, keepdims=True))
            alpha = jnp.exp(m - m_new)
            p = jnp.exp(s - m_new)
            l = alpha * l + jnp.sum(p, axis=0, keepdims=True)
            acc = alpha * acc + _dot(vt_ref[0, h, j].astype(BF16), p.astype(BF16))
            new.append((m_new, l, acc))
        return tuple(new)

    state = lax.fori_loop(0, qb, body, tuple(state))
    o_ref[...] = jnp.concatenate([acc / l for (_, l, acc) in state], axis=0).T


def _moba_prompt(qh, kh, vt, km, batch, seq):
    nb = seq // MOBA_BLOCK
    return pl.pallas_call(
        _moba_attn_kernel,
        grid=(batch, nb),
        in_specs=[pl.BlockSpec((1, HEADS, MOBA_BLOCK, HEAD_DIM), lambda b, i: (b, 0, i, 0)),
                  pl.BlockSpec((1, HEADS, nb, MOBA_BLOCK, HEAD_DIM), lambda b, i: (b, 0, 0, 0, 0)),
                  pl.BlockSpec((1, HEADS, nb, HEAD_DIM, MOBA_BLOCK), lambda b, i: (b, 0, 0, 0, 0)),
                  pl.BlockSpec((1, HEADS, nb, HEAD_DIM), lambda b, i: (b, 0, 0, 0))],
        out_specs=pl.BlockSpec((MOBA_BLOCK, 256), lambda b, i: (b * nb + i, 0)),
        out_shape=jax.ShapeDtypeStruct((batch * seq, 256), F32),
        scratch_shapes=[pltpu.VMEM((HEADS, nb, MOBA_BLOCK), F32)],
        compiler_params=_cparams("parallel", "arbitrary"),
        name="moba_prompt",
    )(qh, kh, vt, km)


def _s5_step_kernel(u_ref, h0_ref, wxs_ref, win_ref, why_ref, apow_ref, d_ref, y_ref, h_ref):
    u = u_ref[...]
    ub = u.astype(BF16)
    h0 = h0_ref[...]
    xs = _dot(ub, wxs_ref[...])
    a_re = apow_ref[0:1, :]
    a_im = apow_ref[1:2, :]
    hr0 = h0[:, :S5_STATE]
    hi0 = h0[:, S5_STATE:]
    hr = a_re * hr0 - a_im * hi0 + xs[:, :S5_STATE]
    hi = a_re * hi0 + a_im * hr0 + xs[:, S5_STATE:]
    h_ref[...] = jnp.concatenate([hr, hi], axis=-1)
    y = _dot(ub, win_ref[...]) + _dot(h0.astype(BF16), why_ref[...]) + d_ref[...] * u
    y_ref[...] = _gelu(y)


def _s5_step(u, h0, sm):
    n = u.shape[0]
    return pl.pallas_call(
        _s5_step_kernel,
        out_shape=[jax.ShapeDtypeStruct((n, BRANCH_WIDTH), F32),
                   jax.ShapeDtypeStruct((n, 2 * S5_STATE), F32)],
        compiler_params=pltpu.CompilerParams(vmem_limit_bytes=VMEM_LIMIT),
        name="s5_step",
    )(u, h0, sm["w_xs"], sm["w_in"], sm["w_hy"], sm["a_pow"], sm["d_row"])


def _expand_mats():
    idx = np.arange(HEAD_DIM * HEAD_DIM)
    rep = (np.arange(HEAD_DIM)[:, None] == idx[None, :] // HEAD_DIM)
    til = (np.arange(HEAD_DIM)[:, None] == idx[None, :] % HEAD_DIM)
    return jnp.asarray(rep, BF16), jnp.asarray(til, BF16), jnp.asarray(til.T, BF16)


def _gla_step_kernel(q_ref, k_ref, v_ref, g_ref, r_ref, s0_ref, rep_ref, til_ref, tilt_ref, gn_ref,
                     y_ref, s_ref):
    rep = rep_ref[...]
    eg = _dot_sel_exact(jnp.exp(g_ref[...]), rep)
    kr = _dot_sel_exact(k_ref[...], rep)
    qr = _dot_sel_exact(q_ref[...] * (HEAD_DIM ** -0.5), rep)
    vt = _dot_sel_exact(v_ref[...], til_ref[...])
    s = eg * s0_ref[...] + kr * vt
    s_ref[...] = s
    o = _dot_sel(qr * s, tilt_ref[...])
    y_ref[...] = _rms_rows(o, gn_ref[...]) * _silu(r_ref[...])


def _gdn_conv_step_kernel(x_ref, c0_ref, cw_ref, qkv_ref, cnew_ref):
    x = x_ref[...]
    c0 = c0_ref[...]
    conv = cw_ref[GDN_CONV - 1:GDN_CONV, :] * x
    for w in range(GDN_CONV - 1):
        conv = conv + cw_ref[w:w + 1, :] * c0[:, w * GDN_QKV:(w + 1) * GDN_QKV]
    qkv_ref[...] = _silu(conv)
    cnew_ref[...] = jnp.concatenate([c0[:, GDN_QKV:], x], axis=-1)


def _gdn_step_kernel(q_ref, k_ref, v_ref, b_ref, g_ref, z_ref, s0_ref, rep_ref, til_ref, tilt_ref,
                     gn_ref, y_ref, s_ref):
    q = q_ref[...]
    k = k_ref[...]
    q = q * lax.rsqrt(jnp.sum(q * q, axis=-1, keepdims=True) + EPS) * (HEAD_DIM ** -0.5)
    k = k * lax.rsqrt(jnp.sum(k * k, axis=-1, keepdims=True) + EPS)
    beta = b_ref[:, 0:1]
    gam = jnp.exp(g_ref[:, 0:1])
    rep = rep_ref[...]
    tilt = tilt_ref[...]
    kr = _dot_sel_exact(k, rep)
    qr = _dot_sel_exact(q, rep)
    s0 = s0_ref[...]
    ks = _dot_sel(kr * s0, tilt)
    qs = _dot_sel(qr * s0, tilt)
    u = v_ref[...] - gam * ks
    qk = jnp.sum(q * k, axis=-1, keepdims=True)
    o = gam * qs + (beta * qk) * u
    s_ref[...] = gam * s0 + kr * _dot_sel_exact(beta * u, til_ref[...])
    y_ref[...] = _rms_rows(o, gn_ref[...]) * _silu(z_ref[...])


def _whole_call(kernel, out_shape, name, *args):
    return pl.pallas_call(kernel, out_shape=out_shape, name=name,
                          compiler_params=pltpu.CompilerParams(vmem_limit_bytes=VMEM_LIMIT))(*args)


PAGES_PER_STEP = 16


def _page_sum_kernel(c_ref, o_ref):
    o_ref[0] = jnp.sum(c_ref[0], axis=1)


def _page_sums(cache):
    depth, pool = cache.shape[:2]
    pb = PAGES_PER_STEP
    sums = pl.pallas_call(
        _page_sum_kernel,
        grid=(depth, pool // pb),
        in_specs=[pl.BlockSpec((1, pb, PAGE_SIZE, HEADS, HEAD_DIM), lambda l, i: (l, i, 0, 0, 0))],
        out_specs=pl.BlockSpec((1, pb, HEADS, HEAD_DIM), lambda l, i: (l, i, 0, 0)),
        out_shape=jax.ShapeDtypeStruct((depth, pool, HEADS, HEAD_DIM), F32),
        compiler_params=_cparams("parallel", "parallel"),
        name="page_sums",
    )(cache)
    return sums.reshape(depth * pool, HEADS * HEAD_DIM)


def _moba_select_kernel(pt_ref, q_ref, k_ref, tab_ref, qg_ref, kg_ref, ones_ref, ps_ref,
                        mq_ref, mk_ref, sel_ref, km_scr):
    ones = ones_ref[...]
    mq_ref[...] = _qk_norm_rope(q_ref[...], qg_ref[...], tab_ref, ones) * (HEAD_DIM ** -0.5)
    mk_ref[...] = _qk_norm_rope(k_ref[...], kg_ref[...], tab_ref, ones)
    nseq = q_ref.shape[0]
    nb = km_scr.shape[0]
    pages_per_block = MOBA_BLOCK // PAGE_SIZE
    blk_id = lax.broadcasted_iota(jnp.int32, (nb, 256), 0)

    def per_seq(b, carry):
        def per_block(n, c2):
            acc = ps_ref[pl.ds(pt_ref[b, pages_per_block * n], 1), :]
            for e in range(1, pages_per_block):
                acc = acc + ps_ref[pl.ds(pt_ref[b, pages_per_block * n + e], 1), :]
            km_scr[pl.ds(n, 1), :] = acc * (1.0 / MOBA_BLOCK)
            return c2

        lax.fori_loop(0, nb, per_block, 0)
        gate = _dot_sel(km_scr[...] * mq_ref[pl.ds(b, 1), :], ones)
        rows = []
        for _ in range(MOBA_TOPK):
            best = jnp.max(gate, axis=0, keepdims=True)
            idx = jnp.min(jnp.where(gate == best, blk_id, nb), axis=0, keepdims=True)
            rows.append(idx)
            gate = jnp.where(blk_id == idx, -jnp.inf, gate)
        rows.append(jnp.zeros((8 - MOBA_TOPK, 256), jnp.int32))
        sel_ref[pl.ds(b, 1)] = jnp.concatenate(rows, axis=0)[None]
        return carry

    lax.fori_loop(0, nseq, per_seq, 0)


def _moba_select(q, k, tabs, page_table, psums, pool, layer, lp):
    nseq, npages = page_table.shape
    nb = npages * PAGE_SIZE // MOBA_BLOCK
    full = lambda shape: pl.BlockSpec(shape, lambda i, pt: (0,) * len(shape))
    grid_spec = pltpu.PrefetchScalarGridSpec(
        num_scalar_prefetch=1,
        grid=(1,),
        in_specs=[full((nseq, 256)), full((nseq, 256)), full((3, 1, 256)), full((1, 256)), full((1, 256)),
                  full((256, 256)), pl.BlockSpec((pool, 256), lambda i, pt: (layer, 0))],
        out_specs=[full((nseq, 256)), full((nseq, 256)), full((nseq, 8, 256))],
        scratch_shapes=[pltpu.VMEM((nb, 256), F32)],
    )
    return pl.pallas_call(
        _moba_select_kernel,
        grid_spec=grid_spec,
        out_shape=[jax.ShapeDtypeStruct((nseq, 256), F32), jax.ShapeDtypeStruct((nseq, 256), F32),
                   jax.ShapeDtypeStruct((nseq, 8, 256), jnp.int32)],
        compiler_params=_cparams("arbitrary"),
        name="moba_select",
    )(page_table, q, k, tabs, lp["moba_q_norm"], lp["moba_k_norm"], _head_ones(), psums)


N_SEL_PAGES = MOBA_TOPK * (MOBA_BLOCK // PAGE_SIZE)


def _moba_step_kernel(sel_ref, pt_ref, q_ref, kn_ref, vn_ref, *refs):
    kp = refs[:N_SEL_PAGES]
    vp = refs[N_SEL_PAGES:2 * N_SEL_PAGES]
    o_ref = refs[2 * N_SEL_PAGES]
    h = pl.program_id(1)
    own = lax.broadcasted_iota(jnp.int32, (HEADS, HEAD_DIM), 0) == h
    qm = jnp.where(own, q_ref[0], 0.0)

    def head_dot(a):
        return jnp.sum(jnp.sum(a * qm, axis=-1, keepdims=True), axis=-2, keepdims=True)

    logits = [head_dot(r[0, 0]) for r in kp]
    l_self = head_dot(kn_ref[0])[None]
    m = l_self
    for lg in logits:
        m = jnp.maximum(m, jnp.max(lg, axis=0, keepdims=True))
    p_self = jnp.exp(l_self - m)
    den = p_self
    num = p_self[0] * vn_ref[0]
    for lg, r in zip(logits, vp):
        p = jnp.exp(lg - m)
        den = den + jnp.sum(p, axis=0, keepdims=True)
        num = num + jnp.sum(p * r[0, 0], axis=0)
    o = jnp.where(own, num / den[0], 0.0)

    @pl.when(h == 0)
    def _():
        o_ref[0] = o

    @pl.when(h != 0)
    def _():
        o_ref[0] = o_ref[0] + o


def _moba_step(q, k_new, v_new, sel, page_table, cache_k, cache_v, layer):
    nseq, npages = page_table.shape
    ppb = MOBA_BLOCK // PAGE_SIZE

    def page_spec(r, e):
        def index(b, h, sel_ref, pt_ref):
            blk = sel_ref[(b * HEADS + h) * MOBA_TOPK + r]
            return (layer, pt_ref[b * npages + ppb * blk + e], 0, 0, 0)
        return pl.BlockSpec((1, 1, PAGE_SIZE, HEADS, HEAD_DIM), index)

    row = pl.BlockSpec((1, HEADS, HEAD_DIM), lambda b, h, s, p: (b, 0, 0))
    pages = [page_spec(r, e) for r in range(MOBA_TOPK) for e in range(ppb)]
    grid_spec = pltpu.PrefetchScalarGridSpec(
        num_scalar_prefetch=2,
        grid=(nseq, HEADS),
        in_specs=[row, row, row] + pages + pages,
        out_specs=row,
    )
    r3 = lambda a: a.reshape(nseq, HEADS, HEAD_DIM)
    out = pl.pallas_call(
        _moba_step_kernel,
        grid_spec=grid_spec,
        out_shape=jax.ShapeDtypeStruct((nseq, HEADS, HEAD_DIM), F32),
        compiler_params=_cparams("parallel", "arbitrary"),
        name="moba_step",
    )(sel.reshape(-1), page_table.reshape(-1), r3(q), r3(k_new), r3(v_new),
      *([cache_k] * N_SEL_PAGES), *([cache_v] * N_SEL_PAGES))
    return out.reshape(nseq, 256)


def _layer_params(l, w):
    tile4 = lambda a: jnp.tile(a, HEADS)[None]
    rep64 = lambda a: jnp.repeat(a, HEAD_DIM)[None]
    s5p = {k: w[k][l] for k in ("s5_a_re", "s5_a_im", "s5_log_dt", "s5_b_re", "s5_b_im",
                                "s5_c_re", "s5_c_im", "s5_d")}
    return dict(
        ln1_g=w["ln1_g"][l][None], w_in=_regroup_w_in(w["w_in"][l]),
        gla_wg=jnp.pad(w["gla_w_gate"][l], ((0, 128 - GLA_RANK), (0, 0))).astype(BF16),
        gla_bg=w["gla_b_gate"][l][None],
        gdn_alog=rep64(w["gdn_a_log"][l]), gdn_dtb=rep64(w["gdn_dt_bias"][l]),
        s5_prompt=_s5_matrices(s5p, S5_CHUNK), s5_step=_s5_matrices(s5p, 1),
        s5_w_glu=w["s5_w_glu"][l].astype(BF16), s5_b_glu=w["s5_b_glu"][l][None],
        gla_norm=tile4(w["gla_norm"][l]), gla_norm_head=w["gla_norm"][l][None],
        gdn_norm=tile4(w["gdn_norm"][l]), gdn_norm_head=w["gdn_norm"][l][None],
        gdn_conv_w=jnp.pad(w["gdn_conv_w"][l], ((0, CONV_PAD - GDN_CONV), (0, 0))),
        moba_q_norm=tile4(w["moba_q_norm"][l]), moba_k_norm=tile4(w["moba_k_norm"][l]),
        w_gate=w["w_gate"][l].astype(BF16),
        w_br=jnp.stack([w["w_br_s5"][l], w["w_br_gla"][l], w["w_br_gdn"][l], w["w_br_moba"][l]]).astype(BF16),
        w_out=w["w_out"][l].astype(BF16), ln2_g=w["ln2_g"][l][None],
        w_ff1=w["w_ff1"][l].astype(BF16), w_ff2=w["w_ff2"][l].astype(BF16),
    )


PROMPT_ROWS = 512


def _prompt_layer(x2d, lp, tabs, batch, seq):
    tm = min(PROMPT_ROWS, batch * seq)
    pr = _inproj(x2d, lp, tm)
    ya, s5_fin = _s5_prompt(pr["s5_u"], lp["s5_prompt"], batch, seq)
    yb, gla_s = _gla_prompt(pr["gla_q"], pr["gla_k"], pr["gla_v"], pr["gla_lr"], pr["gla_r"], lp, batch, seq)
    yc, gdn_s, conv = _gdn_prompt(pr["gdn_qkv"], pr["gdn_b"], pr["gdn_a"], pr["gdn_z"], lp, batch, seq)
    mk, qh, kh, vt, km = _moba_prep(pr["moba_q"], pr["moba_k"], pr["moba_v"], tabs, lp, batch, seq)
    yd = _moba_prompt(qh, kh, vt, km, batch, seq)
    x1 = _merge(x2d, ya, yb, yc, yd, lp, tm)
    x2 = _mlp(x1, lp, tm)
    head4 = lambda a: a.reshape(batch, seq, HEADS, HEAD_DIM)
    s5 = lambda a: a.reshape(batch, S5_GROUPS, S5_P)
    states = (head4(mk), head4(pr["moba_v"]), s5(s5_fin[:, :S5_STATE]), s5(s5_fin[:, S5_STATE:]),
              gla_s, gdn_s, conv)
    return x2, states


def _sample_layer(x2d, lp, tabs, page_table, psums, cache_k, cache_v, layer, st):
    s5_re0, s5_im0, gla0, gdn0, conv0 = st
    n = x2d.shape[0]
    pr = _inproj(x2d, lp, n)
    rows = n * HEADS
    per_head = lambda a: a.reshape(rows, HEAD_DIM)
    flat_state = lambda a: a.reshape(rows, HEAD_DIM * HEAD_DIM)
    rep, til, tilt = _expand_mats()
    sds = jax.ShapeDtypeStruct

    h0 = jnp.concatenate([s5_re0.reshape(n, S5_STATE), s5_im0.reshape(n, S5_STATE)], axis=1)
    ya, s5_new = _s5_step(pr["s5_u"], h0, lp["s5_step"])

    yb, gla_s = _whole_call(
        _gla_step_kernel, [sds((rows, HEAD_DIM), F32), sds((rows, HEAD_DIM * HEAD_DIM), F32)], "gla_step",
        per_head(pr["gla_q"]), per_head(pr["gla_k"]), per_head(pr["gla_v"]), per_head(pr["gla_lr"]),
        per_head(pr["gla_r"]), flat_state(gla0), rep, til, tilt, lp["gla_norm_head"])

    qkv, conv_new = _whole_call(
        _gdn_conv_step_kernel, [sds((n, GDN_QKV), F32), sds((n, (GDN_CONV - 1) * GDN_QKV), F32)],
        "gdn_conv_step", pr["gdn_qkv"], conv0.reshape(n, (GDN_CONV - 1) * GDN_QKV), lp["gdn_conv_w"])
    yc, gdn_s = _whole_call(
        _gdn_step_kernel, [sds((rows, HEAD_DIM), F32), sds((rows, HEAD_DIM * HEAD_DIM), F32)], "gdn_step",
        per_head(qkv[:, :256]), per_head(qkv[:, 256:512]), per_head(qkv[:, 512:]),
        per_head(pr["gdn_b"]), per_head(pr["gdn_a"]), per_head(pr["gdn_z"]), flat_state(gdn0),
        rep, til, tilt, lp["gdn_norm_head"])

    mq, mk, sel = _moba_select(pr["moba_q"], pr["moba_k"], tabs, page_table, psums, cache_k.shape[1], layer, lp)
    sel = sel[:, :MOBA_TOPK, ::HEAD_DIM].transpose(0, 2, 1)
    yd = _moba_step(mq, mk, pr["moba_v"], sel, page_table, cache_k, cache_v, layer)

    x1 = _merge(x2d, ya, yb.reshape(n, 256), yc.reshape(n, 256), yd, lp, n)
    x2 = _mlp(x1, lp, n)
    head4 = lambda a: a.reshape(n, 1, HEADS, HEAD_DIM)
    s5 = lambda a: a.reshape(n, S5_GROUPS, S5_P)
    state4 = lambda a: a.reshape(n, HEADS, HEAD_DIM, HEAD_DIM)
    states = (head4(mk), head4(pr["moba_v"]), s5(s5_new[:, :S5_STATE]), s5(s5_new[:, S5_STATE:]),
              state4(gla_s), state4(gdn_s), conv_new.reshape(n, GDN_CONV - 1, GDN_QKV))
    return x2, states


def kernel(x_prompt, x_sample, cache_moba_k, cache_moba_v, page_table, state_s5_re, state_s5_im, state_gla, state_gdn, state_gdn_conv, ln1_g, w_in, s5_a_re, s5_a_im, s5_log_dt, s5_b_re, s5_b_im, s5_c_re, s5_c_im, s5_d, s5_w_glu, s5_b_glu, gla_w_gate, gla_b_gate, gla_norm, gdn_conv_w, gdn_a_log, gdn_dt_bias, gdn_norm, moba_q_norm, moba_k_norm, w_gate, w_br_s5, w_br_gla, w_br_gdn, w_br_moba, w_out, ln2_g, w_ff1, w_ff2):
    weights = dict(ln1_g=ln1_g, w_in=w_in, s5_a_re=s5_a_re, s5_a_im=s5_a_im, s5_log_dt=s5_log_dt,
                   s5_b_re=s5_b_re, s5_b_im=s5_b_im, s5_c_re=s5_c_re, s5_c_im=s5_c_im, s5_d=s5_d,
                   s5_w_glu=s5_w_glu, s5_b_glu=s5_b_glu, gla_w_gate=gla_w_gate, gla_b_gate=gla_b_gate,
                   gla_norm=gla_norm, gdn_conv_w=gdn_conv_w, gdn_a_log=gdn_a_log, gdn_dt_bias=gdn_dt_bias,
                   gdn_norm=gdn_norm, moba_q_norm=moba_q_norm, moba_k_norm=moba_k_norm, w_gate=w_gate,
                   w_br_s5=w_br_s5, w_br_gla=w_br_gla, w_br_gdn=w_br_gdn, w_br_moba=w_br_moba,
                   w_out=w_out, ln2_g=ln2_g, w_ff1=w_ff1, w_ff2=w_ff2)
    depth = ln1_g.shape[0]
    layers = [_layer_params(l, weights) for l in range(depth)]
    batch, seq, _ = x_prompt.shape
    nseq = x_sample.shape[0]
    npages = page_table.shape[1]
    past_len = npages * PAGE_SIZE

    xp = x_prompt.reshape(batch * seq, D_MODEL)
    tabs_p = _rope_tables(jnp.arange(seq, dtype=jnp.int32))
    p_states = []
    for l in range(depth):
        xp, st = _prompt_layer(xp, layers[l], tabs_p, batch, seq)
        p_states.append(st)

    cache_k, cache_v = cache_moba_k, cache_moba_v
    psums = _page_sums(cache_k)
    tabs_s = _rope_tables(jnp.full((1,), past_len, jnp.int32))
    xs = x_sample.reshape(nseq, D_MODEL)
    s_states = []
    for l in range(depth):
        st0 = (state_s5_re[l], state_s5_im[l], state_gla[l], state_gdn[l], state_gdn_conv[l])
        xs, st = _sample_layer(xs, layers[l], tabs_s, page_table, psums, cache_k, cache_v, l, st0)
        s_states.append(st)

    stack = lambda states: [jnp.stack([s[i] for s in states]) for i in range(len(states[0]))]
    return (xp.reshape(batch, seq, D_MODEL), xs.reshape(nseq, 1, D_MODEL), *stack(p_states), *stack(s_states))
```

```python
import functools
import math

import jax
import jax.numpy as jnp
import numpy as np
from jax import lax
from jax.experimental import pallas as pl
from jax.experimental.pallas import tpu as pltpu

F32 = jnp.float32
BF16 = jnp.bfloat16

D_MODEL = 1024
N_BRANCH = 4
BRANCH_WIDTH = D_MODEL // N_BRANCH
HEADS = 4
HEAD_DIM = BRANCH_WIDTH // HEADS
S5_GROUP = 16
S5_GROUPS = BRANCH_WIDTH // S5_GROUP
S5_P = 64
S5_STATE = S5_GROUPS * S5_P
GLA_RANK = 16
GLA_TAU = 16.0
GDN_CONV = 4
GDN_QKV = 3 * BRANCH_WIDTH
MOBA_BLOCK = 256
MOBA_TOPK = 3
ROT_DIM = HEAD_DIM // 4
ROPE_THETA = 500000.0
PAGE_SIZE = 128
D_FF = 4 * D_MODEL
EPS = 1e-6
NEG = -1e30

LANES = 128
S5_CHUNK = 4
GLA_SUB = 16
GDN_CHUNK = 64
VMEM_LIMIT = 56 * 1024 * 1024

HIGHEST = lax.Precision.HIGHEST


def _cparams(*sem):
    return pltpu.CompilerParams(dimension_semantics=sem, vmem_limit_bytes=VMEM_LIMIT)


def _const_spec(shape):
    zeros = (0,) * len(shape)
    return pl.BlockSpec(shape, lambda *_: zeros)


def _dot(a, b):
    return jnp.dot(a, b, preferred_element_type=F32)


def _dot_nt(a, b):
    return lax.dot_general(a, b, (((1,), (1,)), ((), ())), preferred_element_type=F32)


def _dot_tn(a, b):
    return lax.dot_general(a, b, (((0,), (0,)), ((), ())), preferred_element_type=F32)


def _bdot(a, b):
    return _dot(a.astype(BF16), b.astype(BF16))


def _split2(x):
    hi = x.astype(BF16)
    lo = (x - hi.astype(F32)).astype(BF16)
    return hi, lo


def _dot_sel(x, sel):
    hi, lo = _split2(x)
    return _dot(hi, sel) + _dot(lo, sel)


def _dot_sel_exact(x, sel):
    x1 = x.astype(BF16)
    r1 = x - x1.astype(F32)
    x2 = r1.astype(BF16)
    x3 = (r1 - x2.astype(F32)).astype(BF16)
    return _dot(x1, sel) + _dot(x2, sel) + _dot(x3, sel)


def _dot3(a, b):
    ah, al = _split2(a)
    bh, bl = _split2(b)
    return _dot(ah, bh) + (_dot(ah, bl) + _dot(al, bh))


def _rms_rows(x, g):
    return x * lax.rsqrt(jnp.mean(x * x, axis=-1, keepdims=True) + EPS) * g


def _sigmoid(x):
    return 1.0 / (1.0 + jnp.exp(-x))


def _silu(x):
    return x * _sigmoid(x)


def _softplus(x):
    return jnp.maximum(x, 0.0) + jnp.log1p(jnp.exp(-jnp.abs(x)))


def _head_ones():
    r = np.arange(BRANCH_WIDTH) // HEAD_DIM
    return jnp.asarray(r[:, None] == r[None, :], BF16)


IN_OUTS = (("s5_u", 256), ("gla_q", 256), ("gla_k", 256), ("gla_v", 256), ("gla_r", 256),
           ("gla_lr", 128), ("gdn_qkv", 768), ("gdn_b", 256), ("gdn_a", 256), ("gdn_z", 256),
           ("moba_q", 256), ("moba_k", 256), ("moba_v", 256))
IN_WIDTH = sum(w for _, w in IN_OUTS)


def _regroup_w_in(w_in):
    sizes = (256, 256, 256, 256, GLA_RANK, 256, 256, 256, 256, HEADS, HEADS, 256, 256, 256, 256)
    offs = np.cumsum((0,) + sizes)
    (s5_u, a_q, a_k, a_v, a_lr, a_r, d_q, d_k, d_v, d_b, d_a, d_z, m_q, m_k, m_v) = (
        w_in[:, offs[i]:offs[i + 1]] for i in range(len(sizes)))
    lr = jnp.pad(a_lr, ((0, 0), (0, 128 - GLA_RANK)))
    cols = [s5_u, a_q, a_k, a_v, a_r, lr, d_q, d_k, d_v,
            jnp.repeat(d_b, HEAD_DIM, axis=1), jnp.repeat(d_a, HEAD_DIM, axis=1), d_z, m_q, m_k, m_v]
    return jnp.concatenate(cols, axis=1).astype(BF16)


def _inproj_kernel(x_ref, g_ref, w_ref, wg_ref, bg_ref, alog_ref, dtb_ref, *outs):
    x = x_ref[...]
    hb = _rms_rows(x, g_ref[...]).astype(BF16)
    vals = {}
    off = 0
    for name, n in IN_OUTS:
        vals[name] = _dot(hb, w_ref[:, off:off + n])
        off += n
    z = _bdot(vals["gla_lr"], wg_ref[...]) + bg_ref[...]
    vals["gla_lr"] = -_softplus(-z) * (1.0 / GLA_TAU)
    vals["gdn_b"] = _sigmoid(vals["gdn_b"])
    vals["gdn_a"] = -jnp.exp(alog_ref[...]) * _softplus(vals["gdn_a"] + dtb_ref[...])
    for (name, _), o_ref in zip(IN_OUTS, outs):
        o_ref[...] = vals[name]


def _inproj(x2d, lp, tm):
    n = x2d.shape[0]
    out_shape = []
    out_specs = []
    for name, w in IN_OUTS:
        w_out = 256 if name == "gla_lr" else w
        out_shape.append(jax.ShapeDtypeStruct((n, w_out), F32))
        out_specs.append(pl.BlockSpec((tm, w_out), lambda i: (i, 0)))
    res = pl.pallas_call(
        _inproj_kernel,
        grid=(n // tm,),
        in_specs=[pl.BlockSpec((tm, D_MODEL), lambda i: (i, 0)),
                  _const_spec((1, D_MODEL)), _const_spec((D_MODEL, IN_WIDTH)),
                  _const_spec((128, 256)), _const_spec((1, 256)),
                  _const_spec((1, 256)), _const_spec((1, 256))],
        out_specs=out_specs,
        out_shape=out_shape,
        compiler_params=_cparams("parallel"),
        name="inproj",
    )(x2d, lp["ln1_g"], lp["w_in"], lp["gla_wg"], lp["gla_bg"], lp["gdn_alog"], lp["gdn_dtb"])
    return dict(zip((nm for nm, _ in IN_OUTS), res))


def _merge_kernel(x_ref, ya_ref, yb_ref, yc_ref, yd_ref, g_ref, wgate_ref, wglu_ref, bglu_ref,
                  wbr_ref, wout_ref, o_ref):
    x = x_ref[...]
    hb = _rms_rows(x, g_ref[...]).astype(BF16)
    ya = ya_ref[...]
    ya = ya * _sigmoid(_bdot(ya, wglu_ref[...]) + bglu_ref[...])
    merged = None
    for i, y in enumerate((ya, yb_ref[...], yc_ref[...], yd_ref[...])):
        gate = _sigmoid(_dot(hb, wgate_ref[:, i * D_MODEL:(i + 1) * D_MODEL]))
        term = gate * _dot(y.astype(BF16), wbr_ref[i])
        merged = term if merged is None else merged + term
    o_ref[...] = x + _bdot(merged, wout_ref[...])


def _merge(x2d, ya, yb, yc, yd, lp, tm):
    n = x2d.shape[0]
    row = lambda w: pl.BlockSpec((tm, w), lambda i: (i, 0))
    return pl.pallas_call(
        _merge_kernel,
        grid=(n // tm,),
        in_specs=[row(D_MODEL), row(256), row(256), row(256), row(256),
                  _const_spec((1, D_MODEL)), _const_spec((D_MODEL, N_BRANCH * D_MODEL)),
                  _const_spec((256, 256)), _const_spec((1, 256)),
                  _const_spec((N_BRANCH, 256, D_MODEL)), _const_spec((D_MODEL, D_MODEL))],
        out_specs=row(D_MODEL),
        out_shape=jax.ShapeDtypeStruct((n, D_MODEL), F32),
        compiler_params=_cparams("parallel"),
        name="merge",
    )(x2d, ya, yb, yc, yd, lp["ln1_g"], lp["w_gate"], lp["s5_w_glu"], lp["s5_b_glu"],
      lp["w_br"], lp["w_out"])


def _mlp_kernel(x_ref, g_ref, w1_ref, w2_ref, o_ref):
    x = x_ref[...]
    hb = _rms_rows(x, g_ref[...]).astype(BF16)
    z = jnp.maximum(_dot(hb, w1_ref[...]), 0.0)
    o_ref[...] = x + _bdot(z * z, w2_ref[...])


def _mlp(x2d, lp, tm):
    n = x2d.shape[0]
    row = pl.BlockSpec((tm, D_MODEL), lambda i: (i, 0))
    single = pl.Buffered(1)
    return pl.pallas_call(
        _mlp_kernel,
        grid=(n // tm,),
        in_specs=[row, _const_spec((1, D_MODEL)),
                  pl.BlockSpec((D_MODEL, D_FF), lambda i: (0, 0), pipeline_mode=single),
                  pl.BlockSpec((D_FF, D_MODEL), lambda i: (0, 0), pipeline_mode=single)],
        out_specs=row,
        out_shape=jax.ShapeDtypeStruct((n, D_MODEL), F32),
        compiler_params=_cparams("parallel"),
        name="mlp",
    )(x2d, lp["ln2_g"], lp["w_ff1"], lp["w_ff2"])


def _s5_matrices(p, chunk):
    hp = dict(precision=HIGHEST)
    dt = jnp.exp(p["s5_log_dt"])[:, None]
    ar, ai = p["s5_a_re"], p["s5_a_im"]
    mag = jnp.exp(ar * dt)
    abar_re = mag * jnp.cos(ai * dt)
    abar_im = mag * jnp.sin(ai * dt)
    den = ar * ar + ai * ai
    nr = abar_re - 1.0
    f_re = (nr * ar + abar_im * ai) / den
    f_im = (abar_im * ar - nr * ai) / den
    br, bi = p["s5_b_re"], p["s5_b_im"]
    bbar_re = f_re[..., None] * br - f_im[..., None] * bi
    bbar_im = f_re[..., None] * bi + f_im[..., None] * br
    pw_re = [jnp.ones_like(abar_re)]
    pw_im = [jnp.zeros_like(abar_re)]
    for _ in range(chunk):
        r, i = pw_re[-1], pw_im[-1]
        pw_re.append(r * abar_re - i * abar_im)
        pw_im.append(r * abar_im + i * abar_re)
    pw_re = jnp.stack(pw_re)
    pw_im = jnp.stack(pw_im)
    cr, ci = p["s5_c_re"], p["s5_c_im"]
    ca_re = cr[None] * pw_re[:, :, None, :] - ci[None] * pw_im[:, :, None, :]
    ca_im = cr[None] * pw_im[:, :, None, :] + ci[None] * pw_re[:, :, None, :]
    eye = jnp.eye(S5_GROUPS, dtype=F32)
    kern = (jnp.einsum("tgop,gpi->tgoi", ca_re[:chunk], bbar_re, **hp)
            - jnp.einsum("tgop,gpi->tgoi", ca_im[:chunk], bbar_im, **hp))
    lag = np.zeros((chunk, chunk, chunk), np.float32)
    for s in range(chunk):
        for t in range(s, chunk):
            lag[s, t, t - s] = 1.0
    w_in = jnp.einsum("stk,gh,kgoi->sgitho", jnp.asarray(lag), eye, kern, **hp)
    w_in = w_in.reshape(chunk * BRANCH_WIDTH, chunk * BRANCH_WIDTH)
    rev_re = pw_re[:chunk][::-1]
    rev_im = pw_im[:chunk][::-1]
    ab_re = rev_re[..., None] * bbar_re[None] - rev_im[..., None] * bbar_im[None]
    ab_im = rev_re[..., None] * bbar_im[None] + rev_im[..., None] * bbar_re[None]
    xs_re = jnp.einsum("gh,sgpi->sgihp", eye, ab_re, **hp).reshape(chunk * BRANCH_WIDTH, S5_STATE)
    xs_im = jnp.einsum("gh,sgpi->sgihp", eye, ab_im, **hp).reshape(chunk * BRANCH_WIDTH, S5_STATE)
    w_xs = jnp.concatenate([xs_re, xs_im], axis=1)
    hy_re = jnp.einsum("gh,tgop->gptho", eye, ca_re[1:], **hp).reshape(S5_STATE, chunk * BRANCH_WIDTH)
    hy_im = jnp.einsum("gh,tgop->gptho", eye, ca_im[1:], **hp).reshape(S5_STATE, chunk * BRANCH_WIDTH)
    w_hy = jnp.concatenate([hy_re, -hy_im], axis=0)
    a_pow = jnp.stack([pw_re[chunk].reshape(-1), pw_im[chunk].reshape(-1)])
    d_row = jnp.tile(p["s5_d"], chunk)[None, :]
    return dict(w_xs=w_xs.astype(BF16), w_in=w_in.astype(BF16), w_hy=w_hy.astype(BF16),
                a_pow=a_pow, d_row=d_row)


def _gelu(y):
    c = math.sqrt(2.0 / math.pi)
    return 0.5 * y * (1.0 + jnp.tanh(c * (y + 0.044715 * (y * y * y))))


def _s5_kernel(u_ref, wxs_ref, win_ref, why_ref, apow_ref, d_ref, y_ref, hfin_ref, xs_scr, hs_scr):
    rows = u_ref.shape[0]
    u = u_ref[...]
    ub = u.astype(BF16)
    xs_scr[...] = _dot(ub, wxs_ref[...])
    a_re = apow_ref[0:1, :]
    a_im = apow_ref[1:2, :]

    def step(r, carry):
        hr, hi = carry
        hs_scr[pl.ds(r, 1), :] = jnp.concatenate([hr, hi], axis=-1)
        x = xs_scr[pl.ds(r, 1), :]
        nhr = a_re * hr - a_im * hi + x[:, :S5_STATE]
        nhi = a_re * hi + a_im * hr + x[:, S5_STATE:]
        return nhr, nhi

    zero = jnp.zeros((1, S5_STATE), F32)
    hr, hi = lax.fori_loop(0, rows, step, (zero, zero))
    hfin_ref[0] = jnp.concatenate([hr, hi], axis=-1)
    y = _dot(ub, win_ref[...]) + _dot(hs_scr[...].astype(BF16), why_ref[...]) + d_ref[...] * u
    y_ref[...] = _gelu(y)


def _s5_prompt(u, sm, batch, seq):
    c = S5_CHUNK
    rows = seq // c
    width = c * BRANCH_WIDTH
    u_rows = u.reshape(batch * rows, width)
    single = pl.Buffered(1)
    wspec = lambda shape: pl.BlockSpec(shape, lambda b: (0, 0), pipeline_mode=single)
    y, hfin = pl.pallas_call(
        _s5_kernel,
        grid=(batch,),
        in_specs=[pl.BlockSpec((rows, width), lambda b: (b, 0)),
                  wspec((width, 2 * S5_STATE)), wspec((width, width)), wspec((2 * S5_STATE, width)),
                  _const_spec((2, S5_STATE)), _const_spec((1, width))],
        out_specs=[pl.BlockSpec((rows, width), lambda b: (b, 0)),
                   pl.BlockSpec((1, 1, 2 * S5_STATE), lambda b: (b, 0, 0))],
        out_shape=[jax.ShapeDtypeStruct((batch * rows, width), F32),
                   jax.ShapeDtypeStruct((batch, 1, 2 * S5_STATE), F32)],
        scratch_shapes=[pltpu.VMEM((rows, 2 * S5_STATE), F32), pltpu.VMEM((rows, 2 * S5_STATE), F32)],
        compiler_params=_cparams("parallel"),
        name="s5_prompt",
    )(u_rows, sm["w_xs"], sm["w_in"], sm["w_hy"], sm["a_pow"], sm["d_row"])
    return y.reshape(batch * seq, BRANCH_WIDTH), hfin.reshape(batch, 2 * S5_STATE)


GLA_ROWS = 256


def _gla_kernel(q_ref, k_ref, v_ref, g_ref, r_ref, tri_ref, ones_ref, bmask_ref, gn_ref,
                y_ref, sfin_ref, st_scr, bc_scr, o_scr):
    step = pl.program_id(1)
    sub = GLA_SUB

    @pl.when(step == 0)
    def _():
        st_scr[...] = jnp.zeros_like(st_scr)

    bc_scr[...] = jnp.dot(tri_ref[...], g_ref[...], precision=HIGHEST, preferred_element_type=F32)
    ones = ones_ref[...]
    bmask = bmask_ref[...]
    row = lax.broadcasted_iota(jnp.int32, (sub, BRANCH_WIDTH), 0)

    def body(s, carry):
        r0 = pl.multiple_of(s * sub, sub)
        bc = bc_scr[pl.ds(r0, sub), :]
        q = q_ref[pl.ds(r0, sub), :] * (HEAD_DIM ** -0.5)
        k = k_ref[pl.ds(r0, sub), :]
        v = v_ref[pl.ds(r0, sub), :]
        parts = []
        for j in range(sub):
            e = jnp.exp(jnp.minimum(bc - bc[j:j + 1, :], 0.0))
            parts.append(jnp.where(row >= j, q * k[j:j + 1, :] * e, 0.0))
        att = _dot_sel(jnp.concatenate(parts, axis=0), ones)
        o = att[0:sub] * v[0:1, :]
        for j in range(1, sub):
            o = o + att[j * sub:(j + 1) * sub] * v[j:j + 1, :]
        st = st_scr[...]
        o = o + _dot_nt((q * jnp.exp(bc)).astype(BF16), st.astype(BF16))
        bl = bc[sub - 1:sub, :]
        kt = k * jnp.exp(bl - bc)
        st_scr[...] = st * jnp.exp(bl) + bmask * _dot_tn(v.astype(BF16), kt.astype(BF16))
        o_scr[pl.ds(r0, sub), :] = o
        return carry

    lax.fori_loop(0, q_ref.shape[0] // sub, body, 0)
    o = o_scr[...]
    ms = _dot_sel(o * o, ones) * (1.0 / HEAD_DIM)
    y_ref[...] = o * lax.rsqrt(ms + EPS) * gn_ref[...] * _silu(r_ref[...])

    @pl.when(step == pl.num_programs(1) - 1)
    def _():
        sfin_ref[0] = st_scr[...]


def _sub_tril(rows, sub):
    i = np.arange(rows)
    return jnp.asarray((i[:, None] // sub == i[None, :] // sub) & (i[None, :] <= i[:, None]), F32)


def _unpack_state_t(st):
    b = st.shape[0]
    st = st.reshape(b, HEADS, HEAD_DIM, HEADS, HEAD_DIM)
    diag = jnp.stack([st[:, h, :, h, :] for h in range(HEADS)], axis=1)
    return diag.transpose(0, 1, 3, 2)


def _gla_prompt(q, k, v, g, r, lp, batch, seq):
    rows = min(GLA_ROWS, seq)
    nsteps = seq // rows
    blk = pl.BlockSpec((rows, BRANCH_WIDTH), lambda b, c: (b * nsteps + c, 0))
    hm = np.arange(BRANCH_WIDTH) // HEAD_DIM
    bmask = jnp.asarray(hm[:, None] == hm[None, :], F32)
    y, sfin = pl.pallas_call(
        _gla_kernel,
        grid=(batch, nsteps),
        in_specs=[blk, blk, blk, blk, blk, _const_spec((rows, rows)),
                  _const_spec((BRANCH_WIDTH, BRANCH_WIDTH)), _const_spec((BRANCH_WIDTH, BRANCH_WIDTH)),
                  _const_spec((1, BRANCH_WIDTH))],
        out_specs=[blk, pl.BlockSpec((1, BRANCH_WIDTH, BRANCH_WIDTH), lambda b, c: (b, 0, 0))],
        out_shape=[jax.ShapeDtypeStruct((batch * seq, BRANCH_WIDTH), F32),
                   jax.ShapeDtypeStruct((batch, BRANCH_WIDTH, BRANCH_WIDTH), F32)],
        scratch_shapes=[pltpu.VMEM((BRANCH_WIDTH, BRANCH_WIDTH), F32),
                        pltpu.VMEM((rows, BRANCH_WIDTH), F32), pltpu.VMEM((rows, BRANCH_WIDTH), F32)],
        compiler_params=_cparams("parallel", "arbitrary"),
        name="gla_prompt",
    )(q, k, v, g, r, _sub_tril(rows, GLA_SUB), _head_ones(), bmask, lp["gla_norm"])
    return y, _unpack_state_t(sfin)


CONV_PAD = 8


GDN_ROWS = 512


def _block_diag(x, ones):
    return jnp.concatenate([x] * HEADS, axis=0) * ones


def _unit_lower_inverses(ns, eye, ones):
    c = ns[0].shape[0]
    every = range(len(ns))
    invs = [eye - n for n in ns]
    pws = list(ns)
    for _ in range(int(math.log2(c)) - 1):
        pbs = [pw.astype(BF16) for pw in pws]
        pws = [_dot(pb, _block_diag(pb, ones)) for pb in pbs]
        invs = [invs[i] + _dot(invs[i].astype(BF16), _block_diag(pws[i].astype(BF16), ones)) for i in every]
    inv_parts = [_split2(inv) for inv in invs]
    n_parts = [_split2(n) for n in ns]
    prods = [_dot(jnp.concatenate(n_parts[i], axis=0), _block_diag(inv_parts[i][0], ones)) for i in every]
    cross = [_dot(n_parts[i][0], _block_diag(inv_parts[i][1], ones)) for i in every]
    resids = [eye - invs[i] - (prods[i][:c] + prods[i][c:] + cross[i]) for i in every]
    return [invs[i] + _dot(inv_parts[i][0], _block_diag(resids[i].astype(BF16), ones)) for i in every]


def _gdn_kernel(x_ref, b_ref, g_ref, z_ref, cw_ref, tri_ref, ones_ref, eye_ref, gn_ref,
                y_ref, sfin_ref, conv_ref, s_scr, buf_scr):
    step = pl.program_id(1)
    rows = x_ref.shape[0]
    c = min(GDN_CHUNK, rows)

    @pl.when(step == 0)
    def _():
        s_scr[...] = jnp.zeros_like(s_scr)
        buf_scr[0:CONV_PAD, :] = jnp.zeros((CONV_PAD, GDN_QKV), F32)

    x = x_ref[...]
    buf_scr[CONV_PAD:CONV_PAD + rows, :] = x
    conv = cw_ref[GDN_CONV - 1:GDN_CONV, :] * x
    for w in range(GDN_CONV - 1):
        lag = GDN_CONV - 1 - w
        conv = conv + cw_ref[w:w + 1, :] * buf_scr[CONV_PAD - lag:CONV_PAD - lag + rows, :]
    tail = buf_scr[rows:rows + CONV_PAD, :]
    buf_scr[0:CONV_PAD, :] = tail
    conv_ref[0] = tail
    qkv = _silu(conv)
    ones = ones_ref[...]
    onesf = ones.astype(F32)
    eye = eye_ref[...]
    q = qkv[:, 0:BRANCH_WIDTH]
    k = qkv[:, BRANCH_WIDTH:2 * BRANCH_WIDTH]
    v = qkv[:, 2 * BRANCH_WIDTH:]
    q = q * lax.rsqrt(_dot_sel(q * q, ones) + EPS) * (HEAD_DIM ** -0.5)
    k = k * lax.rsqrt(_dot_sel(k * k, ones) + EPS)
    beta = b_ref[...]
    gc = jnp.dot(tri_ref[...], g_ref[...], precision=HIGHEST, preferred_element_type=F32)
    gam = jnp.exp(gc)
    ri = lax.broadcasted_iota(jnp.int32, (c, BRANCH_WIDTH), 0)
    cj = lax.broadcasted_iota(jnp.int32, (c, BRANCH_WIDTH), 1) % HEAD_DIM

    every = range(rows // c)
    sls = [slice(n * c, (n + 1) * c) for n in every]
    kcs = [k[rs].astype(BF16) for rs in sls]
    kqs = [_dot_nt(jnp.concatenate([k[rs], q[rs]], axis=0).astype(BF16), _block_diag(kcs[n], ones))
           for n, rs in enumerate(sls)]
    decs = []
    for rs in sls:
        grow = jnp.sum(gc[rs] * eye, axis=0, keepdims=True)
        decs.append(jnp.where(ri >= cj, jnp.exp(jnp.minimum(gc[rs] - grow, 0.0)), 0.0))
    invs = _unit_lower_inverses(
        [jnp.where(ri > cj, beta[rs] * decs[n] * kqs[n][:c], 0.0) for n, rs in enumerate(sls)], eye, ones)
    invbs = [inv.astype(BF16) for inv in invs]
    ws_m = [_dot(invbs[n], _block_diag((beta[rs] * gam[rs] * k[rs]).astype(BF16), ones))
            for n, rs in enumerate(sls)]
    u0s = [_dot(invbs[n], _block_diag((beta[rs] * v[rs]).astype(BF16), ones)) for n, rs in enumerate(sls)]
    aqks = [(decs[n] * kqs[n][c:]).astype(BF16) for n in every]
    lhs = [jnp.concatenate([ws_m[n], gam[rs] * q[rs]], axis=0).astype(BF16) for n, rs in enumerate(sls)]
    glasts = [gc[rs][c - 1:c, :] for rs in sls]
    kds = [(k[rs] * jnp.exp(glasts[n] - gc[rs])).astype(BF16) for n, rs in enumerate(sls)]

    s = s_scr[...]
    outs = []
    for n in every:
        ws = _dot(lhs[n], s.astype(BF16))
        ub = (u0s[n] - ws[:c]).astype(BF16)
        outs.append(ws[c:] + _dot(aqks[n], _block_diag(ub, ones)))
        s = jnp.exp(glasts[n]) * s + onesf * _dot_tn(kds[n], ub)
    s_scr[...] = s
    o = jnp.concatenate(outs, axis=0)
    ms = _dot_sel(o * o, ones) * (1.0 / HEAD_DIM)
    y_ref[...] = o * lax.rsqrt(ms + EPS) * gn_ref[...] * _silu(z_ref[...])

    @pl.when(step == pl.num_programs(1) - 1)
    def _():
        sfin_ref[0] = s


def _unpack_state(st):
    b = st.shape[0]
    st = st.reshape(b, HEADS, HEAD_DIM, HEADS, HEAD_DIM)
    return jnp.stack([st[:, h, :, h, :] for h in range(HEADS)], axis=1)


def _gdn_prompt(x, beta, g, z, lp, batch, seq):
    rows = min(GDN_ROWS, seq)
    c = min(GDN_CHUNK, rows)
    nsteps = seq // rows
    blk = lambda w: pl.BlockSpec((rows, w), lambda b, s: (b * nsteps + s, 0))
    eye = jnp.asarray(np.tile(np.eye(c, dtype=np.float32), (1, HEADS)))
    y, sfin, conv = pl.pallas_call(
        _gdn_kernel,
        grid=(batch, nsteps),
        in_specs=[blk(GDN_QKV), blk(256), blk(256), blk(256), _const_spec((CONV_PAD, GDN_QKV)),
                  _const_spec((rows, rows)), _const_spec((256, 256)), _const_spec((c, 256)),
                  _const_spec((1, 256))],
        out_specs=[blk(256),
                   pl.BlockSpec((1, 256, 256), lambda b, s: (b, 0, 0)),
                   pl.BlockSpec((1, CONV_PAD, GDN_QKV), lambda b, s: (b, 0, 0))],
        out_shape=[jax.ShapeDtypeStruct((batch * seq, 256), F32),
                   jax.ShapeDtypeStruct((batch, 256, 256), F32),
                   jax.ShapeDtypeStruct((batch, CONV_PAD, GDN_QKV), F32)],
        scratch_shapes=[pltpu.VMEM((256, 256), F32),
                        pltpu.VMEM((CONV_PAD + rows, GDN_QKV), F32)],
        compiler_params=_cparams("parallel", "arbitrary"),
        name="gdn_prompt",
    )(x, beta, g, z, lp["gdn_conv_w"], _sub_tril(rows, c), _head_ones(), eye, lp["gdn_norm"])
    return y, _unpack_state(sfin), conv[:, CONV_PAD - (GDN_CONV - 1):, :]


HALF_ROT = ROT_DIM // 2


def _rope_tables(pos):
    inv = ROPE_THETA ** (-jnp.arange(HALF_ROT, dtype=F32) / HALF_ROT)
    ang = pos.astype(F32)[:, None] * inv[None, :]
    cos, sin = jnp.cos(ang), jnp.sin(ang)
    n = pos.shape[0]
    rest = HEAD_DIM - ROT_DIM
    head = lambda a, b, fill: jnp.concatenate([a, b, jnp.full((n, rest), fill, F32)], axis=1)
    zero = jnp.zeros_like(sin)
    tabs = [head(cos, cos, 1.0), head(-sin, zero, 0.0), head(zero, sin, 0.0)]
    return jnp.stack([jnp.tile(t, (1, HEADS)) for t in tabs])


def _qk_norm_rope(x, gain, tab_ref, ones):
    y = x * lax.rsqrt(_dot_sel(x * x, ones) * (1.0 / HEAD_DIM) + EPS) * gain
    up = pltpu.roll(y, BRANCH_WIDTH - HALF_ROT, 1)
    down = pltpu.roll(y, HALF_ROT, 1)
    return y * tab_ref[0] + up * tab_ref[1] + down * tab_ref[2]


def _moba_prep_kernel(q_ref, k_ref, v_ref, tab_ref, qg_ref, kg_ref, ones_ref,
                      mk_ref, qh_ref, kh_ref, vt_ref, km_ref):
    ones = ones_ref[...]
    rows = q_ref.shape[0]
    nblk = rows // MOBA_BLOCK
    mq = _qk_norm_rope(q_ref[...], qg_ref[...], tab_ref, ones) * (HEAD_DIM ** -0.5)
    mk = _qk_norm_rope(k_ref[...], kg_ref[...], tab_ref, ones)
    mk_ref[...] = mk
    kmean = jnp.mean(mk.reshape(nblk, MOBA_BLOCK, BRANCH_WIDTH), axis=1)
    vt = v_ref[...].T
    for h in range(HEADS):
        sl = slice(h * HEAD_DIM, (h + 1) * HEAD_DIM)
        qh_ref[0, h] = mq[:, sl]
        km_ref[0, h] = kmean[:, sl]
        for j in range(nblk):
            rs = slice(j * MOBA_BLOCK, (j + 1) * MOBA_BLOCK)
            kh_ref[0, h, j] = mk[rs, sl].astype(BF16)
            vt_ref[0, h, j] = vt[sl, rs].astype(BF16)


def _moba_prep(q, k, v, tabs, lp, batch, seq):
    rows = min(8 * MOBA_BLOCK, seq)
    nsteps = seq // rows
    nblk = rows // MOBA_BLOCK
    nb = seq // MOBA_BLOCK
    blk = pl.BlockSpec((rows, 256), lambda b, r: (b * nsteps + r, 0))
    return pl.pallas_call(
        _moba_prep_kernel,
        grid=(batch, nsteps),
        in_specs=[blk, blk, blk, pl.BlockSpec((3, rows, 256), lambda b, r: (0, r, 0)),
                  _const_spec((1, 256)), _const_spec((1, 256)), _const_spec((256, 256))],
        out_specs=[blk,
                   pl.BlockSpec((1, HEADS, rows, HEAD_DIM), lambda b, r: (b, 0, r, 0)),
                   pl.BlockSpec((1, HEADS, nblk, MOBA_BLOCK, HEAD_DIM), lambda b, r: (b, 0, r, 0, 0)),
                   pl.BlockSpec((1, HEADS, nblk, HEAD_DIM, MOBA_BLOCK), lambda b, r: (b, 0, r, 0, 0)),
                   pl.BlockSpec((1, HEADS, nblk, HEAD_DIM), lambda b, r: (b, 0, r, 0))],
        out_shape=[jax.ShapeDtypeStruct((batch * seq, 256), F32),
                   jax.ShapeDtypeStruct((batch, HEADS, seq, HEAD_DIM), F32),
                   jax.ShapeDtypeStruct((batch, HEADS, nb, MOBA_BLOCK, HEAD_DIM), BF16),
                   jax.ShapeDtypeStruct((batch, HEADS, nb, HEAD_DIM, MOBA_BLOCK), BF16),
                   jax.ShapeDtypeStruct((batch, HEADS, nb, HEAD_DIM), F32)],
        compiler_params=_cparams("parallel", "parallel"),
        name="moba_prep",
    )(q, k, v, tabs, lp["moba_q_norm"], lp["moba_k_norm"], _head_ones())


def _moba_attn_kernel(qh_ref, kh_ref, vt_ref, km_ref, o_ref, sel_scr):
    qb = pl.program_id(1)
    nb = km_ref.shape[2]
    blk = MOBA_BLOCK
    blk_id = lax.broadcasted_iota(jnp.int32, (nb, blk), 0)
    kpos = lax.broadcasted_iota(jnp.int32, (blk, blk), 0)
    qpos = lax.broadcasted_iota(jnp.int32, (blk, blk), 1)
    heads = range(HEADS)
    qs = [qh_ref[0, h].astype(BF16) for h in heads]

    def scores(j):
        return [_dot_nt(kh_ref[0, h, j], qs[h]) for h in heads]

    gates = [lax.dot_general(km_ref[0, h], qh_ref[0, h], (((1,), (1,)), ((), ())), precision=HIGHEST,
                             preferred_element_type=F32) for h in heads]
    own = scores(qb)
    for h in heads:
        gate = jnp.where(blk_id < qb, gates[h], NEG)
        taken = jnp.zeros((nb, blk), jnp.bool_)
        for _ in range(min(MOBA_TOPK, nb)):
            best = jnp.max(gate, axis=0, keepdims=True)
            idx = jnp.min(jnp.where(gate == best, blk_id, nb), axis=0, keepdims=True)
            hit = blk_id == idx
            taken = jnp.logical_or(taken, hit)
            gate = jnp.where(hit, -jnp.inf, gate)
        sel_scr[h] = jnp.where(jnp.logical_and(taken, blk_id < qb), 1.0, 0.0)

    ms, ls, ps = [], [], []
    for h in heads:
        s = jnp.where(kpos <= qpos, own[h], NEG)
        m = jnp.max(s, axis=0, keepdims=True)
        p = jnp.exp(s - m)
        ms.append(m)
        ls.append(jnp.sum(p, axis=0, keepdims=True))
        ps.append(p.astype(BF16))
    accs = [_dot(vt_ref[0, h, qb], ps[h]) for h in heads]

    def body(j, carry):
        ms, ls, accs = carry
        ss = scores(j)
        new_m, new_l, alphas, ps = [], [], [], []
        for h in heads:
            s = jnp.where(sel_scr[h, pl.ds(j, 1), :] > 0.0, ss[h], NEG)
            m_new = jnp.maximum(ms[h], jnp.max(s, axis=0, keepdims=True))
            alpha = jnp.exp(ms[h] - m_new)
            p = jnp.exp(s - m_new)
            new_m.append(m_new)
            new_l.append(alpha * ls[h] + jnp.sum(p, axis=0, keepdims=True))
            alphas.append(alpha)
            ps.append(p.astype(BF16))
        new_acc = [alphas[h] * accs[h] + _dot(vt_ref[0, h, j], ps[h]) for h in heads]
        return tuple(new_m), tuple(new_l), tuple(new_acc)

    ms, ls, accs = lax.fori_loop(0, qb, body, (tuple(ms), tuple(ls), tuple(accs)))
    o_ref[...] = jnp.concatenate([accs[h] / ls[h] for h in heads], axis=0).T


def _moba_prompt(qh, kh, vt, km, batch, seq):
    nb = seq // MOBA_BLOCK
    return pl.pallas_call(
        _moba_attn_kernel,
        grid=(batch, nb),
        in_specs=[pl.BlockSpec((1, HEADS, MOBA_BLOCK, HEAD_DIM), lambda b, i: (b, 0, i, 0)),
                  pl.BlockSpec((1, HEADS, nb, MOBA_BLOCK, HEAD_DIM), lambda b, i: (b, 0, 0, 0, 0)),
                  pl.BlockSpec((1, HEADS, nb, HEAD_DIM, MOBA_BLOCK), lambda b, i: (b, 0, 0, 0, 0)),
                  pl.BlockSpec((1, HEADS, nb, HEAD_DIM), lambda b, i: (b, 0, 0, 0))],
        out_specs=pl.BlockSpec((MOBA_BLOCK, 256), lambda b, i: (b * nb + i, 0)),
        out_shape=jax.ShapeDtypeStruct((batch * seq, 256), F32),
        scratch_shapes=[pltpu.VMEM((HEADS, nb, MOBA_BLOCK), F32)],
        compiler_params=_cparams("parallel", "arbitrary"),
        name="moba_prompt",
    )(qh, kh, vt, km)


def _s5_step_kernel(u_ref, h0_ref, wxs_ref, win_ref, why_ref, apow_ref, d_ref, y_ref, h_ref):
    u = u_ref[...]
    ub = u.astype(BF16)
    h0 = h0_ref[...]
    xs = _dot(ub, wxs_ref[...])
    a_re = apow_ref[0:1, :]
    a_im = apow_ref[1:2, :]
    hr0 = h0[:, :S5_STATE]
    hi0 = h0[:, S5_STATE:]
    hr = a_re * hr0 - a_im * hi0 + xs[:, :S5_STATE]
    hi = a_re * hi0 + a_im * hr0 + xs[:, S5_STATE:]
    h_ref[...] = jnp.concatenate([hr, hi], axis=-1)
    y = _dot(ub, win_ref[...]) + _dot(h0.astype(BF16), why_ref[...]) + d_ref[...] * u
    y_ref[...] = _gelu(y)


def _s5_step(u, h0, sm):
    n = u.shape[0]
    return pl.pallas_call(
        _s5_step_kernel,
        out_shape=[jax.ShapeDtypeStruct((n, BRANCH_WIDTH), F32),
                   jax.ShapeDtypeStruct((n, 2 * S5_STATE), F32)],
        compiler_params=pltpu.CompilerParams(vmem_limit_bytes=VMEM_LIMIT),
        name="s5_step",
    )(u, h0, sm["w_xs"], sm["w_in"], sm["w_hy"], sm["a_pow"], sm["d_row"])


def _expand_mats():
    idx = np.arange(HEAD_DIM * HEAD_DIM)
    rep = (np.arange(HEAD_DIM)[:, None] == idx[None, :] // HEAD_DIM)
    til = (np.arange(HEAD_DIM)[:, None] == idx[None, :] % HEAD_DIM)
    return jnp.asarray(rep, BF16), jnp.asarray(til, BF16), jnp.asarray(til.T, BF16)


def _gla_step_kernel(q_ref, k_ref, v_ref, g_ref, r_ref, s0_ref, rep_ref, til_ref, tilt_ref, gn_ref,
                     y_ref, s_ref):
    rep = rep_ref[...]
    eg = _dot_sel_exact(jnp.exp(g_ref[...]), rep)
    kr = _dot_sel_exact(k_ref[...], rep)
    qr = _dot_sel_exact(q_ref[...] * (HEAD_DIM ** -0.5), rep)
    vt = _dot_sel_exact(v_ref[...], til_ref[...])
    s = eg * s0_ref[...] + kr * vt
    s_ref[...] = s
    o = _dot_sel(qr * s, tilt_ref[...])
    y_ref[...] = _rms_rows(o, gn_ref[...]) * _silu(r_ref[...])


def _gdn_conv_step_kernel(x_ref, c0_ref, cw_ref, qkv_ref, cnew_ref):
    x = x_ref[...]
    c0 = c0_ref[...]
    conv = cw_ref[GDN_CONV - 1:GDN_CONV, :] * x
    for w in range(GDN_CONV - 1):
        conv = conv + cw_ref[w:w + 1, :] * c0[:, w * GDN_QKV:(w + 1) * GDN_QKV]
    qkv_ref[...] = _silu(conv)
    cnew_ref[...] = jnp.concatenate([c0[:, GDN_QKV:], x], axis=-1)


def _gdn_step_kernel(q_ref, k_ref, v_ref, b_ref, g_ref, z_ref, s0_ref, rep_ref, til_ref, tilt_ref,
                     gn_ref, y_ref, s_ref):
    q = q_ref[...]
    k = k_ref[...]
    q = q * lax.rsqrt(jnp.sum(q * q, axis=-1, keepdims=True) + EPS) * (HEAD_DIM ** -0.5)
    k = k * lax.rsqrt(jnp.sum(k * k, axis=-1, keepdims=True) + EPS)
    beta = b_ref[:, 0:1]
    gam = jnp.exp(g_ref[:, 0:1])
    rep = rep_ref[...]
    tilt = tilt_ref[...]
    kr = _dot_sel_exact(k, rep)
    qr = _dot_sel_exact(q, rep)
    s0 = s0_ref[...]
    ks = _dot_sel(kr * s0, tilt)
    qs = _dot_sel(qr * s0, tilt)
    u = v_ref[...] - gam * ks
    qk = jnp.sum(q * k, axis=-1, keepdims=True)
    o = gam * qs + (beta * qk) * u
    s_ref[...] = gam * s0 + kr * _dot_sel_exact(beta * u, til_ref[...])
    y_ref[...] = _rms_rows(o, gn_ref[...]) * _silu(z_ref[...])


def _whole_call(kernel, out_shape, name, *args):
    return pl.pallas_call(kernel, out_shape=out_shape, name=name,
                          compiler_params=pltpu.CompilerParams(vmem_limit_bytes=VMEM_LIMIT))(*args)


def _moba_qk_step_kernel(q_ref, k_ref, tab_ref, qg_ref, kg_ref, ones_ref, mq_ref, mk_ref):
    ones = ones_ref[...]
    mq_ref[...] = _qk_norm_rope(q_ref[...], qg_ref[...], tab_ref, ones) * (HEAD_DIM ** -0.5)
    mk_ref[...] = _qk_norm_rope(k_ref[...], kg_ref[...], tab_ref, ones)


SELECT_PAGES = 8
SEL_ROWS = 8


def _moba_select_kernel(pt_ref, q_ref, *refs):
    pages = refs[:SELECT_PAGES]
    sel_ref = refs[SELECT_PAGES]
    gate_scr = refs[SELECT_PAGES + 1]
    g = pl.program_id(1)
    ppb = MOBA_BLOCK // PAGE_SIZE
    row = lax.broadcasted_iota(jnp.int32, (SEL_ROWS, BRANCH_WIDTH), 0)
    lane_head = lax.broadcasted_iota(jnp.int32, (SEL_ROWS, BRANCH_WIDTH), 1) // HEAD_DIM
    qm = jnp.where(row == lane_head, q_ref[0], 0.0).astype(BF16)
    blk_lane = lax.broadcasted_iota(jnp.int32, (SEL_ROWS, LANES), 1)

    @pl.when(g == 0)
    def _():
        gate_scr[...] = jnp.full((SEL_ROWS, LANES), NEG, F32)

    gate = gate_scr[...]
    for n in range(SELECT_PAGES // ppb):
        logits = None
        for e in range(ppb):
            kt = pages[n * ppb + e][0, 0].reshape(BRANCH_WIDTH, PAGE_SIZE).astype(BF16)
            part = _dot(qm, kt)
            logits = part if logits is None else logits + part
        mean = jnp.sum(logits, axis=-1, keepdims=True) * (1.0 / MOBA_BLOCK)
        gate = jnp.where(blk_lane == g * (SELECT_PAGES // ppb) + n, mean, gate)
    gate_scr[...] = gate

    @pl.when(g == pl.num_programs(1) - 1)
    def _():
        left = gate
        sel = jnp.zeros((SEL_ROWS, LANES), jnp.int32)
        for r in range(MOBA_TOPK):
            best = jnp.max(left, axis=-1, keepdims=True)
            idx = jnp.min(jnp.where(left == best, blk_lane, LANES), axis=-1, keepdims=True)
            sel = jnp.where(blk_lane == r, idx, sel)
            left = jnp.where(blk_lane == idx, -jnp.inf, left)
        sel_ref[0] = sel


def _moba_select(mq, page_table, cache_kt, layer):
    nseq, npages = page_table.shape

    def page_spec(p):
        return pl.BlockSpec((1, 1, HEADS, HEAD_DIM, PAGE_SIZE),
                            lambda b, g, pt: (layer, pt[b, g * SELECT_PAGES + p], 0, 0, 0))

    grid_spec = pltpu.PrefetchScalarGridSpec(
        num_scalar_prefetch=1,
        grid=(nseq, npages // SELECT_PAGES),
        in_specs=[pl.BlockSpec((1, 1, 256), lambda b, g, pt: (b, 0, 0))]
        + [page_spec(p) for p in range(SELECT_PAGES)],
        out_specs=pl.BlockSpec((1, SEL_ROWS, LANES), lambda b, g, pt: (b, 0, 0)),
        scratch_shapes=[pltpu.VMEM((SEL_ROWS, LANES), F32)],
    )
    sel = pl.pallas_call(
        _moba_select_kernel,
        grid_spec=grid_spec,
        out_shape=jax.ShapeDtypeStruct((nseq, SEL_ROWS, LANES), jnp.int32),
        compiler_params=_cparams("parallel", "arbitrary"),
        name="moba_select",
    )(page_table, mq.reshape(nseq, 1, 256), *([cache_kt] * SELECT_PAGES))
    return sel[:, :HEADS, :MOBA_TOPK]


N_SEL_PAGES = MOBA_TOPK * (MOBA_BLOCK // PAGE_SIZE)


def _moba_step_kernel(sel_ref, pt_ref, q_ref, kn_ref, vn_ref, *refs):
    kp = refs[:N_SEL_PAGES]
    vp = refs[N_SEL_PAGES:2 * N_SEL_PAGES]
    o_ref = refs[2 * N_SEL_PAGES]
    q = q_ref[0, 0]
    q8 = jnp.broadcast_to(q, (SEL_ROWS, HEAD_DIM)).astype(BF16)
    logits = [_dot(q8, r[0, 0, 0].astype(BF16))[0:1] for r in kp]
    l_self = jnp.sum(q * kn_ref[0, 0], axis=-1, keepdims=True)
    m = l_self
    for lg in logits:
        m = jnp.maximum(m, jnp.max(lg, axis=-1, keepdims=True))
    p_self = jnp.exp(l_self - m)
    den = p_self
    num = p_self * vn_ref[0, 0]
    for lg, r in zip(logits, vp):
        p = jnp.exp(lg - m)
        den = den + jnp.sum(p, axis=-1, keepdims=True)
        p8 = jnp.broadcast_to(p, (SEL_ROWS, PAGE_SIZE)).astype(BF16)
        num = num + _dot_nt(p8, r[0, 0, 0].astype(BF16))[0:1]
    o_ref[0, 0] = num / den


def _moba_step(q, k_new, v_new, sel, page_table, cache_kt, cache_vt, layer):
    nseq, npages = page_table.shape
    ppb = MOBA_BLOCK // PAGE_SIZE

    def page_spec(r, e):
        def index(b, h, sel_ref, pt_ref):
            blk = sel_ref[(b * HEADS + h) * MOBA_TOPK + r]
            return (layer, pt_ref[b * npages + ppb * blk + e], h, 0, 0)
        return pl.BlockSpec((1, 1, 1, HEAD_DIM, PAGE_SIZE), index)

    row = pl.BlockSpec((1, 1, 1, HEAD_DIM), lambda b, h, s, p: (b, h, 0, 0))
    pages = [page_spec(r, e) for r in range(MOBA_TOPK) for e in range(ppb)]
    grid_spec = pltpu.PrefetchScalarGridSpec(
        num_scalar_prefetch=2,
        grid=(nseq, HEADS),
        in_specs=[row, row, row] + pages + pages,
        out_specs=row,
    )
    r4 = lambda a: a.reshape(nseq, HEADS, 1, HEAD_DIM)
    out = pl.pallas_call(
        _moba_step_kernel,
        grid_spec=grid_spec,
        out_shape=jax.ShapeDtypeStruct((nseq, HEADS, 1, HEAD_DIM), F32),
        compiler_params=_cparams("parallel", "parallel"),
        name="moba_step",
    )(sel.reshape(-1), page_table.reshape(-1), r4(q), r4(k_new), r4(v_new),
      *([cache_kt] * N_SEL_PAGES), *([cache_vt] * N_SEL_PAGES))
    return out.reshape(nseq, 256)


def _layer_params(l, w):
    tile4 = lambda a: jnp.tile(a, HEADS)[None]
    rep64 = lambda a: jnp.repeat(a, HEAD_DIM)[None]
    s5p = {k: w[k][l] for k in ("s5_a_re", "s5_a_im", "s5_log_dt", "s5_b_re", "s5_b_im",
                                "s5_c_re", "s5_c_im", "s5_d")}
    return dict(
        ln1_g=w["ln1_g"][l][None], w_in=_regroup_w_in(w["w_in"][l]),
        gla_wg=jnp.pad(w["gla_w_gate"][l], ((0, 128 - GLA_RANK), (0, 0))).astype(BF16),
        gla_bg=w["gla_b_gate"][l][None],
        gdn_alog=rep64(w["gdn_a_log"][l]), gdn_dtb=rep64(w["gdn_dt_bias"][l]),
        s5_prompt=_s5_matrices(s5p, S5_CHUNK), s5_step=_s5_matrices(s5p, 1),
        s5_w_glu=w["s5_w_glu"][l].astype(BF16), s5_b_glu=w["s5_b_glu"][l][None],
        gla_norm=tile4(w["gla_norm"][l]), gla_norm_head=w["gla_norm"][l][None],
        gdn_norm=tile4(w["gdn_norm"][l]), gdn_norm_head=w["gdn_norm"][l][None],
        gdn_conv_w=jnp.pad(w["gdn_conv_w"][l], ((0, CONV_PAD - GDN_CONV), (0, 0))),
        moba_q_norm=tile4(w["moba_q_norm"][l]), moba_k_norm=tile4(w["moba_k_norm"][l]),
        w_gate=w["w_gate"][l].astype(BF16),
        w_br=jnp.stack([w["w_br_s5"][l], w["w_br_gla"][l], w["w_br_gdn"][l], w["w_br_moba"][l]]).astype(BF16),
        w_out=w["w_out"][l].astype(BF16), ln2_g=w["ln2_g"][l][None],
        w_ff1=w["w_ff1"][l].astype(BF16), w_ff2=w["w_ff2"][l].astype(BF16),
    )


PROMPT_ROWS = 512


def _prompt_layer(x2d, lp, tabs, batch, seq):
    tm = min(PROMPT_ROWS, batch * seq)
    pr = _inproj(x2d, lp, tm)
    ya, s5_fin = _s5_prompt(pr["s5_u"], lp["s5_prompt"], batch, seq)
    yb, gla_s = _gla_prompt(pr["gla_q"], pr["gla_k"], pr["gla_v"], pr["gla_lr"], pr["gla_r"], lp, batch, seq)
    yc, gdn_s, conv = _gdn_prompt(pr["gdn_qkv"], pr["gdn_b"], pr["gdn_a"], pr["gdn_z"], lp, batch, seq)
    mk, qh, kh, vt, km = _moba_prep(pr["moba_q"], pr["moba_k"], pr["moba_v"], tabs, lp, batch, seq)
    yd = _moba_prompt(qh, kh, vt, km, batch, seq)
    x1 = _merge(x2d, ya, yb, yc, yd, lp, tm)
    x2 = _mlp(x1, lp, tm)
    head4 = lambda a: a.reshape(batch, seq, HEADS, HEAD_DIM)
    s5 = lambda a: a.reshape(batch, S5_GROUPS, S5_P)
    states = (head4(mk), head4(pr["moba_v"]), s5(s5_fin[:, :S5_STATE]), s5(s5_fin[:, S5_STATE:]),
              gla_s, gdn_s, conv)
    return x2, states


def _sample_layer(x2d, lp, tabs, page_table, cache_kt, cache_vt, layer, st):
    s5_re0, s5_im0, gla0, gdn0, conv0 = st
    n = x2d.shape[0]
    pr = _inproj(x2d, lp, n)
    rows = n * HEADS
    per_head = lambda a: a.reshape(rows, HEAD_DIM)
    flat_state = lambda a: a.reshape(rows, HEAD_DIM * HEAD_DIM)
    rep, til, tilt = _expand_mats()
    sds = jax.ShapeDtypeStruct

    h0 = jnp.concatenate([s5_re0.reshape(n, S5_STATE), s5_im0.reshape(n, S5_STATE)], axis=1)
    ya, s5_new = _s5_step(pr["s5_u"], h0, lp["s5_step"])

    yb, gla_s = _whole_call(
        _gla_step_kernel, [sds((rows, HEAD_DIM), F32), sds((rows, HEAD_DIM * HEAD_DIM), F32)], "gla_step",
        per_head(pr["gla_q"]), per_head(pr["gla_k"]), per_head(pr["gla_v"]), per_head(pr["gla_lr"]),
        per_head(pr["gla_r"]), flat_state(gla0), rep, til, tilt, lp["gla_norm_head"])

    qkv, conv_new = _whole_call(
        _gdn_conv_step_kernel, [sds((n, GDN_QKV), F32), sds((n, (GDN_CONV - 1) * GDN_QKV), F32)],
        "gdn_conv_step", pr["gdn_qkv"], conv0.reshape(n, (GDN_CONV - 1) * GDN_QKV), lp["gdn_conv_w"])
    yc, gdn_s = _whole_call(
        _gdn_step_kernel, [sds((rows, HEAD_DIM), F32), sds((rows, HEAD_DIM * HEAD_DIM), F32)], "gdn_step",
        per_head(qkv[:, :256]), per_head(qkv[:, 256:512]), per_head(qkv[:, 512:]),
        per_head(pr["gdn_b"]), per_head(pr["gdn_a"]), per_head(pr["gdn_z"]), flat_state(gdn0),
        rep, til, tilt, lp["gdn_norm_head"])

    mq, mk = _whole_call(
        _moba_qk_step_kernel, [sds((n, 256), F32), sds((n, 256), F32)], "moba_qk_step",
        pr["moba_q"], pr["moba_k"], tabs, lp["moba_q_norm"], lp["moba_k_norm"], _head_ones())
    sel = _moba_select(mq, page_table, cache_kt, layer)
    yd = _moba_step(mq, mk, pr["moba_v"], sel, page_table, cache_kt, cache_vt, layer)

    x1 = _merge(x2d, ya, yb.reshape(n, 256), yc.reshape(n, 256), yd, lp, n)
    x2 = _mlp(x1, lp, n)
    head4 = lambda a: a.reshape(n, 1, HEADS, HEAD_DIM)
    s5 = lambda a: a.reshape(n, S5_GROUPS, S5_P)
    state4 = lambda a: a.reshape(n, HEADS, HEAD_DIM, HEAD_DIM)
    states = (head4(mk), head4(pr["moba_v"]), s5(s5_new[:, :S5_STATE]), s5(s5_new[:, S5_STATE:]),
              state4(gla_s), state4(gdn_s), conv_new.reshape(n, GDN_CONV - 1, GDN_QKV))
    return x2, states


def kernel(x_prompt, x_sample, cache_moba_k, cache_moba_v, page_table, state_s5_re, state_s5_im, state_gla, state_gdn, state_gdn_conv, ln1_g, w_in, s5_a_re, s5_a_im, s5_log_dt, s5_b_re, s5_b_im, s5_c_re, s5_c_im, s5_d, s5_w_glu, s5_b_glu, gla_w_gate, gla_b_gate, gla_norm, gdn_conv_w, gdn_a_log, gdn_dt_bias, gdn_norm, moba_q_norm, moba_k_norm, w_gate, w_br_s5, w_br_gla, w_br_gdn, w_br_moba, w_out, ln2_g, w_ff1, w_ff2):
    weights = dict(ln1_g=ln1_g, w_in=w_in, s5_a_re=s5_a_re, s5_a_im=s5_a_im, s5_log_dt=s5_log_dt,
                   s5_b_re=s5_b_re, s5_b_im=s5_b_im, s5_c_re=s5_c_re, s5_c_im=s5_c_im, s5_d=s5_d,
                   s5_w_glu=s5_w_glu, s5_b_glu=s5_b_glu, gla_w_gate=gla_w_gate, gla_b_gate=gla_b_gate,
                   gla_norm=gla_norm, gdn_conv_w=gdn_conv_w, gdn_a_log=gdn_a_log, gdn_dt_bias=gdn_dt_bias,
                   gdn_norm=gdn_norm, moba_q_norm=moba_q_norm, moba_k_norm=moba_k_norm, w_gate=w_gate,
                   w_br_s5=w_br_s5, w_br_gla=w_br_gla, w_br_gdn=w_br_gdn, w_br_moba=w_br_moba,
                   w_out=w_out, ln2_g=ln2_g, w_ff1=w_ff1, w_ff2=w_ff2)
    depth = ln1_g.shape[0]
    layers = [_layer_params(l, weights) for l in range(depth)]
    batch, seq, _ = x_prompt.shape
    nseq = x_sample.shape[0]
    npages = page_table.shape[1]
    past_len = npages * PAGE_SIZE

    xp = x_prompt.reshape(batch * seq, D_MODEL)
    tabs_p = _rope_tables(jnp.arange(seq, dtype=jnp.int32))
    p_states = []
    for l in range(depth):
        xp, st = _prompt_layer(xp, layers[l], tabs_p, batch, seq)
        p_states.append(st)

    cache_kt = cache_moba_k.transpose(0, 1, 3, 4, 2)
    cache_vt = cache_moba_v.transpose(0, 1, 3, 4, 2)
    tabs_s = _rope_tables(jnp.full((1,), past_len, jnp.int32))
    xs = x_sample.reshape(nseq, D_MODEL)
    s_states = []
    for l in range(depth):
        st0 = (state_s5_re[l], state_s5_im[l], state_gla[l], state_gdn[l], state_gdn_conv[l])
        xs, st = _sample_layer(xs, layers[l], tabs_s, page_table, cache_kt, cache_vt, l, st0)
        s_states.append(st)

    stack = lambda states: [jnp.stack([s[i] for s in states]) for i in range(len(states[0]))]
    return (xp.reshape(batch, seq, D_MODEL), xs.reshape(nseq, 1, D_MODEL), *stack(p_states), *stack(s_states))
```

```python
import functools
import math

import jax
import jax.numpy as jnp
import numpy as np
from jax import lax
from jax.experimental import pallas as pl
from jax.experimental.pallas import tpu as pltpu

F32 = jnp.float32
BF16 = jnp.bfloat16

D_MODEL = 1024
N_BRANCH = 4
BRANCH_WIDTH = D_MODEL // N_BRANCH
HEADS = 4
HEAD_DIM = BRANCH_WIDTH // HEADS
S5_GROUP = 16
S5_GROUPS = BRANCH_WIDTH // S5_GROUP
S5_P = 64
S5_STATE = S5_GROUPS * S5_P
GLA_RANK = 16
GLA_TAU = 16.0
GDN_CONV = 4
GDN_QKV = 3 * BRANCH_WIDTH
MOBA_BLOCK = 256
MOBA_TOPK = 3
ROT_DIM = HEAD_DIM // 4
ROPE_THETA = 500000.0
PAGE_SIZE = 128
D_FF = 4 * D_MODEL
EPS = 1e-6
NEG = -1e30

LANES = 128
S5_CHUNK = 4
GLA_SUB = 16
GDN_CHUNK = 64
VMEM_LIMIT = 56 * 1024 * 1024

HIGHEST = lax.Precision.HIGHEST


def _cparams(*sem):
    return pltpu.CompilerParams(dimension_semantics=sem, vmem_limit_bytes=VMEM_LIMIT)


def _const_spec(shape):
    zeros = (0,) * len(shape)
    return pl.BlockSpec(shape, lambda *_: zeros)


def _dot(a, b):
    return jnp.dot(a, b, preferred_element_type=F32)


def _dot_nt(a, b):
    return lax.dot_general(a, b, (((1,), (1,)), ((), ())), preferred_element_type=F32)


def _dot_tn(a, b):
    return lax.dot_general(a, b, (((0,), (0,)), ((), ())), preferred_element_type=F32)


def _bdot(a, b):
    return _dot(a.astype(BF16), b.astype(BF16))


def _split2(x):
    hi = x.astype(BF16)
    lo = (x - hi.astype(F32)).astype(BF16)
    return hi, lo


def _dot_sel(x, sel):
    hi, lo = _split2(x)
    return _dot(hi, sel) + _dot(lo, sel)


def _dot_sel_exact(x, sel):
    x1 = x.astype(BF16)
    r1 = x - x1.astype(F32)
    x2 = r1.astype(BF16)
    x3 = (r1 - x2.astype(F32)).astype(BF16)
    return _dot(x1, sel) + _dot(x2, sel) + _dot(x3, sel)


def _dot3(a, b):
    ah, al = _split2(a)
    bh, bl = _split2(b)
    return _dot(ah, bh) + (_dot(ah, bl) + _dot(al, bh))


def _rms_rows(x, g):
    return x * lax.rsqrt(jnp.mean(x * x, axis=-1, keepdims=True) + EPS) * g


def _sigmoid(x):
    return 1.0 / (1.0 + jnp.exp(-x))


def _silu(x):
    return x * _sigmoid(x)


def _softplus(x):
    return jnp.maximum(x, 0.0) + jnp.log1p(jnp.exp(-jnp.abs(x)))


def _head_ones():
    r = np.arange(BRANCH_WIDTH) // HEAD_DIM
    return jnp.asarray(r[:, None] == r[None, :], BF16)


IN_OUTS = (("s5_u", 256), ("gla_q", 256), ("gla_k", 256), ("gla_v", 256), ("gla_r", 256),
           ("gla_lr", 128), ("gdn_qkv", 768), ("gdn_b", 256), ("gdn_a", 256), ("gdn_z", 256),
           ("moba_q", 256), ("moba_k", 256), ("moba_v", 256))
IN_WIDTH = sum(w for _, w in IN_OUTS)


def _regroup_w_in(w_in):
    sizes = (256, 256, 256, 256, GLA_RANK, 256, 256, 256, 256, HEADS, HEADS, 256, 256, 256, 256)
    offs = np.cumsum((0,) + sizes)
    (s5_u, a_q, a_k, a_v, a_lr, a_r, d_q, d_k, d_v, d_b, d_a, d_z, m_q, m_k, m_v) = (
        w_in[:, offs[i]:offs[i + 1]] for i in range(len(sizes)))
    lr = jnp.pad(a_lr, ((0, 0), (0, 128 - GLA_RANK)))
    cols = [s5_u, a_q, a_k, a_v, a_r, lr, d_q, d_k, d_v,
            jnp.repeat(d_b, HEAD_DIM, axis=1), jnp.repeat(d_a, HEAD_DIM, axis=1), d_z, m_q, m_k, m_v]
    return jnp.concatenate(cols, axis=1).astype(BF16)


def _inproj_kernel(x_ref, g_ref, w_ref, wg_ref, bg_ref, alog_ref, dtb_ref, *outs):
    x = x_ref[...]
    hb = _rms_rows(x, g_ref[...]).astype(BF16)
    vals = {}
    off = 0
    for name, n in IN_OUTS:
        vals[name] = _dot(hb, w_ref[:, off:off + n])
        off += n
    z = _bdot(vals["gla_lr"], wg_ref[...]) + bg_ref[...]
    vals["gla_lr"] = -_softplus(-z) * (1.0 / GLA_TAU)
    vals["gdn_b"] = _sigmoid(vals["gdn_b"])
    vals["gdn_a"] = -jnp.exp(alog_ref[...]) * _softplus(vals["gdn_a"] + dtb_ref[...])
    for (name, _), o_ref in zip(IN_OUTS, outs):
        o_ref[...] = vals[name]


def _inproj(x2d, lp, tm):
    n = x2d.shape[0]
    out_shape = []
    out_specs = []
    for name, w in IN_OUTS:
        w_out = 256 if name == "gla_lr" else w
        out_shape.append(jax.ShapeDtypeStruct((n, w_out), F32))
        out_specs.append(pl.BlockSpec((tm, w_out), lambda i: (i, 0)))
    res = pl.pallas_call(
        _inproj_kernel,
        grid=(n // tm,),
        in_specs=[pl.BlockSpec((tm, D_MODEL), lambda i: (i, 0)),
                  _const_spec((1, D_MODEL)), _const_spec((D_MODEL, IN_WIDTH)),
                  _const_spec((128, 256)), _const_spec((1, 256)),
                  _const_spec((1, 256)), _const_spec((1, 256))],
        out_specs=out_specs,
        out_shape=out_shape,
        compiler_params=_cparams("parallel"),
        name="inproj",
    )(x2d, lp["ln1_g"], lp["w_in"], lp["gla_wg"], lp["gla_bg"], lp["gdn_alog"], lp["gdn_dtb"])
    return dict(zip((nm for nm, _ in IN_OUTS), res))


def _merge_kernel(x_ref, ya_ref, yb_ref, yc_ref, yd_ref, g_ref, wgate_ref, wglu_ref, bglu_ref,
                  wbr_ref, wout_ref, o_ref):
    x = x_ref[...]
    hb = _rms_rows(x, g_ref[...]).astype(BF16)
    ya = ya_ref[...]
    ya = ya * _sigmoid(_bdot(ya, wglu_ref[...]) + bglu_ref[...])
    merged = None
    for i, y in enumerate((ya, yb_ref[...], yc_ref[...], yd_ref[...])):
        gate = _sigmoid(_dot(hb, wgate_ref[:, i * D_MODEL:(i + 1) * D_MODEL]))
        term = gate * _dot(y.astype(BF16), wbr_ref[i])
        merged = term if merged is None else merged + term
    o_ref[...] = x + _bdot(merged, wout_ref[...])


def _merge(x2d, ya, yb, yc, yd, lp, tm):
    n = x2d.shape[0]
    row = lambda w: pl.BlockSpec((tm, w), lambda i: (i, 0))
    return pl.pallas_call(
        _merge_kernel,
        grid=(n // tm,),
        in_specs=[row(D_MODEL), row(256), row(256), row(256), row(256),
                  _const_spec((1, D_MODEL)), _const_spec((D_MODEL, N_BRANCH * D_MODEL)),
                  _const_spec((256, 256)), _const_spec((1, 256)),
                  _const_spec((N_BRANCH, 256, D_MODEL)), _const_spec((D_MODEL, D_MODEL))],
        out_specs=row(D_MODEL),
        out_shape=jax.ShapeDtypeStruct((n, D_MODEL), F32),
        compiler_params=_cparams("parallel"),
        name="merge",
    )(x2d, ya, yb, yc, yd, lp["ln1_g"], lp["w_gate"], lp["s5_w_glu"], lp["s5_b_glu"],
      lp["w_br"], lp["w_out"])


def _mlp_kernel(x_ref, g_ref, w1_ref, w2_ref, o_ref):
    x = x_ref[...]
    hb = _rms_rows(x, g_ref[...]).astype(BF16)
    z = jnp.maximum(_dot(hb, w1_ref[...]), 0.0)
    o_ref[...] = x + _bdot(z * z, w2_ref[...])


def _mlp(x2d, lp, tm):
    n = x2d.shape[0]
    row = pl.BlockSpec((tm, D_MODEL), lambda i: (i, 0))
    single = pl.Buffered(1)
    return pl.pallas_call(
        _mlp_kernel,
        grid=(n // tm,),
        in_specs=[row, _const_spec((1, D_MODEL)),
                  pl.BlockSpec((D_MODEL, D_FF), lambda i: (0, 0), pipeline_mode=single),
                  pl.BlockSpec((D_FF, D_MODEL), lambda i: (0, 0), pipeline_mode=single)],
        out_specs=row,
        out_shape=jax.ShapeDtypeStruct((n, D_MODEL), F32),
        compiler_params=_cparams("parallel"),
        name="mlp",
    )(x2d, lp["ln2_g"], lp["w_ff1"], lp["w_ff2"])


def _s5_matrices(p, chunk):
    hp = dict(precision=HIGHEST)
    dt = jnp.exp(p["s5_log_dt"])[:, None]
    ar, ai = p["s5_a_re"], p["s5_a_im"]
    mag = jnp.exp(ar * dt)
    abar_re = mag * jnp.cos(ai * dt)
    abar_im = mag * jnp.sin(ai * dt)
    den = ar * ar + ai * ai
    nr = abar_re - 1.0
    f_re = (nr * ar + abar_im * ai) / den
    f_im = (abar_im * ar - nr * ai) / den
    br, bi = p["s5_b_re"], p["s5_b_im"]
    bbar_re = f_re[..., None] * br - f_im[..., None] * bi
    bbar_im = f_re[..., None] * bi + f_im[..., None] * br
    pw_re = [jnp.ones_like(abar_re)]
    pw_im = [jnp.zeros_like(abar_re)]
    for _ in range(chunk):
        r, i = pw_re[-1], pw_im[-1]
        pw_re.append(r * abar_re - i * abar_im)
        pw_im.append(r * abar_im + i * abar_re)
    pw_re = jnp.stack(pw_re)
    pw_im = jnp.stack(pw_im)
    cr, ci = p["s5_c_re"], p["s5_c_im"]
    ca_re = cr[None] * pw_re[:, :, None, :] - ci[None] * pw_im[:, :, None, :]
    ca_im = cr[None] * pw_im[:, :, None, :] + ci[None] * pw_re[:, :, None, :]
    g_in = np.arange(BRANCH_WIDTH) // S5_GROUP
    g_st = np.arange(S5_STATE) // S5_P
    same_ii = jnp.asarray(g_in[:, None] == g_in[None, :], F32)
    same_is = jnp.asarray(g_in[:, None] == g_st[None, :], F32)

    def diag_blocks(table, rows_per_group, mask):
        flat = table.transpose(2, 0, 1).reshape(rows_per_group, -1)
        return jnp.tile(flat, (S5_GROUPS, 1)) * mask

    kern = (jnp.einsum("tgop,gpi->tgoi", ca_re[:chunk], bbar_re, **hp)
            - jnp.einsum("tgop,gpi->tgoi", ca_im[:chunk], bbar_im, **hp))
    lag_blocks = [diag_blocks(kern[tau], S5_GROUP, same_ii) for tau in range(chunk)]
    zero_block = jnp.zeros((BRANCH_WIDTH, BRANCH_WIDTH), F32)
    w_in = jnp.concatenate(
        [jnp.concatenate([lag_blocks[t - s] if t >= s else zero_block for t in range(chunk)], axis=1)
         for s in range(chunk)], axis=0)
    rev_re = pw_re[:chunk][::-1]
    rev_im = pw_im[:chunk][::-1]
    ab_re = rev_re[..., None] * bbar_re[None] - rev_im[..., None] * bbar_im[None]
    ab_im = rev_re[..., None] * bbar_im[None] + rev_im[..., None] * bbar_re[None]
    xs_re = jnp.concatenate([diag_blocks(ab_re[s], S5_GROUP, same_is) for s in range(chunk)], axis=0)
    xs_im = jnp.concatenate([diag_blocks(ab_im[s], S5_GROUP, same_is) for s in range(chunk)], axis=0)
    w_xs = jnp.concatenate([xs_re, xs_im], axis=1)
    hy_re = jnp.concatenate([diag_blocks(ca_re[t + 1], S5_P, same_is.T) for t in range(chunk)], axis=1)
    hy_im = jnp.concatenate([diag_blocks(ca_im[t + 1], S5_P, same_is.T) for t in range(chunk)], axis=1)
    w_hy = jnp.concatenate([hy_re, -hy_im], axis=0)
    a_pow = jnp.stack([pw_re[chunk].reshape(-1), pw_im[chunk].reshape(-1)])
    d_row = jnp.tile(p["s5_d"], chunk)[None, :]
    return dict(w_xs=w_xs.astype(BF16), w_in=w_in.astype(BF16), w_hy=w_hy.astype(BF16),
                a_pow=a_pow, d_row=d_row)


def _gelu(y):
    c = math.sqrt(2.0 / math.pi)
    return 0.5 * y * (1.0 + jnp.tanh(c * (y + 0.044715 * (y * y * y))))


def _s5_kernel(u_ref, wxs_ref, win_ref, why_ref, apow_ref, d_ref, y_ref, hfin_ref, xs_scr, hs_scr):
    rows = u_ref.shape[0]
    u = u_ref[...]
    ub = u.astype(BF16)
    xs_scr[...] = _dot(ub, wxs_ref[...])
    a_re = apow_ref[0:1, :]
    a_im = apow_ref[1:2, :]

    def step(r, carry):
        hr, hi = carry
        hs_scr[pl.ds(r, 1), :] = jnp.concatenate([hr, hi], axis=-1)
        x = xs_scr[pl.ds(r, 1), :]
        nhr = a_re * hr - a_im * hi + x[:, :S5_STATE]
        nhi = a_re * hi + a_im * hr + x[:, S5_STATE:]
        return nhr, nhi

    zero = jnp.zeros((1, S5_STATE), F32)
    hr, hi = lax.fori_loop(0, rows, step, (zero, zero))
    hfin_ref[0] = jnp.concatenate([hr, hi], axis=-1)
    y = _dot(ub, win_ref[...]) + _dot(hs_scr[...].astype(BF16), why_ref[...]) + d_ref[...] * u
    y_ref[...] = _gelu(y)


def _s5_prompt(u, sm, batch, seq):
    c = S5_CHUNK
    rows = seq // c
    width = c * BRANCH_WIDTH
    u_rows = u.reshape(batch * rows, width)
    single = pl.Buffered(1)
    wspec = lambda shape: pl.BlockSpec(shape, lambda b: (0, 0), pipeline_mode=single)
    y, hfin = pl.pallas_call(
        _s5_kernel,
        grid=(batch,),
        in_specs=[pl.BlockSpec((rows, width), lambda b: (b, 0)),
                  wspec((width, 2 * S5_STATE)), wspec((width, width)), wspec((2 * S5_STATE, width)),
                  _const_spec((2, S5_STATE)), _const_spec((1, width))],
        out_specs=[pl.BlockSpec((rows, width), lambda b: (b, 0)),
                   pl.BlockSpec((1, 1, 2 * S5_STATE), lambda b: (b, 0, 0))],
        out_shape=[jax.ShapeDtypeStruct((batch * rows, width), F32),
                   jax.ShapeDtypeStruct((batch, 1, 2 * S5_STATE), F32)],
        scratch_shapes=[pltpu.VMEM((rows, 2 * S5_STATE), F32), pltpu.VMEM((rows, 2 * S5_STATE), F32)],
        compiler_params=_cparams("parallel"),
        name="s5_prompt",
    )(u_rows, sm["w_xs"], sm["w_in"], sm["w_hy"], sm["a_pow"], sm["d_row"])
    return y.reshape(batch * seq, BRANCH_WIDTH), hfin.reshape(batch, 2 * S5_STATE)


GLA_ROWS = 256


def _gla_kernel(q_ref, k_ref, v_ref, g_ref, r_ref, tri_ref, ones_ref, bmask_ref, gn_ref,
                y_ref, sfin_ref, st_scr):
    step = pl.program_id(1)
    sub = GLA_SUB
    rows = q_ref.shape[0]
    ns = rows // sub

    @pl.when(step == 0)
    def _():
        st_scr[...] = jnp.zeros_like(st_scr)

    ones = ones_ref[...]
    bmask = bmask_ref[...]
    split = lambda a: a.reshape(ns, sub, BRANCH_WIDTH)
    bc = split(jnp.dot(tri_ref[...], g_ref[...], precision=HIGHEST, preferred_element_type=F32))
    q = split(q_ref[...] * (HEAD_DIM ** -0.5))
    k = split(k_ref[...])
    v = split(v_ref[...])

    row = lax.broadcasted_iota(jnp.int32, (ns, sub, BRANCH_WIDTH), 1)
    parts = []
    for j in range(sub):
        e = jnp.exp(jnp.minimum(bc - bc[:, j:j + 1, :], 0.0))
        parts.append(jnp.where(row >= j, q * k[:, j:j + 1, :] * e, 0.0))
    att = _dot_sel(jnp.concatenate(parts, axis=1).reshape(ns * sub * sub, BRANCH_WIDTH), ones)
    att = att.reshape(ns, sub * sub, BRANCH_WIDTH)
    o = att[:, 0:sub] * v[:, 0:1, :]
    for j in range(1, sub):
        o = o + att[:, j * sub:(j + 1) * sub] * v[:, j:j + 1, :]

    last = bc[:, sub - 1:sub, :]
    qt = (q * jnp.exp(bc)).astype(BF16)
    kt = (k * jnp.exp(last - bc)).astype(BF16)
    vb = v.astype(BF16)
    decay = jnp.exp(last)
    outer = [bmask * _dot_tn(vb[s], kt[s]) for s in range(ns)]
    st = st_scr[...]
    inter = []
    for s in range(ns):
        inter.append(_dot_nt(qt[s], st.astype(BF16)))
        st = st * decay[s] + outer[s]
    st_scr[...] = st
    o = o.reshape(rows, BRANCH_WIDTH) + jnp.concatenate(inter, axis=0)
    ms = _dot_sel(o * o, ones) * (1.0 / HEAD_DIM)
    y_ref[...] = o * lax.rsqrt(ms + EPS) * gn_ref[...] * _silu(r_ref[...])

    @pl.when(step == pl.num_programs(1) - 1)
    def _():
        sfin_ref[0] = st


def _sub_tril(rows, sub):
    i = np.arange(rows)
    return jnp.asarray((i[:, None] // sub == i[None, :] // sub) & (i[None, :] <= i[:, None]), F32)


def _unpack_state_t(st):
    b = st.shape[0]
    st = st.reshape(b, HEADS, HEAD_DIM, HEADS, HEAD_DIM)
    diag = jnp.stack([st[:, h, :, h, :] for h in range(HEADS)], axis=1)
    return diag.transpose(0, 1, 3, 2)


def _gla_prompt(q, k, v, g, r, lp, batch, seq):
    rows = min(GLA_ROWS, seq)
    nsteps = seq // rows
    blk = pl.BlockSpec((rows, BRANCH_WIDTH), lambda b, c: (b * nsteps + c, 0))
    hm = np.arange(BRANCH_WIDTH) // HEAD_DIM
    bmask = jnp.asarray(hm[:, None] == hm[None, :], F32)
    y, sfin = pl.pallas_call(
        _gla_kernel,
        grid=(batch, nsteps),
        in_specs=[blk, blk, blk, blk, blk, _const_spec((rows, rows)),
                  _const_spec((BRANCH_WIDTH, BRANCH_WIDTH)), _const_spec((BRANCH_WIDTH, BRANCH_WIDTH)),
                  _const_spec((1, BRANCH_WIDTH))],
        out_specs=[blk, pl.BlockSpec((1, BRANCH_WIDTH, BRANCH_WIDTH), lambda b, c: (b, 0, 0))],
        out_shape=[jax.ShapeDtypeStruct((batch * seq, BRANCH_WIDTH), F32),
                   jax.ShapeDtypeStruct((batch, BRANCH_WIDTH, BRANCH_WIDTH), F32)],
        scratch_shapes=[pltpu.VMEM((BRANCH_WIDTH, BRANCH_WIDTH), F32)],
        compiler_params=_cparams("parallel", "arbitrary"),
        name="gla_prompt",
    )(q, k, v, g, r, _sub_tril(rows, GLA_SUB), _head_ones(), bmask, lp["gla_norm"])
    return y, _unpack_state_t(sfin)


CONV_PAD = 8


GDN_ROWS = 512


def _block_diag(x, ones):
    return jnp.concatenate([x] * HEADS, axis=0) * ones


def _unit_lower_inverses(ns, eye, ones):
    c = ns[0].shape[0]
    every = range(len(ns))
    invs = [eye - n for n in ns]
    pws = list(ns)
    for _ in range(int(math.log2(c)) - 1):
        pbs = [pw.astype(BF16) for pw in pws]
        pws = [_dot(pb, _block_diag(pb, ones)) for pb in pbs]
        invs = [invs[i] + _dot(invs[i].astype(BF16), _block_diag(pws[i].astype(BF16), ones)) for i in every]
    inv_parts = [_split2(inv) for inv in invs]
    n_parts = [_split2(n) for n in ns]
    prods = [_dot(jnp.concatenate(n_parts[i], axis=0), _block_diag(inv_parts[i][0], ones)) for i in every]
    cross = [_dot(n_parts[i][0], _block_diag(inv_parts[i][1], ones)) for i in every]
    resids = [eye - invs[i] - (prods[i][:c] + prods[i][c:] + cross[i]) for i in every]
    return [invs[i] + _dot(inv_parts[i][0], _block_diag(resids[i].astype(BF16), ones)) for i in every]


def _gdn_kernel(x_ref, b_ref, g_ref, z_ref, cw_ref, tri_ref, ones_ref, eye_ref, gn_ref,
                y_ref, sfin_ref, conv_ref, s_scr, buf_scr):
    step = pl.program_id(1)
    rows = x_ref.shape[0]
    c = min(GDN_CHUNK, rows)

    @pl.when(step == 0)
    def _():
        s_scr[...] = jnp.zeros_like(s_scr)
        buf_scr[0:CONV_PAD, :] = jnp.zeros((CONV_PAD, GDN_QKV), F32)

    x = x_ref[...]
    buf_scr[CONV_PAD:CONV_PAD + rows, :] = x
    conv = cw_ref[GDN_CONV - 1:GDN_CONV, :] * x
    for w in range(GDN_CONV - 1):
        lag = GDN_CONV - 1 - w
        conv = conv + cw_ref[w:w + 1, :] * buf_scr[CONV_PAD - lag:CONV_PAD - lag + rows, :]
    tail = buf_scr[rows:rows + CONV_PAD, :]
    buf_scr[0:CONV_PAD, :] = tail
    conv_ref[0] = tail
    qkv = _silu(conv)
    ones = ones_ref[...]
    onesf = ones.astype(F32)
    eye = eye_ref[...]
    q = qkv[:, 0:BRANCH_WIDTH]
    k = qkv[:, BRANCH_WIDTH:2 * BRANCH_WIDTH]
    v = qkv[:, 2 * BRANCH_WIDTH:]
    q = q * lax.rsqrt(_dot_sel(q * q, ones) + EPS) * (HEAD_DIM ** -0.5)
    k = k * lax.rsqrt(_dot_sel(k * k, ones) + EPS)
    beta = b_ref[...]
    gc = jnp.dot(tri_ref[...], g_ref[...], precision=HIGHEST, preferred_element_type=F32)
    gam = jnp.exp(gc)
    ri = lax.broadcasted_iota(jnp.int32, (c, BRANCH_WIDTH), 0)
    cj = lax.broadcasted_iota(jnp.int32, (c, BRANCH_WIDTH), 1) % HEAD_DIM

    every = range(rows // c)
    sls = [slice(n * c, (n + 1) * c) for n in every]
    kcs = [k[rs].astype(BF16) for rs in sls]
    kqs = [_dot_nt(jnp.concatenate([k[rs], q[rs]], axis=0).astype(BF16), _block_diag(kcs[n], ones))
           for n, rs in enumerate(sls)]
    decs = []
    for rs in sls:
        grow = jnp.sum(gc[rs] * eye, axis=0, keepdims=True)
        decs.append(jnp.where(ri >= cj, jnp.exp(jnp.minimum(gc[rs] - grow, 0.0)), 0.0))
    invs = _unit_lower_inverses(
        [jnp.where(ri > cj, beta[rs] * decs[n] * kqs[n][:c], 0.0) for n, rs in enumerate(sls)], eye, ones)
    invbs = [inv.astype(BF16) for inv in invs]
    ws_m = [_dot(invbs[n], _block_diag((beta[rs] * gam[rs] * k[rs]).astype(BF16), ones))
            for n, rs in enumerate(sls)]
    u0s = [_dot(invbs[n], _block_diag((beta[rs] * v[rs]).astype(BF16), ones)) for n, rs in enumerate(sls)]
    aqks = [(decs[n] * kqs[n][c:]).astype(BF16) for n in every]
    lhs = [jnp.concatenate([ws_m[n], gam[rs] * q[rs]], axis=0).astype(BF16) for n, rs in enumerate(sls)]
    glasts = [gc[rs][c - 1:c, :] for rs in sls]
    kds = [(k[rs] * jnp.exp(glasts[n] - gc[rs])).astype(BF16) for n, rs in enumerate(sls)]

    s = s_scr[...]
    outs = []
    for n in every:
        ws = _dot(lhs[n], s.astype(BF16))
        ub = (u0s[n] - ws[:c]).astype(BF16)
        outs.append(ws[c:] + _dot(aqks[n], _block_diag(ub, ones)))
        s = jnp.exp(glasts[n]) * s + onesf * _dot_tn(kds[n], ub)
    s_scr[...] = s
    o = jnp.concatenate(outs, axis=0)
    ms = _dot_sel(o * o, ones) * (1.0 / HEAD_DIM)
    y_ref[...] = o * lax.rsqrt(ms + EPS) * gn_ref[...] * _silu(z_ref[...])

    @pl.when(step == pl.num_programs(1) - 1)
    def _():
        sfin_ref[0] = s


def _unpack_state(st):
    b = st.shape[0]
    st = st.reshape(b, HEADS, HEAD_DIM, HEADS, HEAD_DIM)
    return jnp.stack([st[:, h, :, h, :] for h in range(HEADS)], axis=1)


def _gdn_prompt(x, beta, g, z, lp, batch, seq):
    rows = min(GDN_ROWS, seq)
    c = min(GDN_CHUNK, rows)
    nsteps = seq // rows
    blk = lambda w: pl.BlockSpec((rows, w), lambda b, s: (b * nsteps + s, 0))
    eye = jnp.asarray(np.tile(np.eye(c, dtype=np.float32), (1, HEADS)))
    y, sfin, conv = pl.pallas_call(
        _gdn_kernel,
        grid=(batch, nsteps),
        in_specs=[blk(GDN_QKV), blk(256), blk(256), blk(256), _const_spec((CONV_PAD, GDN_QKV)),
                  _const_spec((rows, rows)), _const_spec((256, 256)), _const_spec((c, 256)),
                  _const_spec((1, 256))],
        out_specs=[blk(256),
                   pl.BlockSpec((1, 256, 256), lambda b, s: (b, 0, 0)),
                   pl.BlockSpec((1, CONV_PAD, GDN_QKV), lambda b, s: (b, 0, 0))],
        out_shape=[jax.ShapeDtypeStruct((batch * seq, 256), F32),
                   jax.ShapeDtypeStruct((batch, 256, 256), F32),
                   jax.ShapeDtypeStruct((batch, CONV_PAD, GDN_QKV), F32)],
        scratch_shapes=[pltpu.VMEM((256, 256), F32),
                        pltpu.VMEM((CONV_PAD + rows, GDN_QKV), F32)],
        compiler_params=_cparams("parallel", "arbitrary"),
        name="gdn_prompt",
    )(x, beta, g, z, lp["gdn_conv_w"], _sub_tril(rows, c), _head_ones(), eye, lp["gdn_norm"])
    return y, _unpack_state(sfin), conv[:, CONV_PAD - (GDN_CONV - 1):, :]


HALF_ROT = ROT_DIM // 2


def _rope_tables(pos):
    inv = ROPE_THETA ** (-jnp.arange(HALF_ROT, dtype=F32) / HALF_ROT)
    ang = pos.astype(F32)[:, None] * inv[None, :]
    cos, sin = jnp.cos(ang), jnp.sin(ang)
    n = pos.shape[0]
    rest = HEAD_DIM - ROT_DIM
    head = lambda a, b, fill: jnp.concatenate([a, b, jnp.full((n, rest), fill, F32)], axis=1)
    zero = jnp.zeros_like(sin)
    tabs = [head(cos, cos, 1.0), head(-sin, zero, 0.0), head(zero, sin, 0.0)]
    return jnp.stack([jnp.tile(t, (1, HEADS)) for t in tabs])


def _qk_norm_rope(x, gain, tab_ref, ones):
    y = x * lax.rsqrt(_dot_sel(x * x, ones) * (1.0 / HEAD_DIM) + EPS) * gain
    up = pltpu.roll(y, BRANCH_WIDTH - HALF_ROT, 1)
    down = pltpu.roll(y, HALF_ROT, 1)
    return y * tab_ref[0] + up * tab_ref[1] + down * tab_ref[2]


def _moba_prep_kernel(q_ref, k_ref, v_ref, tab_ref, qg_ref, kg_ref, ones_ref,
                      mk_ref, qh_ref, kh_ref, vt_ref, km_ref):
    ones = ones_ref[...]
    rows = q_ref.shape[0]
    nblk = rows // MOBA_BLOCK
    mq = _qk_norm_rope(q_ref[...], qg_ref[...], tab_ref, ones) * (HEAD_DIM ** -0.5)
    mk = _qk_norm_rope(k_ref[...], kg_ref[...], tab_ref, ones)
    mk_ref[...] = mk
    kmean = jnp.mean(mk.reshape(nblk, MOBA_BLOCK, BRANCH_WIDTH), axis=1)
    vt = v_ref[...].T
    for h in range(HEADS):
        sl = slice(h * HEAD_DIM, (h + 1) * HEAD_DIM)
        qh_ref[0, h] = mq[:, sl]
        km_ref[0, h] = kmean[:, sl]
        for j in range(nblk):
            rs = slice(j * MOBA_BLOCK, (j + 1) * MOBA_BLOCK)
            kh_ref[0, h, j] = mk[rs, sl].astype(BF16)
            vt_ref[0, h, j] = vt[sl, rs].astype(BF16)


def _moba_prep(q, k, v, tabs, lp, batch, seq):
    rows = min(8 * MOBA_BLOCK, seq)
    nsteps = seq // rows
    nblk = rows // MOBA_BLOCK
    nb = seq // MOBA_BLOCK
    blk = pl.BlockSpec((rows, 256), lambda b, r: (b * nsteps + r, 0))
    return pl.pallas_call(
        _moba_prep_kernel,
        grid=(batch, nsteps),
        in_specs=[blk, blk, blk, pl.BlockSpec((3, rows, 256), lambda b, r: (0, r, 0)),
                  _const_spec((1, 256)), _const_spec((1, 256)), _const_spec((256, 256))],
        out_specs=[blk,
                   pl.BlockSpec((1, HEADS, rows, HEAD_DIM), lambda b, r: (b, 0, r, 0)),
                   pl.BlockSpec((1, HEADS, nblk, MOBA_BLOCK, HEAD_DIM), lambda b, r: (b, 0, r, 0, 0)),
                   pl.BlockSpec((1, HEADS, nblk, HEAD_DIM, MOBA_BLOCK), lambda b, r: (b, 0, r, 0, 0)),
                   pl.BlockSpec((1, HEADS, nblk, HEAD_DIM), lambda b, r: (b, 0, r, 0))],
        out_shape=[jax.ShapeDtypeStruct((batch * seq, 256), F32),
                   jax.ShapeDtypeStruct((batch, HEADS, seq, HEAD_DIM), F32),
                   jax.ShapeDtypeStruct((batch, HEADS, nb, MOBA_BLOCK, HEAD_DIM), BF16),
                   jax.ShapeDtypeStruct((batch, HEADS, nb, HEAD_DIM, MOBA_BLOCK), BF16),
                   jax.ShapeDtypeStruct((batch, HEADS, nb, HEAD_DIM), F32)],
        compiler_params=_cparams("parallel", "parallel"),
        name="moba_prep",
    )(q, k, v, tabs, lp["moba_q_norm"], lp["moba_k_norm"], _head_ones())


def _moba_attn_kernel(qh_ref, kh_ref, vt_ref, km_ref, o_ref, sel_scr):
    qb = pl.program_id(1)
    nb = km_ref.shape[2]
    blk = MOBA_BLOCK
    blk_id = lax.broadcasted_iota(jnp.int32, (nb, blk), 0)
    kpos = lax.broadcasted_iota(jnp.int32, (blk, blk), 0)
    qpos = lax.broadcasted_iota(jnp.int32, (blk, blk), 1)
    heads = range(HEADS)
    qs = [qh_ref[0, h].astype(BF16) for h in heads]

    def scores(j):
        return [_dot_nt(kh_ref[0, h, j], qs[h]) for h in heads]

    gates = [lax.dot_general(km_ref[0, h], qh_ref[0, h], (((1,), (1,)), ((), ())), precision=HIGHEST,
                             preferred_element_type=F32) for h in heads]
    own = scores(qb)
    for h in heads:
        gate = jnp.where(blk_id < qb, gates[h], NEG)
        taken = jnp.zeros((nb, blk), jnp.bool_)
        for _ in range(min(MOBA_TOPK, nb)):
            best = jnp.max(gate, axis=0, keepdims=True)
            idx = jnp.min(jnp.where(gate == best, blk_id, nb), axis=0, keepdims=True)
            hit = blk_id == idx
            taken = jnp.logical_or(taken, hit)
            gate = jnp.where(hit, -jnp.inf, gate)
        sel_scr[h] = jnp.where(jnp.logical_and(taken, blk_id < qb), 1.0, 0.0)

    ms, ls, ps = [], [], []
    for h in heads:
        s = jnp.where(kpos <= qpos, own[h], NEG)
        m = jnp.max(s, axis=0, keepdims=True)
        p = jnp.exp(s - m)
        ms.append(m)
        ls.append(jnp.sum(p, axis=0, keepdims=True))
        ps.append(p.astype(BF16))
    accs = [_dot(vt_ref[0, h, qb], ps[h]) for h in heads]

    def body(j, carry):
        ms, ls, accs = carry
        ss = scores(j)
        new_m, new_l, alphas, ps = [], [], [], []
        for h in heads:
            s = jnp.where(sel_scr[h, pl.ds(j, 1), :] > 0.0, ss[h], NEG)
            m_new = jnp.maximum(ms[h], jnp.max(s, axis=0, keepdims=True))
            alpha = jnp.exp(ms[h] - m_new)
            p = jnp.exp(s - m_new)
            new_m.append(m_new)
            new_l.append(alpha * ls[h] + jnp.sum(p, axis=0, keepdims=True))
            alphas.append(alpha)
            ps.append(p.astype(BF16))
        new_acc = [alphas[h] * accs[h] + _dot(vt_ref[0, h, j], ps[h]) for h in heads]
        return tuple(new_m), tuple(new_l), tuple(new_acc)

    ms, ls, accs = lax.fori_loop(0, qb, body, (tuple(ms), tuple(ls), tuple(accs)))
    o_ref[...] = jnp.concatenate([accs[h] / ls[h] for h in heads], axis=0).T


def _moba_prompt(qh, kh, vt, km, batch, seq):
    nb = seq // MOBA_BLOCK
    return pl.pallas_call(
        _moba_attn_kernel,
        grid=(batch, nb),
        in_specs=[pl.BlockSpec((1, HEADS, MOBA_BLOCK, HEAD_DIM), lambda b, i: (b, 0, i, 0)),
                  pl.BlockSpec((1, HEADS, nb, MOBA_BLOCK, HEAD_DIM), lambda b, i: (b, 0, 0, 0, 0)),
                  pl.BlockSpec((1, HEADS, nb, HEAD_DIM, MOBA_BLOCK), lambda b, i: (b, 0, 0, 0, 0)),
                  pl.BlockSpec((1, HEADS, nb, HEAD_DIM), lambda b, i: (b, 0, 0, 0))],
        out_specs=pl.BlockSpec((MOBA_BLOCK, 256), lambda b, i: (b * nb + i, 0)),
        out_shape=jax.ShapeDtypeStruct((batch * seq, 256), F32),
        scratch_shapes=[pltpu.VMEM((HEADS, nb, MOBA_BLOCK), F32)],
        compiler_params=_cparams("parallel", "arbitrary"),
        name="moba_prompt",
    )(qh, kh, vt, km)


def _s5_step_kernel(u_ref, h0_ref, wxs_ref, win_ref, why_ref, apow_ref, d_ref, y_ref, h_ref):
    u = u_ref[...]
    ub = u.astype(BF16)
    h0 = h0_ref[...]
    xs = _dot(ub, wxs_ref[...])
    a_re = apow_ref[0:1, :]
    a_im = apow_ref[1:2, :]
    hr0 = h0[:, :S5_STATE]
    hi0 = h0[:, S5_STATE:]
    hr = a_re * hr0 - a_im * hi0 + xs[:, :S5_STATE]
    hi = a_re * hi0 + a_im * hr0 + xs[:, S5_STATE:]
    h_ref[...] = jnp.concatenate([hr, hi], axis=-1)
    y = _dot(ub, win_ref[...]) + _dot(h0.astype(BF16), why_ref[...]) + d_ref[...] * u
    y_ref[...] = _gelu(y)


def _s5_step(u, h0, sm):
    n = u.shape[0]
    return pl.pallas_call(
        _s5_step_kernel,
        out_shape=[jax.ShapeDtypeStruct((n, BRANCH_WIDTH), F32),
                   jax.ShapeDtypeStruct((n, 2 * S5_STATE), F32)],
        compiler_params=pltpu.CompilerParams(vmem_limit_bytes=VMEM_LIMIT),
        name="s5_step",
    )(u, h0, sm["w_xs"], sm["w_in"], sm["w_hy"], sm["a_pow"], sm["d_row"])


def _expand_mats():
    idx = np.arange(HEAD_DIM * HEAD_DIM)
    rep = (np.arange(HEAD_DIM)[:, None] == idx[None, :] // HEAD_DIM)
    til = (np.arange(HEAD_DIM)[:, None] == idx[None, :] % HEAD_DIM)
    return jnp.asarray(rep, BF16), jnp.asarray(til, BF16), jnp.asarray(til.T, BF16)


def _gla_step_kernel(q_ref, k_ref, v_ref, g_ref, r_ref, s0_ref, rep_ref, til_ref, tilt_ref, gn_ref,
                     y_ref, s_ref):
    rep = rep_ref[...]
    eg = _dot_sel_exact(jnp.exp(g_ref[...]), rep)
    kr = _dot_sel_exact(k_ref[...], rep)
    qr = _dot_sel_exact(q_ref[...] * (HEAD_DIM ** -0.5), rep)
    vt = _dot_sel_exact(v_ref[...], til_ref[...])
    s = eg * s0_ref[...] + kr * vt
    s_ref[...] = s
    o = _dot_sel(qr * s, tilt_ref[...])
    y_ref[...] = _rms_rows(o, gn_ref[...]) * _silu(r_ref[...])


def _gdn_conv_step_kernel(x_ref, c0_ref, cw_ref, qkv_ref, cnew_ref):
    x = x_ref[...]
    c0 = c0_ref[...]
    conv = cw_ref[GDN_CONV - 1:GDN_CONV, :] * x
    for w in range(GDN_CONV - 1):
        conv = conv + cw_ref[w:w + 1, :] * c0[:, w * GDN_QKV:(w + 1) * GDN_QKV]
    qkv_ref[...] = _silu(conv)
    cnew_ref[...] = jnp.concatenate([c0[:, GDN_QKV:], x], axis=-1)


def _gdn_step_kernel(q_ref, k_ref, v_ref, b_ref, g_ref, z_ref, s0_ref, rep_ref, til_ref, tilt_ref,
                     gn_ref, y_ref, s_ref):
    q = q_ref[...]
    k = k_ref[...]
    q = q * lax.rsqrt(jnp.sum(q * q, axis=-1, keepdims=True) + EPS) * (HEAD_DIM ** -0.5)
    k = k * lax.rsqrt(jnp.sum(k * k, axis=-1, keepdims=True) + EPS)
    beta = b_ref[:, 0:1]
    gam = jnp.exp(g_ref[:, 0:1])
    rep = rep_ref[...]
    tilt = tilt_ref[...]
    kr = _dot_sel_exact(k, rep)
    qr = _dot_sel_exact(q, rep)
    s0 = s0_ref[...]
    ks = _dot_sel(kr * s0, tilt)
    qs = _dot_sel(qr * s0, tilt)
    u = v_ref[...] - gam * ks
    qk = jnp.sum(q * k, axis=-1, keepdims=True)
    o = gam * qs + (beta * qk) * u
    s_ref[...] = gam * s0 + kr * _dot_sel_exact(beta * u, til_ref[...])
    y_ref[...] = _rms_rows(o, gn_ref[...]) * _silu(z_ref[...])


def _whole_call(kernel, out_shape, name, *args):
    return pl.pallas_call(kernel, out_shape=out_shape, name=name,
                          compiler_params=pltpu.CompilerParams(vmem_limit_bytes=VMEM_LIMIT))(*args)


def _moba_qk_step_kernel(q_ref, k_ref, tab_ref, qg_ref, kg_ref, ones_ref, mq_ref, mk_ref):
    ones = ones_ref[...]
    mq_ref[...] = _qk_norm_rope(q_ref[...], qg_ref[...], tab_ref, ones) * (HEAD_DIM ** -0.5)
    mk_ref[...] = _qk_norm_rope(k_ref[...], kg_ref[...], tab_ref, ones)


SELECT_PAGES = 16
SEL_ROWS = 8


def _moba_select_kernel(pt_ref, q_ref, *refs):
    pages = refs[:SELECT_PAGES]
    sel_ref = refs[SELECT_PAGES]
    gate_scr = refs[SELECT_PAGES + 1]
    g = pl.program_id(1)
    ppb = MOBA_BLOCK // PAGE_SIZE
    qcol = jnp.broadcast_to(q_ref[0], (BRANCH_WIDTH, PAGE_SIZE)).reshape(HEADS, HEAD_DIM, PAGE_SIZE)
    blk_lane = lax.broadcasted_iota(jnp.int32, (HEADS, LANES), 1)

    @pl.when(g == 0)
    def _():
        gate_scr[...] = jnp.full((HEADS, LANES), NEG, F32)

    gate = gate_scr[...]
    for n in range(SELECT_PAGES // ppb):
        tile = pages[n * ppb][0, 0]
        for e in range(1, ppb):
            tile = tile + pages[n * ppb + e][0, 0]
        per_token = jnp.sum(tile * qcol, axis=1)
        mean = jnp.sum(per_token, axis=-1, keepdims=True) * (1.0 / MOBA_BLOCK)
        gate = jnp.where(blk_lane == g * (SELECT_PAGES // ppb) + n, mean, gate)
    gate_scr[...] = gate

    @pl.when(g == pl.num_programs(1) - 1)
    def _():
        left = gate
        sel = jnp.zeros((HEADS, LANES), jnp.int32)
        for r in range(MOBA_TOPK):
            best = jnp.max(left, axis=-1, keepdims=True)
            idx = jnp.min(jnp.where(left == best, blk_lane, LANES), axis=-1, keepdims=True)
            sel = jnp.where(blk_lane == r, idx, sel)
            left = jnp.where(blk_lane == idx, -jnp.inf, left)
        sel_ref[0] = sel


def _moba_select(mq, page_table, cache_kt, layer):
    nseq, npages = page_table.shape

    def page_spec(p):
        return pl.BlockSpec((1, 1, HEADS, HEAD_DIM, PAGE_SIZE),
                            lambda b, g, pt: (layer, pt[b, g * SELECT_PAGES + p], 0, 0, 0))

    grid_spec = pltpu.PrefetchScalarGridSpec(
        num_scalar_prefetch=1,
        grid=(nseq, npages // SELECT_PAGES),
        in_specs=[pl.BlockSpec((1, 256, 1), lambda b, g, pt: (b, 0, 0))]
        + [page_spec(p) for p in range(SELECT_PAGES)],
        out_specs=pl.BlockSpec((1, HEADS, LANES), lambda b, g, pt: (b, 0, 0)),
        scratch_shapes=[pltpu.VMEM((HEADS, LANES), F32)],
    )
    sel = pl.pallas_call(
        _moba_select_kernel,
        grid_spec=grid_spec,
        out_shape=jax.ShapeDtypeStruct((nseq, HEADS, LANES), jnp.int32),
        compiler_params=_cparams("parallel", "arbitrary"),
        name="moba_select",
    )(page_table, mq.reshape(nseq, 256, 1), *([cache_kt] * SELECT_PAGES))
    return sel[:, :, :MOBA_TOPK]


N_SEL_PAGES = MOBA_TOPK * (MOBA_BLOCK // PAGE_SIZE)


def _moba_step_kernel(sel_ref, pt_ref, q_ref, kn_ref, vn_ref, *refs):
    kp = refs[:N_SEL_PAGES]
    vp = refs[N_SEL_PAGES:2 * N_SEL_PAGES]
    o_ref = refs[2 * N_SEL_PAGES]
    q = q_ref[0, 0]
    q8 = jnp.broadcast_to(q, (SEL_ROWS, HEAD_DIM)).astype(BF16)
    logits = [_dot(q8, r[0, 0, 0].astype(BF16))[0:1] for r in kp]
    l_self = jnp.sum(q * kn_ref[0, 0], axis=-1, keepdims=True)
    m = l_self
    for lg in logits:
        m = jnp.maximum(m, jnp.max(lg, axis=-1, keepdims=True))
    p_self = jnp.exp(l_self - m)
    den = p_self
    num = p_self * vn_ref[0, 0]
    for lg, r in zip(logits, vp):
        p = jnp.exp(lg - m)
        den = den + jnp.sum(p, axis=-1, keepdims=True)
        p8 = jnp.broadcast_to(p, (SEL_ROWS, PAGE_SIZE)).astype(BF16)
        num = num + _dot_nt(p8, r[0, 0, 0].astype(BF16))[0:1]
    o_ref[0, 0] = num / den


def _moba_step(q, k_new, v_new, sel, page_table, cache_kt, cache_vt, layer):
    nseq, npages = page_table.shape
    ppb = MOBA_BLOCK // PAGE_SIZE

    def page_spec(r, e):
        def index(b, h, sel_ref, pt_ref):
            blk = sel_ref[(b * HEADS + h) * MOBA_TOPK + r]
            return (layer, pt_ref[b * npages + ppb * blk + e], h, 0, 0)
        return pl.BlockSpec((1, 1, 1, HEAD_DIM, PAGE_SIZE), index)

    row = pl.BlockSpec((1, 1, 1, HEAD_DIM), lambda b, h, s, p: (b, h, 0, 0))
    pages = [page_spec(r, e) for r in range(MOBA_TOPK) for e in range(ppb)]
    grid_spec = pltpu.PrefetchScalarGridSpec(
        num_scalar_prefetch=2,
        grid=(nseq, HEADS),
        in_specs=[row, row, row] + pages + pages,
        out_specs=row,
    )
    r4 = lambda a: a.reshape(nseq, HEADS, 1, HEAD_DIM)
    out = pl.pallas_call(
        _moba_step_kernel,
        grid_spec=grid_spec,
        out_shape=jax.ShapeDtypeStruct((nseq, HEADS, 1, HEAD_DIM), F32),
        compiler_params=_cparams("parallel", "parallel"),
        name="moba_step",
    )(sel.reshape(-1), page_table.reshape(-1), r4(q), r4(k_new), r4(v_new),
      *([cache_kt] * N_SEL_PAGES), *([cache_vt] * N_SEL_PAGES))
    return out.reshape(nseq, 256)


def _layer_params(l, w):
    tile4 = lambda a: jnp.tile(a, HEADS)[None]
    rep64 = lambda a: jnp.repeat(a, HEAD_DIM)[None]
    s5p = {k: w[k][l] for k in ("s5_a_re", "s5_a_im", "s5_log_dt", "s5_b_re", "s5_b_im",
                                "s5_c_re", "s5_c_im", "s5_d")}
    return dict(
        ln1_g=w["ln1_g"][l][None], w_in=_regroup_w_in(w["w_in"][l]),
        gla_wg=jnp.pad(w["gla_w_gate"][l], ((0, 128 - GLA_RANK), (0, 0))).astype(BF16),
        gla_bg=w["gla_b_gate"][l][None],
        gdn_alog=rep64(w["gdn_a_log"][l]), gdn_dtb=rep64(w["gdn_dt_bias"][l]),
        s5_prompt=_s5_matrices(s5p, S5_CHUNK), s5_step=_s5_matrices(s5p, 1),
        s5_w_glu=w["s5_w_glu"][l].astype(BF16), s5_b_glu=w["s5_b_glu"][l][None],
        gla_norm=tile4(w["gla_norm"][l]), gla_norm_head=w["gla_norm"][l][None],
        gdn_norm=tile4(w["gdn_norm"][l]), gdn_norm_head=w["gdn_norm"][l][None],
        gdn_conv_w=jnp.pad(w["gdn_conv_w"][l], ((0, CONV_PAD - GDN_CONV), (0, 0))),
        moba_q_norm=tile4(w["moba_q_norm"][l]), moba_k_norm=tile4(w["moba_k_norm"][l]),
        w_gate=w["w_gate"][l].astype(BF16),
        w_br=jnp.stack([w["w_br_s5"][l], w["w_br_gla"][l], w["w_br_gdn"][l], w["w_br_moba"][l]]).astype(BF16),
        w_out=w["w_out"][l].astype(BF16), ln2_g=w["ln2_g"][l][None],
        w_ff1=w["w_ff1"][l].astype(BF16), w_ff2=w["w_ff2"][l].astype(BF16),
    )


PROMPT_ROWS = 512


def _prompt_layer(x2d, lp, tabs, batch, seq):
    tm = min(PROMPT_ROWS, batch * seq)
    pr = _inproj(x2d, lp, tm)
    ya, s5_fin = _s5_prompt(pr["s5_u"], lp["s5_prompt"], batch, seq)
    yb, gla_s = _gla_prompt(pr["gla_q"], pr["gla_k"], pr["gla_v"], pr["gla_lr"], pr["gla_r"], lp, batch, seq)
    yc, gdn_s, conv = _gdn_prompt(pr["gdn_qkv"], pr["gdn_b"], pr["gdn_a"], pr["gdn_z"], lp, batch, seq)
    mk, qh, kh, vt, km = _moba_prep(pr["moba_q"], pr["moba_k"], pr["moba_v"], tabs, lp, batch, seq)
    yd = _moba_prompt(qh, kh, vt, km, batch, seq)
    x1 = _merge(x2d, ya, yb, yc, yd, lp, tm)
    x2 = _mlp(x1, lp, tm)
    head4 = lambda a: a.reshape(batch, seq, HEADS, HEAD_DIM)
    s5 = lambda a: a.reshape(batch, S5_GROUPS, S5_P)
    states = (head4(mk), head4(pr["moba_v"]), s5(s5_fin[:, :S5_STATE]), s5(s5_fin[:, S5_STATE:]),
              gla_s, gdn_s, conv)
    return x2, states


def _sample_layer(x2d, lp, tabs, page_table, cache_kt, cache_vt, layer, st):
    s5_re0, s5_im0, gla0, gdn0, conv0 = st
    n = x2d.shape[0]
    pr = _inproj(x2d, lp, n)
    rows = n * HEADS
    per_head = lambda a: a.reshape(rows, HEAD_DIM)
    flat_state = lambda a: a.reshape(rows, HEAD_DIM * HEAD_DIM)
    rep, til, tilt = _expand_mats()
    sds = jax.ShapeDtypeStruct

    h0 = jnp.concatenate([s5_re0.reshape(n, S5_STATE), s5_im0.reshape(n, S5_STATE)], axis=1)
    ya, s5_new = _s5_step(pr["s5_u"], h0, lp["s5_step"])

    yb, gla_s = _whole_call(
        _gla_step_kernel, [sds((rows, HEAD_DIM), F32), sds((rows, HEAD_DIM * HEAD_DIM), F32)], "gla_step",
        per_head(pr["gla_q"]), per_head(pr["gla_k"]), per_head(pr["gla_v"]), per_head(pr["gla_lr"]),
        per_head(pr["gla_r"]), flat_state(gla0), rep, til, tilt, lp["gla_norm_head"])

    qkv, conv_new = _whole_call(
        _gdn_conv_step_kernel, [sds((n, GDN_QKV), F32), sds((n, (GDN_CONV - 1) * GDN_QKV), F32)],
        "gdn_conv_step", pr["gdn_qkv"], conv0.reshape(n, (GDN_CONV - 1) * GDN_QKV), lp["gdn_conv_w"])
    yc, gdn_s = _whole_call(
        _gdn_step_kernel, [sds((rows, HEAD_DIM), F32), sds((rows, HEAD_DIM * HEAD_DIM), F32)], "gdn_step",
        per_head(qkv[:, :256]), per_head(qkv[:, 256:512]), per_head(qkv[:, 512:]),
        per_head(pr["gdn_b"]), per_head(pr["gdn_a"]), per_head(pr["gdn_z"]), flat_state(gdn0),
        rep, til, tilt, lp["gdn_norm_head"])

    mq, mk = _whole_call(
        _moba_qk_step_kernel, [sds((n, 256), F32), sds((n, 256), F32)], "moba_qk_step",
        pr["moba_q"], pr["moba_k"], tabs, lp["moba_q_norm"], lp["moba_k_norm"], _head_ones())
    sel = _moba_select(mq, page_table, cache_kt, layer)
    yd = _moba_step(mq, mk, pr["moba_v"], sel, page_table, cache_kt, cache_vt, layer)

    x1 = _merge(x2d, ya, yb.reshape(n, 256), yc.reshape(n, 256), yd, lp, n)
    x2 = _mlp(x1, lp, n)
    head4 = lambda a: a.reshape(n, 1, HEADS, HEAD_DIM)
    s5 = lambda a: a.reshape(n, S5_GROUPS, S5_P)
    state4 = lambda a: a.reshape(n, HEADS, HEAD_DIM, HEAD_DIM)
    states = (head4(mk), head4(pr["moba_v"]), s5(s5_new[:, :S5_STATE]), s5(s5_new[:, S5_STATE:]),
              state4(gla_s), state4(gdn_s), conv_new.reshape(n, GDN_CONV - 1, GDN_QKV))
    return x2, states


def kernel(x_prompt, x_sample, cache_moba_k, cache_moba_v, page_table, state_s5_re, state_s5_im, state_gla, state_gdn, state_gdn_conv, ln1_g, w_in, s5_a_re, s5_a_im, s5_log_dt, s5_b_re, s5_b_im, s5_c_re, s5_c_im, s5_d, s5_w_glu, s5_b_glu, gla_w_gate, gla_b_gate, gla_norm, gdn_conv_w, gdn_a_log, gdn_dt_bias, gdn_norm, moba_q_norm, moba_k_norm, w_gate, w_br_s5, w_br_gla, w_br_gdn, w_br_moba, w_out, ln2_g, w_ff1, w_ff2):
    weights = dict(ln1_g=ln1_g, w_in=w_in, s5_a_re=s5_a_re, s5_a_im=s5_a_im, s5_log_dt=s5_log_dt,
                   s5_b_re=s5_b_re, s5_b_im=s5_b_im, s5_c_re=s5_c_re, s5_c_im=s5_c_im, s5_d=s5_d,
                   s5_w_glu=s5_w_glu, s5_b_glu=s5_b_glu, gla_w_gate=gla_w_gate, gla_b_gate=gla_b_gate,
                   gla_norm=gla_norm, gdn_conv_w=gdn_conv_w, gdn_a_log=gdn_a_log, gdn_dt_bias=gdn_dt_bias,
                   gdn_norm=gdn_norm, moba_q_norm=moba_q_norm, moba_k_norm=moba_k_norm, w_gate=w_gate,
                   w_br_s5=w_br_s5, w_br_gla=w_br_gla, w_br_gdn=w_br_gdn, w_br_moba=w_br_moba,
                   w_out=w_out, ln2_g=ln2_g, w_ff1=w_ff1, w_ff2=w_ff2)
    depth = ln1_g.shape[0]
    layers = [_layer_params(l, weights) for l in range(depth)]
    batch, seq, _ = x_prompt.shape
    nseq = x_sample.shape[0]
    npages = page_table.shape[1]
    past_len = npages * PAGE_SIZE

    xp = x_prompt.reshape(batch * seq, D_MODEL)
    tabs_p = _rope_tables(jnp.arange(seq, dtype=jnp.int32))
    p_states = []
    for l in range(depth):
        xp, st = _prompt_layer(xp, layers[l], tabs_p, batch, seq)
        p_states.append(st)

    cache_kt = cache_moba_k.transpose(0, 1, 3, 4, 2)
    cache_vt = cache_moba_v.transpose(0, 1, 3, 4, 2)
    tabs_s = _rope_tables(jnp.full((1,), past_len, jnp.int32))
    xs = x_sample.reshape(nseq, D_MODEL)
    s_states = []
    for l in range(depth):
        st0 = (state_s5_re[l], state_s5_im[l], state_gla[l], state_gdn[l], state_gdn_conv[l])
        xs, st = _sample_layer(xs, layers[l], tabs_s, page_table, cache_kt, cache_vt, l, st0)
        s_states.append(st)

    stack = lambda states: [jnp.stack([s[i] for s in states]) for i in range(len(states[0]))]
    return (xp.reshape(batch, seq, D_MODEL), xs.reshape(nseq, 1, D_MODEL), *stack(p_states), *stack(s_states))
```

```python
import functools
import math

import jax
import jax.numpy as jnp
import numpy as np
from jax import lax
from jax.experimental import pallas as pl
from jax.experimental.pallas import tpu as pltpu

F32 = jnp.float32
BF16 = jnp.bfloat16

D_MODEL = 1024
N_BRANCH = 4
BRANCH_WIDTH = D_MODEL // N_BRANCH
HEADS = 4
HEAD_DIM = BRANCH_WIDTH // HEADS
S5_GROUP = 16
S5_GROUPS = BRANCH_WIDTH // S5_GROUP
S5_P = 64
S5_STATE = S5_GROUPS * S5_P
GLA_RANK = 16
GLA_TAU = 16.0
GDN_CONV = 4
GDN_QKV = 3 * BRANCH_WIDTH
MOBA_BLOCK = 256
MOBA_TOPK = 3
ROT_DIM = HEAD_DIM // 4
ROPE_THETA = 500000.0
PAGE_SIZE = 128
D_FF = 4 * D_MODEL
EPS = 1e-6
NEG = -1e30

LANES = 128
S5_CHUNK = 4
GLA_SUB = 16
GDN_CHUNK = 64
VMEM_LIMIT = 56 * 1024 * 1024

HIGHEST = lax.Precision.HIGHEST


def _cparams(*sem):
    return pltpu.CompilerParams(dimension_semantics=sem, vmem_limit_bytes=VMEM_LIMIT)


def _const_spec(shape):
    zeros = (0,) * len(shape)
    return pl.BlockSpec(shape, lambda *_: zeros)


def _layer_spec(shape, layer, **kw):
    zeros = (0,) * len(shape)
    return pl.BlockSpec((None,) + tuple(shape), lambda *_: (layer,) + zeros, **kw)


def _dot(a, b):
    return jnp.dot(a, b, preferred_element_type=F32)


def _dot_nt(a, b):
    return lax.dot_general(a, b, (((1,), (1,)), ((), ())), preferred_element_type=F32)


def _dot_tn(a, b):
    return lax.dot_general(a, b, (((0,), (0,)), ((), ())), preferred_element_type=F32)


def _bdot(a, b):
    return _dot(a.astype(BF16), b.astype(BF16))


def _split2(x):
    hi = x.astype(BF16)
    lo = (x - hi.astype(F32)).astype(BF16)
    return hi, lo


def _dot_sel(x, sel):
    hi, lo = _split2(x)
    return _dot(hi, sel) + _dot(lo, sel)


def _dot_sel_exact(x, sel):
    x1 = x.astype(BF16)
    r1 = x - x1.astype(F32)
    x2 = r1.astype(BF16)
    x3 = (r1 - x2.astype(F32)).astype(BF16)
    return _dot(x1, sel) + _dot(x2, sel) + _dot(x3, sel)


def _dot3(a, b):
    ah, al = _split2(a)
    bh, bl = _split2(b)
    return _dot(ah, bh) + (_dot(ah, bl) + _dot(al, bh))


def _rms_rows(x, g):
    return x * lax.rsqrt(jnp.mean(x * x, axis=-1, keepdims=True) + EPS) * g


def _sigmoid(x):
    return 1.0 / (1.0 + jnp.exp(-x))


def _silu(x):
    return x * _sigmoid(x)


def _softplus(x):
    return jnp.maximum(x, 0.0) + jnp.log1p(jnp.exp(-jnp.abs(x)))


def _head_ones():
    r = np.arange(BRANCH_WIDTH) // HEAD_DIM
    return jnp.asarray(r[:, None] == r[None, :], BF16)


IN_OUTS = (("s5_u_lo", 128), ("s5_u_hi", 128), ("gla_q", 256), ("gla_k", 256), ("gla_v", 256), ("gla_r", 256),
           ("gla_lr", 128), ("gdn_qkv", 768), ("gdn_b", 256), ("gdn_a", 256), ("gdn_z", 256),
           ("moba_q", 256), ("moba_k", 256), ("moba_v", 256))
IN_WIDTH = sum(w for _, w in IN_OUTS)


def _regroup_w_in(w_in):
    sizes = (256, 256, 256, 256, GLA_RANK, 256, 256, 256, 256, HEADS, HEADS, 256, 256, 256, 256)
    offs = np.cumsum((0,) + sizes)
    (s5_u, a_q, a_k, a_v, a_lr, a_r, d_q, d_k, d_v, d_b, d_a, d_z, m_q, m_k, m_v) = (
        w_in[:, offs[i]:offs[i + 1]] for i in range(len(sizes)))
    lr = jnp.pad(a_lr, ((0, 0), (0, 128 - GLA_RANK)))
    cols = [s5_u, a_q, a_k, a_v, a_r, lr, d_q, d_k, d_v,
            jnp.repeat(d_b, HEAD_DIM, axis=1), jnp.repeat(d_a, HEAD_DIM, axis=1), d_z, m_q, m_k, m_v]
    return jnp.concatenate(cols, axis=1).astype(BF16)


def _inproj_kernel(x_ref, g_ref, w_ref, wg_ref, bg_ref, alog_ref, dtb_ref, *outs):
    x = x_ref[...]
    hb = _rms_rows(x, g_ref[...]).astype(BF16)
    vals = {}
    off = 0
    for name, n in IN_OUTS:
        vals[name] = _dot(hb, w_ref[:, off:off + n])
        off += n
    z = _bdot(vals["gla_lr"], wg_ref[...]) + bg_ref[...]
    vals["gla_lr"] = -_softplus(-z) * (1.0 / GLA_TAU)
    vals["gdn_b"] = _sigmoid(vals["gdn_b"])
    vals["gdn_a"] = -jnp.exp(alog_ref[...]) * _softplus(vals["gdn_a"] + dtb_ref[...])
    for (name, _), o_ref in zip(IN_OUTS, outs):
        o_ref[...] = vals[name]


def _inproj(x2d, lp, tm):
    n = x2d.shape[0]
    out_shape = []
    out_specs = []
    for name, w in IN_OUTS:
        w_out = 256 if name == "gla_lr" else w
        out_shape.append(jax.ShapeDtypeStruct((n, w_out), F32))
        out_specs.append(pl.BlockSpec((tm, w_out), lambda i: (i, 0)))
    res = pl.pallas_call(
        _inproj_kernel,
        grid=(n // tm,),
        in_specs=[pl.BlockSpec((tm, D_MODEL), lambda i: (i, 0)),
                  _const_spec((1, D_MODEL)), _const_spec((D_MODEL, IN_WIDTH)),
                  _const_spec((128, 256)), _const_spec((1, 256)),
                  _const_spec((1, 256)), _const_spec((1, 256))],
        out_specs=out_specs,
        out_shape=out_shape,
        compiler_params=_cparams("parallel"),
        name="inproj",
    )(x2d, lp["ln1_g"], lp["w_in"], lp["gla_wg"], lp["gla_bg"], lp["gdn_alog"], lp["gdn_dtb"])
    return dict(zip((nm for nm, _ in IN_OUTS), res))


def _merge_kernel(x_ref, ya_lo_ref, ya_hi_ref, yb_ref, yc_ref, yd_ref, g_ref, wgate_ref, wglu_ref, bglu_ref,
                  wbr_ref, wout_ref, o_ref):
    x = x_ref[...]
    hb = _rms_rows(x, g_ref[...]).astype(BF16)
    ya = jnp.concatenate([ya_lo_ref[...], ya_hi_ref[...]], axis=1)
    ya = ya * _sigmoid(_bdot(ya, wglu_ref[...]) + bglu_ref[...])
    merged = None
    for i, y in enumerate((ya, yb_ref[...], yc_ref[...], yd_ref[...])):
        gate = _sigmoid(_dot(hb, wgate_ref[:, i * D_MODEL:(i + 1) * D_MODEL]))
        term = gate * _dot(y.astype(BF16), wbr_ref[i])
        merged = term if merged is None else merged + term
    o_ref[...] = x + _bdot(merged, wout_ref[...])


def _merge(x2d, ya, yb, yc, yd, lp, tm):
    n = x2d.shape[0]
    row = lambda w: pl.BlockSpec((tm, w), lambda i: (i, 0))
    ya_lo, ya_hi = ya
    return pl.pallas_call(
        _merge_kernel,
        grid=(n // tm,),
        in_specs=[row(D_MODEL), row(128), row(128), row(256), row(256), row(256),
                  _const_spec((1, D_MODEL)), _layer_spec((D_MODEL, N_BRANCH * D_MODEL), lp["layer"]),
                  _const_spec((256, 256)), _const_spec((1, 256)),
                  _const_spec((N_BRANCH, 256, D_MODEL)), _layer_spec((D_MODEL, D_MODEL), lp["layer"])],
        out_specs=row(D_MODEL),
        out_shape=jax.ShapeDtypeStruct((n, D_MODEL), F32),
        compiler_params=_cparams("parallel"),
        name="merge",
    )(x2d, ya_lo, ya_hi, yb, yc, yd, lp["ln1_g"], lp["w_gate"], lp["s5_w_glu"], lp["s5_b_glu"],
      lp["w_br"], lp["w_out"])


def _mlp_kernel(x_ref, g_ref, w1_ref, w2_ref, o_ref):
    x = x_ref[...]
    hb = _rms_rows(x, g_ref[...]).astype(BF16)
    z = jnp.maximum(_dot(hb, w1_ref[...]), 0.0)
    o_ref[...] = x + _bdot(z * z, w2_ref[...])


def _mlp(x2d, lp, tm):
    n = x2d.shape[0]
    row = pl.BlockSpec((tm, D_MODEL), lambda i: (i, 0))
    single = pl.Buffered(1)
    return pl.pallas_call(
        _mlp_kernel,
        grid=(n // tm,),
        in_specs=[row, _const_spec((1, D_MODEL)),
                  _layer_spec((D_MODEL, D_FF), lp["layer"], pipeline_mode=single),
                  _layer_spec((D_FF, D_MODEL), lp["layer"], pipeline_mode=single)],
        out_specs=row,
        out_shape=jax.ShapeDtypeStruct((n, D_MODEL), F32),
        compiler_params=_cparams("parallel"),
        name="mlp",
    )(x2d, lp["ln2_g"], lp["w_ff1"], lp["w_ff2"])


def _s5_matrices(p, chunk):
    hp = dict(precision=HIGHEST)
    dt = jnp.exp(p["s5_log_dt"])[:, None]
    ar, ai = p["s5_a_re"], p["s5_a_im"]
    mag = jnp.exp(ar * dt)
    abar_re = mag * jnp.cos(ai * dt)
    abar_im = mag * jnp.sin(ai * dt)
    den = ar * ar + ai * ai
    nr = abar_re - 1.0
    f_re = (nr * ar + abar_im * ai) / den
    f_im = (abar_im * ar - nr * ai) / den
    br, bi = p["s5_b_re"], p["s5_b_im"]
    bbar_re = f_re[..., None] * br - f_im[..., None] * bi
    bbar_im = f_re[..., None] * bi + f_im[..., None] * br
    pw_re = [jnp.ones_like(abar_re)]
    pw_im = [jnp.zeros_like(abar_re)]
    for _ in range(chunk):
        r, i = pw_re[-1], pw_im[-1]
        pw_re.append(r * abar_re - i * abar_im)
        pw_im.append(r * abar_im + i * abar_re)
    pw_re = jnp.stack(pw_re)
    pw_im = jnp.stack(pw_im)
    cr, ci = p["s5_c_re"], p["s5_c_im"]
    ca_re = cr[None] * pw_re[:, :, None, :] - ci[None] * pw_im[:, :, None, :]
    ca_im = cr[None] * pw_im[:, :, None, :] + ci[None] * pw_re[:, :, None, :]
    def table(t):
        lead = t.shape[:-3]
        t = jnp.moveaxis(t, -1, -3)
        return t.reshape(lead + (t.shape[-3], -1))

    kern = (jnp.einsum("tgop,gpi->tgoi", ca_re[:chunk], bbar_re, **hp)
            - jnp.einsum("tgop,gpi->tgoi", ca_im[:chunk], bbar_im, **hp))
    rev_re = pw_re[:chunk][::-1]
    rev_im = pw_im[:chunk][::-1]
    ab_re = rev_re[..., None] * bbar_re[None] - rev_im[..., None] * bbar_im[None]
    ab_im = rev_re[..., None] * bbar_im[None] + rev_im[..., None] * bbar_re[None]
    a_pow = jnp.stack([pw_re[chunk].reshape(-1), pw_im[chunk].reshape(-1)])
    return dict(lag=table(kern),
                xs=jnp.concatenate([table(ab_re), table(ab_im)], axis=-1),
                hy=jnp.stack([table(ca_re[1:]), -table(ca_im[1:])], axis=1),
                a_pow=a_pow, d_row=p["s5_d"][None, :])


def _s5_expand(lag_ref, xs_ref, hy_ref, wxs_scr, win_scr, why_scr):
    chunk = lag_ref.shape[0]
    w = BRANCH_WIDTH

    def group_of(shape, axis, per_group, wrap=None):
        idx = lax.broadcasted_iota(jnp.int32, shape, axis)
        if wrap is not None:
            idx = idx % wrap
        return idx // per_group

    same_ii = group_of((w, w), 0, S5_GROUP) == group_of((w, w), 1, S5_GROUP)
    same_is = (group_of((w, 2 * S5_STATE), 0, S5_GROUP)
               == group_of((w, 2 * S5_STATE), 1, S5_P, wrap=S5_STATE))
    same_si = group_of((S5_STATE, w), 0, S5_P) == group_of((S5_STATE, w), 1, S5_GROUP)
    down = lambda t: jnp.concatenate([t] * S5_GROUPS, axis=0)
    lags = [jnp.where(same_ii, down(lag_ref[tau]), 0.0).astype(BF16) for tau in range(chunk)]
    zero = jnp.zeros((w, w), BF16)
    for s in range(chunk):
        wxs_scr[s * w:(s + 1) * w, :] = jnp.where(same_is, down(xs_ref[s]), 0.0).astype(BF16)
        win_scr[s * w:(s + 1) * w, :] = jnp.concatenate(
            [lags[t - s] if t >= s else zero for t in range(chunk)], axis=1)
        for part in range(2):
            why_scr[part * S5_STATE:(part + 1) * S5_STATE, s * w:(s + 1) * w] = jnp.where(
                same_si, down(hy_ref[s, part]), 0.0).astype(BF16)


def _gelu(y):
    c = math.sqrt(2.0 / math.pi)
    return 0.5 * y * (1.0 + jnp.tanh(c * (y + 0.044715 * (y * y * y))))


def _s5_kernel(u_lo_ref, u_hi_ref, lag_ref, xs_ref, hy_ref, apow_ref, d_ref, y_lo_ref, y_hi_ref, hfin_ref,
               wxs_scr, win_scr, why_scr, xs_scr, hs_scr):
    chunk = lag_ref.shape[0]
    rows = u_lo_ref.shape[0] // chunk
    half = BRANCH_WIDTH // 2

    @pl.when(pl.program_id(0) == 0)
    def _():
        _s5_expand(lag_ref, xs_ref, hy_ref, wxs_scr, win_scr, why_scr)

    u = jnp.concatenate([ref[pl.ds(s, rows, stride=chunk), :]
                         for s in range(chunk) for ref in (u_lo_ref, u_hi_ref)], axis=1)
    ub = u.astype(BF16)
    xs_scr[...] = _dot(ub, wxs_scr[...])
    a_re = apow_ref[0:1, :]
    a_im = apow_ref[1:2, :]

    def step(r, carry):
        hr, hi = carry
        hs_scr[pl.ds(r, 1), :] = jnp.concatenate([hr, hi], axis=-1)
        x = xs_scr[pl.ds(r, 1), :]
        nhr = a_re * hr - a_im * hi + x[:, :S5_STATE]
        nhi = a_re * hi + a_im * hr + x[:, S5_STATE:]
        return nhr, nhi

    zero = jnp.zeros((1, S5_STATE), F32)
    hr, hi = lax.fori_loop(0, rows, step, (zero, zero))
    hfin_ref[0] = jnp.concatenate([hr, hi], axis=-1)
    y = _dot(ub, win_scr[...]) + _dot(hs_scr[...].astype(BF16), why_scr[...])
    for t in range(chunk):
        cols = slice(t * BRANCH_WIDTH, (t + 1) * BRANCH_WIDTH)
        y_t = _gelu(y[:, cols] + d_ref[...] * u[:, cols])
        y_lo_ref[pl.ds(t, rows, stride=chunk), :] = y_t[:, :half]
        y_hi_ref[pl.ds(t, rows, stride=chunk), :] = y_t[:, half:]


def _s5_scratch(chunk):
    width = chunk * BRANCH_WIDTH
    return [pltpu.VMEM((width, 2 * S5_STATE), BF16), pltpu.VMEM((width, width), BF16),
            pltpu.VMEM((2 * S5_STATE, width), BF16)]


def _s5_prompt(u_lo, u_hi, sm, batch, seq):
    c = sm["lag"].shape[0]
    rows = seq // c
    half = BRANCH_WIDTH // 2
    tok = pl.BlockSpec((seq, half), lambda b: (b, 0))
    y_lo, y_hi, hfin = pl.pallas_call(
        _s5_kernel,
        grid=(batch,),
        in_specs=[tok, tok, _const_spec(sm["lag"].shape), _const_spec(sm["xs"].shape),
                  _const_spec(sm["hy"].shape), _const_spec((2, S5_STATE)), _const_spec((1, BRANCH_WIDTH))],
        out_specs=[tok, tok, pl.BlockSpec((1, 1, 2 * S5_STATE), lambda b: (b, 0, 0))],
        out_shape=[jax.ShapeDtypeStruct((batch * seq, half), F32),
                   jax.ShapeDtypeStruct((batch * seq, half), F32),
                   jax.ShapeDtypeStruct((batch, 1, 2 * S5_STATE), F32)],
        scratch_shapes=_s5_scratch(c) + [pltpu.VMEM((rows, 2 * S5_STATE), F32),
                                         pltpu.VMEM((rows, 2 * S5_STATE), F32)],
        compiler_params=_cparams("arbitrary"),
        name="s5_prompt",
    )(u_lo, u_hi, sm["lag"], sm["xs"], sm["hy"], sm["a_pow"], sm["d_row"])
    return (y_lo, y_hi), hfin.reshape(batch, 2 * S5_STATE)


GLA_ROWS = 256


def _gla_kernel(q_ref, k_ref, v_ref, g_ref, r_ref, tri_ref, ones_ref, bmask_ref, gn_ref,
                y_ref, sfin_ref, st_scr):
    step = pl.program_id(1)
    sub = GLA_SUB
    rows = q_ref.shape[0]
    ns = rows // sub

    @pl.when(step == 0)
    def _():
        st_scr[...] = jnp.zeros_like(st_scr)

    ones = ones_ref[...]
    bmask = bmask_ref[...]
    split = lambda a: a.reshape(ns, sub, BRANCH_WIDTH)
    bc = split(jnp.dot(tri_ref[...], g_ref[...], precision=HIGHEST, preferred_element_type=F32))
    q = split(q_ref[...] * (HEAD_DIM ** -0.5))
    k = split(k_ref[...])
    v = split(v_ref[...])

    row = lax.broadcasted_iota(jnp.int32, (ns, sub, BRANCH_WIDTH), 1)
    parts = []
    for j in range(sub):
        e = jnp.exp(jnp.minimum(bc - bc[:, j:j + 1, :], 0.0))
        parts.append(jnp.where(row >= j, q * k[:, j:j + 1, :] * e, 0.0))
    att = _dot_sel(jnp.concatenate(parts, axis=1).reshape(ns * sub * sub, BRANCH_WIDTH), ones)
    att = att.reshape(ns, sub * sub, BRANCH_WIDTH)
    o = att[:, 0:sub] * v[:, 0:1, :]
    for j in range(1, sub):
        o = o + att[:, j * sub:(j + 1) * sub] * v[:, j:j + 1, :]

    last = bc[:, sub - 1:sub, :]
    qt = (q * jnp.exp(bc)).astype(BF16)
    kt = (k * jnp.exp(last - bc)).astype(BF16)
    vb = v.astype(BF16)
    decay = jnp.exp(last)
    outer = [bmask * _dot_tn(vb[s], kt[s]) for s in range(ns)]
    st = st_scr[...]
    inter = []
    for s in range(ns):
        inter.append(_dot_nt(qt[s], st.astype(BF16)))
        st = st * decay[s] + outer[s]
    st_scr[...] = st
    o = o.reshape(rows, BRANCH_WIDTH) + jnp.concatenate(inter, axis=0)
    ms = _dot_sel(o * o, ones) * (1.0 / HEAD_DIM)
    y_ref[...] = o * lax.rsqrt(ms + EPS) * gn_ref[...] * _silu(r_ref[...])

    @pl.when(step == pl.num_programs(1) - 1)
    def _():
        sfin_ref[0] = st


def _sub_tril(rows, sub):
    i = np.arange(rows)
    return jnp.asarray((i[:, None] // sub == i[None, :] // sub) & (i[None, :] <= i[:, None]), F32)


def _unpack_state_t(st):
    b = st.shape[0]
    st = st.reshape(b, HEADS, HEAD_DIM, HEADS, HEAD_DIM)
    diag = jnp.stack([st[:, h, :, h, :] for h in range(HEADS)], axis=1)
    return diag.transpose(0, 1, 3, 2)


def _gla_prompt(q, k, v, g, r, lp, batch, seq):
    rows = min(GLA_ROWS, seq)
    nsteps = seq // rows
    blk = pl.BlockSpec((rows, BRANCH_WIDTH), lambda b, c: (b * nsteps + c, 0))
    hm = np.arange(BRANCH_WIDTH) // HEAD_DIM
    bmask = jnp.asarray(hm[:, None] == hm[None, :], F32)
    y, sfin = pl.pallas_call(
        _gla_kernel,
        grid=(batch, nsteps),
        in_specs=[blk, blk, blk, blk, blk, _const_spec((rows, rows)),
                  _const_spec((BRANCH_WIDTH, BRANCH_WIDTH)), _const_spec((BRANCH_WIDTH, BRANCH_WIDTH)),
                  _const_spec((1, BRANCH_WIDTH))],
        out_specs=[blk, pl.BlockSpec((1, BRANCH_WIDTH, BRANCH_WIDTH), lambda b, c: (b, 0, 0))],
        out_shape=[jax.ShapeDtypeStruct((batch * seq, BRANCH_WIDTH), F32),
                   jax.ShapeDtypeStruct((batch, BRANCH_WIDTH, BRANCH_WIDTH), F32)],
        scratch_shapes=[pltpu.VMEM((BRANCH_WIDTH, BRANCH_WIDTH), F32)],
        compiler_params=_cparams("parallel", "arbitrary"),
        name="gla_prompt",
    )(q, k, v, g, r, _sub_tril(rows, GLA_SUB), _head_ones(), bmask, lp["gla_norm"])
    return y, _unpack_state_t(sfin)


CONV_PAD = 8


GDN_ROWS = 512


def _block_diag(x, ones):
    return jnp.concatenate([x] * HEADS, axis=0) * ones


def _unit_lower_inverses(ns, eye, ones):
    c = ns[0].shape[0]
    every = range(len(ns))
    invs = [eye - n for n in ns]
    pws = list(ns)
    for _ in range(int(math.log2(c)) - 1):
        pbs = [pw.astype(BF16) for pw in pws]
        pws = [_dot(pb, _block_diag(pb, ones)) for pb in pbs]
        invs = [invs[i] + _dot(invs[i].astype(BF16), _block_diag(pws[i].astype(BF16), ones)) for i in every]
    inv_parts = [_split2(inv) for inv in invs]
    n_parts = [_split2(n) for n in ns]
    prods = [_dot(jnp.concatenate(n_parts[i], axis=0), _block_diag(inv_parts[i][0], ones)) for i in every]
    cross = [_dot(n_parts[i][0], _block_diag(inv_parts[i][1], ones)) for i in every]
    resids = [eye - invs[i] - (prods[i][:c] + prods[i][c:] + cross[i]) for i in every]
    return [invs[i] + _dot(inv_parts[i][0], _block_diag(resids[i].astype(BF16), ones)) for i in every]


def _gdn_kernel(x_ref, b_ref, g_ref, z_ref, cw_ref, tri_ref, ones_ref, eye_ref, gn_ref,
                y_ref, sfin_ref, conv_ref, s_scr, buf_scr):
    step = pl.program_id(1)
    rows = x_ref.shape[0]
    c = min(GDN_CHUNK, rows)

    @pl.when(step == 0)
    def _():
        s_scr[...] = jnp.zeros_like(s_scr)
        buf_scr[0:CONV_PAD, :] = jnp.zeros((CONV_PAD, GDN_QKV), F32)

    x = x_ref[...]
    buf_scr[CONV_PAD:CONV_PAD + rows, :] = x
    conv = cw_ref[GDN_CONV - 1:GDN_CONV, :] * x
    for w in range(GDN_CONV - 1):
        lag = GDN_CONV - 1 - w
        conv = conv + cw_ref[w:w + 1, :] * buf_scr[CONV_PAD - lag:CONV_PAD - lag + rows, :]
    tail = buf_scr[rows:rows + CONV_PAD, :]
    buf_scr[0:CONV_PAD, :] = tail
    conv_ref[0] = tail
    qkv = _silu(conv)
    ones = ones_ref[...]
    onesf = ones.astype(F32)
    eye = eye_ref[...]
    q = qkv[:, 0:BRANCH_WIDTH]
    k = qkv[:, BRANCH_WIDTH:2 * BRANCH_WIDTH]
    v = qkv[:, 2 * BRANCH_WIDTH:]
    q = q * lax.rsqrt(_dot_sel(q * q, ones) + EPS) * (HEAD_DIM ** -0.5)
    k = k * lax.rsqrt(_dot_sel(k * k, ones) + EPS)
    beta = b_ref[...]
    gc = jnp.dot(tri_ref[...], g_ref[...], precision=HIGHEST, preferred_element_type=F32)
    gam = jnp.exp(gc)
    ri = lax.broadcasted_iota(jnp.int32, (c, BRANCH_WIDTH), 0)
    cj = lax.broadcasted_iota(jnp.int32, (c, BRANCH_WIDTH), 1) % HEAD_DIM

    every = range(rows // c)
    sls = [slice(n * c, (n + 1) * c) for n in every]
    kcs = [k[rs].astype(BF16) for rs in sls]
    kqs = [_dot_nt(jnp.concatenate([k[rs], q[rs]], axis=0).astype(BF16), _block_diag(kcs[n], ones))
           for n, rs in enumerate(sls)]
    decs = []
    for rs in sls:
        grow = jnp.sum(gc[rs] * eye, axis=0, keepdims=True)
        decs.append(jnp.where(ri >= cj, jnp.exp(jnp.minimum(gc[rs] - grow, 0.0)), 0.0))
    invs = _unit_lower_inverses(
        [jnp.where(ri > cj, beta[rs] * decs[n] * kqs[n][:c], 0.0) for n, rs in enumerate(sls)], eye, ones)
    invbs = [inv.astype(BF16) for inv in invs]
    ws_m = [_dot(invbs[n], _block_diag((beta[rs] * gam[rs] * k[rs]).astype(BF16), ones))
            for n, rs in enumerate(sls)]
    u0s = [_dot(invbs[n], _block_diag((beta[rs] * v[rs]).astype(BF16), ones)) for n, rs in enumerate(sls)]
    aqks = [(decs[n] * kqs[n][c:]).astype(BF16) for n in every]
    lhs = [jnp.concatenate([ws_m[n], gam[rs] * q[rs]], axis=0).astype(BF16) for n, rs in enumerate(sls)]
    glasts = [gc[rs][c - 1:c, :] for rs in sls]
    kds = [(k[rs] * jnp.exp(glasts[n] - gc[rs])).astype(BF16) for n, rs in enumerate(sls)]

    s = s_scr[...]
    outs = []
    for n in every:
        ws = _dot(lhs[n], s.astype(BF16))
        ub = (u0s[n] - ws[:c]).astype(BF16)
        outs.append(ws[c:] + _dot(aqks[n], _block_diag(ub, ones)))
        s = jnp.exp(glasts[n]) * s + onesf * _dot_tn(kds[n], ub)
    s_scr[...] = s
    o = jnp.concatenate(outs, axis=0)
    ms = _dot_sel(o * o, ones) * (1.0 / HEAD_DIM)
    y_ref[...] = o * lax.rsqrt(ms + EPS) * gn_ref[...] * _silu(z_ref[...])

    @pl.when(step == pl.num_programs(1) - 1)
    def _():
        sfin_ref[0] = s


def _unpack_state(st):
    b = st.shape[0]
    st = st.reshape(b, HEADS, HEAD_DIM, HEADS, HEAD_DIM)
    return jnp.stack([st[:, h, :, h, :] for h in range(HEADS)], axis=1)


def _gdn_prompt(x, beta, g, z, lp, batch, seq):
    rows = min(GDN_ROWS, seq)
    c = min(GDN_CHUNK, rows)
    nsteps = seq // rows
    blk = lambda w: pl.BlockSpec((rows, w), lambda b, s: (b * nsteps + s, 0))
    eye = jnp.asarray(np.tile(np.eye(c, dtype=np.float32), (1, HEADS)))
    y, sfin, conv = pl.pallas_call(
        _gdn_kernel,
        grid=(batch, nsteps),
        in_specs=[blk(GDN_QKV), blk(256), blk(256), blk(256), _const_spec((CONV_PAD, GDN_QKV)),
                  _const_spec((rows, rows)), _const_spec((256, 256)), _const_spec((c, 256)),
                  _const_spec((1, 256))],
        out_specs=[blk(256),
                   pl.BlockSpec((1, 256, 256), lambda b, s: (b, 0, 0)),
                   pl.BlockSpec((1, CONV_PAD, GDN_QKV), lambda b, s: (b, 0, 0))],
        out_shape=[jax.ShapeDtypeStruct((batch * seq, 256), F32),
                   jax.ShapeDtypeStruct((batch, 256, 256), F32),
                   jax.ShapeDtypeStruct((batch, CONV_PAD, GDN_QKV), F32)],
        scratch_shapes=[pltpu.VMEM((256, 256), F32),
                        pltpu.VMEM((CONV_PAD + rows, GDN_QKV), F32)],
        compiler_params=_cparams("parallel", "arbitrary"),
        name="gdn_prompt",
    )(x, beta, g, z, lp["gdn_conv_w"], _sub_tril(rows, c), _head_ones(), eye, lp["gdn_norm"])
    return y, _unpack_state(sfin), conv[:, CONV_PAD - (GDN_CONV - 1):, :]


HALF_ROT = ROT_DIM // 2


def _rope_tables(pos):
    inv = ROPE_THETA ** (-jnp.arange(HALF_ROT, dtype=F32) / HALF_ROT)
    ang = pos.astype(F32)[:, None] * inv[None, :]
    cos, sin = jnp.cos(ang), jnp.sin(ang)
    n = pos.shape[0]
    rest = HEAD_DIM - ROT_DIM
    head = lambda a, b, fill: jnp.concatenate([a, b, jnp.full((n, rest), fill, F32)], axis=1)
    zero = jnp.zeros_like(sin)
    tabs = [head(cos, cos, 1.0), head(-sin, zero, 0.0), head(zero, sin, 0.0)]
    return jnp.stack([jnp.tile(t, (1, HEADS)) for t in tabs])


def _qk_norm_rope(x, gain, tab_ref, ones):
    y = x * lax.rsqrt(_dot_sel(x * x, ones) * (1.0 / HEAD_DIM) + EPS) * gain
    up = pltpu.roll(y, BRANCH_WIDTH - HALF_ROT, 1)
    down = pltpu.roll(y, HALF_ROT, 1)
    return y * tab_ref[0] + up * tab_ref[1] + down * tab_ref[2]


def _moba_prep_kernel(q_ref, k_ref, v_ref, tab_ref, qg_ref, kg_ref, ones_ref,
                      kt_ref, vtt_ref, qh_ref, kh_ref, vt_ref, km_ref):
    ones = ones_ref[...]
    rows = q_ref.shape[0]
    nblk = rows // MOBA_BLOCK
    mq = _qk_norm_rope(q_ref[...], qg_ref[...], tab_ref, ones) * (HEAD_DIM ** -0.5)
    mk = _qk_norm_rope(k_ref[...], kg_ref[...], tab_ref, ones)
    kmean = jnp.mean(mk.reshape(nblk, MOBA_BLOCK, BRANCH_WIDTH), axis=1)
    kt = mk.T
    vt = v_ref[...].T
    for h in range(HEADS):
        sl = slice(h * HEAD_DIM, (h + 1) * HEAD_DIM)
        qh_ref[0, h] = mq[:, sl]
        km_ref[0, h] = kmean[:, sl]
        kt_ref[0, h] = kt[sl, :]
        vtt_ref[0, h] = vt[sl, :]
        for j in range(nblk):
            rs = slice(j * MOBA_BLOCK, (j + 1) * MOBA_BLOCK)
            kh_ref[0, h, j] = mk[rs, sl].astype(BF16)
            vt_ref[0, h, j] = vt[sl, rs].astype(BF16)


def _moba_prep(q, k, v, tabs, lp, batch, seq):
    rows = min(8 * MOBA_BLOCK, seq)
    nsteps = seq // rows
    nblk = rows // MOBA_BLOCK
    nb = seq // MOBA_BLOCK
    blk = pl.BlockSpec((rows, 256), lambda b, r: (b * nsteps + r, 0))
    cache_rows = pl.BlockSpec((1, HEADS, HEAD_DIM, rows), lambda b, r: (b, 0, 0, r))
    return pl.pallas_call(
        _moba_prep_kernel,
        grid=(batch, nsteps),
        in_specs=[blk, blk, blk, pl.BlockSpec((3, rows, 256), lambda b, r: (0, r, 0)),
                  _const_spec((1, 256)), _const_spec((1, 256)), _const_spec((256, 256))],
        out_specs=[cache_rows, cache_rows,
                   pl.BlockSpec((1, HEADS, rows, HEAD_DIM), lambda b, r: (b, 0, r, 0)),
                   pl.BlockSpec((1, HEADS, nblk, MOBA_BLOCK, HEAD_DIM), lambda b, r: (b, 0, r, 0, 0)),
                   pl.BlockSpec((1, HEADS, nblk, HEAD_DIM, MOBA_BLOCK), lambda b, r: (b, 0, r, 0, 0)),
                   pl.BlockSpec((1, HEADS, nblk, HEAD_DIM), lambda b, r: (b, 0, r, 0))],
        out_shape=[jax.ShapeDtypeStruct((batch, HEADS, HEAD_DIM, seq), F32),
                   jax.ShapeDtypeStruct((batch, HEADS, HEAD_DIM, seq), F32),
                   jax.ShapeDtypeStruct((batch, HEADS, seq, HEAD_DIM), F32),
                   jax.ShapeDtypeStruct((batch, HEADS, nb, MOBA_BLOCK, HEAD_DIM), BF16),
                   jax.ShapeDtypeStruct((batch, HEADS, nb, HEAD_DIM, MOBA_BLOCK), BF16),
                   jax.ShapeDtypeStruct((batch, HEADS, nb, HEAD_DIM), F32)],
        compiler_params=_cparams("parallel", "parallel"),
        name="moba_prep",
    )(q, k, v, tabs, lp["moba_q_norm"], lp["moba_k_norm"], _head_ones())


def _moba_attn_kernel(qh_ref, kh_ref, vt_ref, km_ref, o_ref, sel_scr):
    qb = pl.program_id(1)
    nb = km_ref.shape[2]
    blk = MOBA_BLOCK
    blk_id = lax.broadcasted_iota(jnp.int32, (nb, blk), 0)
    kpos = lax.broadcasted_iota(jnp.int32, (blk, blk), 0)
    qpos = lax.broadcasted_iota(jnp.int32, (blk, blk), 1)
    heads = range(HEADS)
    qs = [qh_ref[0, h].astype(BF16) for h in heads]

    def scores(j):
        return [_dot_nt(kh_ref[0, h, j], qs[h]) for h in heads]

    gates = [lax.dot_general(km_ref[0, h], qh_ref[0, h], (((1,), (1,)), ((), ())), precision=HIGHEST,
                             preferred_element_type=F32) for h in heads]
    own = scores(qb)
    for h in heads:
        gate = jnp.where(blk_id < qb, gates[h], NEG)
        taken = jnp.zeros((nb, blk), jnp.bool_)
        for _ in range(min(MOBA_TOPK, nb)):
            best = jnp.max(gate, axis=0, keepdims=True)
            idx = jnp.min(jnp.where(gate == best, blk_id, nb), axis=0, keepdims=True)
            hit = blk_id == idx
            taken = jnp.logical_or(taken, hit)
            gate = jnp.where(hit, -jnp.inf, gate)
        sel_scr[h] = jnp.where(jnp.logical_and(taken, blk_id < qb), 1.0, 0.0)

    ms, ls, ps = [], [], []
    for h in heads:
        s = jnp.where(kpos <= qpos, own[h], NEG)
        m = jnp.max(s, axis=0, keepdims=True)
        p = jnp.exp(s - m)
        ms.append(m)
        ls.append(jnp.sum(p, axis=0, keepdims=True))
        ps.append(p.astype(BF16))
    accs = [_dot(vt_ref[0, h, qb], ps[h]) for h in heads]

    def body(j, carry):
        ms, ls, accs = carry
        ss = scores(j)
        new_m, new_l, alphas, ps = [], [], [], []
        for h in heads:
            s = jnp.where(sel_scr[h, pl.ds(j, 1), :] > 0.0, ss[h], NEG)
            m_new = jnp.maximum(ms[h], jnp.max(s, axis=0, keepdims=True))
            alpha = jnp.exp(ms[h] - m_new)
            p = jnp.exp(s - m_new)
            new_m.append(m_new)
            new_l.append(alpha * ls[h] + jnp.sum(p, axis=0, keepdims=True))
            alphas.append(alpha)
            ps.append(p.astype(BF16))
        new_acc = [alphas[h] * accs[h] + _dot(vt_ref[0, h, j], ps[h]) for h in heads]
        return tuple(new_m), tuple(new_l), tuple(new_acc)

    ms, ls, accs = lax.fori_loop(0, qb, body, (tuple(ms), tuple(ls), tuple(accs)))
    o_ref[...] = jnp.concatenate([accs[h] / ls[h] for h in heads], axis=0).T


def _moba_prompt(qh, kh, vt, km, batch, seq):
    nb = seq // MOBA_BLOCK
    return pl.pallas_call(
        _moba_attn_kernel,
        grid=(batch, nb),
        in_specs=[pl.BlockSpec((1, HEADS, MOBA_BLOCK, HEAD_DIM), lambda b, i: (b, 0, i, 0)),
                  pl.BlockSpec((1, HEADS, nb, MOBA_BLOCK, HEAD_DIM), lambda b, i: (b, 0, 0, 0, 0)),
                  pl.BlockSpec((1, HEADS, nb, HEAD_DIM, MOBA_BLOCK), lambda b, i: (b, 0, 0, 0, 0)),
                  pl.BlockSpec((1, HEADS, nb, HEAD_DIM), lambda b, i: (b, 0, 0, 0))],
        out_specs=pl.BlockSpec((MOBA_BLOCK, 256), lambda b, i: (b * nb + i, 0)),
        out_shape=jax.ShapeDtypeStruct((batch * seq, 256), F32),
        scratch_shapes=[pltpu.VMEM((HEADS, nb, MOBA_BLOCK), F32)],
        compiler_params=_cparams("parallel", "arbitrary"),
        name="moba_prompt",
    )(qh, kh, vt, km)


def _s5_step_kernel(u_ref, h0_ref, lag_ref, xs_ref, hy_ref, apow_ref, d_ref, y_ref, h_ref,
                    wxs_ref, win_ref, why_ref):
    _s5_expand(lag_ref, xs_ref, hy_ref, wxs_ref, win_ref, why_ref)
    u = u_ref[...]
    ub = u.astype(BF16)
    h0 = h0_ref[...]
    xs = _dot(ub, wxs_ref[...])
    a_re = apow_ref[0:1, :]
    a_im = apow_ref[1:2, :]
    hr0 = h0[:, :S5_STATE]
    hi0 = h0[:, S5_STATE:]
    hr = a_re * hr0 - a_im * hi0 + xs[:, :S5_STATE]
    hi = a_re * hi0 + a_im * hr0 + xs[:, S5_STATE:]
    h_ref[...] = jnp.concatenate([hr, hi], axis=-1)
    y = _dot(ub, win_ref[...]) + _dot(h0.astype(BF16), why_ref[...]) + d_ref[...] * u
    y_ref[...] = _gelu(y)


def _s5_step(u, h0, sm):
    n = u.shape[0]
    return pl.pallas_call(
        _s5_step_kernel,
        out_shape=[jax.ShapeDtypeStruct((n, BRANCH_WIDTH), F32),
                   jax.ShapeDtypeStruct((n, 2 * S5_STATE), F32)],
        scratch_shapes=_s5_scratch(1),
        compiler_params=pltpu.CompilerParams(vmem_limit_bytes=VMEM_LIMIT),
        name="s5_step",
    )(u, h0, sm["lag"], sm["xs"], sm["hy"], sm["a_pow"], sm["d_row"])


def _expand_mats():
    idx = np.arange(HEAD_DIM * HEAD_DIM)
    rep = (np.arange(HEAD_DIM)[:, None] == idx[None, :] // HEAD_DIM)
    til = (np.arange(HEAD_DIM)[:, None] == idx[None, :] % HEAD_DIM)
    return jnp.asarray(rep, BF16), jnp.asarray(til, BF16), jnp.asarray(til.T, BF16)


def _gla_step_kernel(q_ref, k_ref, v_ref, g_ref, r_ref, s0_ref, rep_ref, til_ref, tilt_ref, gn_ref,
                     y_ref, s_ref):
    rep = rep_ref[...]
    eg = _dot_sel_exact(jnp.exp(g_ref[...]), rep)
    kr = _dot_sel_exact(k_ref[...], rep)
    qr = _dot_sel_exact(q_ref[...] * (HEAD_DIM ** -0.5), rep)
    vt = _dot_sel_exact(v_ref[...], til_ref[...])
    s = eg * s0_ref[...] + kr * vt
    s_ref[...] = s
    o = _dot_sel(qr * s, tilt_ref[...])
    y_ref[...] = _rms_rows(o, gn_ref[...]) * _silu(r_ref[...])


def _gdn_conv_step_kernel(x_ref, c0_ref, cw_ref, qkv_ref, cnew_ref):
    x = x_ref[...]
    c0 = c0_ref[...]
    conv = cw_ref[GDN_CONV - 1:GDN_CONV, :] * x
    for w in range(GDN_CONV - 1):
        conv = conv + cw_ref[w:w + 1, :] * c0[:, w * GDN_QKV:(w + 1) * GDN_QKV]
    qkv_ref[...] = _silu(conv)
    cnew_ref[...] = jnp.concatenate([c0[:, GDN_QKV:], x], axis=-1)


def _gdn_step_kernel(q_ref, k_ref, v_ref, b_ref, g_ref, z_ref, s0_ref, rep_ref, til_ref, tilt_ref,
                     gn_ref, y_ref, s_ref):
    q = q_ref[...]
    k = k_ref[...]
    q = q * lax.rsqrt(jnp.sum(q * q, axis=-1, keepdims=True) + EPS) * (HEAD_DIM ** -0.5)
    k = k * lax.rsqrt(jnp.sum(k * k, axis=-1, keepdims=True) + EPS)
    beta = b_ref[:, 0:1]
    gam = jnp.exp(g_ref[:, 0:1])
    rep = rep_ref[...]
    tilt = tilt_ref[...]
    kr = _dot_sel_exact(k, rep)
    qr = _dot_sel_exact(q, rep)
    s0 = s0_ref[...]
    ks = _dot_sel(kr * s0, tilt)
    qs = _dot_sel(qr * s0, tilt)
    u = v_ref[...] - gam * ks
    qk = jnp.sum(q * k, axis=-1, keepdims=True)
    o = gam * qs + (beta * qk) * u
    s_ref[...] = gam * s0 + kr * _dot_sel_exact(beta * u, til_ref[...])
    y_ref[...] = _rms_rows(o, gn_ref[...]) * _silu(z_ref[...])


def _whole_call(kernel, out_shape, name, *args):
    return pl.pallas_call(kernel, out_shape=out_shape, name=name,
                          compiler_params=pltpu.CompilerParams(vmem_limit_bytes=VMEM_LIMIT))(*args)


def _moba_qk_step_kernel(q_ref, k_ref, tab_ref, qg_ref, kg_ref, ones_ref, mq_ref, mk_ref):
    ones = ones_ref[...]
    mq_ref[...] = _qk_norm_rope(q_ref[...], qg_ref[...], tab_ref, ones) * (HEAD_DIM ** -0.5)
    mk_ref[...] = _qk_norm_rope(k_ref[...], kg_ref[...], tab_ref, ones)


SELECT_PAGES = 16
SEL_ROWS = 8


def _moba_select_kernel(pt_ref, q_ref, *refs):
    pages = refs[:SELECT_PAGES]
    sel_ref = refs[SELECT_PAGES]
    gate_scr = refs[SELECT_PAGES + 1]
    g = pl.program_id(1)
    ppb = MOBA_BLOCK // PAGE_SIZE
    qcol = jnp.broadcast_to(q_ref[0], (BRANCH_WIDTH, PAGE_SIZE)).reshape(HEADS, HEAD_DIM, PAGE_SIZE)
    blk_lane = lax.broadcasted_iota(jnp.int32, (HEADS, LANES), 1)

    @pl.when(g == 0)
    def _():
        gate_scr[...] = jnp.full((HEADS, LANES), NEG, F32)

    gate = gate_scr[...]
    for n in range(SELECT_PAGES // ppb):
        tile = pages[n * ppb][0, 0]
        for e in range(1, ppb):
            tile = tile + pages[n * ppb + e][0, 0]
        per_token = jnp.sum(tile * qcol, axis=1)
        mean = jnp.sum(per_token, axis=-1, keepdims=True) * (1.0 / MOBA_BLOCK)
        gate = jnp.where(blk_lane == g * (SELECT_PAGES // ppb) + n, mean, gate)
    gate_scr[...] = gate

    @pl.when(g == pl.num_programs(1) - 1)
    def _():
        left = gate
        sel = jnp.zeros((HEADS, LANES), jnp.int32)
        for r in range(MOBA_TOPK):
            best = jnp.max(left, axis=-1, keepdims=True)
            idx = jnp.min(jnp.where(left == best, blk_lane, LANES), axis=-1, keepdims=True)
            sel = jnp.where(blk_lane == r, idx, sel)
            left = jnp.where(blk_lane == idx, -jnp.inf, left)
        sel_ref[0] = sel


def _moba_select(mq, page_table, cache_kt, layer):
    nseq, npages = page_table.shape

    def page_spec(p):
        return pl.BlockSpec((1, 1, HEADS, HEAD_DIM, PAGE_SIZE),
                            lambda b, g, pt: (layer, pt[b, g * SELECT_PAGES + p], 0, 0, 0))

    grid_spec = pltpu.PrefetchScalarGridSpec(
        num_scalar_prefetch=1,
        grid=(nseq, npages // SELECT_PAGES),
        in_specs=[pl.BlockSpec((1, 256, 1), lambda b, g, pt: (b, 0, 0))]
        + [page_spec(p) for p in range(SELECT_PAGES)],
        out_specs=pl.BlockSpec((1, HEADS, LANES), lambda b, g, pt: (b, 0, 0)),
        scratch_shapes=[pltpu.VMEM((HEADS, LANES), F32)],
    )
    sel = pl.pallas_call(
        _moba_select_kernel,
        grid_spec=grid_spec,
        out_shape=jax.ShapeDtypeStruct((nseq, HEADS, LANES), jnp.int32),
        compiler_params=_cparams("parallel", "arbitrary"),
        name="moba_select",
    )(page_table, mq.reshape(nseq, 256, 1), *([cache_kt] * SELECT_PAGES))
    return sel[:, :, :MOBA_TOPK]


N_SEL_PAGES = MOBA_TOPK * (MOBA_BLOCK // PAGE_SIZE)


def _moba_step_kernel(sel_ref, pt_ref, q_ref, kn_ref, vn_ref, *refs):
    kp = refs[:N_SEL_PAGES]
    vp = refs[N_SEL_PAGES:2 * N_SEL_PAGES]
    o_ref = refs[2 * N_SEL_PAGES]
    q = q_ref[0, 0]
    q8 = jnp.broadcast_to(q, (SEL_ROWS, HEAD_DIM)).astype(BF16)
    logits = [_dot(q8, r[0, 0, 0].astype(BF16))[0:1] for r in kp]
    l_self = jnp.sum(q * kn_ref[0, 0], axis=-1, keepdims=True)
    m = l_self
    for lg in logits:
        m = jnp.maximum(m, jnp.max(lg, axis=-1, keepdims=True))
    p_self = jnp.exp(l_self - m)
    den = p_self
    num = p_self * vn_ref[0, 0]
    for lg, r in zip(logits, vp):
        p = jnp.exp(lg - m)
        den = den + jnp.sum(p, axis=-1, keepdims=True)
        p8 = jnp.broadcast_to(p, (SEL_ROWS, PAGE_SIZE)).astype(BF16)
        num = num + _dot_nt(p8, r[0, 0, 0].astype(BF16))[0:1]
    o_ref[0, 0] = num / den


def _moba_step(q, k_new, v_new, sel, page_table, cache_kt, cache_vt, layer):
    nseq, npages = page_table.shape
    ppb = MOBA_BLOCK // PAGE_SIZE

    def page_spec(r, e):
        def index(b, h, sel_ref, pt_ref):
            blk = sel_ref[(b * HEADS + h) * MOBA_TOPK + r]
            return (layer, pt_ref[b * npages + ppb * blk + e], h, 0, 0)
        return pl.BlockSpec((1, 1, 1, HEAD_DIM, PAGE_SIZE), index)

    row = pl.BlockSpec((1, 1, 1, HEAD_DIM), lambda b, h, s, p: (b, h, 0, 0))
    pages = [page_spec(r, e) for r in range(MOBA_TOPK) for e in range(ppb)]
    grid_spec = pltpu.PrefetchScalarGridSpec(
        num_scalar_prefetch=2,
        grid=(nseq, HEADS),
        in_specs=[row, row, row] + pages + pages,
        out_specs=row,
    )
    r4 = lambda a: a.reshape(nseq, HEADS, 1, HEAD_DIM)
    out = pl.pallas_call(
        _moba_step_kernel,
        grid_spec=grid_spec,
        out_shape=jax.ShapeDtypeStruct((nseq, HEADS, 1, HEAD_DIM), F32),
        compiler_params=_cparams("parallel", "parallel"),
        name="moba_step",
    )(sel.reshape(-1), page_table.reshape(-1), r4(q), r4(k_new), r4(v_new),
      *([cache_kt] * N_SEL_PAGES), *([cache_vt] * N_SEL_PAGES))
    return out.reshape(nseq, 256)


def _layer_params(l, w):
    tile4 = lambda a: jnp.tile(a, HEADS)[None]
    rep64 = lambda a: jnp.repeat(a, HEAD_DIM)[None]
    s5p = {k: w[k][l] for k in ("s5_a_re", "s5_a_im", "s5_log_dt", "s5_b_re", "s5_b_im",
                                "s5_c_re", "s5_c_im", "s5_d")}
    return dict(
        ln1_g=w["ln1_g"][l][None], w_in=_regroup_w_in(w["w_in"][l]),
        gla_wg=jnp.pad(w["gla_w_gate"][l], ((0, 128 - GLA_RANK), (0, 0))).astype(BF16),
        gla_bg=w["gla_b_gate"][l][None],
        gdn_alog=rep64(w["gdn_a_log"][l]), gdn_dtb=rep64(w["gdn_dt_bias"][l]),
        s5_prompt=_s5_matrices(s5p, S5_CHUNK), s5_step=_s5_matrices(s5p, 1),
        s5_w_glu=w["s5_w_glu"][l].astype(BF16), s5_b_glu=w["s5_b_glu"][l][None],
        gla_norm=tile4(w["gla_norm"][l]), gla_norm_head=w["gla_norm"][l][None],
        gdn_norm=tile4(w["gdn_norm"][l]), gdn_norm_head=w["gdn_norm"][l][None],
        gdn_conv_w=jnp.pad(w["gdn_conv_w"][l], ((0, CONV_PAD - GDN_CONV), (0, 0))),
        moba_q_norm=tile4(w["moba_q_norm"][l]), moba_k_norm=tile4(w["moba_k_norm"][l]),
        w_br=jnp.stack([w["w_br_s5"][l], w["w_br_gla"][l], w["w_br_gdn"][l], w["w_br_moba"][l]]).astype(BF16),
        ln2_g=w["ln2_g"][l][None],
        layer=l, w_gate=w["w_gate_bf16"], w_out=w["w_out_bf16"], w_ff1=w["w_ff1_bf16"], w_ff2=w["w_ff2_bf16"],
    )


PROMPT_ROWS = 512


def _prompt_layer(x2d, lp, tabs, batch, seq):
    tm = min(PROMPT_ROWS, batch * seq)
    pr = _inproj(x2d, lp, tm)
    ya, s5_fin = _s5_prompt(pr["s5_u_lo"], pr["s5_u_hi"], lp["s5_prompt"], batch, seq)
    yb, gla_s = _gla_prompt(pr["gla_q"], pr["gla_k"], pr["gla_v"], pr["gla_lr"], pr["gla_r"], lp, batch, seq)
    yc, gdn_s, conv = _gdn_prompt(pr["gdn_qkv"], pr["gdn_b"], pr["gdn_a"], pr["gdn_z"], lp, batch, seq)
    kt, vtt, qh, kh, vt, km = _moba_prep(pr["moba_q"], pr["moba_k"], pr["moba_v"], tabs, lp, batch, seq)
    yd = _moba_prompt(qh, kh, vt, km, batch, seq)
    x1 = _merge(x2d, ya, yb, yc, yd, lp, tm)
    x2 = _mlp(x1, lp, tm)
    s5 = lambda a: a.reshape(batch, S5_GROUPS, S5_P)
    states = (kt, vtt, s5(s5_fin[:, :S5_STATE]), s5(s5_fin[:, S5_STATE:]), gla_s, gdn_s, conv)
    return x2, states


def _sample_layer(x2d, lp, tabs, page_table, cache_kt, cache_vt, layer, st):
    s5_re0, s5_im0, gla0, gdn0, conv0 = st
    n = x2d.shape[0]
    pr = _inproj(x2d, lp, n)
    rows = n * HEADS
    per_head = lambda a: a.reshape(rows, HEAD_DIM)
    flat_state = lambda a: a.reshape(rows, HEAD_DIM * HEAD_DIM)
    rep, til, tilt = _expand_mats()
    sds = jax.ShapeDtypeStruct

    h0 = jnp.concatenate([s5_re0.reshape(n, S5_STATE), s5_im0.reshape(n, S5_STATE)], axis=1)
    ya, s5_new = _s5_step(jnp.concatenate([pr["s5_u_lo"], pr["s5_u_hi"]], axis=1), h0, lp["s5_step"])
    ya = (ya[:, :BRANCH_WIDTH // 2], ya[:, BRANCH_WIDTH // 2:])

    yb, gla_s = _whole_call(
        _gla_step_kernel, [sds((rows, HEAD_DIM), F32), sds((rows, HEAD_DIM * HEAD_DIM), F32)], "gla_step",
        per_head(pr["gla_q"]), per_head(pr["gla_k"]), per_head(pr["gla_v"]), per_head(pr["gla_lr"]),
        per_head(pr["gla_r"]), flat_state(gla0), rep, til, tilt, lp["gla_norm_head"])

    qkv, conv_new = _whole_call(
        _gdn_conv_step_kernel, [sds((n, GDN_QKV), F32), sds((n, (GDN_CONV - 1) * GDN_QKV), F32)],
        "gdn_conv_step", pr["gdn_qkv"], conv0.reshape(n, (GDN_CONV - 1) * GDN_QKV), lp["gdn_conv_w"])
    yc, gdn_s = _whole_call(
        _gdn_step_kernel, [sds((rows, HEAD_DIM), F32), sds((rows, HEAD_DIM * HEAD_DIM), F32)], "gdn_step",
        per_head(qkv[:, :256]), per_head(qkv[:, 256:512]), per_head(qkv[:, 512:]),
        per_head(pr["gdn_b"]), per_head(pr["gdn_a"]), per_head(pr["gdn_z"]), flat_state(gdn0),
        rep, til, tilt, lp["gdn_norm_head"])

    mq, mk = _whole_call(
        _moba_qk_step_kernel, [sds((n, 256), F32), sds((n, 256), F32)], "moba_qk_step",
        pr["moba_q"], pr["moba_k"], tabs, lp["moba_q_norm"], lp["moba_k_norm"], _head_ones())
    sel = _moba_select(mq, page_table, cache_kt, layer)
    yd = _moba_step(mq, mk, pr["moba_v"], sel, page_table, cache_kt, cache_vt, layer)

    x1 = _merge(x2d, ya, yb.reshape(n, 256), yc.reshape(n, 256), yd, lp, n)
    x2 = _mlp(x1, lp, n)
    head4 = lambda a: a.reshape(n, 1, HEADS, HEAD_DIM)
    s5 = lambda a: a.reshape(n, S5_GROUPS, S5_P)
    state4 = lambda a: a.reshape(n, HEADS, HEAD_DIM, HEAD_DIM)
    states = (head4(mk), head4(pr["moba_v"]), s5(s5_new[:, :S5_STATE]), s5(s5_new[:, S5_STATE:]),
              state4(gla_s), state4(gdn_s), conv_new.reshape(n, GDN_CONV - 1, GDN_QKV))
    return x2, states


def kernel(x_prompt, x_sample, cache_moba_k, cache_moba_v, page_table, state_s5_re, state_s5_im, state_gla, state_gdn, state_gdn_conv, ln1_g, w_in, s5_a_re, s5_a_im, s5_log_dt, s5_b_re, s5_b_im, s5_c_re, s5_c_im, s5_d, s5_w_glu, s5_b_glu, gla_w_gate, gla_b_gate, gla_norm, gdn_conv_w, gdn_a_log, gdn_dt_bias, gdn_norm, moba_q_norm, moba_k_norm, w_gate, w_br_s5, w_br_gla, w_br_gdn, w_br_moba, w_out, ln2_g, w_ff1, w_ff2):
    weights = dict(ln1_g=ln1_g, w_in=w_in, s5_a_re=s5_a_re, s5_a_im=s5_a_im, s5_log_dt=s5_log_dt,
                   s5_b_re=s5_b_re, s5_b_im=s5_b_im, s5_c_re=s5_c_re, s5_c_im=s5_c_im, s5_d=s5_d,
                   s5_w_glu=s5_w_glu, s5_b_glu=s5_b_glu, gla_w_gate=gla_w_gate, gla_b_gate=gla_b_gate,
                   gla_norm=gla_norm, gdn_conv_w=gdn_conv_w, gdn_a_log=gdn_a_log, gdn_dt_bias=gdn_dt_bias,
                   gdn_norm=gdn_norm, moba_q_norm=moba_q_norm, moba_k_norm=moba_k_norm, w_gate=w_gate,
                   w_br_s5=w_br_s5, w_br_gla=w_br_gla, w_br_gdn=w_br_gdn, w_br_moba=w_br_moba,
                   w_out=w_out, ln2_g=ln2_g, w_ff1=w_ff1, w_ff2=w_ff2)
    for name in ("w_gate", "w_out", "w_ff1", "w_ff2"):
        weights[name + "_bf16"] = weights[name].astype(BF16)
    depth = ln1_g.shape[0]
    layers = [_layer_params(l, weights) for l in range(depth)]
    batch, seq, _ = x_prompt.shape
    nseq = x_sample.shape[0]
    npages = page_table.shape[1]
    past_len = npages * PAGE_SIZE

    xp = x_prompt.reshape(batch * seq, D_MODEL)
    tabs_p = _rope_tables(jnp.arange(seq, dtype=jnp.int32))
    p_states = []
    for l in range(depth):
        xp, st = _prompt_layer(xp, layers[l], tabs_p, batch, seq)
        p_states.append(st)

    cache_kt = cache_moba_k.transpose(0, 1, 3, 4, 2)
    cache_vt = cache_moba_v.transpose(0, 1, 3, 4, 2)
    tabs_s = _rope_tables(jnp.full((1,), past_len, jnp.int32))
    xs = x_sample.reshape(nseq, D_MODEL)
    s_states = []
    for l in range(depth):
        st0 = (state_s5_re[l], state_s5_im[l], state_gla[l], state_gdn[l], state_gdn_conv[l])
        xs, st = _sample_layer(xs, layers[l], tabs_s, page_table, cache_kt, cache_vt, l, st0)
        s_states.append(st)

    stack = lambda states: [jnp.stack([s[i] for s in states]) for i in range(len(states[0]))]
    p_out = stack(p_states)
    for i in range(2):
        p_out[i] = p_out[i].transpose(0, 1, 4, 2, 3)
    return (xp.reshape(batch, seq, D_MODEL), xs.reshape(nseq, 1, D_MODEL), *p_out, *stack(s_states))
```

```python
import functools
import math

import jax
import jax.numpy as jnp
import numpy as np
from jax import lax
from jax.experimental import pallas as pl
from jax.experimental.pallas import tpu as pltpu

F32 = jnp.float32
BF16 = jnp.bfloat16

D_MODEL = 1024
N_BRANCH = 4
BRANCH_WIDTH = D_MODEL // N_BRANCH
HEADS = 4
HEAD_DIM = BRANCH_WIDTH // HEADS
S5_GROUP = 16
S5_GROUPS = BRANCH_WIDTH // S5_GROUP
S5_P = 64
S5_STATE = S5_GROUPS * S5_P
GLA_RANK = 16
GLA_TAU = 16.0
GDN_CONV = 4
GDN_QKV = 3 * BRANCH_WIDTH
MOBA_BLOCK = 256
MOBA_TOPK = 3
ROT_DIM = HEAD_DIM // 4
ROPE_THETA = 500000.0
PAGE_SIZE = 128
D_FF = 4 * D_MODEL
EPS = 1e-6
NEG = -1e30

LANES = 128
S5_CHUNK = 4
GLA_SUB = 16
GDN_CHUNK = 64
VMEM_LIMIT = 56 * 1024 * 1024

HIGHEST = lax.Precision.HIGHEST


def _cparams(*sem):
    return pltpu.CompilerParams(dimension_semantics=sem, vmem_limit_bytes=VMEM_LIMIT)


def _const_spec(shape):
    zeros = (0,) * len(shape)
    return pl.BlockSpec(shape, lambda *_: zeros)


def _layer_spec(shape, layer, **kw):
    zeros = (0,) * len(shape)
    return pl.BlockSpec((None,) + tuple(shape), lambda *_: (layer,) + zeros, **kw)


def _dot(a, b):
    return jnp.dot(a, b, preferred_element_type=F32)


def _dot_nt(a, b):
    return lax.dot_general(a, b, (((1,), (1,)), ((), ())), preferred_element_type=F32)


def _dot_tn(a, b):
    return lax.dot_general(a, b, (((0,), (0,)), ((), ())), preferred_element_type=F32)


def _bdot(a, b):
    return _dot(a.astype(BF16), b.astype(BF16))


def _split2(x):
    hi = x.astype(BF16)
    lo = (x - hi.astype(F32)).astype(BF16)
    return hi, lo


def _dot_sel(x, sel):
    hi, lo = _split2(x)
    return _dot(hi, sel) + _dot(lo, sel)


def _dot_sel_exact(x, sel):
    x1 = x.astype(BF16)
    r1 = x - x1.astype(F32)
    x2 = r1.astype(BF16)
    x3 = (r1 - x2.astype(F32)).astype(BF16)
    return _dot(x1, sel) + _dot(x2, sel) + _dot(x3, sel)


def _dot3(a, b):
    ah, al = _split2(a)
    bh, bl = _split2(b)
    return _dot(ah, bh) + (_dot(ah, bl) + _dot(al, bh))


def _rms_rows(x, g):
    return x * lax.rsqrt(jnp.mean(x * x, axis=-1, keepdims=True) + EPS) * g


def _sigmoid(x):
    return 1.0 / (1.0 + jnp.exp(-x))


def _silu(x):
    return x * _sigmoid(x)


def _softplus(x):
    return jnp.maximum(x, 0.0) + jnp.log1p(jnp.exp(-jnp.abs(x)))


def _head_ones():
    r = np.arange(BRANCH_WIDTH) // HEAD_DIM
    return jnp.asarray(r[:, None] == r[None, :], BF16)


IN_OUTS = (("s5_u_lo", 128), ("s5_u_hi", 128), ("gla_q", 256), ("gla_k", 256), ("gla_v", 256), ("gla_r", 256),
           ("gla_lr", 128), ("gdn_qkv", 768), ("gdn_b", 256), ("gdn_a", 256), ("gdn_z", 256),
           ("moba_q", 256), ("moba_k", 256), ("moba_v", 256))
IN_WIDTH = sum(w for _, w in IN_OUTS)


def _regroup_w_in(w_in):
    sizes = (256, 256, 256, 256, GLA_RANK, 256, 256, 256, 256, HEADS, HEADS, 256, 256, 256, 256)
    offs = np.cumsum((0,) + sizes)
    (s5_u, a_q, a_k, a_v, a_lr, a_r, d_q, d_k, d_v, d_b, d_a, d_z, m_q, m_k, m_v) = (
        w_in[:, offs[i]:offs[i + 1]] for i in range(len(sizes)))
    lr = jnp.pad(a_lr, ((0, 0), (0, 128 - GLA_RANK)))
    cols = [s5_u, a_q, a_k, a_v, a_r, lr, d_q, d_k, d_v,
            jnp.repeat(d_b, HEAD_DIM, axis=1), jnp.repeat(d_a, HEAD_DIM, axis=1), d_z, m_q, m_k, m_v]
    return jnp.concatenate(cols, axis=1).astype(BF16)


def _inproj_kernel(x_ref, g_ref, w_ref, wg_ref, bg_ref, alog_ref, dtb_ref, *outs):
    x = x_ref[...]
    hb = _rms_rows(x, g_ref[...]).astype(BF16)
    vals = {}
    off = 0
    for name, n in IN_OUTS:
        vals[name] = _dot(hb, w_ref[:, off:off + n])
        off += n
    z = _bdot(vals["gla_lr"], wg_ref[...]) + bg_ref[...]
    vals["gla_lr"] = -_softplus(-z) * (1.0 / GLA_TAU)
    vals["gdn_b"] = _sigmoid(vals["gdn_b"])
    vals["gdn_a"] = -jnp.exp(alog_ref[...]) * _softplus(vals["gdn_a"] + dtb_ref[...])
    for (name, _), o_ref in zip(IN_OUTS, outs):
        o_ref[...] = vals[name]


def _inproj(x2d, lp, tm):
    n = x2d.shape[0]
    out_shape = []
    out_specs = []
    for name, w in IN_OUTS:
        w_out = 256 if name == "gla_lr" else w
        out_shape.append(jax.ShapeDtypeStruct((n, w_out), F32))
        out_specs.append(pl.BlockSpec((tm, w_out), lambda i: (i, 0)))
    res = pl.pallas_call(
        _inproj_kernel,
        grid=(n // tm,),
        in_specs=[pl.BlockSpec((tm, D_MODEL), lambda i: (i, 0)),
                  _const_spec((1, D_MODEL)), _const_spec((D_MODEL, IN_WIDTH)),
                  _const_spec((128, 256)), _const_spec((1, 256)),
                  _const_spec((1, 256)), _const_spec((1, 256))],
        out_specs=out_specs,
        out_shape=out_shape,
        compiler_params=_cparams("parallel"),
        name="inproj",
    )(x2d, lp["ln1_g"], lp["w_in"], lp["gla_wg"], lp["gla_bg"], lp["gdn_alog"], lp["gdn_dtb"])
    return dict(zip((nm for nm, _ in IN_OUTS), res))


def _merge_kernel(x_ref, ya_lo_ref, ya_hi_ref, yb_ref, yc_ref, yd_ref, g_ref, wgate_ref, wglu_ref, bglu_ref,
                  wbr_ref, wout_ref, o_ref):
    x = x_ref[...]
    hb = _rms_rows(x, g_ref[...]).astype(BF16)
    ya = jnp.concatenate([ya_lo_ref[...], ya_hi_ref[...]], axis=1)
    ya = ya * _sigmoid(_bdot(ya, wglu_ref[...]) + bglu_ref[...])
    merged = None
    for i, y in enumerate((ya, yb_ref[...], yc_ref[...], yd_ref[...])):
        gate = _sigmoid(_dot(hb, wgate_ref[:, i * D_MODEL:(i + 1) * D_MODEL]))
        term = gate * _dot(y.astype(BF16), wbr_ref[i])
        merged = term if merged is None else merged + term
    o_ref[...] = x + _bdot(merged, wout_ref[...])


def _merge(x2d, ya, yb, yc, yd, lp, tm):
    n = x2d.shape[0]
    row = lambda w: pl.BlockSpec((tm, w), lambda i: (i, 0))
    ya_lo, ya_hi = ya
    return pl.pallas_call(
        _merge_kernel,
        grid=(n // tm,),
        in_specs=[row(D_MODEL), row(128), row(128), row(256), row(256), row(256),
                  _const_spec((1, D_MODEL)), _layer_spec((D_MODEL, N_BRANCH * D_MODEL), lp["layer"]),
                  _const_spec((256, 256)), _const_spec((1, 256)),
                  _const_spec((N_BRANCH, 256, D_MODEL)), _layer_spec((D_MODEL, D_MODEL), lp["layer"])],
        out_specs=row(D_MODEL),
        out_shape=jax.ShapeDtypeStruct((n, D_MODEL), F32),
        compiler_params=_cparams("parallel"),
        name="merge",
    )(x2d, ya_lo, ya_hi, yb, yc, yd, lp["ln1_g"], lp["w_gate"], lp["s5_w_glu"], lp["s5_b_glu"],
      lp["w_br"], lp["w_out"])


def _mlp_kernel(x_ref, g_ref, w1_ref, w2_ref, o_ref):
    x = x_ref[...]
    hb = _rms_rows(x, g_ref[...]).astype(BF16)
    z = jnp.maximum(_dot(hb, w1_ref[...]), 0.0)
    o_ref[...] = x + _bdot(z * z, w2_ref[...])


def _mlp(x2d, lp, tm):
    n = x2d.shape[0]
    row = pl.BlockSpec((tm, D_MODEL), lambda i: (i, 0))
    single = pl.Buffered(1)
    return pl.pallas_call(
        _mlp_kernel,
        grid=(n // tm,),
        in_specs=[row, _const_spec((1, D_MODEL)),
                  _layer_spec((D_MODEL, D_FF), lp["layer"], pipeline_mode=single),
                  _layer_spec((D_FF, D_MODEL), lp["layer"], pipeline_mode=single)],
        out_specs=row,
        out_shape=jax.ShapeDtypeStruct((n, D_MODEL), F32),
        compiler_params=_cparams("parallel"),
        name="mlp",
    )(x2d, lp["ln2_g"], lp["w_ff1"], lp["w_ff2"])


def _s5_matrices(p, chunk):
    hp = dict(precision=HIGHEST)
    dt = jnp.exp(p["s5_log_dt"])[:, None]
    ar, ai = p["s5_a_re"], p["s5_a_im"]
    mag = jnp.exp(ar * dt)
    abar_re = mag * jnp.cos(ai * dt)
    abar_im = mag * jnp.sin(ai * dt)
    den = ar * ar + ai * ai
    nr = abar_re - 1.0
    f_re = (nr * ar + abar_im * ai) / den
    f_im = (abar_im * ar - nr * ai) / den
    br, bi = p["s5_b_re"], p["s5_b_im"]
    bbar_re = f_re[..., None] * br - f_im[..., None] * bi
    bbar_im = f_re[..., None] * bi + f_im[..., None] * br
    pw_re = [jnp.ones_like(abar_re)]
    pw_im = [jnp.zeros_like(abar_re)]
    for _ in range(chunk):
        r, i = pw_re[-1], pw_im[-1]
        pw_re.append(r * abar_re - i * abar_im)
        pw_im.append(r * abar_im + i * abar_re)
    pw_re = jnp.stack(pw_re)
    pw_im = jnp.stack(pw_im)
    cr, ci = p["s5_c_re"], p["s5_c_im"]
    ca_re = cr[None] * pw_re[:, :, None, :] - ci[None] * pw_im[:, :, None, :]
    ca_im = cr[None] * pw_im[:, :, None, :] + ci[None] * pw_re[:, :, None, :]
    def table(t):
        lead = t.shape[:-3]
        t = jnp.moveaxis(t, -1, -3)
        return t.reshape(lead + (t.shape[-3], -1))

    kern = (jnp.einsum("tgop,gpi->tgoi", ca_re[:chunk], bbar_re, **hp)
            - jnp.einsum("tgop,gpi->tgoi", ca_im[:chunk], bbar_im, **hp))
    rev_re = pw_re[:chunk][::-1]
    rev_im = pw_im[:chunk][::-1]
    ab_re = rev_re[..., None] * bbar_re[None] - rev_im[..., None] * bbar_im[None]
    ab_im = rev_re[..., None] * bbar_im[None] + rev_im[..., None] * bbar_re[None]
    a_pow = jnp.stack([pw_re[chunk].reshape(-1), pw_im[chunk].reshape(-1)])
    return dict(lag=table(kern),
                xs=jnp.concatenate([table(ab_re), table(ab_im)], axis=-1),
                hy=jnp.stack([table(ca_re[1:]), -table(ca_im[1:])], axis=1),
                a_pow=a_pow, d_row=p["s5_d"][None, :])


def _s5_expand(lag_ref, xs_ref, hy_ref, wxs_scr, win_scr, why_scr):
    chunk = lag_ref.shape[0]
    w = BRANCH_WIDTH

    def group_of(shape, axis, per_group, wrap=None):
        idx = lax.broadcasted_iota(jnp.int32, shape, axis)
        if wrap is not None:
            idx = idx % wrap
        return idx // per_group

    same_ii = group_of((w, w), 0, S5_GROUP) == group_of((w, w), 1, S5_GROUP)
    same_is = (group_of((w, 2 * S5_STATE), 0, S5_GROUP)
               == group_of((w, 2 * S5_STATE), 1, S5_P, wrap=S5_STATE))
    same_si = group_of((S5_STATE, w), 0, S5_P) == group_of((S5_STATE, w), 1, S5_GROUP)
    down = lambda t: jnp.concatenate([t] * S5_GROUPS, axis=0)
    lags = [jnp.where(same_ii, down(lag_ref[tau]), 0.0).astype(BF16) for tau in range(chunk)]
    zero = jnp.zeros((w, w), BF16)
    for s in range(chunk):
        wxs_scr[s * w:(s + 1) * w, :] = jnp.where(same_is, down(xs_ref[s]), 0.0).astype(BF16)
        win_scr[s * w:(s + 1) * w, :] = jnp.concatenate(
            [lags[t - s] if t >= s else zero for t in range(chunk)], axis=1)
        for part in range(2):
            why_scr[part * S5_STATE:(part + 1) * S5_STATE, s * w:(s + 1) * w] = jnp.where(
                same_si, down(hy_ref[s, part]), 0.0).astype(BF16)


def _gelu(y):
    c = math.sqrt(2.0 / math.pi)
    return 0.5 * y * (1.0 + jnp.tanh(c * (y + 0.044715 * (y * y * y))))


def _s5_kernel(u_lo_ref, u_hi_ref, lag_ref, xs_ref, hy_ref, apow_ref, d_ref, y_lo_ref, y_hi_ref, hfin_ref,
               wxs_scr, win_scr, why_scr, xs_scr, hs_scr):
    chunk = lag_ref.shape[0]
    rows = u_lo_ref.shape[0] // chunk
    half = BRANCH_WIDTH // 2

    @pl.when(pl.program_id(0) == 0)
    def _():
        _s5_expand(lag_ref, xs_ref, hy_ref, wxs_scr, win_scr, why_scr)

    u = jnp.concatenate([ref[pl.ds(s, rows, stride=chunk), :]
                         for s in range(chunk) for ref in (u_lo_ref, u_hi_ref)], axis=1)
    ub = u.astype(BF16)
    xs_scr[...] = _dot(ub, wxs_scr[...])
    a_re = apow_ref[0:1, :]
    a_im = apow_ref[1:2, :]

    def step(r, carry):
        hr, hi = carry
        hs_scr[pl.ds(r, 1), :] = jnp.concatenate([hr, hi], axis=-1)
        x = xs_scr[pl.ds(r, 1), :]
        nhr = a_re * hr - a_im * hi + x[:, :S5_STATE]
        nhi = a_re * hi + a_im * hr + x[:, S5_STATE:]
        return nhr, nhi

    zero = jnp.zeros((1, S5_STATE), F32)
    hr, hi = lax.fori_loop(0, rows, step, (zero, zero))
    hfin_ref[0] = jnp.concatenate([hr, hi], axis=-1)
    y = _dot(ub, win_scr[...]) + _dot(hs_scr[...].astype(BF16), why_scr[...])
    for t in range(chunk):
        cols = slice(t * BRANCH_WIDTH, (t + 1) * BRANCH_WIDTH)
        y_t = _gelu(y[:, cols] + d_ref[...] * u[:, cols])
        y_lo_ref[pl.ds(t, rows, stride=chunk), :] = y_t[:, :half]
        y_hi_ref[pl.ds(t, rows, stride=chunk), :] = y_t[:, half:]


def _s5_scratch(chunk):
    width = chunk * BRANCH_WIDTH
    return [pltpu.VMEM((width, 2 * S5_STATE), BF16), pltpu.VMEM((width, width), BF16),
            pltpu.VMEM((2 * S5_STATE, width), BF16)]


def _s5_prompt(u_lo, u_hi, sm, batch, seq):
    c = sm["lag"].shape[0]
    rows = seq // c
    half = BRANCH_WIDTH // 2
    tok = pl.BlockSpec((seq, half), lambda b: (b, 0))
    y_lo, y_hi, hfin = pl.pallas_call(
        _s5_kernel,
        grid=(batch,),
        in_specs=[tok, tok, _const_spec(sm["lag"].shape), _const_spec(sm["xs"].shape),
                  _const_spec(sm["hy"].shape), _const_spec((2, S5_STATE)), _const_spec((1, BRANCH_WIDTH))],
        out_specs=[tok, tok, pl.BlockSpec((1, 1, 2 * S5_STATE), lambda b: (b, 0, 0))],
        out_shape=[jax.ShapeDtypeStruct((batch * seq, half), F32),
                   jax.ShapeDtypeStruct((batch * seq, half), F32),
                   jax.ShapeDtypeStruct((batch, 1, 2 * S5_STATE), F32)],
        scratch_shapes=_s5_scratch(c) + [pltpu.VMEM((rows, 2 * S5_STATE), F32),
                                         pltpu.VMEM((rows, 2 * S5_STATE), F32)],
        compiler_params=_cparams("arbitrary"),
        name="s5_prompt",
    )(u_lo, u_hi, sm["lag"], sm["xs"], sm["hy"], sm["a_pow"], sm["d_row"])
    return (y_lo, y_hi), hfin.reshape(batch, 2 * S5_STATE)


GLA_ROWS = 256


def _gla_kernel(q_ref, k_ref, v_ref, g_ref, r_ref, tri_ref, ones_ref, bmask_ref, gn_ref,
                y_ref, sfin_ref, st_scr):
    step = pl.program_id(1)
    sub = GLA_SUB
    rows = q_ref.shape[0]
    ns = rows // sub

    @pl.when(step == 0)
    def _():
        st_scr[...] = jnp.zeros_like(st_scr)

    ones = ones_ref[...]
    bmask = bmask_ref[...]
    split = lambda a: a.reshape(ns, sub, BRANCH_WIDTH)
    bc = split(jnp.dot(tri_ref[...], g_ref[...], precision=HIGHEST, preferred_element_type=F32))
    q = split(q_ref[...] * (HEAD_DIM ** -0.5))
    k = split(k_ref[...])
    v = split(v_ref[...])

    row = lax.broadcasted_iota(jnp.int32, (ns, sub, BRANCH_WIDTH), 1)
    parts = []
    for j in range(sub):
        e = jnp.exp(jnp.minimum(bc - bc[:, j:j + 1, :], 0.0))
        parts.append(jnp.where(row >= j, q * k[:, j:j + 1, :] * e, 0.0))
    att = _dot(jnp.concatenate(parts, axis=1).reshape(ns * sub * sub, BRANCH_WIDTH).astype(BF16), ones)
    att = att.reshape(ns, sub * sub, BRANCH_WIDTH)
    o = att[:, 0:sub] * v[:, 0:1, :]
    for j in range(1, sub):
        o = o + att[:, j * sub:(j + 1) * sub] * v[:, j:j + 1, :]

    last = bc[:, sub - 1:sub, :]
    qt = (q * jnp.exp(bc)).astype(BF16)
    kt = (k * jnp.exp(last - bc)).astype(BF16)
    vb = v.astype(BF16)
    decay = jnp.exp(last)
    outer = [bmask * _dot_tn(vb[s], kt[s]) for s in range(ns)]
    st = st_scr[...]
    inter = []
    for s in range(ns):
        inter.append(_dot_nt(qt[s], st.astype(BF16)))
        st = st * decay[s] + outer[s]
    st_scr[...] = st
    o = o.reshape(rows, BRANCH_WIDTH) + jnp.concatenate(inter, axis=0)
    ms = _dot_sel(o * o, ones) * (1.0 / HEAD_DIM)
    y_ref[...] = o * lax.rsqrt(ms + EPS) * gn_ref[...] * _silu(r_ref[...])

    @pl.when(step == pl.num_programs(1) - 1)
    def _():
        sfin_ref[0] = st


def _sub_tril(rows, sub):
    i = np.arange(rows)
    return jnp.asarray((i[:, None] // sub == i[None, :] // sub) & (i[None, :] <= i[:, None]), F32)


def _unpack_state_t(st):
    b = st.shape[0]
    st = st.reshape(b, HEADS, HEAD_DIM, HEADS, HEAD_DIM)
    diag = jnp.stack([st[:, h, :, h, :] for h in range(HEADS)], axis=1)
    return diag.transpose(0, 1, 3, 2)


def _gla_prompt(q, k, v, g, r, lp, batch, seq):
    rows = min(GLA_ROWS, seq)
    nsteps = seq // rows
    blk = pl.BlockSpec((rows, BRANCH_WIDTH), lambda b, c: (b * nsteps + c, 0))
    hm = np.arange(BRANCH_WIDTH) // HEAD_DIM
    bmask = jnp.asarray(hm[:, None] == hm[None, :], F32)
    y, sfin = pl.pallas_call(
        _gla_kernel,
        grid=(batch, nsteps),
        in_specs=[blk, blk, blk, blk, blk, _const_spec((rows, rows)),
                  _const_spec((BRANCH_WIDTH, BRANCH_WIDTH)), _const_spec((BRANCH_WIDTH, BRANCH_WIDTH)),
                  _const_spec((1, BRANCH_WIDTH))],
        out_specs=[blk, pl.BlockSpec((1, BRANCH_WIDTH, BRANCH_WIDTH), lambda b, c: (b, 0, 0))],
        out_shape=[jax.ShapeDtypeStruct((batch * seq, BRANCH_WIDTH), F32),
                   jax.ShapeDtypeStruct((batch, BRANCH_WIDTH, BRANCH_WIDTH), F32)],
        scratch_shapes=[pltpu.VMEM((BRANCH_WIDTH, BRANCH_WIDTH), F32)],
        compiler_params=_cparams("parallel", "arbitrary"),
        name="gla_prompt",
    )(q, k, v, g, r, _sub_tril(rows, GLA_SUB), _head_ones(), bmask, lp["gla_norm"])
    return y, _unpack_state_t(sfin)


CONV_PAD = 8


GDN_ROWS = 512


def _block_diag(x, ones):
    return jnp.concatenate([x] * HEADS, axis=0) * ones


def _unit_lower_inverses(ns, eye, ones):
    c = ns[0].shape[0]
    every = range(len(ns))
    invs = [eye - n for n in ns]
    pws = list(ns)
    for _ in range(int(math.log2(c)) - 1):
        pbs = [pw.astype(BF16) for pw in pws]
        pws = [_dot(pb, _block_diag(pb, ones)) for pb in pbs]
        invs = [invs[i] + _dot(invs[i].astype(BF16), _block_diag(pws[i].astype(BF16), ones)) for i in every]
    inv_parts = [_split2(inv) for inv in invs]
    n_parts = [_split2(n) for n in ns]
    prods = [_dot(jnp.concatenate(n_parts[i], axis=0), _block_diag(inv_parts[i][0], ones)) for i in every]
    cross = [_dot(n_parts[i][0], _block_diag(inv_parts[i][1], ones)) for i in every]
    resids = [eye - invs[i] - (prods[i][:c] + prods[i][c:] + cross[i]) for i in every]
    return [invs[i] + _dot(inv_parts[i][0], _block_diag(resids[i].astype(BF16), ones)) for i in every]


def _gdn_kernel(x_ref, b_ref, g_ref, z_ref, cw_ref, tri_ref, ones_ref, eye_ref, gn_ref,
                y_ref, sfin_ref, conv_ref, s_scr, buf_scr):
    step = pl.program_id(1)
    rows = x_ref.shape[0]
    c = min(GDN_CHUNK, rows)

    @pl.when(step == 0)
    def _():
        s_scr[...] = jnp.zeros_like(s_scr)
        buf_scr[0:CONV_PAD, :] = jnp.zeros((CONV_PAD, GDN_QKV), F32)

    x = x_ref[...]
    buf_scr[CONV_PAD:CONV_PAD + rows, :] = x
    conv = cw_ref[GDN_CONV - 1:GDN_CONV, :] * x
    for w in range(GDN_CONV - 1):
        lag = GDN_CONV - 1 - w
        conv = conv + cw_ref[w:w + 1, :] * buf_scr[CONV_PAD - lag:CONV_PAD - lag + rows, :]
    tail = buf_scr[rows:rows + CONV_PAD, :]
    buf_scr[0:CONV_PAD, :] = tail
    conv_ref[0] = tail
    qkv = _silu(conv)
    ones = ones_ref[...]
    onesf = ones.astype(F32)
    eye = eye_ref[...]
    q = qkv[:, 0:BRANCH_WIDTH]
    k = qkv[:, BRANCH_WIDTH:2 * BRANCH_WIDTH]
    v = qkv[:, 2 * BRANCH_WIDTH:]
    q = q * lax.rsqrt(_dot_sel(q * q, ones) + EPS) * (HEAD_DIM ** -0.5)
    k = k * lax.rsqrt(_dot_sel(k * k, ones) + EPS)
    beta = b_ref[...]
    gc = jnp.dot(tri_ref[...], g_ref[...], precision=HIGHEST, preferred_element_type=F32)
    gam = jnp.exp(gc)
    ri = lax.broadcasted_iota(jnp.int32, (c, BRANCH_WIDTH), 0)
    cj = lax.broadcasted_iota(jnp.int32, (c, BRANCH_WIDTH), 1) % HEAD_DIM

    every = range(rows // c)
    sls = [slice(n * c, (n + 1) * c) for n in every]
    kcs = [k[rs].astype(BF16) for rs in sls]
    kqs = [_dot_nt(jnp.concatenate([k[rs], q[rs]], axis=0).astype(BF16), _block_diag(kcs[n], ones))
           for n, rs in enumerate(sls)]
    decs = []
    for rs in sls:
        grow = jnp.sum(gc[rs] * eye, axis=0, keepdims=True)
        decs.append(jnp.where(ri >= cj, jnp.exp(jnp.minimum(gc[rs] - grow, 0.0)), 0.0))
    invs = _unit_lower_inverses(
        [jnp.where(ri > cj, beta[rs] * decs[n] * kqs[n][:c], 0.0) for n, rs in enumerate(sls)], eye, ones)
    invbs = [inv.astype(BF16) for inv in invs]
    ws_m = [_dot(invbs[n], _block_diag((beta[rs] * gam[rs] * k[rs]).astype(BF16), ones))
            for n, rs in enumerate(sls)]
    u0s = [_dot(invbs[n], _block_diag((beta[rs] * v[rs]).astype(BF16), ones)) for n, rs in enumerate(sls)]
    aqks = [(decs[n] * kqs[n][c:]).astype(BF16) for n in every]
    lhs = [jnp.concatenate([ws_m[n], gam[rs] * q[rs]], axis=0).astype(BF16) for n, rs in enumerate(sls)]
    glasts = [gc[rs][c - 1:c, :] for rs in sls]
    kds = [(k[rs] * jnp.exp(glasts[n] - gc[rs])).astype(BF16) for n, rs in enumerate(sls)]

    s = s_scr[...]
    outs = []
    for n in every:
        ws = _dot(lhs[n], s.astype(BF16))
        ub = (u0s[n] - ws[:c]).astype(BF16)
        outs.append(ws[c:] + _dot(aqks[n], _block_diag(ub, ones)))
        s = jnp.exp(glasts[n]) * s + onesf * _dot_tn(kds[n], ub)
    s_scr[...] = s
    o = jnp.concatenate(outs, axis=0)
    ms = _dot_sel(o * o, ones) * (1.0 / HEAD_DIM)
    y_ref[...] = o * lax.rsqrt(ms + EPS) * gn_ref[...] * _silu(z_ref[...])

    @pl.when(step == pl.num_programs(1) - 1)
    def _():
        sfin_ref[0] = s


def _unpack_state(st):
    b = st.shape[0]
    st = st.reshape(b, HEADS, HEAD_DIM, HEADS, HEAD_DIM)
    return jnp.stack([st[:, h, :, h, :] for h in range(HEADS)], axis=1)


def _gdn_prompt(x, beta, g, z, lp, batch, seq):
    rows = min(GDN_ROWS, seq)
    c = min(GDN_CHUNK, rows)
    nsteps = seq // rows
    blk = lambda w: pl.BlockSpec((rows, w), lambda b, s: (b * nsteps + s, 0))
    eye = jnp.asarray(np.tile(np.eye(c, dtype=np.float32), (1, HEADS)))
    y, sfin, conv = pl.pallas_call(
        _gdn_kernel,
        grid=(batch, nsteps),
        in_specs=[blk(GDN_QKV), blk(256), blk(256), blk(256), _const_spec((CONV_PAD, GDN_QKV)),
                  _const_spec((rows, rows)), _const_spec((256, 256)), _const_spec((c, 256)),
                  _const_spec((1, 256))],
        out_specs=[blk(256),
                   pl.BlockSpec((1, 256, 256), lambda b, s: (b, 0, 0)),
                   pl.BlockSpec((1, CONV_PAD, GDN_QKV), lambda b, s: (b, 0, 0))],
        out_shape=[jax.ShapeDtypeStruct((batch * seq, 256), F32),
                   jax.ShapeDtypeStruct((batch, 256, 256), F32),
                   jax.ShapeDtypeStruct((batch, CONV_PAD, GDN_QKV), F32)],
        scratch_shapes=[pltpu.VMEM((256, 256), F32),
                        pltpu.VMEM((CONV_PAD + rows, GDN_QKV), F32)],
        compiler_params=_cparams("parallel", "arbitrary"),
        name="gdn_prompt",
    )(x, beta, g, z, lp["gdn_conv_w"], _sub_tril(rows, c), _head_ones(), eye, lp["gdn_norm"])
    return y, _unpack_state(sfin), conv[:, CONV_PAD - (GDN_CONV - 1):, :]


HALF_ROT = ROT_DIM // 2


def _rope_tables(pos):
    inv = ROPE_THETA ** (-jnp.arange(HALF_ROT, dtype=F32) / HALF_ROT)
    ang = pos.astype(F32)[:, None] * inv[None, :]
    cos, sin = jnp.cos(ang), jnp.sin(ang)
    n = pos.shape[0]
    rest = HEAD_DIM - ROT_DIM
    head = lambda a, b, fill: jnp.concatenate([a, b, jnp.full((n, rest), fill, F32)], axis=1)
    zero = jnp.zeros_like(sin)
    tabs = [head(cos, cos, 1.0), head(-sin, zero, 0.0), head(zero, sin, 0.0)]
    return jnp.stack([jnp.tile(t, (1, HEADS)) for t in tabs])


def _qk_norm_rope(x, gain, tab_ref, ones):
    y = x * lax.rsqrt(_dot_sel(x * x, ones) * (1.0 / HEAD_DIM) + EPS) * gain
    up = pltpu.roll(y, BRANCH_WIDTH - HALF_ROT, 1)
    down = pltpu.roll(y, HALF_ROT, 1)
    return y * tab_ref[0] + up * tab_ref[1] + down * tab_ref[2]


K_AUG = 2 * HEAD_DIM
V_AUG = HEAD_DIM + 16


def _moba_prep_kernel(q_ref, k_ref, v_ref, tab_ref, qg_ref, kg_ref, ones_ref,
                      kt_ref, vtt_ref, qt_ref, kh_ref, vt_ref, km_ref):
    ones = ones_ref[...]
    rows = q_ref.shape[0]
    nblk = rows // MOBA_BLOCK
    mq = _qk_norm_rope(q_ref[...], qg_ref[...], tab_ref, ones) * (HEAD_DIM ** -0.5)
    mk = _qk_norm_rope(k_ref[...], kg_ref[...], tab_ref, ones)
    kmean = jnp.mean(mk.reshape(nblk, MOBA_BLOCK, BRANCH_WIDTH), axis=1)
    qt = mq.T
    kt = mk.T
    vt = v_ref[...].T
    one_col = (lax.broadcasted_iota(jnp.int32, (MOBA_BLOCK, K_AUG - HEAD_DIM), 1) == 0).astype(BF16)
    one_row = (lax.broadcasted_iota(jnp.int32, (V_AUG - HEAD_DIM, MOBA_BLOCK), 0) == 0).astype(BF16)
    for h in range(HEADS):
        sl = slice(h * HEAD_DIM, (h + 1) * HEAD_DIM)
        qt_ref[0, h] = qt[sl, :]
        km_ref[0, h] = kmean[:, sl]
        kt_ref[0, h] = kt[sl, :]
        vtt_ref[0, h] = vt[sl, :]
        for j in range(nblk):
            rs = slice(j * MOBA_BLOCK, (j + 1) * MOBA_BLOCK)
            kh_ref[0, h, j] = jnp.concatenate([mk[rs, sl].astype(BF16), one_col], axis=1)
            vt_ref[0, h, j] = jnp.concatenate([vt[sl, rs].astype(BF16), one_row], axis=0)


def _moba_prep(q, k, v, tabs, lp, batch, seq):
    rows = min(8 * MOBA_BLOCK, seq)
    nsteps = seq // rows
    nblk = rows // MOBA_BLOCK
    nb = seq // MOBA_BLOCK
    blk = pl.BlockSpec((rows, 256), lambda b, r: (b * nsteps + r, 0))
    cache_rows = pl.BlockSpec((1, HEADS, HEAD_DIM, rows), lambda b, r: (b, 0, 0, r))
    return pl.pallas_call(
        _moba_prep_kernel,
        grid=(batch, nsteps),
        in_specs=[blk, blk, blk, pl.BlockSpec((3, rows, 256), lambda b, r: (0, r, 0)),
                  _const_spec((1, 256)), _const_spec((1, 256)), _const_spec((256, 256))],
        out_specs=[cache_rows, cache_rows, cache_rows,
                   pl.BlockSpec((1, HEADS, nblk, MOBA_BLOCK, K_AUG), lambda b, r: (b, 0, r, 0, 0)),
                   pl.BlockSpec((1, HEADS, nblk, V_AUG, MOBA_BLOCK), lambda b, r: (b, 0, r, 0, 0)),
                   pl.BlockSpec((1, HEADS, nblk, HEAD_DIM), lambda b, r: (b, 0, r, 0))],
        out_shape=[jax.ShapeDtypeStruct((batch, HEADS, HEAD_DIM, seq), F32),
                   jax.ShapeDtypeStruct((batch, HEADS, HEAD_DIM, seq), F32),
                   jax.ShapeDtypeStruct((batch, HEADS, HEAD_DIM, seq), F32),
                   jax.ShapeDtypeStruct((batch, HEADS, nb, MOBA_BLOCK, K_AUG), BF16),
                   jax.ShapeDtypeStruct((batch, HEADS, nb, V_AUG, MOBA_BLOCK), BF16),
                   jax.ShapeDtypeStruct((batch, HEADS, nb, HEAD_DIM), F32)],
        compiler_params=_cparams("parallel", "parallel"),
        name="moba_prep",
    )(q, k, v, tabs, lp["moba_q_norm"], lp["moba_k_norm"], _head_ones())


def _moba_attn_kernel(qt_ref, kh_ref, vt_ref, km_ref, o_ref, bias_scr):
    qb = pl.program_id(1)
    nb = km_ref.shape[2]
    blk = MOBA_BLOCK
    heads = range(HEADS)
    blk_id = lax.broadcasted_iota(jnp.int32, (nb, blk), 0)
    kpos = lax.broadcasted_iota(jnp.int32, (blk, blk), 0)
    qpos = lax.broadcasted_iota(jnp.int32, (blk, blk), 1)
    first_row = lax.broadcasted_iota(jnp.int32, (K_AUG - HEAD_DIM, blk), 0) == 0
    qts = [qt_ref[0, h] for h in heads]
    qtb = [qt.astype(BF16) for qt in qts]

    def scores(j, biases):
        out = []
        for h in heads:
            extra = jnp.where(first_row, biases[h], 0.0).astype(BF16)
            out.append(_dot(kh_ref[0, h, j], jnp.concatenate([qtb[h], extra], axis=0)))
        return out

    gates = [jnp.dot(km_ref[0, h], qts[h], precision=HIGHEST, preferred_element_type=F32) for h in heads]
    own = scores(qb, [jnp.zeros((1, blk), F32)] * HEADS)
    for h in heads:
        gate = jnp.where(blk_id < qb, gates[h], NEG)
        taken = jnp.zeros((nb, blk), jnp.bool_)
        for _ in range(min(MOBA_TOPK, nb)):
            best = jnp.max(gate, axis=0, keepdims=True)
            idx = jnp.min(jnp.where(gate == best, blk_id, nb), axis=0, keepdims=True)
            hit = blk_id == idx
            taken = jnp.logical_or(taken, hit)
            gate = jnp.where(hit, -jnp.inf, gate)
        bias_scr[h] = jnp.where(jnp.logical_and(taken, blk_id < qb), 0.0, NEG)

    ms, ps = [], []
    for h in heads:
        s = jnp.where(kpos <= qpos, own[h], NEG)
        m = jnp.max(s, axis=0, keepdims=True)
        ms.append(m)
        ps.append(jnp.exp(s - m).astype(BF16))
    accs = [_dot(vt_ref[0, h, qb], ps[h]) for h in heads]

    def bias_rows(j):
        return [bias_scr[h, pl.ds(j, 1), :] for h in heads]

    def update(carry, blocks):
        ms, accs = carry
        ss = [scores(j, bias_rows(j)) for j in blocks]
        new_m, new_acc = [], []
        for h in heads:
            m_new = ms[h]
            for s in ss:
                m_new = jnp.maximum(m_new, jnp.max(s[h], axis=0, keepdims=True))
            acc = jnp.exp(ms[h] - m_new) * accs[h]
            for j, s in zip(blocks, ss):
                acc = acc + _dot(vt_ref[0, h, j], jnp.exp(s[h] - m_new).astype(BF16))
            new_m.append(m_new)
            new_acc.append(acc)
        return tuple(new_m), tuple(new_acc)

    carry = (tuple(ms), tuple(accs))
    start = 0
    for width in (4, 2, 1):
        trips = (qb - start) // width
        carry = lax.fori_loop(
            0, trips, lambda i, c, w=width, s=start: update(c, tuple(s + w * i + e for e in range(w))), carry)
        start = start + trips * width
    ms, accs = carry
    outs = [accs[h][:HEAD_DIM] / accs[h][HEAD_DIM:HEAD_DIM + 1] for h in heads]
    o_ref[...] = jnp.concatenate(outs, axis=0).T


def _moba_prompt(qt, kh, vt, km, batch, seq):
    nb = seq // MOBA_BLOCK
    return pl.pallas_call(
        _moba_attn_kernel,
        grid=(batch, nb),
        in_specs=[pl.BlockSpec((1, HEADS, HEAD_DIM, MOBA_BLOCK), lambda b, i: (b, 0, 0, i)),
                  pl.BlockSpec((1, HEADS, nb, MOBA_BLOCK, K_AUG), lambda b, i: (b, 0, 0, 0, 0)),
                  pl.BlockSpec((1, HEADS, nb, V_AUG, MOBA_BLOCK), lambda b, i: (b, 0, 0, 0, 0)),
                  pl.BlockSpec((1, HEADS, nb, HEAD_DIM), lambda b, i: (b, 0, 0, 0))],
        out_specs=pl.BlockSpec((MOBA_BLOCK, 256), lambda b, i: (b * nb + i, 0)),
        out_shape=jax.ShapeDtypeStruct((batch * seq, 256), F32),
        scratch_shapes=[pltpu.VMEM((HEADS, nb, MOBA_BLOCK), F32)],
        compiler_params=_cparams("parallel", "arbitrary"),
        name="moba_prompt",
    )(qt, kh, vt, km)


def _s5_step_kernel(u_ref, h0_ref, lag_ref, xs_ref, hy_ref, apow_ref, d_ref, y_ref, h_ref,
                    wxs_ref, win_ref, why_ref):
    _s5_expand(lag_ref, xs_ref, hy_ref, wxs_ref, win_ref, why_ref)
    u = u_ref[...]
    ub = u.astype(BF16)
    h0 = h0_ref[...]
    xs = _dot(ub, wxs_ref[...])
    a_re = apow_ref[0:1, :]
    a_im = apow_ref[1:2, :]
    hr0 = h0[:, :S5_STATE]
    hi0 = h0[:, S5_STATE:]
    hr = a_re * hr0 - a_im * hi0 + xs[:, :S5_STATE]
    hi = a_re * hi0 + a_im * hr0 + xs[:, S5_STATE:]
    h_ref[...] = jnp.concatenate([hr, hi], axis=-1)
    y = _dot(ub, win_ref[...]) + _dot(h0.astype(BF16), why_ref[...]) + d_ref[...] * u
    y_ref[...] = _gelu(y)


def _s5_step(u, h0, sm):
    n = u.shape[0]
    return pl.pallas_call(
        _s5_step_kernel,
        out_shape=[jax.ShapeDtypeStruct((n, BRANCH_WIDTH), F32),
                   jax.ShapeDtypeStruct((n, 2 * S5_STATE), F32)],
        scratch_shapes=_s5_scratch(1),
        compiler_params=pltpu.CompilerParams(vmem_limit_bytes=VMEM_LIMIT),
        name="s5_step",
    )(u, h0, sm["lag"], sm["xs"], sm["hy"], sm["a_pow"], sm["d_row"])


def _expand_mats():
    idx = np.arange(HEAD_DIM * HEAD_DIM)
    rep = (np.arange(HEAD_DIM)[:, None] == idx[None, :] // HEAD_DIM)
    til = (np.arange(HEAD_DIM)[:, None] == idx[None, :] % HEAD_DIM)
    return jnp.asarray(rep, BF16), jnp.asarray(til, BF16), jnp.asarray(til.T, BF16)


def _gla_step_kernel(q_ref, k_ref, v_ref, g_ref, r_ref, s0_ref, rep_ref, til_ref, tilt_ref, gn_ref,
                     y_ref, s_ref):
    rep = rep_ref[...]
    eg = _dot_sel_exact(jnp.exp(g_ref[...]), rep)
    kr = _dot_sel_exact(k_ref[...], rep)
    qr = _dot_sel_exact(q_ref[...] * (HEAD_DIM ** -0.5), rep)
    vt = _dot_sel_exact(v_ref[...], til_ref[...])
    s = eg * s0_ref[...] + kr * vt
    s_ref[...] = s
    o = _dot_sel(qr * s, tilt_ref[...])
    y_ref[...] = _rms_rows(o, gn_ref[...]) * _silu(r_ref[...])


def _gdn_conv_step_kernel(x_ref, c0_ref, cw_ref, qkv_ref, cnew_ref):
    x = x_ref[...]
    c0 = c0_ref[...]
    conv = cw_ref[GDN_CONV - 1:GDN_CONV, :] * x
    for w in range(GDN_CONV - 1):
        conv = conv + cw_ref[w:w + 1, :] * c0[:, w * GDN_QKV:(w + 1) * GDN_QKV]
    qkv_ref[...] = _silu(conv)
    cnew_ref[...] = jnp.concatenate([c0[:, GDN_QKV:], x], axis=-1)


def _gdn_step_kernel(q_ref, k_ref, v_ref, b_ref, g_ref, z_ref, s0_ref, rep_ref, til_ref, tilt_ref,
                     gn_ref, y_ref, s_ref):
    q = q_ref[...]
    k = k_ref[...]
    q = q * lax.rsqrt(jnp.sum(q * q, axis=-1, keepdims=True) + EPS) * (HEAD_DIM ** -0.5)
    k = k * lax.rsqrt(jnp.sum(k * k, axis=-1, keepdims=True) + EPS)
    beta = b_ref[:, 0:1]
    gam = jnp.exp(g_ref[:, 0:1])
    rep = rep_ref[...]
    tilt = tilt_ref[...]
    kr = _dot_sel_exact(k, rep)
    qr = _dot_sel_exact(q, rep)
    s0 = s0_ref[...]
    ks = _dot_sel(kr * s0, tilt)
    qs = _dot_sel(qr * s0, tilt)
    u = v_ref[...] - gam * ks
    qk = jnp.sum(q * k, axis=-1, keepdims=True)
    o = gam * qs + (beta * qk) * u
    s_ref[...] = gam * s0 + kr * _dot_sel_exact(beta * u, til_ref[...])
    y_ref[...] = _rms_rows(o, gn_ref[...]) * _silu(z_ref[...])


def _whole_call(kernel, out_shape, name, *args):
    return pl.pallas_call(kernel, out_shape=out_shape, name=name,
                          compiler_params=pltpu.CompilerParams(vmem_limit_bytes=VMEM_LIMIT))(*args)


def _moba_qk_step_kernel(q_ref, k_ref, tab_ref, qg_ref, kg_ref, ones_ref, mq_ref, mk_ref):
    ones = ones_ref[...]
    mq_ref[...] = _qk_norm_rope(q_ref[...], qg_ref[...], tab_ref, ones) * (HEAD_DIM ** -0.5)
    mk_ref[...] = _qk_norm_rope(k_ref[...], kg_ref[...], tab_ref, ones)


SELECT_PAGES = 16
SEL_ROWS = 8


def _moba_select_kernel(pt_ref, q_ref, *refs):
    pages = refs[:SELECT_PAGES]
    sel_ref = refs[SELECT_PAGES]
    gate_scr = refs[SELECT_PAGES + 1]
    g = pl.program_id(1)
    ppb = MOBA_BLOCK // PAGE_SIZE
    qcol = jnp.broadcast_to(q_ref[0], (BRANCH_WIDTH, PAGE_SIZE)).reshape(HEADS, HEAD_DIM, PAGE_SIZE)
    blk_lane = lax.broadcasted_iota(jnp.int32, (HEADS, LANES), 1)

    @pl.when(g == 0)
    def _():
        gate_scr[...] = jnp.full((HEADS, LANES), NEG, F32)

    gate = gate_scr[...]
    for n in range(SELECT_PAGES // ppb):
        tile = pages[n * ppb][0, 0]
        for e in range(1, ppb):
            tile = tile + pages[n * ppb + e][0, 0]
        per_token = jnp.sum(tile * qcol, axis=1)
        mean = jnp.sum(per_token, axis=-1, keepdims=True) * (1.0 / MOBA_BLOCK)
        gate = jnp.where(blk_lane == g * (SELECT_PAGES // ppb) + n, mean, gate)
    gate_scr[...] = gate

    @pl.when(g == pl.num_programs(1) - 1)
    def _():
        left = gate
        sel = jnp.zeros((HEADS, LANES), jnp.int32)
        for r in range(MOBA_TOPK):
            best = jnp.max(left, axis=-1, keepdims=True)
            idx = jnp.min(jnp.where(left == best, blk_lane, LANES), axis=-1, keepdims=True)
            sel = jnp.where(blk_lane == r, idx, sel)
            left = jnp.where(blk_lane == idx, -jnp.inf, left)
        sel_ref[0] = sel


def _moba_select(mq, page_table, cache_kt, layer):
    nseq, npages = page_table.shape

    def page_spec(p):
        return pl.BlockSpec((1, 1, HEADS, HEAD_DIM, PAGE_SIZE),
                            lambda b, g, pt: (layer, pt[b, g * SELECT_PAGES + p], 0, 0, 0))

    grid_spec = pltpu.PrefetchScalarGridSpec(
        num_scalar_prefetch=1,
        grid=(nseq, npages // SELECT_PAGES),
        in_specs=[pl.BlockSpec((1, 256, 1), lambda b, g, pt: (b, 0, 0))]
        + [page_spec(p) for p in range(SELECT_PAGES)],
        out_specs=pl.BlockSpec((1, HEADS, LANES), lambda b, g, pt: (b, 0, 0)),
        scratch_shapes=[pltpu.VMEM((HEADS, LANES), F32)],
    )
    sel = pl.pallas_call(
        _moba_select_kernel,
        grid_spec=grid_spec,
        out_shape=jax.ShapeDtypeStruct((nseq, HEADS, LANES), jnp.int32),
        compiler_params=_cparams("parallel", "arbitrary"),
        name="moba_select",
    )(page_table, mq.reshape(nseq, 256, 1), *([cache_kt] * SELECT_PAGES))
    return sel[:, :, :MOBA_TOPK]


N_SEL_PAGES = MOBA_TOPK * (MOBA_BLOCK // PAGE_SIZE)


def _moba_step_kernel(sel_ref, pt_ref, q_ref, kn_ref, vn_ref, *refs):
    kp = refs[:N_SEL_PAGES]
    vp = refs[N_SEL_PAGES:2 * N_SEL_PAGES]
    o_ref = refs[2 * N_SEL_PAGES]
    q = q_ref[0, 0]
    q8 = jnp.broadcast_to(q, (SEL_ROWS, HEAD_DIM)).astype(BF16)
    logits = [_dot(q8, r[0, 0, 0].astype(BF16))[0:1] for r in kp]
    l_self = jnp.sum(q * kn_ref[0, 0], axis=-1, keepdims=True)
    m = l_self
    for lg in logits:
        m = jnp.maximum(m, jnp.max(lg, axis=-1, keepdims=True))
    p_self = jnp.exp(l_self - m)
    den = p_self
    num = p_self * vn_ref[0, 0]
    for lg, r in zip(logits, vp):
        p = jnp.exp(lg - m)
        den = den + jnp.sum(p, axis=-1, keepdims=True)
        p8 = jnp.broadcast_to(p, (SEL_ROWS, PAGE_SIZE)).astype(BF16)
        num = num + _dot_nt(p8, r[0, 0, 0].astype(BF16))[0:1]
    o_ref[0, 0] = num / den


def _moba_step(q, k_new, v_new, sel, page_table, cache_kt, cache_vt, layer):
    nseq, npages = page_table.shape
    ppb = MOBA_BLOCK // PAGE_SIZE

    def page_spec(r, e):
        def index(b, h, sel_ref, pt_ref):
            blk = sel_ref[(b * HEADS + h) * MOBA_TOPK + r]
            return (layer, pt_ref[b * npages + ppb * blk + e], h, 0, 0)
        return pl.BlockSpec((1, 1, 1, HEAD_DIM, PAGE_SIZE), index)

    row = pl.BlockSpec((1, 1, 1, HEAD_DIM), lambda b, h, s, p: (b, h, 0, 0))
    pages = [page_spec(r, e) for r in range(MOBA_TOPK) for e in range(ppb)]
    grid_spec = pltpu.PrefetchScalarGridSpec(
        num_scalar_prefetch=2,
        grid=(nseq, HEADS),
        in_specs=[row, row, row] + pages + pages,
        out_specs=row,
    )
    r4 = lambda a: a.reshape(nseq, HEADS, 1, HEAD_DIM)
    out = pl.pallas_call(
        _moba_step_kernel,
        grid_spec=grid_spec,
        out_shape=jax.ShapeDtypeStruct((nseq, HEADS, 1, HEAD_DIM), F32),
        compiler_params=_cparams("parallel", "parallel"),
        name="moba_step",
    )(sel.reshape(-1), page_table.reshape(-1), r4(q), r4(k_new), r4(v_new),
      *([cache_kt] * N_SEL_PAGES), *([cache_vt] * N_SEL_PAGES))
    return out.reshape(nseq, 256)


def _layer_params(l, w):
    tile4 = lambda a: jnp.tile(a, HEADS)[None]
    rep64 = lambda a: jnp.repeat(a, HEAD_DIM)[None]
    s5p = {k: w[k][l] for k in ("s5_a_re", "s5_a_im", "s5_log_dt", "s5_b_re", "s5_b_im",
                                "s5_c_re", "s5_c_im", "s5_d")}
    return dict(
        ln1_g=w["ln1_g"][l][None], w_in=_regroup_w_in(w["w_in"][l]),
        gla_wg=jnp.pad(w["gla_w_gate"][l], ((0, 128 - GLA_RANK), (0, 0))).astype(BF16),
        gla_bg=w["gla_b_gate"][l][None],
        gdn_alog=rep64(w["gdn_a_log"][l]), gdn_dtb=rep64(w["gdn_dt_bias"][l]),
        s5_prompt=_s5_matrices(s5p, S5_CHUNK), s5_step=_s5_matrices(s5p, 1),
        s5_w_glu=w["s5_w_glu"][l].astype(BF16), s5_b_glu=w["s5_b_glu"][l][None],
        gla_norm=tile4(w["gla_norm"][l]), gla_norm_head=w["gla_norm"][l][None],
        gdn_norm=tile4(w["gdn_norm"][l]), gdn_norm_head=w["gdn_norm"][l][None],
        gdn_conv_w=jnp.pad(w["gdn_conv_w"][l], ((0, CONV_PAD - GDN_CONV), (0, 0))),
        moba_q_norm=tile4(w["moba_q_norm"][l]), moba_k_norm=tile4(w["moba_k_norm"][l]),
        w_br=jnp.stack([w["w_br_s5"][l], w["w_br_gla"][l], w["w_br_gdn"][l], w["w_br_moba"][l]]).astype(BF16),
        ln2_g=w["ln2_g"][l][None],
        layer=l, w_gate=w["w_gate_bf16"], w_out=w["w_out_bf16"], w_ff1=w["w_ff1_bf16"], w_ff2=w["w_ff2_bf16"],
    )


PROMPT_ROWS = 512


def _prompt_layer(x2d, lp, tabs, batch, seq):
    tm = min(PROMPT_ROWS, batch * seq)
    pr = _inproj(x2d, lp, tm)
    ya, s5_fin = _s5_prompt(pr["s5_u_lo"], pr["s5_u_hi"], lp["s5_prompt"], batch, seq)
    yb, gla_s = _gla_prompt(pr["gla_q"], pr["gla_k"], pr["gla_v"], pr["gla_lr"], pr["gla_r"], lp, batch, seq)
    yc, gdn_s, conv = _gdn_prompt(pr["gdn_qkv"], pr["gdn_b"], pr["gdn_a"], pr["gdn_z"], lp, batch, seq)
    kt, vtt, qt, kh, vt, km = _moba_prep(pr["moba_q"], pr["moba_k"], pr["moba_v"], tabs, lp, batch, seq)
    yd = _moba_prompt(qt, kh, vt, km, batch, seq)
    x1 = _merge(x2d, ya, yb, yc, yd, lp, tm)
    x2 = _mlp(x1, lp, tm)
    s5 = lambda a: a.reshape(batch, S5_GROUPS, S5_P)
    states = (kt, vtt, s5(s5_fin[:, :S5_STATE]), s5(s5_fin[:, S5_STATE:]), gla_s, gdn_s, conv)
    return x2, states


def _sample_layer(x2d, lp, tabs, page_table, cache_kt, cache_vt, layer, st):
    s5_re0, s5_im0, gla0, gdn0, conv0 = st
    n = x2d.shape[0]
    pr = _inproj(x2d, lp, n)
    rows = n * HEADS
    per_head = lambda a: a.reshape(rows, HEAD_DIM)
    flat_state = lambda a: a.reshape(rows, HEAD_DIM * HEAD_DIM)
    rep, til, tilt = _expand_mats()
    sds = jax.ShapeDtypeStruct

    h0 = jnp.concatenate([s5_re0.reshape(n, S5_STATE), s5_im0.reshape(n, S5_STATE)], axis=1)
    ya, s5_new = _s5_step(jnp.concatenate([pr["s5_u_lo"], pr["s5_u_hi"]], axis=1), h0, lp["s5_step"])
    ya = (ya[:, :BRANCH_WIDTH // 2], ya[:, BRANCH_WIDTH // 2:])

    yb, gla_s = _whole_call(
        _gla_step_kernel, [sds((rows, HEAD_DIM), F32), sds((rows, HEAD_DIM * HEAD_DIM), F32)], "gla_step",
        per_head(pr["gla_q"]), per_head(pr["gla_k"]), per_head(pr["gla_v"]), per_head(pr["gla_lr"]),
        per_head(pr["gla_r"]), flat_state(gla0), rep, til, tilt, lp["gla_norm_head"])

    qkv, conv_new = _whole_call(
        _gdn_conv_step_kernel, [sds((n, GDN_QKV), F32), sds((n, (GDN_CONV - 1) * GDN_QKV), F32)],
        "gdn_conv_step", pr["gdn_qkv"], conv0.reshape(n, (GDN_CONV - 1) * GDN_QKV), lp["gdn_conv_w"])
    yc, gdn_s = _whole_call(
        _gdn_step_kernel, [sds((rows, HEAD_DIM), F32), sds((rows, HEAD_DIM * HEAD_DIM), F32)], "gdn_step",
        per_head(qkv[:, :256]), per_head(qkv[:, 256:512]), per_head(qkv[:, 512:]),
        per_head(pr["gdn_b"]), per_head(pr["gdn_a"]), per_head(pr["gdn_z"]), flat_state(gdn0),
        rep, til, tilt, lp["gdn_norm_head"])

    mq, mk = _whole_call(
        _moba_qk_step_kernel, [sds((n, 256), F32), sds((n, 256), F32)], "moba_qk_step",
        pr["moba_q"], pr["moba_k"], tabs, lp["moba_q_norm"], lp["moba_k_norm"], _head_ones())
    sel = _moba_select(mq, page_table, cache_kt, layer)
    yd = _moba_step(mq, mk, pr["moba_v"], sel, page_table, cache_kt, cache_vt, layer)

    x1 = _merge(x2d, ya, yb.reshape(n, 256), yc.reshape(n, 256), yd, lp, n)
    x2 = _mlp(x1, lp, n)
    head4 = lambda a: a.reshape(n, 1, HEADS, HEAD_DIM)
    s5 = lambda a: a.reshape(n, S5_GROUPS, S5_P)
    state4 = lambda a: a.reshape(n, HEADS, HEAD_DIM, HEAD_DIM)
    states = (head4(mk), head4(pr["moba_v"]), s5(s5_new[:, :S5_STATE]), s5(s5_new[:, S5_STATE:]),
              state4(gla_s), state4(gdn_s), conv_new.reshape(n, GDN_CONV - 1, GDN_QKV))
    return x2, states


def kernel(x_prompt, x_sample, cache_moba_k, cache_moba_v, page_table, state_s5_re, state_s5_im, state_gla, state_gdn, state_gdn_conv, ln1_g, w_in, s5_a_re, s5_a_im, s5_log_dt, s5_b_re, s5_b_im, s5_c_re, s5_c_im, s5_d, s5_w_glu, s5_b_glu, gla_w_gate, gla_b_gate, gla_norm, gdn_conv_w, gdn_a_log, gdn_dt_bias, gdn_norm, moba_q_norm, moba_k_norm, w_gate, w_br_s5, w_br_gla, w_br_gdn, w_br_moba, w_out, ln2_g, w_ff1, w_ff2):
    weights = dict(ln1_g=ln1_g, w_in=w_in, s5_a_re=s5_a_re, s5_a_im=s5_a_im, s5_log_dt=s5_log_dt,
                   s5_b_re=s5_b_re, s5_b_im=s5_b_im, s5_c_re=s5_c_re, s5_c_im=s5_c_im, s5_d=s5_d,
                   s5_w_glu=s5_w_glu, s5_b_glu=s5_b_glu, gla_w_gate=gla_w_gate, gla_b_gate=gla_b_gate,
                   gla_norm=gla_norm, gdn_conv_w=gdn_conv_w, gdn_a_log=gdn_a_log, gdn_dt_bias=gdn_dt_bias,
                   gdn_norm=gdn_norm, moba_q_norm=moba_q_norm, moba_k_norm=moba_k_norm, w_gate=w_gate,
                   w_br_s5=w_br_s5, w_br_gla=w_br_gla, w_br_gdn=w_br_gdn, w_br_moba=w_br_moba,
                   w_out=w_out, ln2_g=ln2_g, w_ff1=w_ff1, w_ff2=w_ff2)
    for name in ("w_gate", "w_out", "w_ff1", "w_ff2"):
        weights[name + "_bf16"] = weights[name].astype(BF16)
    depth = ln1_g.shape[0]
    layers = [_layer_params(l, weights) for l in range(depth)]
    batch, seq, _ = x_prompt.shape
    nseq = x_sample.shape[0]
    npages = page_table.shape[1]
    past_len = npages * PAGE_SIZE

    xp = x_prompt.reshape(batch * seq, D_MODEL)
    tabs_p = _rope_tables(jnp.arange(seq, dtype=jnp.int32))
    p_states = []
    for l in range(depth):
        xp, st = _prompt_layer(xp, layers[l], tabs_p, batch, seq)
        p_states.append(st)

    cache_kt = cache_moba_k.transpose(0, 1, 3, 4, 2)
    cache_vt = cache_moba_v.transpose(0, 1, 3, 4, 2)
    tabs_s = _rope_tables(jnp.full((1,), past_len, jnp.int32))
    xs = x_sample.reshape(nseq, D_MODEL)
    s_states = []
    for l in range(depth):
        st0 = (state_s5_re[l], state_s5_im[l], state_gla[l], state_gdn[l], state_gdn_conv[l])
        xs, st = _sample_layer(xs, layers[l], tabs_s, page_table, cache_kt, cache_vt, l, st0)
        s_states.append(st)

    stack = lambda states: [jnp.stack([s[i] for s in states]) for i in range(len(states[0]))]
    p_out = stack(p_states)
    for i in range(2):
        p_out[i] = p_out[i].transpose(0, 1, 4, 2, 3)
    return (xp.reshape(batch, seq, D_MODEL), xs.reshape(nseq, 1, D_MODEL), *p_out, *stack(s_states))
```

```python
import functools
import math

import jax
import jax.numpy as jnp
import numpy as np
from jax import lax
from jax.experimental import pallas as pl
from jax.experimental.pallas import tpu as pltpu

F32 = jnp.float32
BF16 = jnp.bfloat16

D_MODEL = 1024
N_BRANCH = 4
BRANCH_WIDTH = D_MODEL // N_BRANCH
HEADS = 4
HEAD_DIM = BRANCH_WIDTH // HEADS
S5_GROUP = 16
S5_GROUPS = BRANCH_WIDTH // S5_GROUP
S5_P = 64
S5_STATE = S5_GROUPS * S5_P
GLA_RANK = 16
GLA_TAU = 16.0
GDN_CONV = 4
GDN_QKV = 3 * BRANCH_WIDTH
MOBA_BLOCK = 256
MOBA_TOPK = 3
ROT_DIM = HEAD_DIM // 4
ROPE_THETA = 500000.0
PAGE_SIZE = 128
D_FF = 4 * D_MODEL
EPS = 1e-6
NEG = -1e30

LANES = 128
S5_CHUNK = 8
GLA_SUB = 16
GDN_CHUNK = 64
VMEM_LIMIT = 56 * 1024 * 1024

HIGHEST = lax.Precision.HIGHEST


def _cparams(*sem):
    return pltpu.CompilerParams(dimension_semantics=sem, vmem_limit_bytes=VMEM_LIMIT)


def _const_spec(shape):
    zeros = (0,) * len(shape)
    return pl.BlockSpec(shape, lambda *_: zeros)


def _layer_spec(shape, layer, **kw):
    zeros = (0,) * len(shape)
    return pl.BlockSpec((None,) + tuple(shape), lambda *_: (layer,) + zeros, **kw)


def _dot(a, b):
    return jnp.dot(a, b, preferred_element_type=F32)


def _dot_nt(a, b):
    return lax.dot_general(a, b, (((1,), (1,)), ((), ())), preferred_element_type=F32)


def _dot_tn(a, b):
    return lax.dot_general(a, b, (((0,), (0,)), ((), ())), preferred_element_type=F32)


def _bdot(a, b):
    return _dot(a.astype(BF16), b.astype(BF16))


def _split2(x):
    hi = x.astype(BF16)
    lo = (x - hi.astype(F32)).astype(BF16)
    return hi, lo


def _dot_sel(x, sel):
    hi, lo = _split2(x)
    return _dot(hi, sel) + _dot(lo, sel)


def _dot_sel_exact(x, sel):
    x1 = x.astype(BF16)
    r1 = x - x1.astype(F32)
    x2 = r1.astype(BF16)
    x3 = (r1 - x2.astype(F32)).astype(BF16)
    return _dot(x1, sel) + _dot(x2, sel) + _dot(x3, sel)


def _dot3(a, b):
    ah, al = _split2(a)
    bh, bl = _split2(b)
    return _dot(ah, bh) + (_dot(ah, bl) + _dot(al, bh))


def _rms_rows(x, g):
    return x * lax.rsqrt(jnp.mean(x * x, axis=-1, keepdims=True) + EPS) * g


def _sigmoid(x):
    return 1.0 / (1.0 + jnp.exp(-x))


def _silu(x):
    return x * _sigmoid(x)


def _softplus(x):
    return jnp.maximum(x, 0.0) + jnp.log1p(jnp.exp(-jnp.abs(x)))


def _head_ones():
    r = np.arange(BRANCH_WIDTH) // HEAD_DIM
    return jnp.asarray(r[:, None] == r[None, :], BF16)


IN_OUTS = (("s5_u_lo", 128), ("s5_u_hi", 128), ("gla_q", 256), ("gla_k", 256), ("gla_v", 256), ("gla_r", 256),
           ("gla_lr", 128), ("gdn_qkv", 768), ("gdn_b", 256), ("gdn_a", 256), ("gdn_z", 256),
           ("moba_q", 256), ("moba_k", 256), ("moba_v", 256))
IN_WIDTH = sum(w for _, w in IN_OUTS)


def _regroup_w_in(w_in):
    sizes = (256, 256, 256, 256, GLA_RANK, 256, 256, 256, 256, HEADS, HEADS, 256, 256, 256, 256)
    offs = np.cumsum((0,) + sizes)
    (s5_u, a_q, a_k, a_v, a_lr, a_r, d_q, d_k, d_v, d_b, d_a, d_z, m_q, m_k, m_v) = (
        w_in[:, offs[i]:offs[i + 1]] for i in range(len(sizes)))
    lr = jnp.pad(a_lr, ((0, 0), (0, 128 - GLA_RANK)))
    cols = [s5_u, a_q, a_k, a_v, a_r, lr, d_q, d_k, d_v,
            jnp.repeat(d_b, HEAD_DIM, axis=1), jnp.repeat(d_a, HEAD_DIM, axis=1), d_z, m_q, m_k, m_v]
    return jnp.concatenate(cols, axis=1).astype(BF16)


def _inproj_kernel(x_ref, g_ref, w_ref, wg_ref, bg_ref, alog_ref, dtb_ref, *outs):
    x = x_ref[...]
    hb = _rms_rows(x, g_ref[...]).astype(BF16)
    vals = {}
    off = 0
    for name, n in IN_OUTS:
        vals[name] = _dot(hb, w_ref[:, off:off + n])
        off += n
    z = _bdot(vals["gla_lr"], wg_ref[...]) + bg_ref[...]
    vals["gla_lr"] = -_softplus(-z) * (1.0 / GLA_TAU)
    vals["gdn_b"] = _sigmoid(vals["gdn_b"])
    vals["gdn_a"] = -jnp.exp(alog_ref[...]) * _softplus(vals["gdn_a"] + dtb_ref[...])
    for (name, _), o_ref in zip(IN_OUTS, outs):
        o_ref[...] = vals[name]


def _inproj(x2d, lp, tm):
    n = x2d.shape[0]
    out_shape = []
    out_specs = []
    for name, w in IN_OUTS:
        w_out = 256 if name == "gla_lr" else w
        out_shape.append(jax.ShapeDtypeStruct((n, w_out), F32))
        out_specs.append(pl.BlockSpec((tm, w_out), lambda i: (i, 0)))
    res = pl.pallas_call(
        _inproj_kernel,
        grid=(n // tm,),
        in_specs=[pl.BlockSpec((tm, D_MODEL), lambda i: (i, 0)),
                  _const_spec((1, D_MODEL)), _const_spec((D_MODEL, IN_WIDTH)),
                  _const_spec((128, 256)), _const_spec((1, 256)),
                  _const_spec((1, 256)), _const_spec((1, 256))],
        out_specs=out_specs,
        out_shape=out_shape,
        compiler_params=_cparams("parallel"),
        name="inproj",
    )(x2d, lp["ln1_g"], lp["w_in"], lp["gla_wg"], lp["gla_bg"], lp["gdn_alog"], lp["gdn_dtb"])
    return dict(zip((nm for nm, _ in IN_OUTS), res))


def _merge_kernel(x_ref, ya_lo_ref, ya_hi_ref, yb_ref, yc_ref, yd_ref, g_ref, wgate_ref, wglu_ref, bglu_ref,
                  wbr_ref, wout_ref, o_ref):
    x = x_ref[...]
    hb = _rms_rows(x, g_ref[...]).astype(BF16)
    ya = jnp.concatenate([ya_lo_ref[...], ya_hi_ref[...]], axis=1)
    ya = ya * _sigmoid(_bdot(ya, wglu_ref[...]) + bglu_ref[...])
    merged = None
    for i, y in enumerate((ya, yb_ref[...], yc_ref[...], yd_ref[...])):
        gate = _sigmoid(_dot(hb, wgate_ref[:, i * D_MODEL:(i + 1) * D_MODEL]))
        term = gate * _dot(y.astype(BF16), wbr_ref[i])
        merged = term if merged is None else merged + term
    o_ref[...] = x + _bdot(merged, wout_ref[...])


def _merge(x2d, ya, yb, yc, yd, lp, tm):
    n = x2d.shape[0]
    row = lambda w: pl.BlockSpec((tm, w), lambda i: (i, 0))
    ya_lo, ya_hi = ya
    return pl.pallas_call(
        _merge_kernel,
        grid=(n // tm,),
        in_specs=[row(D_MODEL), row(128), row(128), row(256), row(256), row(256),
                  _const_spec((1, D_MODEL)), _layer_spec((D_MODEL, N_BRANCH * D_MODEL), lp["layer"]),
                  _const_spec((256, 256)), _const_spec((1, 256)),
                  _const_spec((N_BRANCH, 256, D_MODEL)), _layer_spec((D_MODEL, D_MODEL), lp["layer"])],
        out_specs=row(D_MODEL),
        out_shape=jax.ShapeDtypeStruct((n, D_MODEL), F32),
        compiler_params=_cparams("parallel"),
        name="merge",
    )(x2d, ya_lo, ya_hi, yb, yc, yd, lp["ln1_g"], lp["w_gate"], lp["s5_w_glu"], lp["s5_b_glu"],
      lp["w_br"], lp["w_out"])


def _mlp_kernel(x_ref, g_ref, w1_ref, w2_ref, o_ref):
    x = x_ref[...]
    hb = _rms_rows(x, g_ref[...]).astype(BF16)
    z = jnp.maximum(_dot(hb, w1_ref[...]), 0.0)
    o_ref[...] = x + _bdot(z * z, w2_ref[...])


def _mlp(x2d, lp, tm):
    n = x2d.shape[0]
    row = pl.BlockSpec((tm, D_MODEL), lambda i: (i, 0))
    single = pl.Buffered(1)
    return pl.pallas_call(
        _mlp_kernel,
        grid=(n // tm,),
        in_specs=[row, _const_spec((1, D_MODEL)),
                  _layer_spec((D_MODEL, D_FF), lp["layer"], pipeline_mode=single),
                  _layer_spec((D_FF, D_MODEL), lp["layer"], pipeline_mode=single)],
        out_specs=row,
        out_shape=jax.ShapeDtypeStruct((n, D_MODEL), F32),
        compiler_params=_cparams("parallel"),
        name="mlp",
    )(x2d, lp["ln2_g"], lp["w_ff1"], lp["w_ff2"])


def _s5_matrices(p, chunk):
    hp = dict(precision=HIGHEST)
    dt = jnp.exp(p["s5_log_dt"])[:, None]
    ar, ai = p["s5_a_re"], p["s5_a_im"]
    mag = jnp.exp(ar * dt)
    abar_re = mag * jnp.cos(ai * dt)
    abar_im = mag * jnp.sin(ai * dt)
    den = ar * ar + ai * ai
    nr = abar_re - 1.0
    f_re = (nr * ar + abar_im * ai) / den
    f_im = (abar_im * ar - nr * ai) / den
    br, bi = p["s5_b_re"], p["s5_b_im"]
    bbar_re = f_re[..., None] * br - f_im[..., None] * bi
    bbar_im = f_re[..., None] * bi + f_im[..., None] * br
    pw_re = [jnp.ones_like(abar_re)]
    pw_im = [jnp.zeros_like(abar_re)]
    for _ in range(chunk):
        r, i = pw_re[-1], pw_im[-1]
        pw_re.append(r * abar_re - i * abar_im)
        pw_im.append(r * abar_im + i * abar_re)
    pw_re = jnp.stack(pw_re)
    pw_im = jnp.stack(pw_im)
    cr, ci = p["s5_c_re"], p["s5_c_im"]
    ca_re = cr[None] * pw_re[:, :, None, :] - ci[None] * pw_im[:, :, None, :]
    ca_im = cr[None] * pw_im[:, :, None, :] + ci[None] * pw_re[:, :, None, :]
    def table(t):
        lead = t.shape[:-3]
        t = jnp.moveaxis(t, -1, -3)
        return t.reshape(lead + (t.shape[-3], -1))

    kern = (jnp.einsum("tgop,gpi->tgoi", ca_re[:chunk], bbar_re, **hp)
            - jnp.einsum("tgop,gpi->tgoi", ca_im[:chunk], bbar_im, **hp))
    rev_re = pw_re[:chunk][::-1]
    rev_im = pw_im[:chunk][::-1]
    ab_re = rev_re[..., None] * bbar_re[None] - rev_im[..., None] * bbar_im[None]
    ab_im = rev_re[..., None] * bbar_im[None] + rev_im[..., None] * bbar_re[None]
    a_pow = jnp.stack([pw_re[chunk].reshape(-1), pw_im[chunk].reshape(-1)])
    halves = lambda t, n: jnp.stack([t[..., :n], t[..., n:]])
    xs_re, xs_im = halves(table(ab_re), S5_HALF_STATE), halves(table(ab_im), S5_HALF_STATE)
    hy = jnp.stack([table(ca_re[1:]), -table(ca_im[1:])], axis=1)
    return dict(lag=halves(table(kern), S5_HALF),
                xs=jnp.concatenate([xs_re, xs_im], axis=-1),
                hy=halves(hy, S5_HALF),
                a_pow=a_pow, d_row=p["s5_d"][None, :])


S5_HALF = BRANCH_WIDTH // 2
S5_HALF_GROUPS = S5_GROUPS // 2
S5_HALF_STATE = S5_STATE // 2


def _s5_expand(lag_ref, xs_ref, hy_ref, wxs_scr, win_scr, why_scr):
    chunk = lag_ref.shape[1]
    w = S5_HALF

    def group_of(shape, axis, per_group, wrap=None):
        idx = lax.broadcasted_iota(jnp.int32, shape, axis)
        if wrap is not None:
            idx = idx % wrap
        return idx // per_group

    same_ii = group_of((w, w), 0, S5_GROUP) == group_of((w, w), 1, S5_GROUP)
    same_is = (group_of((w, 2 * S5_HALF_STATE), 0, S5_GROUP)
               == group_of((w, 2 * S5_HALF_STATE), 1, S5_P, wrap=S5_HALF_STATE))
    same_si = group_of((S5_HALF_STATE, w), 0, S5_P) == group_of((S5_HALF_STATE, w), 1, S5_GROUP)
    down = lambda t: jnp.concatenate([t] * S5_HALF_GROUPS, axis=0)
    zero = jnp.zeros((w, w), BF16)
    for hf in range(2):
        lags = [jnp.where(same_ii, down(lag_ref[hf, tau]), 0.0).astype(BF16) for tau in range(chunk)]
        for s in range(chunk):
            wxs_scr[hf, s * w:(s + 1) * w, :] = jnp.where(same_is, down(xs_ref[hf, s]), 0.0).astype(BF16)
            win_scr[hf, s * w:(s + 1) * w, :] = jnp.concatenate(
                [lags[t - s] if t >= s else zero for t in range(chunk)], axis=1)
            for part in range(2):
                why_scr[hf, part * S5_HALF_STATE:(part + 1) * S5_HALF_STATE, s * w:(s + 1) * w] = jnp.where(
                    same_si, down(hy_ref[hf, s, part]), 0.0).astype(BF16)


def _s5_state_increment(u_halves, wxs_ref):
    lo = _dot(u_halves[0], wxs_ref[0])
    hi = _dot(u_halves[1], wxs_ref[1])
    n = S5_HALF_STATE
    return jnp.concatenate([lo[:, :n], hi[:, :n], lo[:, n:], hi[:, n:]], axis=1)


def _s5_outputs(u_halves, h_bf16, win_ref, why_ref):
    n = S5_HALF_STATE
    outs = []
    for hf in range(2):
        h_half = jnp.concatenate([h_bf16[:, hf * n:(hf + 1) * n],
                                  h_bf16[:, S5_STATE + hf * n:S5_STATE + (hf + 1) * n]], axis=1)
        outs.append(_dot(u_halves[hf], win_ref[hf]) + _dot(h_half, why_ref[hf]))
    return outs


def _gelu(y):
    c = math.sqrt(2.0 / math.pi)
    return 0.5 * y * (1.0 + jnp.tanh(c * (y + 0.044715 * (y * y * y))))


def _s5_kernel(u_lo_ref, u_hi_ref, lag_ref, xs_ref, hy_ref, apow_ref, d_ref, y_lo_ref, y_hi_ref, hfin_ref,
               wxs_scr, win_scr, why_scr, xs_scr, hs_scr):
    chunk = lag_ref.shape[1]
    rows = u_lo_ref.shape[0] // chunk

    @pl.when(pl.program_id(0) == 0)
    def _():
        _s5_expand(lag_ref, xs_ref, hy_ref, wxs_scr, win_scr, why_scr)

    us = [jnp.concatenate([ref[pl.ds(s, rows, stride=chunk), :] for s in range(chunk)], axis=1)
          for ref in (u_lo_ref, u_hi_ref)]
    ubs = [u.astype(BF16) for u in us]
    xs_scr[...] = _s5_state_increment(ubs, wxs_scr)
    a_re = apow_ref[0:1, :]
    a_im = apow_ref[1:2, :]

    def step(r, carry):
        hr, hi = carry
        hs_scr[pl.ds(r, 1), :] = jnp.concatenate([hr, hi], axis=-1)
        x = xs_scr[pl.ds(r, 1), :]
        nhr = a_re * hr - a_im * hi + x[:, :S5_STATE]
        nhi = a_re * hi + a_im * hr + x[:, S5_STATE:]
        return nhr, nhi

    zero = jnp.zeros((1, S5_STATE), F32)
    hr, hi = lax.fori_loop(0, rows, step, (zero, zero))
    hfin_ref[0] = jnp.concatenate([hr, hi], axis=-1)
    ys = _s5_outputs(ubs, hs_scr[...].astype(BF16), win_scr, why_scr)
    for hf, y_ref in enumerate((y_lo_ref, y_hi_ref)):
        d = d_ref[:, hf * S5_HALF:(hf + 1) * S5_HALF]
        for t in range(chunk):
            cols = slice(t * S5_HALF, (t + 1) * S5_HALF)
            y_ref[pl.ds(t, rows, stride=chunk), :] = _gelu(ys[hf][:, cols] + d * us[hf][:, cols])


def _s5_scratch(chunk):
    width = chunk * S5_HALF
    return [pltpu.VMEM((2, width, 2 * S5_HALF_STATE), BF16), pltpu.VMEM((2, width, width), BF16),
            pltpu.VMEM((2, 2 * S5_HALF_STATE, width), BF16)]


def _s5_prompt(u_lo, u_hi, sm, batch, seq):
    c = sm["lag"].shape[1]
    rows = seq // c
    half = BRANCH_WIDTH // 2
    tok = pl.BlockSpec((seq, half), lambda b: (b, 0))
    y_lo, y_hi, hfin = pl.pallas_call(
        _s5_kernel,
        grid=(batch,),
        in_specs=[tok, tok, _const_spec(sm["lag"].shape), _const_spec(sm["xs"].shape),
                  _const_spec(sm["hy"].shape), _const_spec((2, S5_STATE)), _const_spec((1, BRANCH_WIDTH))],
        out_specs=[tok, tok, pl.BlockSpec((1, 1, 2 * S5_STATE), lambda b: (b, 0, 0))],
        out_shape=[jax.ShapeDtypeStruct((batch * seq, half), F32),
                   jax.ShapeDtypeStruct((batch * seq, half), F32),
                   jax.ShapeDtypeStruct((batch, 1, 2 * S5_STATE), F32)],
        scratch_shapes=_s5_scratch(c) + [pltpu.VMEM((rows, 2 * S5_STATE), F32),
                                         pltpu.VMEM((rows, 2 * S5_STATE), F32)],
        compiler_params=_cparams("arbitrary"),
        name="s5_prompt",
    )(u_lo, u_hi, sm["lag"], sm["xs"], sm["hy"], sm["a_pow"], sm["d_row"])
    return (y_lo, y_hi), hfin.reshape(batch, 2 * S5_STATE)


GLA_ROWS = 256


def _gla_kernel(q_ref, k_ref, v_ref, g_ref, r_ref, tri_ref, ones_ref, bmask_ref, gn_ref,
                y_ref, sfin_ref, st_scr):
    step = pl.program_id(1)
    sub = GLA_SUB
    rows = q_ref.shape[0]
    ns = rows // sub

    @pl.when(step == 0)
    def _():
        st_scr[...] = jnp.zeros_like(st_scr)

    ones = ones_ref[...]
    bmask = bmask_ref[...]
    split = lambda a: a.reshape(ns, sub, BRANCH_WIDTH)
    bc = split(jnp.dot(tri_ref[...], g_ref[...], precision=HIGHEST, preferred_element_type=F32))
    q = split(q_ref[...] * (HEAD_DIM ** -0.5))
    k = split(k_ref[...])
    v = split(v_ref[...])

    row = lax.broadcasted_iota(jnp.int32, (ns, sub, BRANCH_WIDTH), 1)
    parts = []
    for j in range(sub):
        e = jnp.exp(jnp.minimum(bc - bc[:, j:j + 1, :], 0.0))
        parts.append(jnp.where(row >= j, q * k[:, j:j + 1, :] * e, 0.0))
    att = _dot(jnp.concatenate(parts, axis=1).reshape(ns * sub * sub, BRANCH_WIDTH).astype(BF16), ones)
    att = att.reshape(ns, sub * sub, BRANCH_WIDTH)
    o = att[:, 0:sub] * v[:, 0:1, :]
    for j in range(1, sub):
        o = o + att[:, j * sub:(j + 1) * sub] * v[:, j:j + 1, :]

    last = bc[:, sub - 1:sub, :]
    qt = (q * jnp.exp(bc)).astype(BF16)
    kt = (k * jnp.exp(last - bc)).astype(BF16)
    vb = v.astype(BF16)
    decay = jnp.exp(last)
    outer = [bmask * _dot_tn(vb[s], kt[s]) for s in range(ns)]
    st = st_scr[...]
    inter = []
    for s in range(ns):
        inter.append(_dot_nt(qt[s], st.astype(BF16)))
        st = st * decay[s] + outer[s]
    st_scr[...] = st
    o = o.reshape(rows, BRANCH_WIDTH) + jnp.concatenate(inter, axis=0)
    ms = _dot_sel(o * o, ones) * (1.0 / HEAD_DIM)
    y_ref[...] = o * lax.rsqrt(ms + EPS) * gn_ref[...] * _silu(r_ref[...])

    @pl.when(step == pl.num_programs(1) - 1)
    def _():
        sfin_ref[0] = st


def _sub_tril(rows, sub):
    i = np.arange(rows)
    return jnp.asarray((i[:, None] // sub == i[None, :] // sub) & (i[None, :] <= i[:, None]), F32)


def _unpack_state_t(st):
    b = st.shape[0]
    st = st.reshape(b, HEADS, HEAD_DIM, HEADS, HEAD_DIM)
    diag = jnp.stack([st[:, h, :, h, :] for h in range(HEADS)], axis=1)
    return diag.transpose(0, 1, 3, 2)


def _gla_prompt(q, k, v, g, r, lp, batch, seq):
    rows = min(GLA_ROWS, seq)
    nsteps = seq // rows
    blk = pl.BlockSpec((rows, BRANCH_WIDTH), lambda b, c: (b * nsteps + c, 0))
    hm = np.arange(BRANCH_WIDTH) // HEAD_DIM
    bmask = jnp.asarray(hm[:, None] == hm[None, :], F32)
    y, sfin = pl.pallas_call(
        _gla_kernel,
        grid=(batch, nsteps),
        in_specs=[blk, blk, blk, blk, blk, _const_spec((rows, rows)),
                  _const_spec((BRANCH_WIDTH, BRANCH_WIDTH)), _const_spec((BRANCH_WIDTH, BRANCH_WIDTH)),
                  _const_spec((1, BRANCH_WIDTH))],
        out_specs=[blk, pl.BlockSpec((1, BRANCH_WIDTH, BRANCH_WIDTH), lambda b, c: (b, 0, 0))],
        out_shape=[jax.ShapeDtypeStruct((batch * seq, BRANCH_WIDTH), F32),
                   jax.ShapeDtypeStruct((batch, BRANCH_WIDTH, BRANCH_WIDTH), F32)],
        scratch_shapes=[pltpu.VMEM((BRANCH_WIDTH, BRANCH_WIDTH), F32)],
        compiler_params=_cparams("parallel", "arbitrary"),
        name="gla_prompt",
    )(q, k, v, g, r, _sub_tril(rows, GLA_SUB), _head_ones(), bmask, lp["gla_norm"])
    return y, _unpack_state_t(sfin)


CONV_PAD = 8


GDN_ROWS = 512


def _block_diag(x, ones):
    return jnp.concatenate([x] * HEADS, axis=0) * ones


def _unit_lower_inverses(ns, eye, ones):
    c = ns[0].shape[0]
    every = range(len(ns))
    invs = [eye - n for n in ns]
    pws = list(ns)
    for _ in range(int(math.log2(c)) - 1):
        pbs = [pw.astype(BF16) for pw in pws]
        pws = [_dot(pb, _block_diag(pb, ones)) for pb in pbs]
        invs = [invs[i] + _dot(invs[i].astype(BF16), _block_diag(pws[i].astype(BF16), ones)) for i in every]
    inv_parts = [_split2(inv) for inv in invs]
    n_parts = [_split2(n) for n in ns]
    prods = [_dot(jnp.concatenate(n_parts[i], axis=0), _block_diag(inv_parts[i][0], ones)) for i in every]
    cross = [_dot(n_parts[i][0], _block_diag(inv_parts[i][1], ones)) for i in every]
    resids = [eye - invs[i] - (prods[i][:c] + prods[i][c:] + cross[i]) for i in every]
    return [invs[i] + _dot(inv_parts[i][0], _block_diag(resids[i].astype(BF16), ones)) for i in every]


def _gdn_kernel(x_ref, b_ref, g_ref, z_ref, cw_ref, tri_ref, ones_ref, eye_ref, gn_ref,
                y_ref, sfin_ref, conv_ref, s_scr, buf_scr):
    step = pl.program_id(1)
    rows = x_ref.shape[0]
    c = min(GDN_CHUNK, rows)

    @pl.when(step == 0)
    def _():
        s_scr[...] = jnp.zeros_like(s_scr)
        buf_scr[0:CONV_PAD, :] = jnp.zeros((CONV_PAD, GDN_QKV), F32)

    x = x_ref[...]
    buf_scr[CONV_PAD:CONV_PAD + rows, :] = x
    conv = cw_ref[GDN_CONV - 1:GDN_CONV, :] * x
    for w in range(GDN_CONV - 1):
        lag = GDN_CONV - 1 - w
        conv = conv + cw_ref[w:w + 1, :] * buf_scr[CONV_PAD - lag:CONV_PAD - lag + rows, :]
    tail = buf_scr[rows:rows + CONV_PAD, :]
    buf_scr[0:CONV_PAD, :] = tail
    conv_ref[0] = tail
    qkv = _silu(conv)
    ones = ones_ref[...]
    onesf = ones.astype(F32)
    eye = eye_ref[...]
    q = qkv[:, 0:BRANCH_WIDTH]
    k = qkv[:, BRANCH_WIDTH:2 * BRANCH_WIDTH]
    v = qkv[:, 2 * BRANCH_WIDTH:]
    q = q * lax.rsqrt(_dot_sel(q * q, ones) + EPS) * (HEAD_DIM ** -0.5)
    k = k * lax.rsqrt(_dot_sel(k * k, ones) + EPS)
    beta = b_ref[...]
    gc = jnp.dot(tri_ref[...], g_ref[...], precision=HIGHEST, preferred_element_type=F32)
    gam = jnp.exp(gc)
    ri = lax.broadcasted_iota(jnp.int32, (c, BRANCH_WIDTH), 0)
    cj = lax.broadcasted_iota(jnp.int32, (c, BRANCH_WIDTH), 1) % HEAD_DIM

    every = range(rows // c)
    sls = [slice(n * c, (n + 1) * c) for n in every]
    kcs = [k[rs].astype(BF16) for rs in sls]
    kqs = [_dot_nt(jnp.concatenate([k[rs], q[rs]], axis=0).astype(BF16), _block_diag(kcs[n], ones))
           for n, rs in enumerate(sls)]
    decs = []
    for rs in sls:
        grow = jnp.sum(gc[rs] * eye, axis=0, keepdims=True)
        decs.append(jnp.where(ri >= cj, jnp.exp(jnp.minimum(gc[rs] - grow, 0.0)), 0.0))
    invs = _unit_lower_inverses(
        [jnp.where(ri > cj, beta[rs] * decs[n] * kqs[n][:c], 0.0) for n, rs in enumerate(sls)], eye, ones)
    invbs = [inv.astype(BF16) for inv in invs]
    ws_m = [_dot(invbs[n], _block_diag((beta[rs] * gam[rs] * k[rs]).astype(BF16), ones))
            for n, rs in enumerate(sls)]
    u0s = [_dot(invbs[n], _block_diag((beta[rs] * v[rs]).astype(BF16), ones)) for n, rs in enumerate(sls)]
    aqks = [(decs[n] * kqs[n][c:]).astype(BF16) for n in every]
    lhs = [jnp.concatenate([ws_m[n], gam[rs] * q[rs]], axis=0).astype(BF16) for n, rs in enumerate(sls)]
    glasts = [gc[rs][c - 1:c, :] for rs in sls]
    kds = [(k[rs] * jnp.exp(glasts[n] - gc[rs])).astype(BF16) for n, rs in enumerate(sls)]

    s = s_scr[...]
    outs = []
    for n in every:
        ws = _dot(lhs[n], s.astype(BF16))
        ub = (u0s[n] - ws[:c]).astype(BF16)
        outs.append(ws[c:] + _dot(aqks[n], _block_diag(ub, ones)))
        s = jnp.exp(glasts[n]) * s + onesf * _dot_tn(kds[n], ub)
    s_scr[...] = s
    o = jnp.concatenate(outs, axis=0)
    ms = _dot_sel(o * o, ones) * (1.0 / HEAD_DIM)
    y_ref[...] = o * lax.rsqrt(ms + EPS) * gn_ref[...] * _silu(z_ref[...])

    @pl.when(step == pl.num_programs(1) - 1)
    def _():
        sfin_ref[0] = s


def _unpack_state(st):
    b = st.shape[0]
    st = st.reshape(b, HEADS, HEAD_DIM, HEADS, HEAD_DIM)
    return jnp.stack([st[:, h, :, h, :] for h in range(HEADS)], axis=1)


def _gdn_prompt(x, beta, g, z, lp, batch, seq):
    rows = min(GDN_ROWS, seq)
    c = min(GDN_CHUNK, rows)
    nsteps = seq // rows
    blk = lambda w: pl.BlockSpec((rows, w), lambda b, s: (b * nsteps + s, 0))
    eye = jnp.asarray(np.tile(np.eye(c, dtype=np.float32), (1, HEADS)))
    y, sfin, conv = pl.pallas_call(
        _gdn_kernel,
        grid=(batch, nsteps),
        in_specs=[blk(GDN_QKV), blk(256), blk(256), blk(256), _const_spec((CONV_PAD, GDN_QKV)),
                  _const_spec((rows, rows)), _const_spec((256, 256)), _const_spec((c, 256)),
                  _const_spec((1, 256))],
        out_specs=[blk(256),
                   pl.BlockSpec((1, 256, 256), lambda b, s: (b, 0, 0)),
                   pl.BlockSpec((1, CONV_PAD, GDN_QKV), lambda b, s: (b, 0, 0))],
        out_shape=[jax.ShapeDtypeStruct((batch * seq, 256), F32),
                   jax.ShapeDtypeStruct((batch, 256, 256), F32),
                   jax.ShapeDtypeStruct((batch, CONV_PAD, GDN_QKV), F32)],
        scratch_shapes=[pltpu.VMEM((256, 256), F32),
                        pltpu.VMEM((CONV_PAD + rows, GDN_QKV), F32)],
        compiler_params=_cparams("parallel", "arbitrary"),
        name="gdn_prompt",
    )(x, beta, g, z, lp["gdn_conv_w"], _sub_tril(rows, c), _head_ones(), eye, lp["gdn_norm"])
    return y, _unpack_state(sfin), conv[:, CONV_PAD - (GDN_CONV - 1):, :]


HALF_ROT = ROT_DIM // 2


def _rope_tables(pos):
    inv = ROPE_THETA ** (-jnp.arange(HALF_ROT, dtype=F32) / HALF_ROT)
    ang = pos.astype(F32)[:, None] * inv[None, :]
    cos, sin = jnp.cos(ang), jnp.sin(ang)
    n = pos.shape[0]
    rest = HEAD_DIM - ROT_DIM
    head = lambda a, b, fill: jnp.concatenate([a, b, jnp.full((n, rest), fill, F32)], axis=1)
    zero = jnp.zeros_like(sin)
    tabs = [head(cos, cos, 1.0), head(-sin, zero, 0.0), head(zero, sin, 0.0)]
    return jnp.stack([jnp.tile(t, (1, HEADS)) for t in tabs])


def _qk_norm_rope(x, gain, tab_ref, ones):
    y = x * lax.rsqrt(_dot_sel(x * x, ones) * (1.0 / HEAD_DIM) + EPS) * gain
    up = pltpu.roll(y, BRANCH_WIDTH - HALF_ROT, 1)
    down = pltpu.roll(y, HALF_ROT, 1)
    return y * tab_ref[0] + up * tab_ref[1] + down * tab_ref[2]


K_AUG = 2 * HEAD_DIM
V_AUG = HEAD_DIM + 16


def _moba_prep_kernel(q_ref, k_ref, v_ref, tab_ref, qg_ref, kg_ref, ones_ref,
                      kt_ref, vtt_ref, qt_ref, kh_ref, vt_ref, km_ref):
    ones = ones_ref[...]
    rows = q_ref.shape[0]
    nblk = rows // MOBA_BLOCK
    mq = _qk_norm_rope(q_ref[...], qg_ref[...], tab_ref, ones) * (HEAD_DIM ** -0.5)
    mk = _qk_norm_rope(k_ref[...], kg_ref[...], tab_ref, ones)
    kmean = jnp.mean(mk.reshape(nblk, MOBA_BLOCK, BRANCH_WIDTH), axis=1)
    qt = mq.T
    kt = mk.T
    vt = v_ref[...].T
    one_col = (lax.broadcasted_iota(jnp.int32, (MOBA_BLOCK, K_AUG - HEAD_DIM), 1) == 0).astype(BF16)
    one_row = (lax.broadcasted_iota(jnp.int32, (V_AUG - HEAD_DIM, MOBA_BLOCK), 0) == 0).astype(BF16)
    for h in range(HEADS):
        sl = slice(h * HEAD_DIM, (h + 1) * HEAD_DIM)
        qt_ref[0, h] = qt[sl, :]
        km_ref[0, h] = kmean[:, sl]
        kt_ref[0, h] = kt[sl, :]
        vtt_ref[0, h] = vt[sl, :]
        for j in range(nblk):
            rs = slice(j * MOBA_BLOCK, (j + 1) * MOBA_BLOCK)
            kh_ref[0, h, j] = jnp.concatenate([mk[rs, sl].astype(BF16), one_col], axis=1)
            vt_ref[0, h, j] = jnp.concatenate([vt[sl, rs].astype(BF16), one_row], axis=0)


def _moba_prep(q, k, v, tabs, lp, batch, seq):
    rows = min(8 * MOBA_BLOCK, seq)
    nsteps = seq // rows
    nblk = rows // MOBA_BLOCK
    nb = seq // MOBA_BLOCK
    blk = pl.BlockSpec((rows, 256), lambda b, r: (b * nsteps + r, 0))
    cache_rows = pl.BlockSpec((1, HEADS, HEAD_DIM, rows), lambda b, r: (b, 0, 0, r))
    return pl.pallas_call(
        _moba_prep_kernel,
        grid=(batch, nsteps),
        in_specs=[blk, blk, blk, pl.BlockSpec((3, rows, 256), lambda b, r: (0, r, 0)),
                  _const_spec((1, 256)), _const_spec((1, 256)), _const_spec((256, 256))],
        out_specs=[cache_rows, cache_rows, cache_rows,
                   pl.BlockSpec((1, HEADS, nblk, MOBA_BLOCK, K_AUG), lambda b, r: (b, 0, r, 0, 0)),
                   pl.BlockSpec((1, HEADS, nblk, V_AUG, MOBA_BLOCK), lambda b, r: (b, 0, r, 0, 0)),
                   pl.BlockSpec((1, HEADS, nblk, HEAD_DIM), lambda b, r: (b, 0, r, 0))],
        out_shape=[jax.ShapeDtypeStruct((batch, HEADS, HEAD_DIM, seq), F32),
                   jax.ShapeDtypeStruct((batch, HEADS, HEAD_DIM, seq), F32),
                   jax.ShapeDtypeStruct((batch, HEADS, HEAD_DIM, seq), F32),
                   jax.ShapeDtypeStruct((batch, HEADS, nb, MOBA_BLOCK, K_AUG), BF16),
                   jax.ShapeDtypeStruct((batch, HEADS, nb, V_AUG, MOBA_BLOCK), BF16),
                   jax.ShapeDtypeStruct((batch, HEADS, nb, HEAD_DIM), F32)],
        compiler_params=_cparams("parallel", "parallel"),
        name="moba_prep",
    )(q, k, v, tabs, lp["moba_q_norm"], lp["moba_k_norm"], _head_ones())


def _moba_attn_kernel(qt_ref, kh_ref, vt_ref, km_ref, o_ref, bias_scr):
    qb = pl.program_id(1)
    nb = km_ref.shape[2]
    blk = MOBA_BLOCK
    heads = range(HEADS)
    blk_id = lax.broadcasted_iota(jnp.int32, (nb, blk), 0)
    kpos = lax.broadcasted_iota(jnp.int32, (blk, blk), 0)
    qpos = lax.broadcasted_iota(jnp.int32, (blk, blk), 1)
    first_row = lax.broadcasted_iota(jnp.int32, (K_AUG - HEAD_DIM, blk), 0) == 0
    qts = [qt_ref[0, h] for h in heads]
    qtb = [qt.astype(BF16) for qt in qts]

    def scores(j, biases):
        out = []
        for h in heads:
            extra = jnp.where(first_row, biases[h], 0.0).astype(BF16)
            out.append(_dot(kh_ref[0, h, j], jnp.concatenate([qtb[h], extra], axis=0)))
        return out

    gates = [jnp.dot(km_ref[0, h], qts[h], precision=HIGHEST, preferred_element_type=F32) for h in heads]
    own = scores(qb, [jnp.zeros((1, blk), F32)] * HEADS)
    for h in heads:
        gate = jnp.where(blk_id < qb, gates[h], NEG)
        taken = jnp.zeros((nb, blk), jnp.bool_)
        for _ in range(min(MOBA_TOPK, nb)):
            best = jnp.max(gate, axis=0, keepdims=True)
            idx = jnp.min(jnp.where(gate == best, blk_id, nb), axis=0, keepdims=True)
            hit = blk_id == idx
            taken = jnp.logical_or(taken, hit)
            gate = jnp.where(hit, -jnp.inf, gate)
        bias_scr[h] = jnp.where(jnp.logical_and(taken, blk_id < qb), 0.0, NEG)

    ms, ps = [], []
    for h in heads:
        s = jnp.where(kpos <= qpos, own[h], NEG)
        m = jnp.max(s, axis=0, keepdims=True)
        ms.append(m)
        ps.append(jnp.exp(s - m).astype(BF16))
    accs = [_dot(vt_ref[0, h, qb], ps[h]) for h in heads]

    def bias_rows(j):
        return [bias_scr[h, pl.ds(j, 1), :] for h in heads]

    def update(carry, blocks):
        ms, accs = carry
        ss = [scores(j, bias_rows(j)) for j in blocks]
        new_m, new_acc = [], []
        for h in heads:
            m_new = ms[h]
            for s in ss:
                m_new = jnp.maximum(m_new, jnp.max(s[h], axis=0, keepdims=True))
            acc = jnp.exp(ms[h] - m_new) * accs[h]
            for j, s in zip(blocks, ss):
                acc = acc + _dot(vt_ref[0, h, j], jnp.exp(s[h] - m_new).astype(BF16))
            new_m.append(m_new)
            new_acc.append(acc)
        return tuple(new_m), tuple(new_acc)

    carry = (tuple(ms), tuple(accs))
    start = 0
    for width in (4, 2, 1):
        trips = (qb - start) // width
        carry = lax.fori_loop(
            0, trips, lambda i, c, w=width, s=start: update(c, tuple(s + w * i + e for e in range(w))), carry)
        start = start + trips * width
    ms, accs = carry
    outs = [accs[h][:HEAD_DIM] / accs[h][HEAD_DIM:HEAD_DIM + 1] for h in heads]
    o_ref[...] = jnp.concatenate(outs, axis=0).T


def _moba_prompt(qt, kh, vt, km, batch, seq):
    nb = seq // MOBA_BLOCK
    return pl.pallas_call(
        _moba_attn_kernel,
        grid=(batch, nb),
        in_specs=[pl.BlockSpec((1, HEADS, HEAD_DIM, MOBA_BLOCK), lambda b, i: (b, 0, 0, i)),
                  pl.BlockSpec((1, HEADS, nb, MOBA_BLOCK, K_AUG), lambda b, i: (b, 0, 0, 0, 0)),
                  pl.BlockSpec((1, HEADS, nb, V_AUG, MOBA_BLOCK), lambda b, i: (b, 0, 0, 0, 0)),
                  pl.BlockSpec((1, HEADS, nb, HEAD_DIM), lambda b, i: (b, 0, 0, 0))],
        out_specs=pl.BlockSpec((MOBA_BLOCK, 256), lambda b, i: (b * nb + i, 0)),
        out_shape=jax.ShapeDtypeStruct((batch * seq, 256), F32),
        scratch_shapes=[pltpu.VMEM((HEADS, nb, MOBA_BLOCK), F32)],
        compiler_params=_cparams("parallel", "arbitrary"),
        name="moba_prompt",
    )(qt, kh, vt, km)


def _s5_step_kernel(u_ref, h0_ref, lag_ref, xs_ref, hy_ref, apow_ref, d_ref, y_ref, h_ref,
                    wxs_ref, win_ref, why_ref):
    _s5_expand(lag_ref, xs_ref, hy_ref, wxs_ref, win_ref, why_ref)
    u = u_ref[...]
    ubs = [u[:, :S5_HALF].astype(BF16), u[:, S5_HALF:].astype(BF16)]
    h0 = h0_ref[...]
    xs = _s5_state_increment(ubs, wxs_ref)
    a_re = apow_ref[0:1, :]
    a_im = apow_ref[1:2, :]
    hr0 = h0[:, :S5_STATE]
    hi0 = h0[:, S5_STATE:]
    hr = a_re * hr0 - a_im * hi0 + xs[:, :S5_STATE]
    hi = a_re * hi0 + a_im * hr0 + xs[:, S5_STATE:]
    h_ref[...] = jnp.concatenate([hr, hi], axis=-1)
    y = jnp.concatenate(_s5_outputs(ubs, h0.astype(BF16), win_ref, why_ref), axis=1)
    y_ref[...] = _gelu(y + d_ref[...] * u)


def _s5_step(u, h0, sm):
    n = u.shape[0]
    return pl.pallas_call(
        _s5_step_kernel,
        out_shape=[jax.ShapeDtypeStruct((n, BRANCH_WIDTH), F32),
                   jax.ShapeDtypeStruct((n, 2 * S5_STATE), F32)],
        scratch_shapes=_s5_scratch(1),
        compiler_params=pltpu.CompilerParams(vmem_limit_bytes=VMEM_LIMIT),
        name="s5_step",
    )(u, h0, sm["lag"], sm["xs"], sm["hy"], sm["a_pow"], sm["d_row"])


def _expand_mats():
    idx = np.arange(HEAD_DIM * HEAD_DIM)
    rep = (np.arange(HEAD_DIM)[:, None] == idx[None, :] // HEAD_DIM)
    til = (np.arange(HEAD_DIM)[:, None] == idx[None, :] % HEAD_DIM)
    return jnp.asarray(rep, BF16), jnp.asarray(til, BF16), jnp.asarray(til.T, BF16)


def _gla_step_kernel(q_ref, k_ref, v_ref, g_ref, r_ref, s0_ref, rep_ref, til_ref, tilt_ref, gn_ref,
                     y_ref, s_ref):
    rep = rep_ref[...]
    eg = _dot_sel_exact(jnp.exp(g_ref[...]), rep)
    kr = _dot_sel_exact(k_ref[...], rep)
    qr = _dot_sel_exact(q_ref[...] * (HEAD_DIM ** -0.5), rep)
    vt = _dot_sel_exact(v_ref[...], til_ref[...])
    s = eg * s0_ref[...] + kr * vt
    s_ref[...] = s
    o = _dot_sel(qr * s, tilt_ref[...])
    y_ref[...] = _rms_rows(o, gn_ref[...]) * _silu(r_ref[...])


def _gdn_conv_step_kernel(x_ref, c0_ref, cw_ref, qkv_ref, cnew_ref):
    x = x_ref[...]
    c0 = c0_ref[...]
    conv = cw_ref[GDN_CONV - 1:GDN_CONV, :] * x
    for w in range(GDN_CONV - 1):
        conv = conv + cw_ref[w:w + 1, :] * c0[:, w * GDN_QKV:(w + 1) * GDN_QKV]
    qkv_ref[...] = _silu(conv)
    cnew_ref[...] = jnp.concatenate([c0[:, GDN_QKV:], x], axis=-1)


def _gdn_step_kernel(q_ref, k_ref, v_ref, b_ref, g_ref, z_ref, s0_ref, rep_ref, til_ref, tilt_ref,
                     gn_ref, y_ref, s_ref):
    q = q_ref[...]
    k = k_ref[...]
    q = q * lax.rsqrt(jnp.sum(q * q, axis=-1, keepdims=True) + EPS) * (HEAD_DIM ** -0.5)
    k = k * lax.rsqrt(jnp.sum(k * k, axis=-1, keepdims=True) + EPS)
    beta = b_ref[:, 0:1]
    gam = jnp.exp(g_ref[:, 0:1])
    rep = rep_ref[...]
    tilt = tilt_ref[...]
    kr = _dot_sel_exact(k, rep)
    qr = _dot_sel_exact(q, rep)
    s0 = s0_ref[...]
    ks = _dot_sel(kr * s0, tilt)
    qs = _dot_sel(qr * s0, tilt)
    u = v_ref[...] - gam * ks
    qk = jnp.sum(q * k, axis=-1, keepdims=True)
    o = gam * qs + (beta * qk) * u
    s_ref[...] = gam * s0 + kr * _dot_sel_exact(beta * u, til_ref[...])
    y_ref[...] = _rms_rows(o, gn_ref[...]) * _silu(z_ref[...])


def _whole_call(kernel, out_shape, name, *args):
    return pl.pallas_call(kernel, out_shape=out_shape, name=name,
                          compiler_params=pltpu.CompilerParams(vmem_limit_bytes=VMEM_LIMIT))(*args)


def _moba_qk_step_kernel(q_ref, k_ref, tab_ref, qg_ref, kg_ref, ones_ref, mq_ref, mk_ref):
    ones = ones_ref[...]
    mq_ref[...] = _qk_norm_rope(q_ref[...], qg_ref[...], tab_ref, ones) * (HEAD_DIM ** -0.5)
    mk_ref[...] = _qk_norm_rope(k_ref[...], kg_ref[...], tab_ref, ones)


SELECT_PAGES = 16
SEL_ROWS = 8


def _moba_select_kernel(pt_ref, q_ref, *refs):
    pages = refs[:SELECT_PAGES]
    sel_ref = refs[SELECT_PAGES]
    gate_scr = refs[SELECT_PAGES + 1]
    g = pl.program_id(1)
    ppb = MOBA_BLOCK // PAGE_SIZE
    qcol = jnp.broadcast_to(q_ref[0], (BRANCH_WIDTH, LANES))
    token_ones = jnp.ones((PAGE_SIZE, LANES), BF16)
    blk_lane = lax.broadcasted_iota(jnp.int32, (HEADS, LANES), 1)

    @pl.when(g == 0)
    def _():
        gate_scr[...] = jnp.full((HEADS, LANES), NEG, F32)

    gate = gate_scr[...]
    for n in range(SELECT_PAGES // ppb):
        tile = pages[n * ppb][0, 0]
        for e in range(1, ppb):
            tile = tile + pages[n * ppb + e][0, 0]
        ksum = _dot(tile.reshape(BRANCH_WIDTH, PAGE_SIZE).astype(BF16), token_ones)
        mean = jnp.sum((ksum * qcol).reshape(HEADS, HEAD_DIM, LANES), axis=1) * (1.0 / MOBA_BLOCK)
        gate = jnp.where(blk_lane == g * (SELECT_PAGES // ppb) + n, mean, gate)
    gate_scr[...] = gate

    @pl.when(g == pl.num_programs(1) - 1)
    def _():
        left = gate
        sel = jnp.zeros((HEADS, LANES), jnp.int32)
        for r in range(MOBA_TOPK):
            best = jnp.max(left, axis=-1, keepdims=True)
            idx = jnp.min(jnp.where(left == best, blk_lane, LANES), axis=-1, keepdims=True)
            sel = jnp.where(blk_lane == r, idx, sel)
            left = jnp.where(blk_lane == idx, -jnp.inf, left)
        sel_ref[0] = sel


def _moba_select(mq, page_table, cache_kt, layer):
    nseq, npages = page_table.shape

    def page_spec(p):
        return pl.BlockSpec((1, 1, HEADS, HEAD_DIM, PAGE_SIZE),
                            lambda b, g, pt: (layer, pt[b, g * SELECT_PAGES + p], 0, 0, 0))

    grid_spec = pltpu.PrefetchScalarGridSpec(
        num_scalar_prefetch=1,
        grid=(nseq, npages // SELECT_PAGES),
        in_specs=[pl.BlockSpec((1, 256, 1), lambda b, g, pt: (b, 0, 0))]
        + [page_spec(p) for p in range(SELECT_PAGES)],
        out_specs=pl.BlockSpec((1, HEADS, LANES), lambda b, g, pt: (b, 0, 0)),
        scratch_shapes=[pltpu.VMEM((HEADS, LANES), F32)],
    )
    sel = pl.pallas_call(
        _moba_select_kernel,
        grid_spec=grid_spec,
        out_shape=jax.ShapeDtypeStruct((nseq, HEADS, LANES), jnp.int32),
        compiler_params=_cparams("parallel", "arbitrary"),
        name="moba_select",
    )(page_table, mq.reshape(nseq, 256, 1), *([cache_kt] * SELECT_PAGES))
    return sel[:, :, :MOBA_TOPK]


N_SEL_PAGES = MOBA_TOPK * (MOBA_BLOCK // PAGE_SIZE)


def _moba_step_kernel(sel_ref, pt_ref, q_ref, kn_ref, vn_ref, *refs):
    kp = refs[:N_SEL_PAGES]
    vp = refs[N_SEL_PAGES:2 * N_SEL_PAGES]
    o_ref = refs[2 * N_SEL_PAGES]
    q = q_ref[0, 0]
    q8 = jnp.broadcast_to(q, (SEL_ROWS, HEAD_DIM)).astype(BF16)
    logits = [_dot(q8, r[0, 0, 0].astype(BF16))[0:1] for r in kp]
    l_self = jnp.sum(q * kn_ref[0, 0], axis=-1, keepdims=True)
    m = l_self
    for lg in logits:
        m = jnp.maximum(m, jnp.max(lg, axis=-1, keepdims=True))
    p_self = jnp.exp(l_self - m)
    den = p_self
    num = p_self * vn_ref[0, 0]
    for lg, r in zip(logits, vp):
        p = jnp.exp(lg - m)
        den = den + jnp.sum(p, axis=-1, keepdims=True)
        p8 = jnp.broadcast_to(p, (SEL_ROWS, PAGE_SIZE)).astype(BF16)
        num = num + _dot_nt(p8, r[0, 0, 0].astype(BF16))[0:1]
    o_ref[0, 0] = num / den


def _moba_step(q, k_new, v_new, sel, page_table, cache_kt, cache_vt, layer):
    nseq, npages = page_table.shape
    ppb = MOBA_BLOCK // PAGE_SIZE

    def page_spec(r, e):
        def index(b, h, sel_ref, pt_ref):
            blk = sel_ref[(b * HEADS + h) * MOBA_TOPK + r]
            return (layer, pt_ref[b * npages + ppb * blk + e], h, 0, 0)
        return pl.BlockSpec((1, 1, 1, HEAD_DIM, PAGE_SIZE), index)

    row = pl.BlockSpec((1, 1, 1, HEAD_DIM), lambda b, h, s, p: (b, h, 0, 0))
    pages = [page_spec(r, e) for r in range(MOBA_TOPK) for e in range(ppb)]
    grid_spec = pltpu.PrefetchScalarGridSpec(
        num_scalar_prefetch=2,
        grid=(nseq, HEADS),
        in_specs=[row, row, row] + pages + pages,
        out_specs=row,
    )
    r4 = lambda a: a.reshape(nseq, HEADS, 1, HEAD_DIM)
    out = pl.pallas_call(
        _moba_step_kernel,
        grid_spec=grid_spec,
        out_shape=jax.ShapeDtypeStruct((nseq, HEADS, 1, HEAD_DIM), F32),
        compiler_params=_cparams("parallel", "parallel"),
        name="moba_step",
    )(sel.reshape(-1), page_table.reshape(-1), r4(q), r4(k_new), r4(v_new),
      *([cache_kt] * N_SEL_PAGES), *([cache_vt] * N_SEL_PAGES))
    return out.reshape(nseq, 256)


def _layer_params(l, w):
    tile4 = lambda a: jnp.tile(a, HEADS)[None]
    rep64 = lambda a: jnp.repeat(a, HEAD_DIM)[None]
    s5p = {k: w[k][l] for k in ("s5_a_re", "s5_a_im", "s5_log_dt", "s5_b_re", "s5_b_im",
                                "s5_c_re", "s5_c_im", "s5_d")}
    return dict(
        ln1_g=w["ln1_g"][l][None], w_in=_regroup_w_in(w["w_in"][l]),
        gla_wg=jnp.pad(w["gla_w_gate"][l], ((0, 128 - GLA_RANK), (0, 0))).astype(BF16),
        gla_bg=w["gla_b_gate"][l][None],
        gdn_alog=rep64(w["gdn_a_log"][l]), gdn_dtb=rep64(w["gdn_dt_bias"][l]),
        s5_prompt=_s5_matrices(s5p, S5_CHUNK), s5_step=_s5_matrices(s5p, 1),
        s5_w_glu=w["s5_w_glu"][l].astype(BF16), s5_b_glu=w["s5_b_glu"][l][None],
        gla_norm=tile4(w["gla_norm"][l]), gla_norm_head=w["gla_norm"][l][None],
        gdn_norm=tile4(w["gdn_norm"][l]), gdn_norm_head=w["gdn_norm"][l][None],
        gdn_conv_w=jnp.pad(w["gdn_conv_w"][l], ((0, CONV_PAD - GDN_CONV), (0, 0))),
        moba_q_norm=tile4(w["moba_q_norm"][l]), moba_k_norm=tile4(w["moba_k_norm"][l]),
        w_br=jnp.stack([w["w_br_s5"][l], w["w_br_gla"][l], w["w_br_gdn"][l], w["w_br_moba"][l]]).astype(BF16),
        ln2_g=w["ln2_g"][l][None],
        layer=l, w_gate=w["w_gate_bf16"], w_out=w["w_out_bf16"], w_ff1=w["w_ff1_bf16"], w_ff2=w["w_ff2_bf16"],
    )


PROMPT_ROWS = 512


def _prompt_layer(x2d, lp, tabs, batch, seq):
    tm = min(PROMPT_ROWS, batch * seq)
    pr = _inproj(x2d, lp, tm)
    ya, s5_fin = _s5_prompt(pr["s5_u_lo"], pr["s5_u_hi"], lp["s5_prompt"], batch, seq)
    yb, gla_s = _gla_prompt(pr["gla_q"], pr["gla_k"], pr["gla_v"], pr["gla_lr"], pr["gla_r"], lp, batch, seq)
    yc, gdn_s, conv = _gdn_prompt(pr["gdn_qkv"], pr["gdn_b"], pr["gdn_a"], pr["gdn_z"], lp, batch, seq)
    kt, vtt, qt, kh, vt, km = _moba_prep(pr["moba_q"], pr["moba_k"], pr["moba_v"], tabs, lp, batch, seq)
    yd = _moba_prompt(qt, kh, vt, km, batch, seq)
    x1 = _merge(x2d, ya, yb, yc, yd, lp, tm)
    x2 = _mlp(x1, lp, tm)
    s5 = lambda a: a.reshape(batch, S5_GROUPS, S5_P)
    states = (kt, vtt, s5(s5_fin[:, :S5_STATE]), s5(s5_fin[:, S5_STATE:]), gla_s, gdn_s, conv)
    return x2, states


def _sample_layer(x2d, lp, tabs, page_table, cache_kt, cache_vt, layer, st):
    s5_re0, s5_im0, gla0, gdn0, conv0 = st
    n = x2d.shape[0]
    pr = _inproj(x2d, lp, n)
    rows = n * HEADS
    per_head = lambda a: a.reshape(rows, HEAD_DIM)
    flat_state = lambda a: a.reshape(rows, HEAD_DIM * HEAD_DIM)
    rep, til, tilt = _expand_mats()
    sds = jax.ShapeDtypeStruct

    h0 = jnp.concatenate([s5_re0.reshape(n, S5_STATE), s5_im0.reshape(n, S5_STATE)], axis=1)
    ya, s5_new = _s5_step(jnp.concatenate([pr["s5_u_lo"], pr["s5_u_hi"]], axis=1), h0, lp["s5_step"])
    ya = (ya[:, :BRANCH_WIDTH // 2], ya[:, BRANCH_WIDTH // 2:])

    yb, gla_s = _whole_call(
        _gla_step_kernel, [sds((rows, HEAD_DIM), F32), sds((rows, HEAD_DIM * HEAD_DIM), F32)], "gla_step",
        per_head(pr["gla_q"]), per_head(pr["gla_k"]), per_head(pr["gla_v"]), per_head(pr["gla_lr"]),
        per_head(pr["gla_r"]), flat_state(gla0), rep, til, tilt, lp["gla_norm_head"])

    qkv, conv_new = _whole_call(
        _gdn_conv_step_kernel, [sds((n, GDN_QKV), F32), sds((n, (GDN_CONV - 1) * GDN_QKV), F32)],
        "gdn_conv_step", pr["gdn_qkv"], conv0.reshape(n, (GDN_CONV - 1) * GDN_QKV), lp["gdn_conv_w"])
    yc, gdn_s = _whole_call(
        _gdn_step_kernel, [sds((rows, HEAD_DIM), F32), sds((rows, HEAD_DIM * HEAD_DIM), F32)], "gdn_step",
        per_head(qkv[:, :256]), per_head(qkv[:, 256:512]), per_head(qkv[:, 512:]),
        per_head(pr["gdn_b"]), per_head(pr["gdn_a"]), per_head(pr["gdn_z"]), flat_state(gdn0),
        rep, til, tilt, lp["gdn_norm_head"])

    mq, mk = _whole_call(
        _moba_qk_step_kernel, [sds((n, 256), F32), sds((n, 256), F32)], "moba_qk_step",
        pr["moba_q"], pr["moba_k"], tabs, lp["moba_q_norm"], lp["moba_k_norm"], _head_ones())
    sel = _moba_select(mq, page_table, cache_kt, layer)
    yd = _moba_step(mq, mk, pr["moba_v"], sel, page_table, cache_kt, cache_vt, layer)

    x1 = _merge(x2d, ya, yb.reshape(n, 256), yc.reshape(n, 256), yd, lp, n)
    x2 = _mlp(x1, lp, n)
    head4 = lambda a: a.reshape(n, 1, HEADS, HEAD_DIM)
    s5 = lambda a: a.reshape(n, S5_GROUPS, S5_P)
    state4 = lambda a: a.reshape(n, HEADS, HEAD_DIM, HEAD_DIM)
    states = (head4(mk), head4(pr["moba_v"]), s5(s5_new[:, :S5_STATE]), s5(s5_new[:, S5_STATE:]),
              state4(gla_s), state4(gdn_s), conv_new.reshape(n, GDN_CONV - 1, GDN_QKV))
    return x2, states


def kernel(x_prompt, x_sample, cache_moba_k, cache_moba_v, page_table, state_s5_re, state_s5_im, state_gla, state_gdn, state_gdn_conv, ln1_g, w_in, s5_a_re, s5_a_im, s5_log_dt, s5_b_re, s5_b_im, s5_c_re, s5_c_im, s5_d, s5_w_glu, s5_b_glu, gla_w_gate, gla_b_gate, gla_norm, gdn_conv_w, gdn_a_log, gdn_dt_bias, gdn_norm, moba_q_norm, moba_k_norm, w_gate, w_br_s5, w_br_gla, w_br_gdn, w_br_moba, w_out, ln2_g, w_ff1, w_ff2):
    weights = dict(ln1_g=ln1_g, w_in=w_in, s5_a_re=s5_a_re, s5_a_im=s5_a_im, s5_log_dt=s5_log_dt,
                   s5_b_re=s5_b_re, s5_b_im=s5_b_im, s5_c_re=s5_c_re, s5_c_im=s5_c_im, s5_d=s5_d,
                   s5_w_glu=s5_w_glu, s5_b_glu=s5_b_glu, gla_w_gate=gla_w_gate, gla_b_gate=gla_b_gate,
                   gla_norm=gla_norm, gdn_conv_w=gdn_conv_w, gdn_a_log=gdn_a_log, gdn_dt_bias=gdn_dt_bias,
                   gdn_norm=gdn_norm, moba_q_norm=moba_q_norm, moba_k_norm=moba_k_norm, w_gate=w_gate,
                   w_br_s5=w_br_s5, w_br_gla=w_br_gla, w_br_gdn=w_br_gdn, w_br_moba=w_br_moba,
                   w_out=w_out, ln2_g=ln2_g, w_ff1=w_ff1, w_ff2=w_ff2)
    for name in ("w_gate", "w_out", "w_ff1", "w_ff2"):
        weights[name + "_bf16"] = weights[name].astype(BF16)
    depth = ln1_g.shape[0]
    layers = [_layer_params(l, weights) for l in range(depth)]
    batch, seq, _ = x_prompt.shape
    nseq = x_sample.shape[0]
    npages = page_table.shape[1]
    past_len = npages * PAGE_SIZE

    xp = x_prompt.reshape(batch * seq, D_MODEL)
    tabs_p = _rope_tables(jnp.arange(seq, dtype=jnp.int32))
    p_states = []
    for l in range(depth):
        xp, st = _prompt_layer(xp, layers[l], tabs_p, batch, seq)
        p_states.append(st)

    cache_kt = cache_moba_k.transpose(0, 1, 3, 4, 2)
    cache_vt = cache_moba_v.transpose(0, 1, 3, 4, 2)
    tabs_s = _rope_tables(jnp.full((1,), past_len, jnp.int32))
    xs = x_sample.reshape(nseq, D_MODEL)
    s_states = []
    for l in range(depth):
        st0 = (state_s5_re[l], state_s5_im[l], state_gla[l], state_gdn[l], state_gdn_conv[l])
        xs, st = _sample_layer(xs, layers[l], tabs_s, page_table, cache_kt, cache_vt, l, st0)
        s_states.append(st)

    stack = lambda states: [jnp.stack([s[i] for s in states]) for i in range(len(states[0]))]
    p_out = stack(p_states)
    for i in range(2):
        p_out[i] = p_out[i].transpose(0, 1, 4, 2, 3)
    return (xp.reshape(batch, seq, D_MODEL), xs.reshape(nseq, 1, D_MODEL), *p_out, *stack(s_states))
```

```python
import math

import jax
import jax.numpy as jnp
import numpy as np
from jax import lax
from jax.experimental import pallas as pl
from jax.experimental.pallas import tpu as pltpu

F32 = jnp.float32
BF16 = jnp.bfloat16

D_MODEL = 1024
N_BRANCH = 4
BRANCH_WIDTH = D_MODEL // N_BRANCH
HEADS = 4
HEAD_DIM = BRANCH_WIDTH // HEADS
S5_GROUP = 16
S5_GROUPS = BRANCH_WIDTH // S5_GROUP
S5_P = 64
S5_STATE = S5_GROUPS * S5_P
GLA_RANK = 16
GLA_TAU = 16.0
GDN_CONV = 4
GDN_QKV = 3 * BRANCH_WIDTH
MOBA_BLOCK = 256
MOBA_TOPK = 3
ROT_DIM = HEAD_DIM // 4
ROPE_THETA = 500000.0
PAGE_SIZE = 128
D_FF = 4 * D_MODEL
EPS = 1e-6
NEG = -1e30

LANES = 128
S5_CHUNK = 8
GLA_SUB = 16
GDN_CHUNK = 64
VMEM_LIMIT = 56 * 1024 * 1024

HIGHEST = lax.Precision.HIGHEST


def _cparams(*sem):
    return pltpu.CompilerParams(dimension_semantics=sem, vmem_limit_bytes=VMEM_LIMIT)


def _const_spec(shape):
    zeros = (0,) * len(shape)
    return pl.BlockSpec(shape, lambda *_: zeros)


def _layer_spec(shape, layer, **kw):
    zeros = (0,) * len(shape)
    return pl.BlockSpec((None,) + tuple(shape), lambda *_: (layer,) + zeros, **kw)


def _dot(a, b):
    return jnp.dot(a, b, preferred_element_type=F32)


def _dot_nt(a, b):
    return lax.dot_general(a, b, (((1,), (1,)), ((), ())), preferred_element_type=F32)


def _dot_tn(a, b):
    return lax.dot_general(a, b, (((0,), (0,)), ((), ())), preferred_element_type=F32)


def _bdot(a, b):
    return _dot(a.astype(BF16), b.astype(BF16))


def _split2(x):
    hi = x.astype(BF16)
    lo = (x - hi.astype(F32)).astype(BF16)
    return hi, lo


def _dot_sel(x, sel):
    hi, lo = _split2(x)
    return _dot(hi, sel) + _dot(lo, sel)


def _dot_sel_exact(x, sel):
    x1 = x.astype(BF16)
    r1 = x - x1.astype(F32)
    x2 = r1.astype(BF16)
    x3 = (r1 - x2.astype(F32)).astype(BF16)
    return _dot(x1, sel) + _dot(x2, sel) + _dot(x3, sel)


def _rms_rows(x, g):
    return x * lax.rsqrt(jnp.mean(x * x, axis=-1, keepdims=True) + EPS) * g


def _sigmoid(x):
    return 1.0 / (1.0 + jnp.exp(-x))


def _silu(x):
    return x * _sigmoid(x)


def _softplus(x):
    return jnp.maximum(x, 0.0) + jnp.log1p(jnp.exp(-jnp.abs(x)))


def _head_ones():
    r = np.arange(BRANCH_WIDTH) // HEAD_DIM
    return jnp.asarray(r[:, None] == r[None, :], BF16)


IN_OUTS = (("s5_u_lo", 128), ("s5_u_hi", 128), ("gla_q", 256), ("gla_k", 256), ("gla_v", 256), ("gla_r", 256),
           ("gla_lr", 128), ("gdn_qkv", 768), ("gdn_b", 0), ("gdn_a", 0), ("gdn_z", 256),
           ("moba_q", 256), ("moba_k", 256), ("moba_v", 256))
IN_WIDTH = sum(w for _, w in IN_OUTS)
GDN_B_COL = GLA_RANK
GDN_A_COL = GLA_RANK + HEADS


def _regroup_w_in(w_in):
    sizes = (256, 256, 256, 256, GLA_RANK, 256, 256, 256, 256, HEADS, HEADS, 256, 256, 256, 256)
    offs = np.cumsum((0,) + sizes)
    (s5_u, a_q, a_k, a_v, a_lr, a_r, d_q, d_k, d_v, d_b, d_a, d_z, m_q, m_k, m_v) = (
        w_in[:, offs[i]:offs[i + 1]] for i in range(len(sizes)))
    small = jnp.pad(jnp.concatenate([a_lr, d_b, d_a], axis=1), ((0, 0), (0, 128 - GLA_RANK - 2 * HEADS)))
    cols = [s5_u, a_q, a_k, a_v, a_r, small, d_q, d_k, d_v, d_z, m_q, m_k, m_v]
    return jnp.concatenate(cols, axis=1).astype(BF16)


def _head_spread():
    m = np.zeros((128, 2 * BRANCH_WIDTH), np.float32)
    for h in range(HEADS):
        m[GDN_B_COL + h, h * HEAD_DIM:(h + 1) * HEAD_DIM] = 1.0
        m[GDN_A_COL + h, BRANCH_WIDTH + h * HEAD_DIM:BRANCH_WIDTH + (h + 1) * HEAD_DIM] = 1.0
    return jnp.asarray(m, BF16)


def _inproj_kernel(x_ref, g_ref, w_ref, wg_ref, bg_ref, alog_ref, dtb_ref, spread_ref, *outs):
    x = x_ref[...]
    hb = _rms_rows(x, g_ref[...]).astype(BF16)
    vals = {}
    off = 0
    for name, n in IN_OUTS:
        if n:
            vals[name] = _dot(hb, w_ref[:, off:off + n])
            off += n
    small = vals["gla_lr"]
    z = _bdot(small, wg_ref[...]) + bg_ref[...]
    vals["gla_lr"] = -_softplus(-z) * (1.0 / GLA_TAU)
    per_head = _dot_sel_exact(small, spread_ref[...])
    vals["gdn_b"] = _sigmoid(per_head[:, :BRANCH_WIDTH])
    vals["gdn_a"] = -jnp.exp(alog_ref[...]) * _softplus(per_head[:, BRANCH_WIDTH:] + dtb_ref[...])
    for (name, _), o_ref in zip(IN_OUTS, outs):
        o_ref[...] = vals[name]


def _inproj(x2d, lp, tm):
    n = x2d.shape[0]
    out_shape = []
    out_specs = []
    for name, w in IN_OUTS:
        w_out = 256 if name in ("gla_lr", "gdn_b", "gdn_a") else w
        out_shape.append(jax.ShapeDtypeStruct((n, w_out), F32))
        out_specs.append(pl.BlockSpec((tm, w_out), lambda i: (i, 0)))
    res = pl.pallas_call(
        _inproj_kernel,
        grid=(n // tm,),
        in_specs=[pl.BlockSpec((tm, D_MODEL), lambda i: (i, 0)),
                  _const_spec((1, D_MODEL)), _const_spec((D_MODEL, IN_WIDTH)),
                  _const_spec((128, 256)), _const_spec((1, 256)),
                  _const_spec((1, 256)), _const_spec((1, 256)), _const_spec((128, 2 * BRANCH_WIDTH))],
        out_specs=out_specs,
        out_shape=out_shape,
        compiler_params=_cparams("parallel"),
        name="inproj",
    )(x2d, lp["ln1_g"], lp["w_in"], lp["gla_wg"], lp["gla_bg"], lp["gdn_alog"], lp["gdn_dtb"], _head_spread())
    return dict(zip((nm for nm, _ in IN_OUTS), res))


def _merge_kernel(x_ref, ya_lo_ref, ya_hi_ref, yb_ref, yc_ref, yd_ref, g_ref, wgate_ref, wglu_ref, bglu_ref,
                  wbr_ref, wout_ref, o_ref):
    x = x_ref[...]
    hb = _rms_rows(x, g_ref[...]).astype(BF16)
    ya = jnp.concatenate([ya_lo_ref[...], ya_hi_ref[...]], axis=1)
    ya = ya * _sigmoid(_bdot(ya, wglu_ref[...]) + bglu_ref[...])
    merged = None
    for i, y in enumerate((ya, yb_ref[...], yc_ref[...], yd_ref[...])):
        gate = _sigmoid(_dot(hb, wgate_ref[:, i * D_MODEL:(i + 1) * D_MODEL]))
        term = gate * _dot(y.astype(BF16), wbr_ref[i])
        merged = term if merged is None else merged + term
    o_ref[...] = x + _bdot(merged, wout_ref[...])


def _merge(x2d, ya, yb, yc, yd, lp, tm):
    n = x2d.shape[0]
    row = lambda w: pl.BlockSpec((tm, w), lambda i: (i, 0))
    ya_lo, ya_hi = ya
    return pl.pallas_call(
        _merge_kernel,
        grid=(n // tm,),
        in_specs=[row(D_MODEL), row(128), row(128), row(256), row(256), row(256),
                  _const_spec((1, D_MODEL)), _layer_spec((D_MODEL, N_BRANCH * D_MODEL), lp["layer"]),
                  _const_spec((256, 256)), _const_spec((1, 256)),
                  _const_spec((N_BRANCH, 256, D_MODEL)), _layer_spec((D_MODEL, D_MODEL), lp["layer"])],
        out_specs=row(D_MODEL),
        out_shape=jax.ShapeDtypeStruct((n, D_MODEL), F32),
        compiler_params=_cparams("parallel"),
        name="merge",
    )(x2d, ya_lo, ya_hi, yb, yc, yd, lp["ln1_g"], lp["w_gate"], lp["s5_w_glu"], lp["s5_b_glu"],
      lp["w_br"], lp["w_out"])


def _mlp_kernel(x_ref, g_ref, w1_ref, w2_ref, o_ref):
    x = x_ref[...]
    hb = _rms_rows(x, g_ref[...]).astype(BF16)
    z = jnp.maximum(_dot(hb, w1_ref[...]), 0.0)
    o_ref[...] = x + _bdot(z * z, w2_ref[...])


def _mlp(x2d, lp, tm):
    n = x2d.shape[0]
    row = pl.BlockSpec((tm, D_MODEL), lambda i: (i, 0))
    single = pl.Buffered(1)
    return pl.pallas_call(
        _mlp_kernel,
        grid=(n // tm,),
        in_specs=[row, _const_spec((1, D_MODEL)),
                  _layer_spec((D_MODEL, D_FF), lp["layer"], pipeline_mode=single),
                  _layer_spec((D_FF, D_MODEL), lp["layer"], pipeline_mode=single)],
        out_specs=row,
        out_shape=jax.ShapeDtypeStruct((n, D_MODEL), F32),
        compiler_params=_cparams("parallel"),
        name="mlp",
    )(x2d, lp["ln2_g"], lp["w_ff1"], lp["w_ff2"])


def _s5_matrices(p, chunk):
    hp = dict(precision=HIGHEST)
    dt = jnp.exp(p["s5_log_dt"])[:, None]
    ar, ai = p["s5_a_re"], p["s5_a_im"]
    mag = jnp.exp(ar * dt)
    abar_re = mag * jnp.cos(ai * dt)
    abar_im = mag * jnp.sin(ai * dt)
    den = ar * ar + ai * ai
    nr = abar_re - 1.0
    f_re = (nr * ar + abar_im * ai) / den
    f_im = (abar_im * ar - nr * ai) / den
    br, bi = p["s5_b_re"], p["s5_b_im"]
    bbar_re = f_re[..., None] * br - f_im[..., None] * bi
    bbar_im = f_re[..., None] * bi + f_im[..., None] * br
    pw_re = [jnp.ones_like(abar_re)]
    pw_im = [jnp.zeros_like(abar_re)]
    for _ in range(chunk):
        r, i = pw_re[-1], pw_im[-1]
        pw_re.append(r * abar_re - i * abar_im)
        pw_im.append(r * abar_im + i * abar_re)
    pw_re = jnp.stack(pw_re)
    pw_im = jnp.stack(pw_im)
    cr, ci = p["s5_c_re"], p["s5_c_im"]
    ca_re = cr[None] * pw_re[:, :, None, :] - ci[None] * pw_im[:, :, None, :]
    ca_im = cr[None] * pw_im[:, :, None, :] + ci[None] * pw_re[:, :, None, :]
    def table(t):
        lead = t.shape[:-3]
        t = jnp.moveaxis(t, -1, -3)
        return t.reshape(lead + (t.shape[-3], -1))

    kern = (jnp.einsum("tgop,gpi->tgoi", ca_re[:chunk], bbar_re, **hp)
            - jnp.einsum("tgop,gpi->tgoi", ca_im[:chunk], bbar_im, **hp))
    rev_re = pw_re[:chunk][::-1]
    rev_im = pw_im[:chunk][::-1]
    ab_re = rev_re[..., None] * bbar_re[None] - rev_im[..., None] * bbar_im[None]
    ab_im = rev_re[..., None] * bbar_im[None] + rev_im[..., None] * bbar_re[None]
    a_pow = jnp.stack([pw_re[chunk].reshape(-1), pw_im[chunk].reshape(-1)])
    halves = lambda t, n: jnp.stack([t[..., :n], t[..., n:]])
    xs_re, xs_im = halves(table(ab_re), S5_HALF_STATE), halves(table(ab_im), S5_HALF_STATE)
    hy = jnp.stack([table(ca_re[1:]), -table(ca_im[1:])], axis=1)
    return dict(lag=halves(table(kern), S5_HALF),
                xs=jnp.concatenate([xs_re, xs_im], axis=-1),
                hy=halves(hy, S5_HALF),
                a_pow=a_pow, d_row=p["s5_d"][None, :])


S5_HALF = BRANCH_WIDTH // 2
S5_HALF_GROUPS = S5_GROUPS // 2
S5_HALF_STATE = S5_STATE // 2


def _s5_expand(lag_ref, xs_ref, hy_ref, wxs_scr, win_scr, why_scr):
    chunk = lag_ref.shape[1]
    w = S5_HALF

    def group_of(shape, axis, per_group, wrap=None):
        idx = lax.broadcasted_iota(jnp.int32, shape, axis)
        if wrap is not None:
            idx = idx % wrap
        return idx // per_group

    same_ii = group_of((w, w), 0, S5_GROUP) == group_of((w, w), 1, S5_GROUP)
    same_is = (group_of((w, 2 * S5_HALF_STATE), 0, S5_GROUP)
               == group_of((w, 2 * S5_HALF_STATE), 1, S5_P, wrap=S5_HALF_STATE))
    same_si = group_of((S5_HALF_STATE, w), 0, S5_P) == group_of((S5_HALF_STATE, w), 1, S5_GROUP)
    down = lambda t: jnp.concatenate([t] * S5_HALF_GROUPS, axis=0)
    zero = jnp.zeros((w, w), BF16)
    for hf in range(2):
        lags = [jnp.where(same_ii, down(lag_ref[hf, tau]), 0.0).astype(BF16) for tau in range(chunk)]
        for s in range(chunk):
            wxs_scr[hf, s * w:(s + 1) * w, :] = jnp.where(same_is, down(xs_ref[hf, s]), 0.0).astype(BF16)
            win_scr[hf, s * w:(s + 1) * w, :] = jnp.concatenate(
                [lags[t - s] if t >= s else zero for t in range(chunk)], axis=1)
            for part in range(2):
                why_scr[hf, part * S5_HALF_STATE:(part + 1) * S5_HALF_STATE, s * w:(s + 1) * w] = jnp.where(
                    same_si, down(hy_ref[hf, s, part]), 0.0).astype(BF16)


def _s5_state_increment(u_halves, wxs_ref):
    lo = _dot(u_halves[0], wxs_ref[0])
    hi = _dot(u_halves[1], wxs_ref[1])
    n = S5_HALF_STATE
    return jnp.concatenate([lo[:, :n], hi[:, :n], lo[:, n:], hi[:, n:]], axis=1)


def _s5_outputs(u_halves, h_bf16, win_ref, why_ref):
    n = S5_HALF_STATE
    outs = []
    for hf in range(2):
        h_half = jnp.concatenate([h_bf16[:, hf * n:(hf + 1) * n],
                                  h_bf16[:, S5_STATE + hf * n:S5_STATE + (hf + 1) * n]], axis=1)
        outs.append(_dot(u_halves[hf], win_ref[hf]) + _dot(h_half, why_ref[hf]))
    return outs


def _gelu(y):
    c = math.sqrt(2.0 / math.pi)
    return 0.5 * y * (1.0 + jnp.tanh(c * (y + 0.044715 * (y * y * y))))


def _s5_kernel(u_lo_ref, u_hi_ref, lag_ref, xs_ref, hy_ref, apow_ref, d_ref, y_lo_ref, y_hi_ref, hfin_ref,
               wxs_scr, win_scr, why_scr, xs_scr, hs_scr):
    chunk = lag_ref.shape[1]
    rows = u_lo_ref.shape[0] // chunk

    @pl.when(pl.program_id(0) == 0)
    def _():
        _s5_expand(lag_ref, xs_ref, hy_ref, wxs_scr, win_scr, why_scr)

    us = [jnp.concatenate([ref[pl.ds(s, rows, stride=chunk), :] for s in range(chunk)], axis=1)
          for ref in (u_lo_ref, u_hi_ref)]
    ubs = [u.astype(BF16) for u in us]
    xs_scr[...] = _s5_state_increment(ubs, wxs_scr)
    a_re = apow_ref[0:1, :]
    a_im = apow_ref[1:2, :]

    def step(r, carry):
        hr, hi = carry
        hs_scr[pl.ds(r, 1), :] = jnp.concatenate([hr, hi], axis=-1)
        x = xs_scr[pl.ds(r, 1), :]
        nhr = a_re * hr - a_im * hi + x[:, :S5_STATE]
        nhi = a_re * hi + a_im * hr + x[:, S5_STATE:]
        return nhr, nhi

    zero = jnp.zeros((1, S5_STATE), F32)
    hr, hi = lax.fori_loop(0, rows, step, (zero, zero))
    hfin_ref[0] = jnp.concatenate([hr, hi], axis=-1)
    ys = _s5_outputs(ubs, hs_scr[...].astype(BF16), win_scr, why_scr)
    for hf, y_ref in enumerate((y_lo_ref, y_hi_ref)):
        d = d_ref[:, hf * S5_HALF:(hf + 1) * S5_HALF]
        for t in range(chunk):
            cols = slice(t * S5_HALF, (t + 1) * S5_HALF)
            y_ref[pl.ds(t, rows, stride=chunk), :] = _gelu(ys[hf][:, cols] + d * us[hf][:, cols])


def _s5_scratch(chunk):
    width = chunk * S5_HALF
    return [pltpu.VMEM((2, width, 2 * S5_HALF_STATE), BF16), pltpu.VMEM((2, width, width), BF16),
            pltpu.VMEM((2, 2 * S5_HALF_STATE, width), BF16)]


def _s5_prompt(u_lo, u_hi, sm, batch, seq):
    c = sm["lag"].shape[1]
    rows = seq // c
    half = BRANCH_WIDTH // 2
    tok = pl.BlockSpec((seq, half), lambda b: (b, 0))
    y_lo, y_hi, hfin = pl.pallas_call(
        _s5_kernel,
        grid=(batch,),
        in_specs=[tok, tok, _const_spec(sm["lag"].shape), _const_spec(sm["xs"].shape),
                  _const_spec(sm["hy"].shape), _const_spec((2, S5_STATE)), _const_spec((1, BRANCH_WIDTH))],
        out_specs=[tok, tok, pl.BlockSpec((1, 1, 2 * S5_STATE), lambda b: (b, 0, 0))],
        out_shape=[jax.ShapeDtypeStruct((batch * seq, half), F32),
                   jax.ShapeDtypeStruct((batch * seq, half), F32),
                   jax.ShapeDtypeStruct((batch, 1, 2 * S5_STATE), F32)],
        scratch_shapes=_s5_scratch(c) + [pltpu.VMEM((rows, 2 * S5_STATE), F32),
                                         pltpu.VMEM((rows, 2 * S5_STATE), F32)],
        compiler_params=_cparams("arbitrary"),
        name="s5_prompt",
    )(u_lo, u_hi, sm["lag"], sm["xs"], sm["hy"], sm["a_pow"], sm["d_row"])
    return (y_lo, y_hi), hfin.reshape(batch, 2 * S5_STATE)


GLA_ROWS = 256


def _gla_kernel(q_ref, k_ref, v_ref, g_ref, r_ref, tri_ref, ones_ref, bmask_ref, gn_ref,
                y_ref, sfin_ref, st_scr):
    step = pl.program_id(1)
    sub = GLA_SUB
    rows = q_ref.shape[0]
    ns = rows // sub

    @pl.when(step == 0)
    def _():
        st_scr[...] = jnp.zeros_like(st_scr)

    ones = ones_ref[...]
    bmask = bmask_ref[...]
    split = lambda a: a.reshape(ns, sub, BRANCH_WIDTH)
    bc = split(jnp.dot(tri_ref[...], g_ref[...], precision=HIGHEST, preferred_element_type=F32))
    q = split(q_ref[...] * (HEAD_DIM ** -0.5))
    k = split(k_ref[...])
    v = split(v_ref[...])

    row = lax.broadcasted_iota(jnp.int32, (ns, sub, BRANCH_WIDTH), 1)
    parts = []
    for j in range(sub):
        e = jnp.exp(jnp.minimum(bc - bc[:, j:j + 1, :], 0.0))
        parts.append(jnp.where(row >= j, q * k[:, j:j + 1, :] * e, 0.0))
    att = _dot(jnp.concatenate(parts, axis=1).reshape(ns * sub * sub, BRANCH_WIDTH).astype(BF16), ones)
    att = att.reshape(ns, sub * sub, BRANCH_WIDTH)
    o = att[:, 0:sub] * v[:, 0:1, :]
    for j in range(1, sub):
        o = o + att[:, j * sub:(j + 1) * sub] * v[:, j:j + 1, :]

    last = bc[:, sub - 1:sub, :]
    qt = (q * jnp.exp(bc)).astype(BF16)
    kt = (k * jnp.exp(last - bc)).astype(BF16)
    vb = v.astype(BF16)
    decay = jnp.exp(last)
    outer = [bmask * _dot_tn(vb[s], kt[s]) for s in range(ns)]
    st = st_scr[...]
    inter = []
    for s in range(ns):
        inter.append(_dot_nt(qt[s], st.astype(BF16)))
        st = st * decay[s] + outer[s]
    st_scr[...] = st
    o = o.reshape(rows, BRANCH_WIDTH) + jnp.concatenate(inter, axis=0)
    ms = _dot_sel(o * o, ones) * (1.0 / HEAD_DIM)
    y_ref[...] = o * lax.rsqrt(ms + EPS) * gn_ref[...] * _silu(r_ref[...])

    @pl.when(step == pl.num_programs(1) - 1)
    def _():
        sfin_ref[0] = st


def _sub_tril(rows, sub):
    i = np.arange(rows)
    return jnp.asarray((i[:, None] // sub == i[None, :] // sub) & (i[None, :] <= i[:, None]), F32)


def _unpack_state_t(st):
    b = st.shape[0]
    st = st.reshape(b, HEADS, HEAD_DIM, HEADS, HEAD_DIM)
    diag = jnp.stack([st[:, h, :, h, :] for h in range(HEADS)], axis=1)
    return diag.transpose(0, 1, 3, 2)


def _gla_prompt(q, k, v, g, r, lp, batch, seq):
    rows = min(GLA_ROWS, seq)
    nsteps = seq // rows
    blk = pl.BlockSpec((rows, BRANCH_WIDTH), lambda b, c: (b * nsteps + c, 0))
    hm = np.arange(BRANCH_WIDTH) // HEAD_DIM
    bmask = jnp.asarray(hm[:, None] == hm[None, :], F32)
    y, sfin = pl.pallas_call(
        _gla_kernel,
        grid=(batch, nsteps),
        in_specs=[blk, blk, blk, blk, blk, _const_spec((rows, rows)),
                  _const_spec((BRANCH_WIDTH, BRANCH_WIDTH)), _const_spec((BRANCH_WIDTH, BRANCH_WIDTH)),
                  _const_spec((1, BRANCH_WIDTH))],
        out_specs=[blk, pl.BlockSpec((1, BRANCH_WIDTH, BRANCH_WIDTH), lambda b, c: (b, 0, 0))],
        out_shape=[jax.ShapeDtypeStruct((batch * seq, BRANCH_WIDTH), F32),
                   jax.ShapeDtypeStruct((batch, BRANCH_WIDTH, BRANCH_WIDTH), F32)],
        scratch_shapes=[pltpu.VMEM((BRANCH_WIDTH, BRANCH_WIDTH), F32)],
        compiler_params=_cparams("parallel", "arbitrary"),
        name="gla_prompt",
    )(q, k, v, g, r, _sub_tril(rows, GLA_SUB), _head_ones(), bmask, lp["gla_norm"])
    return y, _unpack_state_t(sfin)


CONV_PAD = 8


GDN_ROWS = 512


def _block_diag(x, ones):
    return jnp.concatenate([x] * HEADS, axis=0) * ones


def _unit_lower_inverses(ns, eye, ones):
    c = ns[0].shape[0]
    every = range(len(ns))
    invs = [eye - n for n in ns]
    pws = list(ns)
    for _ in range(int(math.log2(c)) - 1):
        pbs = [pw.astype(BF16) for pw in pws]
        pws = [_dot(pb, _block_diag(pb, ones)) for pb in pbs]
        invs = [invs[i] + _dot(invs[i].astype(BF16), _block_diag(pws[i].astype(BF16), ones)) for i in every]
    inv_parts = [_split2(inv) for inv in invs]
    n_parts = [_split2(n) for n in ns]
    prods = [_dot(jnp.concatenate(n_parts[i], axis=0), _block_diag(inv_parts[i][0], ones)) for i in every]
    cross = [_dot(n_parts[i][0], _block_diag(inv_parts[i][1], ones)) for i in every]
    resids = [eye - invs[i] - (prods[i][:c] + prods[i][c:] + cross[i]) for i in every]
    return [invs[i] + _dot(inv_parts[i][0], _block_diag(resids[i].astype(BF16), ones)) for i in every]


def _gdn_kernel(x_ref, b_ref, g_ref, z_ref, cw_ref, tri_ref, ones_ref, eye_ref, gn_ref,
                y_ref, sfin_ref, conv_ref, s_scr, buf_scr):
    step = pl.program_id(1)
    rows = x_ref.shape[0]
    c = min(GDN_CHUNK, rows)

    @pl.when(step == 0)
    def _():
        s_scr[...] = jnp.zeros_like(s_scr)
        buf_scr[0:CONV_PAD, :] = jnp.zeros((CONV_PAD, GDN_QKV), F32)

    x = x_ref[...]
    buf_scr[CONV_PAD:CONV_PAD + rows, :] = x
    conv = cw_ref[GDN_CONV - 1:GDN_CONV, :] * x
    for w in range(GDN_CONV - 1):
        lag = GDN_CONV - 1 - w
        conv = conv + cw_ref[w:w + 1, :] * buf_scr[CONV_PAD - lag:CONV_PAD - lag + rows, :]
    tail = buf_scr[rows:rows + CONV_PAD, :]
    buf_scr[0:CONV_PAD, :] = tail
    conv_ref[0] = tail
    qkv = _silu(conv)
    ones = ones_ref[...]
    onesf = ones.astype(F32)
    eye = eye_ref[...]
    q = qkv[:, 0:BRANCH_WIDTH]
    k = qkv[:, BRANCH_WIDTH:2 * BRANCH_WIDTH]
    v = qkv[:, 2 * BRANCH_WIDTH:]
    q = q * lax.rsqrt(_dot_sel(q * q, ones) + EPS) * (HEAD_DIM ** -0.5)
    k = k * lax.rsqrt(_dot_sel(k * k, ones) + EPS)
    beta = b_ref[...]
    gc = jnp.dot(tri_ref[...], g_ref[...], precision=HIGHEST, preferred_element_type=F32)
    gam = jnp.exp(gc)
    ri = lax.broadcasted_iota(jnp.int32, (c, BRANCH_WIDTH), 0)
    cj = lax.broadcasted_iota(jnp.int32, (c, BRANCH_WIDTH), 1) % HEAD_DIM

    every = range(rows // c)
    sls = [slice(n * c, (n + 1) * c) for n in every]
    kcs = [k[rs].astype(BF16) for rs in sls]
    kqs = [_dot_nt(jnp.concatenate([k[rs], q[rs]], axis=0).astype(BF16), _block_diag(kcs[n], ones))
           for n, rs in enumerate(sls)]
    decs = []
    for rs in sls:
        grow = jnp.sum(gc[rs] * eye, axis=0, keepdims=True)
        decs.append(jnp.where(ri >= cj, jnp.exp(jnp.minimum(gc[rs] - grow, 0.0)), 0.0))
    invs = _unit_lower_inverses(
        [jnp.where(ri > cj, beta[rs] * decs[n] * kqs[n][:c], 0.0) for n, rs in enumerate(sls)], eye, ones)
    invbs = [inv.astype(BF16) for inv in invs]
    ws_m = [_dot(invbs[n], _block_diag((beta[rs] * gam[rs] * k[rs]).astype(BF16), ones))
            for n, rs in enumerate(sls)]
    u0s = [_dot(invbs[n], _block_diag((beta[rs] * v[rs]).astype(BF16), ones)) for n, rs in enumerate(sls)]
    aqks = [(decs[n] * kqs[n][c:]).astype(BF16) for n in every]
    lhs = [jnp.concatenate([ws_m[n], gam[rs] * q[rs]], axis=0).astype(BF16) for n, rs in enumerate(sls)]
    glasts = [gc[rs][c - 1:c, :] for rs in sls]
    kds = [(k[rs] * jnp.exp(glasts[n] - gc[rs])).astype(BF16) for n, rs in enumerate(sls)]

    s = s_scr[...]
    outs = []
    for n in every:
        ws = _dot(lhs[n], s.astype(BF16))
        ub = (u0s[n] - ws[:c]).astype(BF16)
        outs.append(ws[c:] + _dot(aqks[n], _block_diag(ub, ones)))
        s = jnp.exp(glasts[n]) * s + onesf * _dot_tn(kds[n], ub)
    s_scr[...] = s
    o = jnp.concatenate(outs, axis=0)
    ms = _dot_sel(o * o, ones) * (1.0 / HEAD_DIM)
    y_ref[...] = o * lax.rsqrt(ms + EPS) * gn_ref[...] * _silu(z_ref[...])

    @pl.when(step == pl.num_programs(1) - 1)
    def _():
        sfin_ref[0] = s


def _unpack_state(st):
    b = st.shape[0]
    st = st.reshape(b, HEADS, HEAD_DIM, HEADS, HEAD_DIM)
    return jnp.stack([st[:, h, :, h, :] for h in range(HEADS)], axis=1)


def _gdn_prompt(x, beta, g, z, lp, batch, seq):
    rows = min(GDN_ROWS, seq)
    c = min(GDN_CHUNK, rows)
    assert c == HEAD_DIM and seq % rows == 0 and rows % c == 0
    nsteps = seq // rows
    blk = lambda w: pl.BlockSpec((rows, w), lambda b, s: (b * nsteps + s, 0))
    eye =jnp.asarray(np.tile(np.eye(c, dtype=np.float32), (1, HEADS)))
    y, sfin, conv = pl.pallas_call(
        _gdn_kernel,
        grid=(batch, nsteps),
        in_specs=[blk(GDN_QKV), blk(256), blk(256), blk(256), _const_spec((CONV_PAD, GDN_QKV)),
                  _const_spec((rows, rows)), _const_spec((256, 256)), _const_spec((c, 256)),
                  _const_spec((1, 256))],
        out_specs=[blk(256),
                   pl.BlockSpec((1, 256, 256), lambda b, s: (b, 0, 0)),
                   pl.BlockSpec((1, CONV_PAD, GDN_QKV), lambda b, s: (b, 0, 0))],
        out_shape=[jax.ShapeDtypeStruct((batch * seq, 256), F32),
                   jax.ShapeDtypeStruct((batch, 256, 256), F32),
                   jax.ShapeDtypeStruct((batch, CONV_PAD, GDN_QKV), F32)],
        scratch_shapes=[pltpu.VMEM((256, 256), F32),
                        pltpu.VMEM((CONV_PAD + rows, GDN_QKV), F32)],
        compiler_params=_cparams("parallel", "arbitrary"),
        name="gdn_prompt",
    )(x, beta, g, z, lp["gdn_conv_w"], _sub_tril(rows, c), _head_ones(), eye, lp["gdn_norm"])
    return y, _unpack_state(sfin), conv[:, CONV_PAD - (GDN_CONV - 1):, :]


HALF_ROT = ROT_DIM // 2


def _rope_tables(pos):
    inv = ROPE_THETA ** (-jnp.arange(HALF_ROT, dtype=F32) / HALF_ROT)
    ang = pos.astype(F32)[:, None] * inv[None, :]
    cos, sin = jnp.cos(ang), jnp.sin(ang)
    n = pos.shape[0]
    rest = HEAD_DIM - ROT_DIM
    head = lambda a, b, fill: jnp.concatenate([a, b, jnp.full((n, rest), fill, F32)], axis=1)
    zero = jnp.zeros_like(sin)
    tabs = [head(cos, cos, 1.0), head(-sin, zero, 0.0), head(zero, sin, 0.0)]
    return jnp.stack([jnp.tile(t, (1, HEADS)) for t in tabs])


def _qk_norm_rope(x, gain, tab_ref, ones):
    y = x * lax.rsqrt(_dot_sel(x * x, ones) * (1.0 / HEAD_DIM) + EPS) * gain
    up = pltpu.roll(y, BRANCH_WIDTH - HALF_ROT, 1)
    down = pltpu.roll(y, HALF_ROT, 1)
    return y * tab_ref[0] + up * tab_ref[1] + down * tab_ref[2]


K_AUG = 2 * HEAD_DIM
V_AUG = HEAD_DIM + 16


def _moba_prep_kernel(q_ref, k_ref, v_ref, tab_ref, qg_ref, kg_ref, ones_ref,
                      kt_ref, vtt_ref, qt_ref, kh_ref, vt_ref, km_ref):
    ones = ones_ref[...]
    rows = q_ref.shape[0]
    nblk = rows // MOBA_BLOCK
    mq = _qk_norm_rope(q_ref[...], qg_ref[...], tab_ref, ones) * (HEAD_DIM ** -0.5)
    mk = _qk_norm_rope(k_ref[...], kg_ref[...], tab_ref, ones)
    kmean = jnp.mean(mk.reshape(nblk, MOBA_BLOCK, BRANCH_WIDTH), axis=1)
    qt = mq.T
    kt = mk.T
    vt = v_ref[...].T
    one_col = (lax.broadcasted_iota(jnp.int32, (MOBA_BLOCK, K_AUG - HEAD_DIM), 1) == 0).astype(BF16)
    one_row = (lax.broadcasted_iota(jnp.int32, (V_AUG - HEAD_DIM, MOBA_BLOCK), 0) == 0).astype(BF16)
    for h in range(HEADS):
        sl = slice(h * HEAD_DIM, (h + 1) * HEAD_DIM)
        qt_ref[0, h] = qt[sl, :]
        km_ref[0, h] = kmean[:, sl]
        kt_ref[0, h] = kt[sl, :]
        vtt_ref[0, h] = vt[sl, :]
        for j in range(nblk):
            rs = slice(j * MOBA_BLOCK, (j + 1) * MOBA_BLOCK)
            kh_ref[0, h, j] = jnp.concatenate([mk[rs, sl].astype(BF16), one_col], axis=1)
            vt_ref[0, h, j] = jnp.concatenate([vt[sl, rs].astype(BF16), one_row], axis=0)


def _moba_prep(q, k, v, tabs, lp, batch, seq):
    rows = min(8 * MOBA_BLOCK, seq)
    nsteps = seq // rows
    nblk = rows // MOBA_BLOCK
    nb = seq // MOBA_BLOCK
    blk = pl.BlockSpec((rows, 256), lambda b, r: (b * nsteps + r, 0))
    cache_rows = pl.BlockSpec((1, HEADS, HEAD_DIM, rows), lambda b, r: (b, 0, 0, r))
    return pl.pallas_call(
        _moba_prep_kernel,
        grid=(batch, nsteps),
        in_specs=[blk, blk, blk, pl.BlockSpec((3, rows, 256), lambda b, r: (0, r, 0)),
                  _const_spec((1, 256)), _const_spec((1, 256)), _const_spec((256, 256))],
        out_specs=[cache_rows, cache_rows, cache_rows,
                   pl.BlockSpec((1, HEADS, nblk, MOBA_BLOCK, K_AUG), lambda b, r: (b, 0, r, 0, 0)),
                   pl.BlockSpec((1, HEADS, nblk, V_AUG, MOBA_BLOCK), lambda b, r: (b, 0, r, 0, 0)),
                   pl.BlockSpec((1, HEADS, nblk, HEAD_DIM), lambda b, r: (b, 0, r, 0))],
        out_shape=[jax.ShapeDtypeStruct((batch, HEADS, HEAD_DIM, seq), F32),
                   jax.ShapeDtypeStruct((batch, HEADS, HEAD_DIM, seq), F32),
                   jax.ShapeDtypeStruct((batch, HEADS, HEAD_DIM, seq), F32),
                   jax.ShapeDtypeStruct((batch, HEADS, nb, MOBA_BLOCK, K_AUG), BF16),
                   jax.ShapeDtypeStruct((batch, HEADS, nb, V_AUG, MOBA_BLOCK), BF16),
                   jax.ShapeDtypeStruct((batch, HEADS, nb, HEAD_DIM), F32)],
        compiler_params=_cparams("parallel", "parallel"),
        name="moba_prep",
    )(q, k, v, tabs, lp["moba_q_norm"], lp["moba_k_norm"], _head_ones())


def _moba_attn_kernel(qt_ref, kh_ref, vt_ref, km_ref, o_ref, bias_scr):
    qb = pl.program_id(1)
    nb = km_ref.shape[2]
    blk = MOBA_BLOCK
    heads = range(HEADS)
    blk_id = lax.broadcasted_iota(jnp.int32, (nb, blk), 0)
    kpos = lax.broadcasted_iota(jnp.int32, (blk, blk), 0)
    qpos = lax.broadcasted_iota(jnp.int32, (blk, blk), 1)
    first_row = lax.broadcasted_iota(jnp.int32, (K_AUG - HEAD_DIM, blk), 0) == 0
    qts = [qt_ref[0, h] for h in heads]
    qtb = [qt.astype(BF16) for qt in qts]

    def scores(j, biases):
        out = []
        for h in heads:
            extra = jnp.where(first_row, biases[h], 0.0).astype(BF16)
            out.append(_dot(kh_ref[0, h, j], jnp.concatenate([qtb[h], extra], axis=0)))
        return out

    gates = [jnp.dot(km_ref[0, h], qts[h], precision=HIGHEST, preferred_element_type=F32) for h in heads]
    own = scores(qb, [jnp.zeros((1, blk), F32)] * HEADS)
    for h in heads:
        gate = jnp.where(blk_id < qb, gates[h], NEG)
        taken = jnp.zeros((nb, blk), jnp.bool_)
        for _ in range(min(MOBA_TOPK, nb)):
            best = jnp.max(gate, axis=0, keepdims=True)
            idx = jnp.min(jnp.where(gate == best, blk_id, nb), axis=0, keepdims=True)
            hit = blk_id == idx
            taken = jnp.logical_or(taken, hit)
            gate = jnp.where(hit, -jnp.inf, gate)
        bias_scr[h] = jnp.where(jnp.logical_and(taken, blk_id < qb), 0.0, NEG)

    ms, ps = [], []
    for h in heads:
        s = jnp.where(kpos <= qpos, own[h], NEG)
        m = jnp.max(s, axis=0, keepdims=True)
        ms.append(m)
        ps.append(jnp.exp(s - m).astype(BF16))
    accs = [_dot(vt_ref[0, h, qb], ps[h]) for h in heads]

    def bias_rows(j):
        return [bias_scr[h, pl.ds(j, 1), :] for h in heads]

    def update(carry, blocks):
        ms, accs = carry
        ss = [scores(j, bias_rows(j)) for j in blocks]
        new_m, new_acc = [], []
        for h in heads:
            m_new = ms[h]
            for s in ss:
                m_new = jnp.maximum(m_new, jnp.max(s[h], axis=0, keepdims=True))
            acc = jnp.exp(ms[h] - m_new) * accs[h]
            for j, s in zip(blocks, ss):
                acc = acc + _dot(vt_ref[0, h, j], jnp.exp(s[h] - m_new).astype(BF16))
            new_m.append(m_new)
            new_acc.append(acc)
        return tuple(new_m), tuple(new_acc)

    carry = (tuple(ms), tuple(accs))
    start = 0
    for width in (4, 2, 1):
        trips = (qb - start) // width
        carry = lax.fori_loop(
            0, trips, lambda i, c, w=width, s=start: update(c, tuple(s + w * i + e for e in range(w))), carry)
        start = start + trips * width
    ms, accs = carry
    outs = [accs[h][:HEAD_DIM] / accs[h][HEAD_DIM:HEAD_DIM + 1] for h in heads]
    o_ref[...] = jnp.concatenate(outs, axis=0).T


def _moba_prompt(qt, kh, vt, km, batch, seq):
    nb = seq // MOBA_BLOCK
    return pl.pallas_call(
        _moba_attn_kernel,
        grid=(batch, nb),
        in_specs=[pl.BlockSpec((1, HEADS, HEAD_DIM, MOBA_BLOCK), lambda b, i: (b, 0, 0, i)),
                  pl.BlockSpec((1, HEADS, nb, MOBA_BLOCK, K_AUG), lambda b, i: (b, 0, 0, 0, 0)),
                  pl.BlockSpec((1, HEADS, nb, V_AUG, MOBA_BLOCK), lambda b, i: (b, 0, 0, 0, 0)),
                  pl.BlockSpec((1, HEADS, nb, HEAD_DIM), lambda b, i: (b, 0, 0, 0))],
        out_specs=pl.BlockSpec((MOBA_BLOCK, 256), lambda b, i: (b * nb + i, 0)),
        out_shape=jax.ShapeDtypeStruct((batch * seq, 256), F32),
        scratch_shapes=[pltpu.VMEM((HEADS, nb, MOBA_BLOCK), F32)],
        compiler_params=_cparams("parallel", "arbitrary"),
        name="moba_prompt",
    )(qt, kh, vt, km)


def _s5_step_kernel(u_ref, h0_ref, lag_ref, xs_ref, hy_ref, apow_ref, d_ref, y_ref, h_ref,
                    wxs_ref, win_ref, why_ref):
    _s5_expand(lag_ref, xs_ref, hy_ref, wxs_ref, win_ref, why_ref)
    u = u_ref[...]
    ubs = [u[:, :S5_HALF].astype(BF16), u[:, S5_HALF:].astype(BF16)]
    h0 = h0_ref[...]
    xs = _s5_state_increment(ubs, wxs_ref)
    a_re = apow_ref[0:1, :]
    a_im = apow_ref[1:2, :]
    hr0 = h0[:, :S5_STATE]
    hi0 = h0[:, S5_STATE:]
    hr = a_re * hr0 - a_im * hi0 + xs[:, :S5_STATE]
    hi = a_re * hi0 + a_im * hr0 + xs[:, S5_STATE:]
    h_ref[...] = jnp.concatenate([hr, hi], axis=-1)
    y = jnp.concatenate(_s5_outputs(ubs, h0.astype(BF16), win_ref, why_ref), axis=1)
    y_ref[...] = _gelu(y + d_ref[...] * u)


def _s5_step(u, h0, sm):
    n = u.shape[0]
    return pl.pallas_call(
        _s5_step_kernel,
        out_shape=[jax.ShapeDtypeStruct((n, BRANCH_WIDTH), F32),
                   jax.ShapeDtypeStruct((n, 2 * S5_STATE), F32)],
        scratch_shapes=_s5_scratch(1),
        compiler_params=pltpu.CompilerParams(vmem_limit_bytes=VMEM_LIMIT),
        name="s5_step",
    )(u, h0, sm["lag"], sm["xs"], sm["hy"], sm["a_pow"], sm["d_row"])


def _expand_mats():
    idx = np.arange(HEAD_DIM * HEAD_DIM)
    rep = (np.arange(HEAD_DIM)[:, None] == idx[None, :] // HEAD_DIM)
    til = (np.arange(HEAD_DIM)[:, None] == idx[None, :] % HEAD_DIM)
    return jnp.asarray(rep, BF16), jnp.asarray(til, BF16), jnp.asarray(til.T, BF16)


def _gla_step_kernel(q_ref, k_ref, v_ref, g_ref, r_ref, s0_ref, rep_ref, til_ref, tilt_ref, gn_ref,
                     y_ref, s_ref):
    rep = rep_ref[...]
    eg = _dot_sel_exact(jnp.exp(g_ref[...]), rep)
    kr = _dot_sel_exact(k_ref[...], rep)
    qr = _dot_sel_exact(q_ref[...] * (HEAD_DIM ** -0.5), rep)
    vt = _dot_sel_exact(v_ref[...], til_ref[...])
    s = eg * s0_ref[...] + kr * vt
    s_ref[...] = s
    o = _dot_sel(qr * s, tilt_ref[...])
    y_ref[...] = _rms_rows(o, gn_ref[...]) * _silu(r_ref[...])


def _gdn_conv_step_kernel(x_ref, c0_ref, cw_ref, qkv_ref, cnew_ref):
    x = x_ref[...]
    c0 = c0_ref[...]
    conv = cw_ref[GDN_CONV - 1:GDN_CONV, :] * x
    for w in range(GDN_CONV - 1):
        conv = conv + cw_ref[w:w + 1, :] * c0[:, w * GDN_QKV:(w + 1) * GDN_QKV]
    qkv_ref[...] = _silu(conv)
    cnew_ref[...] = jnp.concatenate([c0[:, GDN_QKV:], x], axis=-1)


def _gdn_step_kernel(q_ref, k_ref, v_ref, b_ref, g_ref, z_ref, s0_ref, rep_ref, til_ref, tilt_ref,
                     gn_ref, y_ref, s_ref):
    q = q_ref[...]
    k = k_ref[...]
    q = q * lax.rsqrt(jnp.sum(q * q, axis=-1, keepdims=True) + EPS) * (HEAD_DIM ** -0.5)
    k = k * lax.rsqrt(jnp.sum(k * k, axis=-1, keepdims=True) + EPS)
    beta = b_ref[:, 0:1]
    gam = jnp.exp(g_ref[:, 0:1])
    rep = rep_ref[...]
    tilt = tilt_ref[...]
    kr = _dot_sel_exact(k, rep)
    qr = _dot_sel_exact(q, rep)
    s0 = s0_ref[...]
    ks = _dot_sel(kr * s0, tilt)
    qs = _dot_sel(qr * s0, tilt)
    u = v_ref[...] - gam * ks
    qk = jnp.sum(q * k, axis=-1, keepdims=True)
    o = gam * qs + (beta * qk) * u
    s_ref[...] = gam * s0 + kr * _dot_sel_exact(beta * u, til_ref[...])
    y_ref[...] = _rms_rows(o, gn_ref[...]) * _silu(z_ref[...])


def _whole_call(kernel, out_shape, name, *args):
    return pl.pallas_call(kernel, out_shape=out_shape, name=name,
                          compiler_params=pltpu.CompilerParams(vmem_limit_bytes=VMEM_LIMIT))(*args)


def _moba_qk_step_kernel(q_ref, k_ref, tab_ref, qg_ref, kg_ref, ones_ref, mq_ref, mk_ref):
    ones = ones_ref[...]
    mq_ref[...] = _qk_norm_rope(q_ref[...], qg_ref[...], tab_ref, ones) * (HEAD_DIM ** -0.5)
    mk_ref[...] = _qk_norm_rope(k_ref[...], kg_ref[...], tab_ref, ones)


SELECT_PAGES = 16
SEL_ROWS = 8


def _moba_select_kernel(pt_ref, q_ref, *refs):
    pages = refs[:SELECT_PAGES]
    sel_ref = refs[SELECT_PAGES]
    gate_scr = refs[SELECT_PAGES + 1]
    g = pl.program_id(1)
    ppb = MOBA_BLOCK // PAGE_SIZE
    qcol = jnp.broadcast_to(q_ref[0], (BRANCH_WIDTH, LANES))
    token_ones = jnp.ones((PAGE_SIZE, LANES), BF16)
    blk_lane = lax.broadcasted_iota(jnp.int32, (HEADS, LANES), 1)

    @pl.when(g == 0)
    def _():
        gate_scr[...] = jnp.full((HEADS, LANES), NEG, F32)

    gate = gate_scr[...]
    for n in range(SELECT_PAGES // ppb):
        tile = pages[n * ppb][0, 0]
        for e in range(1, ppb):
            tile = tile + pages[n * ppb + e][0, 0]
        ksum = _dot(tile.reshape(BRANCH_WIDTH, PAGE_SIZE).astype(BF16), token_ones)
        mean = jnp.sum((ksum * qcol).reshape(HEADS, HEAD_DIM, LANES), axis=1) * (1.0 / MOBA_BLOCK)
        gate = jnp.where(blk_lane == g * (SELECT_PAGES // ppb) + n, mean, gate)
    gate_scr[...] = gate

    @pl.when(g == pl.num_programs(1) - 1)
    def _():
        left = gate
        sel = jnp.zeros((HEADS, LANES), jnp.int32)
        for r in range(MOBA_TOPK):
            best = jnp.max(left, axis=-1, keepdims=True)
            idx = jnp.min(jnp.where(left == best, blk_lane, LANES), axis=-1, keepdims=True)
            sel = jnp.where(blk_lane == r, idx, sel)
            left = jnp.where(blk_lane == idx, -jnp.inf, left)
        sel_ref[0] = sel


def _moba_select(mq, page_table, cache_kt, layer):
    nseq, npages = page_table.shape

    def page_spec(p):
        return pl.BlockSpec((1, 1, HEADS, HEAD_DIM, PAGE_SIZE),
                            lambda b, g, pt: (layer, pt[b, g * SELECT_PAGES + p], 0, 0, 0))

    grid_spec = pltpu.PrefetchScalarGridSpec(
        num_scalar_prefetch=1,
        grid=(nseq, npages // SELECT_PAGES),
        in_specs=[pl.BlockSpec((1, 256, 1), lambda b, g, pt: (b, 0, 0))]
        + [page_spec(p) for p in range(SELECT_PAGES)],
        out_specs=pl.BlockSpec((1, HEADS, LANES), lambda b, g, pt: (b, 0, 0)),
        scratch_shapes=[pltpu.VMEM((HEADS, LANES), F32)],
    )
    sel = pl.pallas_call(
        _moba_select_kernel,
        grid_spec=grid_spec,
        out_shape=jax.ShapeDtypeStruct((nseq, HEADS, LANES), jnp.int32),
        compiler_params=_cparams("parallel", "arbitrary"),
        name="moba_select",
    )(page_table, mq.reshape(nseq, 256, 1), *([cache_kt] * SELECT_PAGES))
    return sel[:, :, :MOBA_TOPK]


N_SEL_PAGES = MOBA_TOPK * (MOBA_BLOCK // PAGE_SIZE)


def _moba_step_kernel(sel_ref, pt_ref, q_ref, kn_ref, vn_ref, *refs):
    n_pages = HEADS * N_SEL_PAGES
    o_ref = refs[2 * n_pages]
    for h in range(HEADS):
        kp = refs[h * N_SEL_PAGES:(h + 1) * N_SEL_PAGES]
        vp = refs[n_pages + h * N_SEL_PAGES:n_pages + (h + 1) * N_SEL_PAGES]
        q = q_ref[0, h]
        q8 = jnp.broadcast_to(q, (SEL_ROWS, HEAD_DIM)).astype(BF16)
        logits = [_dot(q8, r[0, 0, 0].astype(BF16))[0:1] for r in kp]
        l_self = jnp.sum(q * kn_ref[0, h], axis=-1, keepdims=True)
        m = l_self
        for lg in logits:
            m = jnp.maximum(m, jnp.max(lg, axis=-1, keepdims=True))
        p_self = jnp.exp(l_self - m)
        den = p_self
        num = p_self * vn_ref[0, h]
        for lg, r in zip(logits, vp):
            p = jnp.exp(lg - m)
            den = den + jnp.sum(p, axis=-1, keepdims=True)
            p8 = jnp.broadcast_to(p, (SEL_ROWS, PAGE_SIZE)).astype(BF16)
            num = num + _dot_nt(p8, r[0, 0, 0].astype(BF16))[0:1]
        o_ref[0, h] = num / den


def _moba_step(q, k_new, v_new, sel, page_table, cache_kt, cache_vt, layer):
    nseq, npages = page_table.shape
    ppb = MOBA_BLOCK // PAGE_SIZE

    def page_spec(h, r, e):
        def index(b, sel_ref, pt_ref):
            blk = sel_ref[(b * HEADS + h) * MOBA_TOPK + r]
            return (layer, pt_ref[b * npages + ppb * blk + e], h, 0, 0)
        return pl.BlockSpec((1, 1, 1, HEAD_DIM, PAGE_SIZE), index)

    row = pl.BlockSpec((1, HEADS, 1, HEAD_DIM), lambda b, s, p: (b, 0, 0, 0))
    pages = [page_spec(h, r, e) for h in range(HEADS) for r in range(MOBA_TOPK) for e in range(ppb)]
    grid_spec = pltpu.PrefetchScalarGridSpec(
        num_scalar_prefetch=2,
        grid=(nseq,),
        in_specs=[row, row, row] + pages + pages,
        out_specs=row,
    )
    r4 = lambda a: a.reshape(nseq, HEADS, 1, HEAD_DIM)
    out = pl.pallas_call(
        _moba_step_kernel,
        grid_spec=grid_spec,
        out_shape=jax.ShapeDtypeStruct((nseq, HEADS, 1, HEAD_DIM), F32),
        compiler_params=_cparams("parallel"),
        name="moba_step",
    )(sel.reshape(-1), page_table.reshape(-1), r4(q), r4(k_new), r4(v_new),
      *([cache_kt] * len(pages)), *([cache_vt] * len(pages)))
    return out.reshape(nseq, 256)


def _layer_params(l, w):
    tile4 = lambda a: jnp.tile(a, HEADS)[None]
    rep64 = lambda a: jnp.repeat(a, HEAD_DIM)[None]
    s5p = {k: w[k][l] for k in ("s5_a_re", "s5_a_im", "s5_log_dt", "s5_b_re", "s5_b_im",
                                "s5_c_re", "s5_c_im", "s5_d")}
    return dict(
        ln1_g=w["ln1_g"][l][None], w_in=_regroup_w_in(w["w_in"][l]),
        gla_wg=jnp.pad(w["gla_w_gate"][l], ((0, 128 - GLA_RANK), (0, 0))).astype(BF16),
        gla_bg=w["gla_b_gate"][l][None],
        gdn_alog=rep64(w["gdn_a_log"][l]), gdn_dtb=rep64(w["gdn_dt_bias"][l]),
        s5_prompt=_s5_matrices(s5p, S5_CHUNK), s5_step=_s5_matrices(s5p, 1),
        s5_w_glu=w["s5_w_glu"][l].astype(BF16), s5_b_glu=w["s5_b_glu"][l][None],
        gla_norm=tile4(w["gla_norm"][l]), gla_norm_head=w["gla_norm"][l][None],
        gdn_norm=tile4(w["gdn_norm"][l]), gdn_norm_head=w["gdn_norm"][l][None],
        gdn_conv_w=jnp.pad(w["gdn_conv_w"][l], ((0, CONV_PAD - GDN_CONV), (0, 0))),
        moba_q_norm=tile4(w["moba_q_norm"][l]), moba_k_norm=tile4(w["moba_k_norm"][l]),
        w_br=jnp.stack([w["w_br_s5"][l], w["w_br_gla"][l], w["w_br_gdn"][l], w["w_br_moba"][l]]).astype(BF16),
        ln2_g=w["ln2_g"][l][None],
        layer=l, w_gate=w["w_gate_bf16"], w_out=w["w_out_bf16"], w_ff1=w["w_ff1_bf16"], w_ff2=w["w_ff2_bf16"],
    )


PROMPT_ROWS = 512


def _prompt_layer(x2d, lp, tabs, batch, seq):
    tm = min(PROMPT_ROWS, batch * seq)
    pr = _inproj(x2d, lp, tm)
    ya, s5_fin = _s5_prompt(pr["s5_u_lo"], pr["s5_u_hi"], lp["s5_prompt"], batch, seq)
    yb, gla_s = _gla_prompt(pr["gla_q"], pr["gla_k"], pr["gla_v"], pr["gla_lr"], pr["gla_r"], lp, batch, seq)
    yc, gdn_s, conv = _gdn_prompt(pr["gdn_qkv"], pr["gdn_b"], pr["gdn_a"], pr["gdn_z"], lp, batch, seq)
    kt, vtt, qt, kh, vt, km = _moba_prep(pr["moba_q"], pr["moba_k"], pr["moba_v"], tabs, lp, batch, seq)
    yd = _moba_prompt(qt, kh, vt, km, batch, seq)
    x1 = _merge(x2d, ya, yb, yc, yd, lp, tm)
    x2 = _mlp(x1, lp, tm)
    s5 = lambda a: a.reshape(batch, S5_GROUPS, S5_P)
    states = (kt, vtt, s5(s5_fin[:, :S5_STATE]), s5(s5_fin[:, S5_STATE:]), gla_s, gdn_s, conv)
    return x2, states


def _sample_layer(x2d, lp, tabs, page_table, cache_kt, cache_vt, layer, st):
    s5_re0, s5_im0, gla0, gdn0, conv0 = st
    n = x2d.shape[0]
    pr = _inproj(x2d, lp, n)
    rows = n * HEADS
    per_head = lambda a: a.reshape(rows, HEAD_DIM)
    flat_state = lambda a: a.reshape(rows, HEAD_DIM * HEAD_DIM)
    rep, til, tilt = _expand_mats()
    sds = jax.ShapeDtypeStruct

    h0 = jnp.concatenate([s5_re0.reshape(n, S5_STATE), s5_im0.reshape(n, S5_STATE)], axis=1)
    ya, s5_new = _s5_step(jnp.concatenate([pr["s5_u_lo"], pr["s5_u_hi"]], axis=1), h0, lp["s5_step"])
    ya = (ya[:, :BRANCH_WIDTH // 2], ya[:, BRANCH_WIDTH // 2:])

    yb, gla_s = _whole_call(
        _gla_step_kernel, [sds((rows, HEAD_DIM), F32), sds((rows, HEAD_DIM * HEAD_DIM), F32)], "gla_step",
        per_head(pr["gla_q"]), per_head(pr["gla_k"]), per_head(pr["gla_v"]), per_head(pr["gla_lr"]),
        per_head(pr["gla_r"]), flat_state(gla0), rep, til, tilt, lp["gla_norm_head"])

    qkv, conv_new = _whole_call(
        _gdn_conv_step_kernel, [sds((n, GDN_QKV), F32), sds((n, (GDN_CONV - 1) * GDN_QKV), F32)],
        "gdn_conv_step", pr["gdn_qkv"], conv0.reshape(n, (GDN_CONV - 1) * GDN_QKV), lp["gdn_conv_w"])
    yc, gdn_s = _whole_call(
        _gdn_step_kernel, [sds((rows, HEAD_DIM), F32), sds((rows, HEAD_DIM * HEAD_DIM), F32)], "gdn_step",
        per_head(qkv[:, :256]), per_head(qkv[:, 256:512]), per_head(qkv[:, 512:]),
        per_head(pr["gdn_b"]), per_head(pr["gdn_a"]), per_head(pr["gdn_z"]), flat_state(gdn0),
        rep, til, tilt, lp["gdn_norm_head"])

    mq, mk = _whole_call(
        _moba_qk_step_kernel, [sds((n, 256), F32), sds((n, 256), F32)], "moba_qk_step",
        pr["moba_q"], pr["moba_k"], tabs, lp["moba_q_norm"], lp["moba_k_norm"], _head_ones())
    sel = _moba_select(mq, page_table, cache_kt, layer)
    yd = _moba_step(mq, mk, pr["moba_v"], sel, page_table, cache_kt, cache_vt, layer)

    x1 = _merge(x2d, ya, yb.reshape(n, 256), yc.reshape(n, 256), yd, lp, n)
    x2 = _mlp(x1, lp, n)
    head4 = lambda a: a.reshape(n, 1, HEADS, HEAD_DIM)
    s5 = lambda a: a.reshape(n, S5_GROUPS, S5_P)
    state4 = lambda a: a.reshape(n, HEADS, HEAD_DIM, HEAD_DIM)
    states = (head4(mk), head4(pr["moba_v"]), s5(s5_new[:, :S5_STATE]), s5(s5_new[:, S5_STATE:]),
              state4(gla_s), state4(gdn_s), conv_new.reshape(n, GDN_CONV - 1, GDN_QKV))
    return x2, states


def kernel(x_prompt, x_sample, cache_moba_k, cache_moba_v, page_table, state_s5_re, state_s5_im, state_gla, state_gdn, state_gdn_conv, ln1_g, w_in, s5_a_re, s5_a_im, s5_log_dt, s5_b_re, s5_b_im, s5_c_re, s5_c_im, s5_d, s5_w_glu, s5_b_glu, gla_w_gate, gla_b_gate, gla_norm, gdn_conv_w, gdn_a_log, gdn_dt_bias, gdn_norm, moba_q_norm, moba_k_norm, w_gate, w_br_s5, w_br_gla, w_br_gdn, w_br_moba, w_out, ln2_g, w_ff1, w_ff2):
    weights = dict(ln1_g=ln1_g, w_in=w_in, s5_a_re=s5_a_re, s5_a_im=s5_a_im, s5_log_dt=s5_log_dt,
                   s5_b_re=s5_b_re, s5_b_im=s5_b_im, s5_c_re=s5_c_re, s5_c_im=s5_c_im, s5_d=s5_d,
                   s5_w_glu=s5_w_glu, s5_b_glu=s5_b_glu, gla_w_gate=gla_w_gate, gla_b_gate=gla_b_gate,
                   gla_norm=gla_norm, gdn_conv_w=gdn_conv_w, gdn_a_log=gdn_a_log, gdn_dt_bias=gdn_dt_bias,
                   gdn_norm=gdn_norm, moba_q_norm=moba_q_norm, moba_k_norm=moba_k_norm, w_gate=w_gate,
                   w_br_s5=w_br_s5, w_br_gla=w_br_gla, w_br_gdn=w_br_gdn, w_br_moba=w_br_moba,
                   w_out=w_out, ln2_g=ln2_g, w_ff1=w_ff1, w_ff2=w_ff2)
    for name in ("w_gate", "w_out", "w_ff1", "w_ff2"):
        weights[name + "_bf16"] = weights[name].astype(BF16)
    depth = ln1_g.shape[0]
    layers = [_layer_params(l, weights) for l in range(depth)]
    batch, seq, _ = x_prompt.shape
    nseq = x_sample.shape[0]
    npages = page_table.shape[1]
    past_len = npages * PAGE_SIZE
    assert seq % (S5_CHUNK * 8) == 0 and seq % GLA_ROWS == 0 and seq % MOBA_BLOCK == 0
    assert (batch * seq) % min(PROMPT_ROWS, batch * seq) == 0 and x_sample.shape[1] == 1
    assert npages % SELECT_PAGES == 0 and past_len % MOBA_BLOCK == 0 and past_len // MOBA_BLOCK >= MOBA_TOPK

    xp = x_prompt.reshape(batch * seq, D_MODEL)
    tabs_p = _rope_tables(jnp.arange(seq, dtype=jnp.int32))
    p_states = []
    for l in range(depth):
        xp, st = _prompt_layer(xp, layers[l], tabs_p, batch, seq)
        p_states.append(st)

    cache_kt = cache_moba_k.transpose(0, 1, 3, 4, 2)
    cache_vt = cache_moba_v.transpose(0, 1, 3, 4, 2)
    tabs_s = _rope_tables(jnp.full((1,), past_len, jnp.int32))
    xs = x_sample.reshape(nseq, D_MODEL)
    s_states = []
    for l in range(depth):
        st0 = (state_s5_re[l], state_s5_im[l], state_gla[l], state_gdn[l], state_gdn_conv[l])
        xs, st = _sample_layer(xs, layers[l], tabs_s, page_table, cache_kt, cache_vt, l, st0)
        s_states.append(st)

    stack = lambda states: [jnp.stack([s[i] for s in states]) for i in range(len(states[0]))]
    p_out = stack(p_states)
    for i in range(2):
        p_out[i] = p_out[i].transpose(0, 1, 4, 2, 3)
    return (xp.reshape(batch, seq, D_MODEL), xs.reshape(nseq, 1, D_MODEL), *p_out, *stack(s_states))
```

```python
import math

import jax
import jax.numpy as jnp
import numpy as np
from jax import lax
from jax.experimental import pallas as pl
from jax.experimental.pallas import tpu as pltpu

F32 = jnp.float32
BF16 = jnp.bfloat16

D_MODEL = 1024
N_BRANCH = 4
BRANCH_WIDTH = D_MODEL // N_BRANCH
HEADS = 4
HEAD_DIM = BRANCH_WIDTH // HEADS
S5_GROUP = 16
S5_GROUPS = BRANCH_WIDTH // S5_GROUP
S5_P = 64
S5_STATE = S5_GROUPS * S5_P
GLA_RANK = 16
GLA_TAU = 16.0
GDN_CONV = 4
GDN_QKV = 3 * BRANCH_WIDTH
MOBA_BLOCK = 256
MOBA_TOPK = 3
ROT_DIM = HEAD_DIM // 4
ROPE_THETA = 500000.0
PAGE_SIZE = 128
D_FF = 4 * D_MODEL
EPS = 1e-6
NEG = -1e30

LANES = 128
S5_CHUNK = 8
GLA_SUB = 16
GDN_CHUNK = 64
VMEM_LIMIT = 56 * 1024 * 1024

HIGHEST = lax.Precision.HIGHEST


def _cparams(*sem):
    return pltpu.CompilerParams(dimension_semantics=sem, vmem_limit_bytes=VMEM_LIMIT)


def _const_spec(shape):
    zeros = (0,) * len(shape)
    return pl.BlockSpec(shape, lambda *_: zeros)


def _layer_spec(shape, layer, **kw):
    zeros = (0,) * len(shape)
    return pl.BlockSpec((None,) + tuple(shape), lambda *_: (layer,) + zeros, **kw)


def _dot(a, b):
    return jnp.dot(a, b, preferred_element_type=F32)


def _dot_nt(a, b):
    return lax.dot_general(a, b, (((1,), (1,)), ((), ())), preferred_element_type=F32)


def _dot_tn(a, b):
    return lax.dot_general(a, b, (((0,), (0,)), ((), ())), preferred_element_type=F32)


def _bdot(a, b):
    return _dot(a.astype(BF16), b.astype(BF16))


def _split2(x):
    hi = x.astype(BF16)
    lo = (x - hi.astype(F32)).astype(BF16)
    return hi, lo


def _dot_sel(x, sel):
    hi, lo = _split2(x)
    return _dot(hi, sel) + _dot(lo, sel)


def _dot_sel_exact(x, sel):
    x1 = x.astype(BF16)
    r1 = x - x1.astype(F32)
    x2 = r1.astype(BF16)
    x3 = (r1 - x2.astype(F32)).astype(BF16)
    return _dot(x1, sel) + _dot(x2, sel) + _dot(x3, sel)


def _rms_rows(x, g):
    return x * lax.rsqrt(jnp.mean(x * x, axis=-1, keepdims=True) + EPS) * g


def _sigmoid(x):
    return 1.0 / (1.0 + jnp.exp(-x))


def _silu(x):
    return x * _sigmoid(x)


def _softplus(x):
    return jnp.maximum(x, 0.0) + jnp.log1p(jnp.exp(-jnp.abs(x)))


def _head_ones():
    r = np.arange(BRANCH_WIDTH) // HEAD_DIM
    return jnp.asarray(r[:, None] == r[None, :], BF16)


IN_OUTS = (("s5_u_lo", 128), ("s5_u_hi", 128), ("gla_q", 256), ("gla_k", 256), ("gla_v", 256), ("gla_r", 256),
           ("gla_lr", 128), ("gdn_qkv", 768), ("gdn_b", 0), ("gdn_a", 0), ("gdn_z", 256),
           ("moba_q", 256), ("moba_k", 256), ("moba_v", 256))
IN_WIDTH = sum(w for _, w in IN_OUTS)
GDN_B_COL = GLA_RANK
GDN_A_COL = GLA_RANK + HEADS


def _regroup_w_in(w_in):
    sizes = (256, 256, 256, 256, GLA_RANK, 256, 256, 256, 256, HEADS, HEADS, 256, 256, 256, 256)
    offs = np.cumsum((0,) + sizes)
    (s5_u, a_q, a_k, a_v, a_lr, a_r, d_q, d_k, d_v, d_b, d_a, d_z, m_q, m_k, m_v) = (
        w_in[:, offs[i]:offs[i + 1]] for i in range(len(sizes)))
    small = jnp.pad(jnp.concatenate([a_lr, d_b, d_a], axis=1), ((0, 0), (0, 128 - GLA_RANK - 2 * HEADS)))
    cols = [s5_u, a_q, a_k, a_v, a_r, small, d_q, d_k, d_v, d_z, m_q, m_k, m_v]
    return jnp.concatenate(cols, axis=1).astype(BF16)


def _head_spread():
    m = np.zeros((128, 2 * BRANCH_WIDTH), np.float32)
    for h in range(HEADS):
        m[GDN_B_COL + h, h * HEAD_DIM:(h + 1) * HEAD_DIM] = 1.0
        m[GDN_A_COL + h, BRANCH_WIDTH + h * HEAD_DIM:BRANCH_WIDTH + (h + 1) * HEAD_DIM] = 1.0
    return jnp.asarray(m, BF16)


def _inproj_kernel(x_ref, g_ref, w_ref, wg_ref, bg_ref, alog_ref, dtb_ref, spread_ref, *outs):
    x = x_ref[...]
    hb = _rms_rows(x, g_ref[...]).astype(BF16)
    vals = {}
    off = 0
    for name, n in IN_OUTS:
        if n:
            vals[name] = _dot(hb, w_ref[:, off:off + n])
            off += n
    small = vals["gla_lr"]
    z = _bdot(small, wg_ref[...]) + bg_ref[...]
    vals["gla_lr"] = -_softplus(-z) * (1.0 / GLA_TAU)
    per_head = _dot_sel_exact(small, spread_ref[...])
    vals["gdn_b"] = _sigmoid(per_head[:, :BRANCH_WIDTH])
    vals["gdn_a"] = -jnp.exp(alog_ref[...]) * _softplus(per_head[:, BRANCH_WIDTH:] + dtb_ref[...])
    for (name, _), o_ref in zip(IN_OUTS, outs):
        o_ref[...] = vals[name]


def _inproj(x2d, lp, tm):
    n = x2d.shape[0]
    out_shape = []
    out_specs = []
    for name, w in IN_OUTS:
        w_out = 256 if name in ("gla_lr", "gdn_b", "gdn_a") else w
        out_shape.append(jax.ShapeDtypeStruct((n, w_out), F32))
        out_specs.append(pl.BlockSpec((tm, w_out), lambda i: (i, 0)))
    res = pl.pallas_call(
        _inproj_kernel,
        grid=(n // tm,),
        in_specs=[pl.BlockSpec((tm, D_MODEL), lambda i: (i, 0)),
                  _const_spec((1, D_MODEL)), _const_spec((D_MODEL, IN_WIDTH)),
                  _const_spec((128, 256)), _const_spec((1, 256)),
                  _const_spec((1, 256)), _const_spec((1, 256)), _const_spec((128, 2 * BRANCH_WIDTH))],
        out_specs=out_specs,
        out_shape=out_shape,
        compiler_params=_cparams("parallel"),
        name="inproj",
    )(x2d, lp["ln1_g"], lp["w_in"], lp["gla_wg"], lp["gla_bg"], lp["gdn_alog"], lp["gdn_dtb"], _head_spread())
    return dict(zip((nm for nm, _ in IN_OUTS), res))


def _merge_kernel(x_ref, ya_lo_ref, ya_hi_ref, yb_ref, yc_ref, yd_ref, g_ref, wgate_ref, wglu_ref, bglu_ref,
                  wbr_ref, wout_ref, o_ref):
    x = x_ref[...]
    hb = _rms_rows(x, g_ref[...]).astype(BF16)
    ya = jnp.concatenate([ya_lo_ref[...], ya_hi_ref[...]], axis=1)
    ya = ya * _sigmoid(_bdot(ya, wglu_ref[...]) + bglu_ref[...])
    merged = None
    for i, y in enumerate((ya, yb_ref[...], yc_ref[...], yd_ref[...])):
        gate = _sigmoid(_dot(hb, wgate_ref[:, i * D_MODEL:(i + 1) * D_MODEL]))
        term = gate * _dot(y.astype(BF16), wbr_ref[i])
        merged = term if merged is None else merged + term
    o_ref[...] = x + _bdot(merged, wout_ref[...])


def _merge(x2d, ya, yb, yc, yd, lp, tm):
    n = x2d.shape[0]
    row = lambda w: pl.BlockSpec((tm, w), lambda i: (i, 0))
    ya_lo, ya_hi = ya
    return pl.pallas_call(
        _merge_kernel,
        grid=(n // tm,),
        in_specs=[row(D_MODEL), row(128), row(128), row(256), row(256), row(256),
                  _const_spec((1, D_MODEL)), _layer_spec((D_MODEL, N_BRANCH * D_MODEL), lp["layer"]),
                  _const_spec((256, 256)), _const_spec((1, 256)),
                  _const_spec((N_BRANCH, 256, D_MODEL)), _layer_spec((D_MODEL, D_MODEL), lp["layer"])],
        out_specs=row(D_MODEL),
        out_shape=jax.ShapeDtypeStruct((n, D_MODEL), F32),
        compiler_params=_cparams("parallel"),
        name="merge",
    )(x2d, ya_lo, ya_hi, yb, yc, yd, lp["ln1_g"], lp["w_gate"], lp["s5_w_glu"], lp["s5_b_glu"],
      lp["w_br"], lp["w_out"])


def _mlp_kernel(x_ref, g_ref, w1_ref, w2_ref, o_ref):
    x = x_ref[...]
    hb = _rms_rows(x, g_ref[...]).astype(BF16)
    z = jnp.maximum(_dot(hb, w1_ref[...]), 0.0)
    o_ref[...] = x + _bdot(z * z, w2_ref[...])


def _mlp(x2d, lp, tm):
    n = x2d.shape[0]
    row = pl.BlockSpec((tm, D_MODEL), lambda i: (i, 0))
    single = pl.Buffered(1)
    return pl.pallas_call(
        _mlp_kernel,
        grid=(n // tm,),
        in_specs=[row, _const_spec((1, D_MODEL)),
                  _layer_spec((D_MODEL, D_FF), lp["layer"], pipeline_mode=single),
                  _layer_spec((D_FF, D_MODEL), lp["layer"], pipeline_mode=single)],
        out_specs=row,
        out_shape=jax.ShapeDtypeStruct((n, D_MODEL), F32),
        compiler_params=_cparams("parallel"),
        name="mlp",
    )(x2d, lp["ln2_g"], lp["w_ff1"], lp["w_ff2"])


def _s5_matrices(p, chunk):
    hp = dict(precision=HIGHEST)
    dt = jnp.exp(p["s5_log_dt"])[:, None]
    ar, ai = p["s5_a_re"], p["s5_a_im"]
    mag = jnp.exp(ar * dt)
    abar_re = mag * jnp.cos(ai * dt)
    abar_im = mag * jnp.sin(ai * dt)
    den = ar * ar + ai * ai
    nr = abar_re - 1.0
    f_re = (nr * ar + abar_im * ai) / den
    f_im = (abar_im * ar - nr * ai) / den
    br, bi = p["s5_b_re"], p["s5_b_im"]
    bbar_re = f_re[..., None] * br - f_im[..., None] * bi
    bbar_im = f_re[..., None] * bi + f_im[..., None] * br
    pw_re = [jnp.ones_like(abar_re)]
    pw_im = [jnp.zeros_like(abar_re)]
    for _ in range(chunk):
        r, i = pw_re[-1], pw_im[-1]
        pw_re.append(r * abar_re - i * abar_im)
        pw_im.append(r * abar_im + i * abar_re)
    pw_re = jnp.stack(pw_re)
    pw_im = jnp.stack(pw_im)
    cr, ci = p["s5_c_re"], p["s5_c_im"]
    ca_re = cr[None] * pw_re[:, :, None, :] - ci[None] * pw_im[:, :, None, :]
    ca_im = cr[None] * pw_im[:, :, None, :] + ci[None] * pw_re[:, :, None, :]
    def table(t):
        lead = t.shape[:-3]
        t = jnp.moveaxis(t, -1, -3)
        return t.reshape(lead + (t.shape[-3], -1))

    kern = (jnp.einsum("tgop,gpi->tgoi", ca_re[:chunk], bbar_re, **hp)
            - jnp.einsum("tgop,gpi->tgoi", ca_im[:chunk], bbar_im, **hp))
    rev_re = pw_re[:chunk][::-1]
    rev_im = pw_im[:chunk][::-1]
    ab_re = rev_re[..., None] * bbar_re[None] - rev_im[..., None] * bbar_im[None]
    ab_im = rev_re[..., None] * bbar_im[None] + rev_im[..., None] * bbar_re[None]
    a_pow = jnp.stack([pw_re[chunk].reshape(-1), pw_im[chunk].reshape(-1)])
    halves = lambda t, n: jnp.stack([t[..., :n], t[..., n:]])
    xs_re, xs_im = halves(table(ab_re), S5_HALF_STATE), halves(table(ab_im), S5_HALF_STATE)
    hy = jnp.stack([table(ca_re[1:]), -table(ca_im[1:])], axis=1)
    return dict(lag=halves(table(kern), S5_HALF),
                xs=jnp.concatenate([xs_re, xs_im], axis=-1),
                hy=halves(hy, S5_HALF),
                a_pow=a_pow, d_row=p["s5_d"][None, :])


S5_HALF = BRANCH_WIDTH // 2
S5_HALF_GROUPS = S5_GROUPS // 2
S5_HALF_STATE = S5_STATE // 2


def _s5_expand(lag_ref, xs_ref, hy_ref, wxs_scr, win_scr, why_scr):
    chunk = lag_ref.shape[1]
    w = S5_HALF

    def group_of(shape, axis, per_group, wrap=None):
        idx = lax.broadcasted_iota(jnp.int32, shape, axis)
        if wrap is not None:
            idx = idx % wrap
        return idx // per_group

    same_ii = group_of((w, w), 0, S5_GROUP) == group_of((w, w), 1, S5_GROUP)
    same_is = (group_of((w, 2 * S5_HALF_STATE), 0, S5_GROUP)
               == group_of((w, 2 * S5_HALF_STATE), 1, S5_P, wrap=S5_HALF_STATE))
    same_si = group_of((S5_HALF_STATE, w), 0, S5_P) == group_of((S5_HALF_STATE, w), 1, S5_GROUP)
    down = lambda t: jnp.concatenate([t] * S5_HALF_GROUPS, axis=0)
    zero = jnp.zeros((w, w), BF16)
    for hf in range(2):
        lags = [jnp.where(same_ii, down(lag_ref[hf, tau]), 0.0).astype(BF16) for tau in range(chunk)]
        for s in range(chunk):
            wxs_scr[hf, s * w:(s + 1) * w, :] = jnp.where(same_is, down(xs_ref[hf, s]), 0.0).astype(BF16)
            win_scr[hf, s * w:(s + 1) * w, :] = jnp.concatenate(
                [lags[t - s] if t >= s else zero for t in range(chunk)], axis=1)
            for part in range(2):
                why_scr[hf, part * S5_HALF_STATE:(part + 1) * S5_HALF_STATE, s * w:(s + 1) * w] = jnp.where(
                    same_si, down(hy_ref[hf, s, part]), 0.0).astype(BF16)


def _s5_state_increment(u_halves, wxs_ref):
    lo = _dot(u_halves[0], wxs_ref[0])
    hi = _dot(u_halves[1], wxs_ref[1])
    n = S5_HALF_STATE
    return jnp.concatenate([lo[:, :n], hi[:, :n], lo[:, n:], hi[:, n:]], axis=1)


def _s5_outputs(u_halves, h_bf16, win_ref, why_ref):
    n = S5_HALF_STATE
    outs = []
    for hf in range(2):
        h_half = jnp.concatenate([h_bf16[:, hf * n:(hf + 1) * n],
                                  h_bf16[:, S5_STATE + hf * n:S5_STATE + (hf + 1) * n]], axis=1)
        outs.append(_dot(u_halves[hf], win_ref[hf]) + _dot(h_half, why_ref[hf]))
    return outs


def _gelu(y):
    c = math.sqrt(2.0 / math.pi)
    return 0.5 * y * (1.0 + jnp.tanh(c * (y + 0.044715 * (y * y * y))))


def _s5_kernel(u_lo_ref, u_hi_ref, lag_ref, xs_ref, hy_ref, apow_ref, d_ref, y_lo_ref, y_hi_ref, hfin_ref,
               wxs_scr, win_scr, why_scr, xs_scr, hs_scr):
    chunk = lag_ref.shape[1]
    rows = u_lo_ref.shape[0] // chunk

    @pl.when(pl.program_id(0) == 0)
    def _():
        _s5_expand(lag_ref, xs_ref, hy_ref, wxs_scr, win_scr, why_scr)

    us = [jnp.concatenate([ref[pl.ds(s, rows, stride=chunk), :] for s in range(chunk)], axis=1)
          for ref in (u_lo_ref, u_hi_ref)]
    ubs = [u.astype(BF16) for u in us]
    xs_scr[...] = _s5_state_increment(ubs, wxs_scr)
    a_re = apow_ref[0:1, :]
    a_im = apow_ref[1:2, :]

    def step(r, carry):
        hr, hi = carry
        hs_scr[pl.ds(r, 1), :] = jnp.concatenate([hr, hi], axis=-1)
        x = xs_scr[pl.ds(r, 1), :]
        nhr = a_re * hr - a_im * hi + x[:, :S5_STATE]
        nhi = a_re * hi + a_im * hr + x[:, S5_STATE:]
        return nhr, nhi

    zero = jnp.zeros((1, S5_STATE), F32)
    hr, hi = lax.fori_loop(0, rows, step, (zero, zero))
    hfin_ref[0] = jnp.concatenate([hr, hi], axis=-1)
    ys = _s5_outputs(ubs, hs_scr[...].astype(BF16), win_scr, why_scr)
    for hf, y_ref in enumerate((y_lo_ref, y_hi_ref)):
        d = d_ref[:, hf * S5_HALF:(hf + 1) * S5_HALF]
        for t in range(chunk):
            cols = slice(t * S5_HALF, (t + 1) * S5_HALF)
            y_ref[pl.ds(t, rows, stride=chunk), :] = _gelu(ys[hf][:, cols] + d * us[hf][:, cols])


def _s5_scratch(chunk):
    width = chunk * S5_HALF
    return [pltpu.VMEM((2, width, 2 * S5_HALF_STATE), BF16), pltpu.VMEM((2, width, width), BF16),
            pltpu.VMEM((2, 2 * S5_HALF_STATE, width), BF16)]


def _s5_prompt(u_lo, u_hi, sm, batch, seq):
    c = sm["lag"].shape[1]
    rows = seq // c
    half = BRANCH_WIDTH // 2
    tok = pl.BlockSpec((seq, half), lambda b: (b, 0))
    y_lo, y_hi, hfin = pl.pallas_call(
        _s5_kernel,
        grid=(batch,),
        in_specs=[tok, tok, _const_spec(sm["lag"].shape), _const_spec(sm["xs"].shape),
                  _const_spec(sm["hy"].shape), _const_spec((2, S5_STATE)), _const_spec((1, BRANCH_WIDTH))],
        out_specs=[tok, tok, pl.BlockSpec((1, 1, 2 * S5_STATE), lambda b: (b, 0, 0))],
        out_shape=[jax.ShapeDtypeStruct((batch * seq, half), F32),
                   jax.ShapeDtypeStruct((batch * seq, half), F32),
                   jax.ShapeDtypeStruct((batch, 1, 2 * S5_STATE), F32)],
        scratch_shapes=_s5_scratch(c) + [pltpu.VMEM((rows, 2 * S5_STATE), F32),
                                         pltpu.VMEM((rows, 2 * S5_STATE), F32)],
        compiler_params=_cparams("arbitrary"),
        name="s5_prompt",
    )(u_lo, u_hi, sm["lag"], sm["xs"], sm["hy"], sm["a_pow"], sm["d_row"])
    return (y_lo, y_hi), hfin.reshape(batch, 2 * S5_STATE)


GLA_ROWS = 256


def _gla_kernel(q_ref, k_ref, v_ref, g_ref, r_ref, tri_ref, ones_ref, bmask_ref, gn_ref,
                y_ref, sfin_ref, st_scr):
    step = pl.program_id(1)
    sub = GLA_SUB
    rows = q_ref.shape[0]
    ns = rows // sub

    @pl.when(step == 0)
    def _():
        st_scr[...] = jnp.zeros_like(st_scr)

    ones = ones_ref[...]
    bmask = bmask_ref[...]
    split = lambda a: a.reshape(ns, sub, BRANCH_WIDTH)
    bc = split(jnp.dot(tri_ref[...], g_ref[...], precision=HIGHEST, preferred_element_type=F32))
    q = split(q_ref[...] * (HEAD_DIM ** -0.5))
    k = split(k_ref[...])
    v = split(v_ref[...])

    row = lax.broadcasted_iota(jnp.int32, (ns, sub, BRANCH_WIDTH), 1)
    parts = []
    for j in range(sub):
        e = jnp.exp(jnp.minimum(bc - bc[:, j:j + 1, :], 0.0))
        parts.append(jnp.where(row >= j, q * k[:, j:j + 1, :] * e, 0.0))
    att = _dot(jnp.concatenate(parts, axis=1).reshape(ns * sub * sub, BRANCH_WIDTH).astype(BF16), ones)
    att = att.reshape(ns, sub * sub, BRANCH_WIDTH)
    o = att[:, 0:sub] * v[:, 0:1, :]
    for j in range(1, sub):
        o = o + att[:, j * sub:(j + 1) * sub] * v[:, j:j + 1, :]

    last = bc[:, sub - 1:sub, :]
    qt = (q * jnp.exp(bc)).astype(BF16)
    kt = (k * jnp.exp(last - bc)).astype(BF16)
    vb = v.astype(BF16)
    decay = jnp.exp(last)
    outer = [bmask * _dot_tn(vb[s], kt[s]) for s in range(ns)]
    st = st_scr[...]
    inter = []
    for s in range(ns):
        inter.append(_dot_nt(qt[s], st.astype(BF16)))
        st = st * decay[s] + outer[s]
    st_scr[...] = st
    o = o.reshape(rows, BRANCH_WIDTH) + jnp.concatenate(inter, axis=0)
    ms = _dot_sel(o * o, ones) * (1.0 / HEAD_DIM)
    y_ref[...] = o * lax.rsqrt(ms + EPS) * gn_ref[...] * _silu(r_ref[...])

    @pl.when(step == pl.num_programs(1) - 1)
    def _():
        sfin_ref[0] = st


def _sub_tril(rows, sub):
    i = np.arange(rows)
    return jnp.asarray((i[:, None] // sub == i[None, :] // sub) & (i[None, :] <= i[:, None]), F32)


def _unpack_state_t(st):
    b = st.shape[0]
    st = st.reshape(b, HEADS, HEAD_DIM, HEADS, HEAD_DIM)
    diag = jnp.stack([st[:, h, :, h, :] for h in range(HEADS)], axis=1)
    return diag.transpose(0, 1, 3, 2)


def _gla_prompt(q, k, v, g, r, lp, batch, seq):
    rows = min(GLA_ROWS, seq)
    nsteps = seq // rows
    blk = pl.BlockSpec((rows, BRANCH_WIDTH), lambda b, c: (b * nsteps + c, 0))
    hm = np.arange(BRANCH_WIDTH) // HEAD_DIM
    bmask = jnp.asarray(hm[:, None] == hm[None, :], F32)
    y, sfin = pl.pallas_call(
        _gla_kernel,
        grid=(batch, nsteps),
        in_specs=[blk, blk, blk, blk, blk, _const_spec((rows, rows)),
                  _const_spec((BRANCH_WIDTH, BRANCH_WIDTH)), _const_spec((BRANCH_WIDTH, BRANCH_WIDTH)),
                  _const_spec((1, BRANCH_WIDTH))],
        out_specs=[blk, pl.BlockSpec((1, BRANCH_WIDTH, BRANCH_WIDTH), lambda b, c: (b, 0, 0))],
        out_shape=[jax.ShapeDtypeStruct((batch * seq, BRANCH_WIDTH), F32),
                   jax.ShapeDtypeStruct((batch, BRANCH_WIDTH, BRANCH_WIDTH), F32)],
        scratch_shapes=[pltpu.VMEM((BRANCH_WIDTH, BRANCH_WIDTH), F32)],
        compiler_params=_cparams("parallel", "arbitrary"),
        name="gla_prompt",
    )(q, k, v, g, r, _sub_tril(rows, GLA_SUB), _head_ones(), bmask, lp["gla_norm"])
    return y, _unpack_state_t(sfin)


CONV_PAD = 8


GDN_ROWS = 512


def _block_diag(x, ones):
    return jnp.concatenate([x] * HEADS, axis=0) * ones


def _unit_lower_inverses(ns, eye, ones):
    c = ns[0].shape[0]
    every = range(len(ns))
    invs = [eye - n for n in ns]
    pws = list(ns)
    for _ in range(int(math.log2(c)) - 1):
        pbs = [pw.astype(BF16) for pw in pws]
        pws = [_dot(pb, _block_diag(pb, ones)) for pb in pbs]
        invs = [invs[i] + _dot(invs[i].astype(BF16), _block_diag(pws[i].astype(BF16), ones)) for i in every]
    inv_parts = [_split2(inv) for inv in invs]
    n_parts = [_split2(n) for n in ns]
    prods = [_dot(jnp.concatenate(n_parts[i], axis=0), _block_diag(inv_parts[i][0], ones)) for i in every]
    cross = [_dot(n_parts[i][0], _block_diag(inv_parts[i][1], ones)) for i in every]
    resids = [eye - invs[i] - (prods[i][:c] + prods[i][c:] + cross[i]) for i in every]
    return [invs[i] + _dot(inv_parts[i][0], _block_diag(resids[i].astype(BF16), ones)) for i in every]


def _gdn_kernel(x_ref, b_ref, g_ref, z_ref, cw_ref, tri_ref, ones_ref, eye_ref, gn_ref,
                y_ref, sfin_ref, conv_ref, s_scr, buf_scr):
    step = pl.program_id(1)
    rows = x_ref.shape[0]
    c = min(GDN_CHUNK, rows)

    @pl.when(step == 0)
    def _():
        s_scr[...] = jnp.zeros_like(s_scr)
        buf_scr[0:CONV_PAD, :] = jnp.zeros((CONV_PAD, GDN_QKV), F32)

    x = x_ref[...]
    buf_scr[CONV_PAD:CONV_PAD + rows, :] = x
    conv = cw_ref[GDN_CONV - 1:GDN_CONV, :] * x
    for w in range(GDN_CONV - 1):
        lag = GDN_CONV - 1 - w
        conv = conv + cw_ref[w:w + 1, :] * buf_scr[CONV_PAD - lag:CONV_PAD - lag + rows, :]
    tail = buf_scr[rows:rows + CONV_PAD, :]
    buf_scr[0:CONV_PAD, :] = tail
    conv_ref[0] = tail
    qkv = _silu(conv)
    ones = ones_ref[...]
    onesf = ones.astype(F32)
    eye = eye_ref[...]
    q = qkv[:, 0:BRANCH_WIDTH]
    k = qkv[:, BRANCH_WIDTH:2 * BRANCH_WIDTH]
    v = qkv[:, 2 * BRANCH_WIDTH:]
    q = q * lax.rsqrt(_dot_sel(q * q, ones) + EPS) * (HEAD_DIM ** -0.5)
    k = k * lax.rsqrt(_dot_sel(k * k, ones) + EPS)
    beta = b_ref[...]
    gc = jnp.dot(tri_ref[...], g_ref[...], precision=HIGHEST, preferred_element_type=F32)
    gam = jnp.exp(gc)
    ri = lax.broadcasted_iota(jnp.int32, (c, BRANCH_WIDTH), 0)
    cj = lax.broadcasted_iota(jnp.int32, (c, BRANCH_WIDTH), 1) % HEAD_DIM

    every = range(rows // c)
    sls = [slice(n * c, (n + 1) * c) for n in every]
    kcs = [k[rs].astype(BF16) for rs in sls]
    kqs = [_dot_nt(jnp.concatenate([k[rs], q[rs]], axis=0).astype(BF16), _block_diag(kcs[n], ones))
           for n, rs in enumerate(sls)]
    decs = []
    for rs in sls:
        grow = jnp.sum(gc[rs] * eye, axis=0, keepdims=True)
        decs.append(jnp.where(ri >= cj, jnp.exp(jnp.minimum(gc[rs] - grow, 0.0)), 0.0))
    invs = _unit_lower_inverses(
        [jnp.where(ri > cj, beta[rs] * decs[n] * kqs[n][:c], 0.0) for n, rs in enumerate(sls)], eye, ones)
    invbs = [inv.astype(BF16) for inv in invs]
    ws_m = [_dot(invbs[n], _block_diag((beta[rs] * gam[rs] * k[rs]).astype(BF16), ones))
            for n, rs in enumerate(sls)]
    u0s = [_dot(invbs[n], _block_diag((beta[rs] * v[rs]).astype(BF16), ones)) for n, rs in enumerate(sls)]
    aqks = [(decs[n] * kqs[n][c:]).astype(BF16) for n in every]
    lhs = [jnp.concatenate([ws_m[n], gam[rs] * q[rs]], axis=0).astype(BF16) for n, rs in enumerate(sls)]
    glasts = [gc[rs][c - 1:c, :] for rs in sls]
    kds = [(k[rs] * jnp.exp(glasts[n] - gc[rs])).astype(BF16) for n, rs in enumerate(sls)]

    s = s_scr[...]
    outs = []
    for n in every:
        ws = _dot(lhs[n], s.astype(BF16))
        ub = (u0s[n] - ws[:c]).astype(BF16)
        outs.append(ws[c:] + _dot(aqks[n], _block_diag(ub, ones)))
        s = jnp.exp(glasts[n]) * s + onesf * _dot_tn(kds[n], ub)
    s_scr[...] = s
    o = jnp.concatenate(outs, axis=0)
    ms = _dot_sel(o * o, ones) * (1.0 / HEAD_DIM)
    y_ref[...] = o * lax.rsqrt(ms + EPS) * gn_ref[...] * _silu(z_ref[...])

    @pl.when(step == pl.num_programs(1) - 1)
    def _():
        sfin_ref[0] = s


def _unpack_state(st):
    b = st.shape[0]
    st = st.reshape(b, HEADS, HEAD_DIM, HEADS, HEAD_DIM)
    return jnp.stack([st[:, h, :, h, :] for h in range(HEADS)], axis=1)


def _gdn_prompt(x, beta, g, z, lp, batch, seq):
    rows = min(GDN_ROWS, seq)
    c = min(GDN_CHUNK, rows)
    assert c == HEAD_DIM and seq % rows == 0 and rows % c == 0
    nsteps = seq // rows
    blk = lambda w: pl.BlockSpec((rows, w), lambda b, s: (b * nsteps + s, 0))
    eye =jnp.asarray(np.tile(np.eye(c, dtype=np.float32), (1, HEADS)))
    y, sfin, conv = pl.pallas_call(
        _gdn_kernel,
        grid=(batch, nsteps),
        in_specs=[blk(GDN_QKV), blk(256), blk(256), blk(256), _const_spec((CONV_PAD, GDN_QKV)),
                  _const_spec((rows, rows)), _const_spec((256, 256)), _const_spec((c, 256)),
                  _const_spec((1, 256))],
        out_specs=[blk(256),
                   pl.BlockSpec((1, 256, 256), lambda b, s: (b, 0, 0)),
                   pl.BlockSpec((1, CONV_PAD, GDN_QKV), lambda b, s: (b, 0, 0))],
        out_shape=[jax.ShapeDtypeStruct((batch * seq, 256), F32),
                   jax.ShapeDtypeStruct((batch, 256, 256), F32),
                   jax.ShapeDtypeStruct((batch, CONV_PAD, GDN_QKV), F32)],
        scratch_shapes=[pltpu.VMEM((256, 256), F32),
                        pltpu.VMEM((CONV_PAD + rows, GDN_QKV), F32)],
        compiler_params=_cparams("parallel", "arbitrary"),
        name="gdn_prompt",
    )(x, beta, g, z, lp["gdn_conv_w"], _sub_tril(rows, c), _head_ones(), eye, lp["gdn_norm"])
    return y, _unpack_state(sfin), conv[:, CONV_PAD - (GDN_CONV - 1):, :]


HALF_ROT = ROT_DIM // 2


def _rope_tables(pos):
    inv = ROPE_THETA ** (-jnp.arange(HALF_ROT, dtype=F32) / HALF_ROT)
    ang = pos.astype(F32)[:, None] * inv[None, :]
    cos, sin = jnp.cos(ang), jnp.sin(ang)
    n = pos.shape[0]
    rest = HEAD_DIM - ROT_DIM
    head = lambda a, b, fill: jnp.concatenate([a, b, jnp.full((n, rest), fill, F32)], axis=1)
    zero = jnp.zeros_like(sin)
    tabs = [head(cos, cos, 1.0), head(-sin, zero, 0.0), head(zero, sin, 0.0)]
    return jnp.stack([jnp.tile(t, (1, HEADS)) for t in tabs])


def _qk_norm_rope(x, gain, tab_ref, ones):
    y = x * lax.rsqrt(_dot_sel(x * x, ones) * (1.0 / HEAD_DIM) + EPS) * gain
    up = pltpu.roll(y, BRANCH_WIDTH - HALF_ROT, 1)
    down = pltpu.roll(y, HALF_ROT, 1)
    return y * tab_ref[0] + up * tab_ref[1] + down * tab_ref[2]


K_AUG = 2 * HEAD_DIM
V_AUG = HEAD_DIM + 16


def _moba_prep_kernel(q_ref, k_ref, v_ref, tab_ref, qg_ref, kg_ref, ones_ref,
                      kt_ref, vtt_ref, qt_ref, kh_ref, vt_ref, km_ref):
    ones = ones_ref[...]
    rows = q_ref.shape[0]
    nblk = rows // MOBA_BLOCK
    mq = _qk_norm_rope(q_ref[...], qg_ref[...], tab_ref, ones) * (HEAD_DIM ** -0.5)
    mk = _qk_norm_rope(k_ref[...], kg_ref[...], tab_ref, ones)
    kmean = jnp.mean(mk.reshape(nblk, MOBA_BLOCK, BRANCH_WIDTH), axis=1)
    qt = mq.T
    kt = mk.T
    vt = v_ref[...].T
    one_col = (lax.broadcasted_iota(jnp.int32, (MOBA_BLOCK, K_AUG - HEAD_DIM), 1) == 0).astype(BF16)
    one_row = (lax.broadcasted_iota(jnp.int32, (V_AUG - HEAD_DIM, MOBA_BLOCK), 0) == 0).astype(BF16)
    for h in range(HEADS):
        sl = slice(h * HEAD_DIM, (h + 1) * HEAD_DIM)
        qt_ref[0, h] = qt[sl, :]
        km_ref[0, h] = kmean[:, sl]
        kt_ref[0, h] = kt[sl, :]
        vtt_ref[0, h] = vt[sl, :]
        for j in range(nblk):
            rs = slice(j * MOBA_BLOCK, (j + 1) * MOBA_BLOCK)
            kh_ref[0, h, j] = jnp.concatenate([mk[rs, sl].astype(BF16), one_col], axis=1)
            vt_ref[0, h, j] = jnp.concatenate([vt[sl, rs].astype(BF16), one_row], axis=0)


def _moba_prep(q, k, v, tabs, lp, batch, seq):
    rows = min(8 * MOBA_BLOCK, seq)
    nsteps = seq // rows
    nblk = rows // MOBA_BLOCK
    nb = seq // MOBA_BLOCK
    blk = pl.BlockSpec((rows, 256), lambda b, r: (b * nsteps + r, 0))
    cache_rows = pl.BlockSpec((1, HEADS, HEAD_DIM, rows), lambda b, r: (b, 0, 0, r))
    return pl.pallas_call(
        _moba_prep_kernel,
        grid=(batch, nsteps),
        in_specs=[blk, blk, blk, pl.BlockSpec((3, rows, 256), lambda b, r: (0, r, 0)),
                  _const_spec((1, 256)), _const_spec((1, 256)), _const_spec((256, 256))],
        out_specs=[cache_rows, cache_rows, cache_rows,
                   pl.BlockSpec((1, HEADS, nblk, MOBA_BLOCK, K_AUG), lambda b, r: (b, 0, r, 0, 0)),
                   pl.BlockSpec((1, HEADS, nblk, V_AUG, MOBA_BLOCK), lambda b, r: (b, 0, r, 0, 0)),
                   pl.BlockSpec((1, HEADS, nblk, HEAD_DIM), lambda b, r: (b, 0, r, 0))],
        out_shape=[jax.ShapeDtypeStruct((batch, HEADS, HEAD_DIM, seq), F32),
                   jax.ShapeDtypeStruct((batch, HEADS, HEAD_DIM, seq), F32),
                   jax.ShapeDtypeStruct((batch, HEADS, HEAD_DIM, seq), F32),
                   jax.ShapeDtypeStruct((batch, HEADS, nb, MOBA_BLOCK, K_AUG), BF16),
                   jax.ShapeDtypeStruct((batch, HEADS, nb, V_AUG, MOBA_BLOCK), BF16),
                   jax.ShapeDtypeStruct((batch, HEADS, nb, HEAD_DIM), F32)],
        compiler_params=_cparams("parallel", "parallel"),
        name="moba_prep",
    )(q, k, v, tabs, lp["moba_q_norm"], lp["moba_k_norm"], _head_ones())


def _moba_attn_kernel(qt_ref, kh_ref, vt_ref, km_ref, o_ref, bias_scr):
    qb = pl.program_id(1)
    nb = km_ref.shape[2]
    blk = MOBA_BLOCK
    heads = range(HEADS)
    blk_id = lax.broadcasted_iota(jnp.int32, (nb, blk), 0)
    kpos = lax.broadcasted_iota(jnp.int32, (blk, blk), 0)
    qpos = lax.broadcasted_iota(jnp.int32, (blk, blk), 1)
    first_row = lax.broadcasted_iota(jnp.int32, (K_AUG - HEAD_DIM, blk), 0) == 0
    qts = [qt_ref[0, h] for h in heads]
    qtb = [qt.astype(BF16) for qt in qts]

    def scores(j, biases):
        out = []
        for h in heads:
            extra = jnp.where(first_row, biases[h], 0.0).astype(BF16)
            out.append(_dot(kh_ref[0, h, j], jnp.concatenate([qtb[h], extra], axis=0)))
        return out

    gates = [jnp.dot(km_ref[0, h], qts[h], precision=HIGHEST, preferred_element_type=F32) for h in heads]
    own = scores(qb, [jnp.zeros((1, blk), F32)] * HEADS)
    for h in heads:
        gate = jnp.where(blk_id < qb, gates[h], NEG)
        taken = jnp.zeros((nb, blk), jnp.bool_)
        for _ in range(min(MOBA_TOPK, nb)):
            best = jnp.max(gate, axis=0, keepdims=True)
            idx = jnp.min(jnp.where(gate == best, blk_id, nb), axis=0, keepdims=True)
            hit = blk_id == idx
            taken = jnp.logical_or(taken, hit)
            gate = jnp.where(hit, -jnp.inf, gate)
        bias_scr[h] = jnp.where(jnp.logical_and(taken, blk_id < qb), 0.0, NEG)

    ms, ps = [], []
    for h in heads:
        s = jnp.where(kpos <= qpos, own[h], NEG)
        m = jnp.max(s, axis=0, keepdims=True)
        ms.append(m)
        ps.append(jnp.exp(s - m).astype(BF16))
    accs = [_dot(vt_ref[0, h, qb], ps[h]) for h in heads]

    def bias_rows(j):
        return [bias_scr[h, pl.ds(j, 1), :] for h in heads]

    def update(carry, blocks):
        ms, accs = carry
        ss = [scores(j, bias_rows(j)) for j in blocks]
        new_m, new_acc = [], []
        for h in heads:
            m_new = ms[h]
            for s in ss:
                m_new = jnp.maximum(m_new, jnp.max(s[h], axis=0, keepdims=True))
            acc = jnp.exp(ms[h] - m_new) * accs[h]
            for j, s in zip(blocks, ss):
                acc = acc + _dot(vt_ref[0, h, j], jnp.exp(s[h] - m_new).astype(BF16))
            new_m.append(m_new)
            new_acc.append(acc)
        return tuple(new_m), tuple(new_acc)

    carry = (tuple(ms), tuple(accs))
    start = 0
    for width in (4, 2, 1):
        trips = (qb - start) // width
        carry = lax.fori_loop(
            0, trips, lambda i, c, w=width, s=start: update(c, tuple(s + w * i + e for e in range(w))), carry)
        start = start + trips * width
    ms, accs = carry
    outs = [accs[h][:HEAD_DIM] / accs[h][HEAD_DIM:HEAD_DIM + 1] for h in heads]
    o_ref[...] = jnp.concatenate(outs, axis=0).T


def _moba_prompt(qt, kh, vt, km, batch, seq):
    nb = seq // MOBA_BLOCK
    return pl.pallas_call(
        _moba_attn_kernel,
        grid=(batch, nb),
        in_specs=[pl.BlockSpec((1, HEADS, HEAD_DIM, MOBA_BLOCK), lambda b, i: (b, 0, 0, i)),
                  pl.BlockSpec((1, HEADS, nb, MOBA_BLOCK, K_AUG), lambda b, i: (b, 0, 0, 0, 0)),
                  pl.BlockSpec((1, HEADS, nb, V_AUG, MOBA_BLOCK), lambda b, i: (b, 0, 0, 0, 0)),
                  pl.BlockSpec((1, HEADS, nb, HEAD_DIM), lambda b, i: (b, 0, 0, 0))],
        out_specs=pl.BlockSpec((MOBA_BLOCK, 256), lambda b, i: (b * nb + i, 0)),
        out_shape=jax.ShapeDtypeStruct((batch * seq, 256), F32),
        scratch_shapes=[pltpu.VMEM((HEADS, nb, MOBA_BLOCK), F32)],
        compiler_params=_cparams("parallel", "arbitrary"),
        name="moba_prompt",
    )(qt, kh, vt, km)


def _s5_step_kernel(u_ref, h0_ref, lag_ref, xs_ref, hy_ref, apow_ref, d_ref, y_ref, h_ref,
                    wxs_ref, win_ref, why_ref):
    _s5_expand(lag_ref, xs_ref, hy_ref, wxs_ref, win_ref, why_ref)
    u = u_ref[...]
    ubs = [u[:, :S5_HALF].astype(BF16), u[:, S5_HALF:].astype(BF16)]
    h0 = h0_ref[...]
    xs = _s5_state_increment(ubs, wxs_ref)
    a_re = apow_ref[0:1, :]
    a_im = apow_ref[1:2, :]
    hr0 = h0[:, :S5_STATE]
    hi0 = h0[:, S5_STATE:]
    hr = a_re * hr0 - a_im * hi0 + xs[:, :S5_STATE]
    hi = a_re * hi0 + a_im * hr0 + xs[:, S5_STATE:]
    h_ref[...] = jnp.concatenate([hr, hi], axis=-1)
    y = jnp.concatenate(_s5_outputs(ubs, h0.astype(BF16), win_ref, why_ref), axis=1)
    y_ref[...] = _gelu(y + d_ref[...] * u)


def _s5_step(u, h0, sm):
    n = u.shape[0]
    return pl.pallas_call(
        _s5_step_kernel,
        out_shape=[jax.ShapeDtypeStruct((n, BRANCH_WIDTH), F32),
                   jax.ShapeDtypeStruct((n, 2 * S5_STATE), F32)],
        scratch_shapes=_s5_scratch(1),
        compiler_params=pltpu.CompilerParams(vmem_limit_bytes=VMEM_LIMIT),
        name="s5_step",
    )(u, h0, sm["lag"], sm["xs"], sm["hy"], sm["a_pow"], sm["d_row"])


def _expand_mats():
    idx = np.arange(HEAD_DIM * HEAD_DIM)
    rep = (np.arange(HEAD_DIM)[:, None] == idx[None, :] // HEAD_DIM)
    til = (np.arange(HEAD_DIM)[:, None] == idx[None, :] % HEAD_DIM)
    return jnp.asarray(rep, BF16), jnp.asarray(til, BF16), jnp.asarray(til.T, BF16)


def _gla_step_kernel(q_ref, k_ref, v_ref, g_ref, r_ref, s0_ref, rep_ref, til_ref, tilt_ref, gn_ref,
                     y_ref, s_ref):
    rep = rep_ref[...]
    eg = _dot_sel_exact(jnp.exp(g_ref[...]), rep)
    kr = _dot_sel_exact(k_ref[...], rep)
    qr = _dot_sel_exact(q_ref[...] * (HEAD_DIM ** -0.5), rep)
    vt = _dot_sel_exact(v_ref[...], til_ref[...])
    s = eg * s0_ref[...] + kr * vt
    s_ref[...] = s
    o = _dot_sel(qr * s, tilt_ref[...])
    y_ref[...] = _rms_rows(o, gn_ref[...]) * _silu(r_ref[...])


def _gdn_conv_step_kernel(x_ref, c0_ref, cw_ref, qkv_ref, cnew_ref):
    x = x_ref[...]
    c0 = c0_ref[...]
    conv = cw_ref[GDN_CONV - 1:GDN_CONV, :] * x
    for w in range(GDN_CONV - 1):
        conv = conv + cw_ref[w:w + 1, :] * c0[:, w * GDN_QKV:(w + 1) * GDN_QKV]
    qkv_ref[...] = _silu(conv)
    cnew_ref[...] = jnp.concatenate([c0[:, GDN_QKV:], x], axis=-1)


def _gdn_step_kernel(q_ref, k_ref, v_ref, b_ref, g_ref, z_ref, s0_ref, rep_ref, til_ref, tilt_ref,
                     gn_ref, y_ref, s_ref):
    q = q_ref[...]
    k = k_ref[...]
    q = q * lax.rsqrt(jnp.sum(q * q, axis=-1, keepdims=True) + EPS) * (HEAD_DIM ** -0.5)
    k = k * lax.rsqrt(jnp.sum(k * k, axis=-1, keepdims=True) + EPS)
    beta = b_ref[:, 0:1]
    gam = jnp.exp(g_ref[:, 0:1])
    rep = rep_ref[...]
    tilt = tilt_ref[...]
    kr = _dot_sel_exact(k, rep)
    qr = _dot_sel_exact(q, rep)
    s0 = s0_ref[...]
    ks = _dot_sel(kr * s0, tilt)
    qs = _dot_sel(qr * s0, tilt)
    u = v_ref[...] - gam * ks
    qk = jnp.sum(q * k, axis=-1, keepdims=True)
    o = gam * qs + (beta * qk) * u
    s_ref[...] = gam * s0 + kr * _dot_sel_exact(beta * u, til_ref[...])
    y_ref[...] = _rms_rows(o, gn_ref[...]) * _silu(z_ref[...])


def _whole_call(kernel, out_shape, name, *args):
    return pl.pallas_call(kernel, out_shape=out_shape, name=name,
                          compiler_params=pltpu.CompilerParams(vmem_limit_bytes=VMEM_LIMIT))(*args)


def _moba_qk_step_kernel(q_ref, k_ref, tab_ref, qg_ref, kg_ref, ones_ref, mq_ref, mk_ref):
    ones = ones_ref[...]
    mq_ref[...] = _qk_norm_rope(q_ref[...], qg_ref[...], tab_ref, ones) * (HEAD_DIM ** -0.5)
    mk_ref[...] = _qk_norm_rope(k_ref[...], kg_ref[...], tab_ref, ones)


SELECT_PAGES = 64
SEL_ROWS = 8


def _moba_select_kernel(pt_ref, q_ref, *refs):
    pages, sel_ref, gate_scr = refs[:-2], refs[-2], refs[-1]
    per_step = len(pages)
    g = pl.program_id(1)
    ppb = MOBA_BLOCK // PAGE_SIZE
    qcol = jnp.broadcast_to(q_ref[0], (BRANCH_WIDTH, LANES))
    token_ones = jnp.ones((PAGE_SIZE, LANES), BF16)
    blk_lane = lax.broadcasted_iota(jnp.int32, (HEADS, LANES), 1)

    @pl.when(g == 0)
    def _():
        gate_scr[...] = jnp.full((HEADS, LANES), NEG, F32)

    gate = gate_scr[...]
    for n in range(per_step // ppb):
        tile = pages[n * ppb][0, 0]
        for e in range(1, ppb):
            tile = tile + pages[n * ppb + e][0, 0]
        ksum = _dot(tile.reshape(BRANCH_WIDTH, PAGE_SIZE).astype(BF16), token_ones)
        mean = jnp.sum((ksum * qcol).reshape(HEADS, HEAD_DIM, LANES), axis=1) * (1.0 / MOBA_BLOCK)
        gate = jnp.where(blk_lane == g * (per_step // ppb) + n, mean, gate)
    gate_scr[...] = gate

    @pl.when(g == pl.num_programs(1) - 1)
    def _():
        left = gate
        sel = jnp.zeros((HEADS, LANES), jnp.int32)
        for r in range(MOBA_TOPK):
            best = jnp.max(left, axis=-1, keepdims=True)
            idx = jnp.min(jnp.where(left == best, blk_lane, LANES), axis=-1, keepdims=True)
            sel = jnp.where(blk_lane == r, idx, sel)
            left = jnp.where(blk_lane == idx, -jnp.inf, left)
        sel_ref[0] = sel


def _moba_select(mq, page_table, cache_kt, layer):
    nseq, npages = page_table.shape
    per_step = min(SELECT_PAGES, npages)
    assert npages % per_step == 0 and per_step % (MOBA_BLOCK // PAGE_SIZE) == 0

    def page_spec(p):
        return pl.BlockSpec((1, 1, HEADS, HEAD_DIM, PAGE_SIZE),
                            lambda b, g, pt: (layer, pt[b, g * per_step + p], 0, 0, 0))

    grid_spec = pltpu.PrefetchScalarGridSpec(
        num_scalar_prefetch=1,
        grid=(nseq, npages // per_step),
        in_specs=[pl.BlockSpec((1, 256, 1), lambda b, g, pt: (b, 0, 0))]
        + [page_spec(p) for p in range(per_step)],
        out_specs=pl.BlockSpec((1, HEADS, LANES), lambda b, g, pt: (b, 0, 0)),
        scratch_shapes=[pltpu.VMEM((HEADS, LANES), F32)],
    )
    sel = pl.pallas_call(
        _moba_select_kernel,
        grid_spec=grid_spec,
        out_shape=jax.ShapeDtypeStruct((nseq, HEADS, LANES), jnp.int32),
        compiler_params=_cparams("parallel", "arbitrary"),
        name="moba_select",
    )(page_table, mq.reshape(nseq, 256, 1), *([cache_kt] * per_step))
    return sel[:, :, :MOBA_TOPK]


N_SEL_PAGES = MOBA_TOPK * (MOBA_BLOCK // PAGE_SIZE)


def _moba_step_kernel(sel_ref, pt_ref, q_ref, kn_ref, vn_ref, *refs):
    n_pages = HEADS * N_SEL_PAGES
    o_ref = refs[2 * n_pages]
    for h in range(HEADS):
        kp = refs[h * N_SEL_PAGES:(h + 1) * N_SEL_PAGES]
        vp = refs[n_pages + h * N_SEL_PAGES:n_pages + (h + 1) * N_SEL_PAGES]
        q = q_ref[0, h]
        q8 = jnp.broadcast_to(q, (SEL_ROWS, HEAD_DIM)).astype(BF16)
        logits = [_dot(q8, r[0, 0, 0].astype(BF16))[0:1] for r in kp]
        l_self = jnp.sum(q * kn_ref[0, h], axis=-1, keepdims=True)
        m = l_self
        for lg in logits:
            m = jnp.maximum(m, jnp.max(lg, axis=-1, keepdims=True))
        p_self = jnp.exp(l_self - m)
        den = p_self
        num = p_self * vn_ref[0, h]
        for lg, r in zip(logits, vp):
            p = jnp.exp(lg - m)
            den = den + jnp.sum(p, axis=-1, keepdims=True)
            p8 = jnp.broadcast_to(p, (SEL_ROWS, PAGE_SIZE)).astype(BF16)
            num = num + _dot_nt(p8, r[0, 0, 0].astype(BF16))[0:1]
        o_ref[0, h] = num / den


def _moba_step(q, k_new, v_new, sel, page_table, cache_kt, cache_vt, layer):
    nseq, npages = page_table.shape
    ppb = MOBA_BLOCK // PAGE_SIZE

    def page_spec(h, r, e):
        def index(b, sel_ref, pt_ref):
            blk = sel_ref[(b * HEADS + h) * MOBA_TOPK + r]
            return (layer, pt_ref[b * npages + ppb * blk + e], h, 0, 0)
        return pl.BlockSpec((1, 1, 1, HEAD_DIM, PAGE_SIZE), index)

    row = pl.BlockSpec((1, HEADS, 1, HEAD_DIM), lambda b, s, p: (b, 0, 0, 0))
    pages = [page_spec(h, r, e) for h in range(HEADS) for r in range(MOBA_TOPK) for e in range(ppb)]
    grid_spec = pltpu.PrefetchScalarGridSpec(
        num_scalar_prefetch=2,
        grid=(nseq,),
        in_specs=[row, row, row] + pages + pages,
        out_specs=row,
    )
    r4 = lambda a: a.reshape(nseq, HEADS, 1, HEAD_DIM)
    out = pl.pallas_call(
        _moba_step_kernel,
        grid_spec=grid_spec,
        out_shape=jax.ShapeDtypeStruct((nseq, HEADS, 1, HEAD_DIM), F32),
        compiler_params=_cparams("parallel"),
        name="moba_step",
    )(sel.reshape(-1), page_table.reshape(-1), r4(q), r4(k_new), r4(v_new),
      *([cache_kt] * len(pages)), *([cache_vt] * len(pages)))
    return out.reshape(nseq, 256)


def _layer_params(l, w):
    tile4 = lambda a: jnp.tile(a, HEADS)[None]
    rep64 = lambda a: jnp.repeat(a, HEAD_DIM)[None]
    s5p = {k: w[k][l] for k in ("s5_a_re", "s5_a_im", "s5_log_dt", "s5_b_re", "s5_b_im",
                                "s5_c_re", "s5_c_im", "s5_d")}
    return dict(
        ln1_g=w["ln1_g"][l][None], w_in=_regroup_w_in(w["w_in"][l]),
        gla_wg=jnp.pad(w["gla_w_gate"][l], ((0, 128 - GLA_RANK), (0, 0))).astype(BF16),
        gla_bg=w["gla_b_gate"][l][None],
        gdn_alog=rep64(w["gdn_a_log"][l]), gdn_dtb=rep64(w["gdn_dt_bias"][l]),
        s5_prompt=_s5_matrices(s5p, S5_CHUNK), s5_step=_s5_matrices(s5p, 1),
        s5_w_glu=w["s5_w_glu"][l].astype(BF16), s5_b_glu=w["s5_b_glu"][l][None],
        gla_norm=tile4(w["gla_norm"][l]), gla_norm_head=w["gla_norm"][l][None],
        gdn_norm=tile4(w["gdn_norm"][l]), gdn_norm_head=w["gdn_norm"][l][None],
        gdn_conv_w=jnp.pad(w["gdn_conv_w"][l], ((0, CONV_PAD - GDN_CONV), (0, 0))),
        moba_q_norm=tile4(w["moba_q_norm"][l]), moba_k_norm=tile4(w["moba_k_norm"][l]),
        w_br=jnp.stack([w["w_br_s5"][l], w["w_br_gla"][l], w["w_br_gdn"][l], w["w_br_moba"][l]]).astype(BF16),
        ln2_g=w["ln2_g"][l][None],
        layer=l, w_gate=w["w_gate_bf16"], w_out=w["w_out_bf16"], w_ff1=w["w_ff1_bf16"], w_ff2=w["w_ff2_bf16"],
    )


PROMPT_ROWS = 512


def _prompt_layer(x2d, lp, tabs, batch, seq):
    tm = min(PROMPT_ROWS, batch * seq)
    pr = _inproj(x2d, lp, tm)
    ya, s5_fin = _s5_prompt(pr["s5_u_lo"], pr["s5_u_hi"], lp["s5_prompt"], batch, seq)
    yb, gla_s = _gla_prompt(pr["gla_q"], pr["gla_k"], pr["gla_v"], pr["gla_lr"], pr["gla_r"], lp, batch, seq)
    yc, gdn_s, conv = _gdn_prompt(pr["gdn_qkv"], pr["gdn_b"], pr["gdn_a"], pr["gdn_z"], lp, batch, seq)
    kt, vtt, qt, kh, vt, km = _moba_prep(pr["moba_q"], pr["moba_k"], pr["moba_v"], tabs, lp, batch, seq)
    yd = _moba_prompt(qt, kh, vt, km, batch, seq)
    x1 = _merge(x2d, ya, yb, yc, yd, lp, tm)
    x2 = _mlp(x1, lp, tm)
    s5 = lambda a: a.reshape(batch, S5_GROUPS, S5_P)
    states = (kt, vtt, s5(s5_fin[:, :S5_STATE]), s5(s5_fin[:, S5_STATE:]), gla_s, gdn_s, conv)
    return x2, states


def _sample_layer(x2d, lp, tabs, page_table, cache_kt, cache_vt, layer, st):
    s5_re0, s5_im0, gla0, gdn0, conv0 = st
    n = x2d.shape[0]
    pr = _inproj(x2d, lp, n)
    rows = n * HEADS
    per_head = lambda a: a.reshape(rows, HEAD_DIM)
    flat_state = lambda a: a.reshape(rows, HEAD_DIM * HEAD_DIM)
    rep, til, tilt = _expand_mats()
    sds = jax.ShapeDtypeStruct

    h0 = jnp.concatenate([s5_re0.reshape(n, S5_STATE), s5_im0.reshape(n, S5_STATE)], axis=1)
    ya, s5_new = _s5_step(jnp.concatenate([pr["s5_u_lo"], pr["s5_u_hi"]], axis=1), h0, lp["s5_step"])
    ya = (ya[:, :BRANCH_WIDTH // 2], ya[:, BRANCH_WIDTH // 2:])

    yb, gla_s = _whole_call(
        _gla_step_kernel, [sds((rows, HEAD_DIM), F32), sds((rows, HEAD_DIM * HEAD_DIM), F32)], "gla_step",
        per_head(pr["gla_q"]), per_head(pr["gla_k"]), per_head(pr["gla_v"]), per_head(pr["gla_lr"]),
        per_head(pr["gla_r"]), flat_state(gla0), rep, til, tilt, lp["gla_norm_head"])

    qkv, conv_new = _whole_call(
        _gdn_conv_step_kernel, [sds((n, GDN_QKV), F32), sds((n, (GDN_CONV - 1) * GDN_QKV), F32)],
        "gdn_conv_step", pr["gdn_qkv"], conv0.reshape(n, (GDN_CONV - 1) * GDN_QKV), lp["gdn_conv_w"])
    yc, gdn_s = _whole_call(
        _gdn_step_kernel, [sds((rows, HEAD_DIM), F32), sds((rows, HEAD_DIM * HEAD_DIM), F32)], "gdn_step",
        per_head(qkv[:, :256]), per_head(qkv[:, 256:512]), per_head(qkv[:, 512:]),
        per_head(pr["gdn_b"]), per_head(pr["gdn_a"]), per_head(pr["gdn_z"]), flat_state(gdn0),
        rep, til, tilt, lp["gdn_norm_head"])

    mq, mk = _whole_call(
        _moba_qk_step_kernel, [sds((n, 256), F32), sds((n, 256), F32)], "moba_qk_step",
        pr["moba_q"], pr["moba_k"], tabs, lp["moba_q_norm"], lp["moba_k_norm"], _head_ones())
    sel = _moba_select(mq, page_table, cache_kt, layer)
    yd = _moba_step(mq, mk, pr["moba_v"], sel, page_table, cache_kt, cache_vt, layer)

    x1 = _merge(x2d, ya, yb.reshape(n, 256), yc.reshape(n, 256), yd, lp, n)
    x2 = _mlp(x1, lp, n)
    head4 = lambda a: a.reshape(n, 1, HEADS, HEAD_DIM)
    s5 = lambda a: a.reshape(n, S5_GROUPS, S5_P)
    state4 = lambda a: a.reshape(n, HEADS, HEAD_DIM, HEAD_DIM)
    states = (head4(mk), head4(pr["moba_v"]), s5(s5_new[:, :S5_STATE]), s5(s5_new[:, S5_STATE:]),
              state4(gla_s), state4(gdn_s), conv_new.reshape(n, GDN_CONV - 1, GDN_QKV))
    return x2, states


def kernel(x_prompt, x_sample, cache_moba_k, cache_moba_v, page_table, state_s5_re, state_s5_im, state_gla, state_gdn, state_gdn_conv, ln1_g, w_in, s5_a_re, s5_a_im, s5_log_dt, s5_b_re, s5_b_im, s5_c_re, s5_c_im, s5_d, s5_w_glu, s5_b_glu, gla_w_gate, gla_b_gate, gla_norm, gdn_conv_w, gdn_a_log, gdn_dt_bias, gdn_norm, moba_q_norm, moba_k_norm, w_gate, w_br_s5, w_br_gla, w_br_gdn, w_br_moba, w_out, ln2_g, w_ff1, w_ff2):
    weights = dict(ln1_g=ln1_g, w_in=w_in, s5_a_re=s5_a_re, s5_a_im=s5_a_im, s5_log_dt=s5_log_dt,
                   s5_b_re=s5_b_re, s5_b_im=s5_b_im, s5_c_re=s5_c_re, s5_c_im=s5_c_im, s5_d=s5_d,
                   s5_w_glu=s5_w_glu, s5_b_glu=s5_b_glu, gla_w_gate=gla_w_gate, gla_b_gate=gla_b_gate,
                   gla_norm=gla_norm, gdn_conv_w=gdn_conv_w, gdn_a_log=gdn_a_log, gdn_dt_bias=gdn_dt_bias,
                   gdn_norm=gdn_norm, moba_q_norm=moba_q_norm, moba_k_norm=moba_k_norm, w_gate=w_gate,
                   w_br_s5=w_br_s5, w_br_gla=w_br_gla, w_br_gdn=w_br_gdn, w_br_moba=w_br_moba,
                   w_out=w_out, ln2_g=ln2_g, w_ff1=w_ff1, w_ff2=w_ff2)
    for name in ("w_gate", "w_out", "w_ff1", "w_ff2"):
        weights[name + "_bf16"] = weights[name].astype(BF16)
    depth = ln1_g.shape[0]
    layers = [_layer_params(l, weights) for l in range(depth)]
    batch, seq, _ = x_prompt.shape
    nseq = x_sample.shape[0]
    npages = page_table.shape[1]
    past_len = npages * PAGE_SIZE
    assert seq % (S5_CHUNK * 8) == 0 and seq % GLA_ROWS == 0 and seq % MOBA_BLOCK == 0
    assert (batch * seq) % min(PROMPT_ROWS, batch * seq) == 0 and x_sample.shape[1] == 1
    assert past_len % MOBA_BLOCK == 0 and past_len // MOBA_BLOCK >= MOBA_TOPK

    xp = x_prompt.reshape(batch * seq, D_MODEL)
    tabs_p = _rope_tables(jnp.arange(seq, dtype=jnp.int32))
    p_states = []
    for l in range(depth):
        xp, st = _prompt_layer(xp, layers[l], tabs_p, batch, seq)
        p_states.append(st)

    cache_kt = cache_moba_k.transpose(0, 1, 3, 4, 2)
    cache_vt = cache_moba_v.transpose(0, 1, 3, 4, 2)
    tabs_s = _rope_tables(jnp.full((1,), past_len, jnp.int32))
    xs = x_sample.reshape(nseq, D_MODEL)
    s_states = []
    for l in range(depth):
        st0 = (state_s5_re[l], state_s5_im[l], state_gla[l], state_gdn[l], state_gdn_conv[l])
        xs, st = _sample_layer(xs, layers[l], tabs_s, page_table, cache_kt, cache_vt, l, st0)
        s_states.append(st)

    stack = lambda states: [jnp.stack([s[i] for s in states]) for i in range(len(states[0]))]
    p_out = stack(p_states)
    for i in range(2):
        p_out[i] = p_out[i].transpose(0, 1, 4, 2, 3)
    return (xp.reshape(batch, seq, D_MODEL), xs.reshape(nseq, 1, D_MODEL), *p_out, *stack(s_states))
```

```python
import math

import jax
import jax.numpy as jnp
import numpy as np
from jax import lax
from jax.experimental import pallas as pl
from jax.experimental.pallas import tpu as pltpu

F32 = jnp.float32
BF16 = jnp.bfloat16

D_MODEL = 1024
N_BRANCH = 4
BRANCH_WIDTH = D_MODEL // N_BRANCH
HEADS = 4
HEAD_DIM = BRANCH_WIDTH // HEADS
S5_GROUP = 16
S5_GROUPS = BRANCH_WIDTH // S5_GROUP
S5_P = 64
S5_STATE = S5_GROUPS * S5_P
GLA_RANK = 16
GLA_TAU = 16.0
GDN_CONV = 4
GDN_QKV = 3 * BRANCH_WIDTH
MOBA_BLOCK = 256
MOBA_TOPK = 3
ROT_DIM = HEAD_DIM // 4
ROPE_THETA = 500000.0
PAGE_SIZE = 128
D_FF = 4 * D_MODEL
EPS = 1e-6
NEG = -1e30

LANES = 128
S5_CHUNK = 8
GLA_SUB = 16
GDN_CHUNK = 64
VMEM_LIMIT = 56 * 1024 * 1024

HIGHEST = lax.Precision.HIGHEST


def _cparams(*sem):
    return pltpu.CompilerParams(dimension_semantics=sem, vmem_limit_bytes=VMEM_LIMIT)


def _const_spec(shape):
    zeros = (0,) * len(shape)
    return pl.BlockSpec(shape, lambda *_: zeros)


def _layer_spec(shape, layer, **kw):
    zeros = (0,) * len(shape)
    return pl.BlockSpec((None,) + tuple(shape), lambda *_: (layer,) + zeros, **kw)


def _dot(a, b):
    return jnp.dot(a, b, preferred_element_type=F32)


def _dot_nt(a, b):
    return lax.dot_general(a, b, (((1,), (1,)), ((), ())), preferred_element_type=F32)


def _dot_tn(a, b):
    return lax.dot_general(a, b, (((0,), (0,)), ((), ())), preferred_element_type=F32)


def _bdot(a, b):
    return _dot(a.astype(BF16), b.astype(BF16))


def _split2(x):
    hi = x.astype(BF16)
    lo = (x - hi.astype(F32)).astype(BF16)
    return hi, lo


def _dot_sel(x, sel):
    hi, lo = _split2(x)
    return _dot(hi, sel) + _dot(lo, sel)


def _dot_sel_exact(x, sel):
    x1 = x.astype(BF16)
    r1 = x - x1.astype(F32)
    x2 = r1.astype(BF16)
    x3 = (r1 - x2.astype(F32)).astype(BF16)
    return _dot(x1, sel) + _dot(x2, sel) + _dot(x3, sel)


def _rms_rows(x, g):
    return x * lax.rsqrt(jnp.mean(x * x, axis=-1, keepdims=True) + EPS) * g


def _sigmoid(x):
    return 1.0 / (1.0 + jnp.exp(-x))


def _silu(x):
    return x * _sigmoid(x)


def _softplus(x):
    return jnp.maximum(x, 0.0) + jnp.log1p(jnp.exp(-jnp.abs(x)))


def _head_ones():
    r = np.arange(BRANCH_WIDTH) // HEAD_DIM
    return jnp.asarray(r[:, None] == r[None, :], BF16)


IN_OUTS = (("s5_u_lo", 128), ("s5_u_hi", 128), ("gla_q", 256), ("gla_k", 256), ("gla_v", 256), ("gla_r", 256),
           ("gla_lr", 128), ("gdn_qkv", 768), ("gdn_b", 0), ("gdn_a", 0), ("gdn_z", 256),
           ("moba_q", 256), ("moba_k", 256), ("moba_v", 256))
IN_WIDTH = sum(w for _, w in IN_OUTS)
GDN_B_COL = GLA_RANK
GDN_A_COL = GLA_RANK + HEADS


def _regroup_w_in(w_in):
    sizes = (256, 256, 256, 256, GLA_RANK, 256, 256, 256, 256, HEADS, HEADS, 256, 256, 256, 256)
    offs = np.cumsum((0,) + sizes)
    (s5_u, a_q, a_k, a_v, a_lr, a_r, d_q, d_k, d_v, d_b, d_a, d_z, m_q, m_k, m_v) = (
        w_in[:, offs[i]:offs[i + 1]] for i in range(len(sizes)))
    small = jnp.pad(jnp.concatenate([a_lr, d_b, d_a], axis=1), ((0, 0), (0, 128 - GLA_RANK - 2 * HEADS)))
    cols = [s5_u, a_q, a_k, a_v, a_r, small, d_q, d_k, d_v, d_z, m_q, m_k, m_v]
    return jnp.concatenate(cols, axis=1).astype(BF16)


def _head_spread():
    m = np.zeros((128, 2 * BRANCH_WIDTH), np.float32)
    for h in range(HEADS):
        m[GDN_B_COL + h, h * HEAD_DIM:(h + 1) * HEAD_DIM] = 1.0
        m[GDN_A_COL + h, BRANCH_WIDTH + h * HEAD_DIM:BRANCH_WIDTH + (h + 1) * HEAD_DIM] = 1.0
    return jnp.asarray(m, BF16)


def _inproj_kernel(x_ref, g_ref, w_ref, wg_ref, bg_ref, alog_ref, dtb_ref, spread_ref, *outs):
    x = x_ref[...]
    hb = _rms_rows(x, g_ref[...]).astype(BF16)
    vals = {}
    off = 0
    for name, n in IN_OUTS:
        if n:
            vals[name] = _dot(hb, w_ref[:, off:off + n])
            off += n
    small = vals["gla_lr"]
    z = _bdot(small, wg_ref[...]) + bg_ref[...]
    vals["gla_lr"] = -_softplus(-z) * (1.0 / GLA_TAU)
    per_head = _dot_sel_exact(small, spread_ref[...])
    vals["gdn_b"] = _sigmoid(per_head[:, :BRANCH_WIDTH])
    vals["gdn_a"] = -jnp.exp(alog_ref[...]) * _softplus(per_head[:, BRANCH_WIDTH:] + dtb_ref[...])
    for (name, _), o_ref in zip(IN_OUTS, outs):
        o_ref[...] = vals[name]


def _inproj(x2d, lp, tm):
    n = x2d.shape[0]
    out_shape = []
    out_specs = []
    for name, w in IN_OUTS:
        w_out = 256 if name in ("gla_lr", "gdn_b", "gdn_a") else w
        out_shape.append(jax.ShapeDtypeStruct((n, w_out), F32))
        out_specs.append(pl.BlockSpec((tm, w_out), lambda i: (i, 0)))
    res = pl.pallas_call(
        _inproj_kernel,
        grid=(n // tm,),
        in_specs=[pl.BlockSpec((tm, D_MODEL), lambda i: (i, 0)),
                  _const_spec((1, D_MODEL)), _const_spec((D_MODEL, IN_WIDTH)),
                  _const_spec((128, 256)), _const_spec((1, 256)),
                  _const_spec((1, 256)), _const_spec((1, 256)), _const_spec((128, 2 * BRANCH_WIDTH))],
        out_specs=out_specs,
        out_shape=out_shape,
        compiler_params=_cparams("parallel"),
        name="inproj",
    )(x2d, lp["ln1_g"], lp["w_in"], lp["gla_wg"], lp["gla_bg"], lp["gdn_alog"], lp["gdn_dtb"], _head_spread())
    return dict(zip((nm for nm, _ in IN_OUTS), res))


def _merge_kernel(x_ref, ya_lo_ref, ya_hi_ref, yb_ref, yc_ref, yd_ref, g_ref, wgate_ref, wglu_ref, bglu_ref,
                  wbr_ref, wout_ref, o_ref):
    x = x_ref[...]
    hb = _rms_rows(x, g_ref[...]).astype(BF16)
    ya = jnp.concatenate([ya_lo_ref[...], ya_hi_ref[...]], axis=1)
    ya = ya * _sigmoid(_bdot(ya, wglu_ref[...]) + bglu_ref[...])
    merged = None
    for i, y in enumerate((ya, yb_ref[...], yc_ref[...], yd_ref[...])):
        gate = _sigmoid(_dot(hb, wgate_ref[:, i * D_MODEL:(i + 1) * D_MODEL]))
        term = gate * _dot(y.astype(BF16), wbr_ref[i])
        merged = term if merged is None else merged + term
    o_ref[...] = x + _bdot(merged, wout_ref[...])


def _merge(x2d, ya, yb, yc, yd, lp, tm):
    n = x2d.shape[0]
    row = lambda w: pl.BlockSpec((tm, w), lambda i: (i, 0))
    ya_lo, ya_hi = ya
    return pl.pallas_call(
        _merge_kernel,
        grid=(n // tm,),
        in_specs=[row(D_MODEL), row(128), row(128), row(256), row(256), row(256),
                  _const_spec((1, D_MODEL)), _layer_spec((D_MODEL, N_BRANCH * D_MODEL), lp["layer"]),
                  _const_spec((256, 256)), _const_spec((1, 256)),
                  _const_spec((N_BRANCH, 256, D_MODEL)), _layer_spec((D_MODEL, D_MODEL), lp["layer"])],
        out_specs=row(D_MODEL),
        out_shape=jax.ShapeDtypeStruct((n, D_MODEL), F32),
        compiler_params=_cparams("parallel"),
        name="merge",
    )(x2d, ya_lo, ya_hi, yb, yc, yd, lp["ln1_g"], lp["w_gate"], lp["s5_w_glu"], lp["s5_b_glu"],
      lp["w_br"], lp["w_out"])


def _mlp_kernel(x_ref, g_ref, w1_ref, w2_ref, o_ref):
    x = x_ref[...]
    hb = _rms_rows(x, g_ref[...]).astype(BF16)
    z = jnp.maximum(_dot(hb, w1_ref[...]), 0.0)
    o_ref[...] = x + _bdot(z * z, w2_ref[...])


def _mlp(x2d, lp, tm):
    n = x2d.shape[0]
    row = pl.BlockSpec((tm, D_MODEL), lambda i: (i, 0))
    single = pl.Buffered(1)
    return pl.pallas_call(
        _mlp_kernel,
        grid=(n // tm,),
        in_specs=[row, _const_spec((1, D_MODEL)),
                  _layer_spec((D_MODEL, D_FF), lp["layer"], pipeline_mode=single),
                  _layer_spec((D_FF, D_MODEL), lp["layer"], pipeline_mode=single)],
        out_specs=row,
        out_shape=jax.ShapeDtypeStruct((n, D_MODEL), F32),
        compiler_params=_cparams("parallel"),
        name="mlp",
    )(x2d, lp["ln2_g"], lp["w_ff1"], lp["w_ff2"])


def _s5_matrices(p, chunk):
    hp = dict(precision=HIGHEST)
    dt = jnp.exp(p["s5_log_dt"])[:, None]
    ar, ai = p["s5_a_re"], p["s5_a_im"]
    mag = jnp.exp(ar * dt)
    abar_re = mag * jnp.cos(ai * dt)
    abar_im = mag * jnp.sin(ai * dt)
    den = ar * ar + ai * ai
    nr = abar_re - 1.0
    f_re = (nr * ar + abar_im * ai) / den
    f_im = (abar_im * ar - nr * ai) / den
    br, bi = p["s5_b_re"], p["s5_b_im"]
    bbar_re = f_re[..., None] * br - f_im[..., None] * bi
    bbar_im = f_re[..., None] * bi + f_im[..., None] * br
    pw_re = [jnp.ones_like(abar_re)]
    pw_im = [jnp.zeros_like(abar_re)]
    for _ in range(chunk):
        r, i = pw_re[-1], pw_im[-1]
        pw_re.append(r * abar_re - i * abar_im)
        pw_im.append(r * abar_im + i * abar_re)
    pw_re = jnp.stack(pw_re)
    pw_im = jnp.stack(pw_im)
    cr, ci = p["s5_c_re"], p["s5_c_im"]
    ca_re = cr[None] * pw_re[:, :, None, :] - ci[None] * pw_im[:, :, None, :]
    ca_im = cr[None] * pw_im[:, :, None, :] + ci[None] * pw_re[:, :, None, :]
    def table(t):
        lead = t.shape[:-3]
        t = jnp.moveaxis(t, -1, -3)
        return t.reshape(lead + (t.shape[-3], -1))

    kern = (jnp.einsum("tgop,gpi->tgoi", ca_re[:chunk], bbar_re, **hp)
            - jnp.einsum("tgop,gpi->tgoi", ca_im[:chunk], bbar_im, **hp))
    rev_re = pw_re[:chunk][::-1]
    rev_im = pw_im[:chunk][::-1]
    ab_re = rev_re[..., None] * bbar_re[None] - rev_im[..., None] * bbar_im[None]
    ab_im = rev_re[..., None] * bbar_im[None] + rev_im[..., None] * bbar_re[None]
    a_pow = jnp.stack([pw_re[chunk].reshape(-1), pw_im[chunk].reshape(-1)])
    halves = lambda t, n: jnp.stack([t[..., :n], t[..., n:]])
    xs_re, xs_im = halves(table(ab_re), S5_HALF_STATE), halves(table(ab_im), S5_HALF_STATE)
    hy = jnp.stack([table(ca_re[1:]), -table(ca_im[1:])], axis=1)
    return dict(lag=halves(table(kern), S5_HALF),
                xs=jnp.concatenate([xs_re, xs_im], axis=-1),
                hy=halves(hy, S5_HALF),
                a_pow=a_pow, d_row=p["s5_d"][None, :])


S5_HALF = BRANCH_WIDTH // 2
S5_HALF_GROUPS = S5_GROUPS // 2
S5_HALF_STATE = S5_STATE // 2


def _s5_expand(lag_ref, xs_ref, hy_ref, wxs_scr, win_scr, why_scr):
    chunk = lag_ref.shape[1]
    w = S5_HALF

    def group_of(shape, axis, per_group, wrap=None):
        idx = lax.broadcasted_iota(jnp.int32, shape, axis)
        if wrap is not None:
            idx = idx % wrap
        return idx // per_group

    same_ii = group_of((w, w), 0, S5_GROUP) == group_of((w, w), 1, S5_GROUP)
    same_is = (group_of((w, 2 * S5_HALF_STATE), 0, S5_GROUP)
               == group_of((w, 2 * S5_HALF_STATE), 1, S5_P, wrap=S5_HALF_STATE))
    same_si = group_of((S5_HALF_STATE, w), 0, S5_P) == group_of((S5_HALF_STATE, w), 1, S5_GROUP)
    down = lambda t: jnp.concatenate([t] * S5_HALF_GROUPS, axis=0)
    zero = jnp.zeros((w, w), BF16)
    for hf in range(2):
        lags = [jnp.where(same_ii, down(lag_ref[hf, tau]), 0.0).astype(BF16) for tau in range(chunk)]
        for s in range(chunk):
            wxs_scr[hf, s * w:(s + 1) * w, :] = jnp.where(same_is, down(xs_ref[hf, s]), 0.0).astype(BF16)
            win_scr[hf, s * w:(s + 1) * w, :] = jnp.concatenate(
                [lags[t - s] if t >= s else zero for t in range(chunk)], axis=1)
            for part in range(2):
                why_scr[hf, part * S5_HALF_STATE:(part + 1) * S5_HALF_STATE, s * w:(s + 1) * w] = jnp.where(
                    same_si, down(hy_ref[hf, s, part]), 0.0).astype(BF16)


def _s5_state_increment(u_halves, wxs_ref):
    lo = _dot(u_halves[0], wxs_ref[0])
    hi = _dot(u_halves[1], wxs_ref[1])
    n = S5_HALF_STATE
    return jnp.concatenate([lo[:, :n], hi[:, :n], lo[:, n:], hi[:, n:]], axis=1)


def _s5_outputs(u_halves, h_bf16, win_ref, why_ref):
    n = S5_HALF_STATE
    outs = []
    for hf in range(2):
        h_half = jnp.concatenate([h_bf16[:, hf * n:(hf + 1) * n],
                                  h_bf16[:, S5_STATE + hf * n:S5_STATE + (hf + 1) * n]], axis=1)
        outs.append(_dot(u_halves[hf], win_ref[hf]) + _dot(h_half, why_ref[hf]))
    return outs


def _gelu(y):
    c = math.sqrt(2.0 / math.pi)
    return 0.5 * y * (1.0 + jnp.tanh(c * (y + 0.044715 * (y * y * y))))


def _s5_kernel(u_lo_ref, u_hi_ref, lag_ref, xs_ref, hy_ref, apow_ref, d_ref, y_lo_ref, y_hi_ref, hfin_ref,
               wxs_scr, win_scr, why_scr, xs_scr, hs_scr):
    chunk = lag_ref.shape[1]
    rows = u_lo_ref.shape[0] // chunk

    @pl.when(pl.program_id(0) == 0)
    def _():
        _s5_expand(lag_ref, xs_ref, hy_ref, wxs_scr, win_scr, why_scr)

    us = [jnp.concatenate([ref[pl.ds(s, rows, stride=chunk), :] for s in range(chunk)], axis=1)
          for ref in (u_lo_ref, u_hi_ref)]
    ubs = [u.astype(BF16) for u in us]
    xs_scr[...] = _s5_state_increment(ubs, wxs_scr)
    a_re = apow_ref[0:1, :]
    a_im = apow_ref[1:2, :]

    def step(r, carry):
        hr, hi = carry
        hs_scr[pl.ds(r, 1), :] = jnp.concatenate([hr, hi], axis=-1)
        x = xs_scr[pl.ds(r, 1), :]
        nhr = a_re * hr - a_im * hi + x[:, :S5_STATE]
        nhi = a_re * hi + a_im * hr + x[:, S5_STATE:]
        return nhr, nhi

    zero = jnp.zeros((1, S5_STATE), F32)
    hr, hi = lax.fori_loop(0, rows, step, (zero, zero))
    hfin_ref[0] = jnp.concatenate([hr, hi], axis=-1)
    ys = _s5_outputs(ubs, hs_scr[...].astype(BF16), win_scr, why_scr)
    for hf, y_ref in enumerate((y_lo_ref, y_hi_ref)):
        d = d_ref[:, hf * S5_HALF:(hf + 1) * S5_HALF]
        for t in range(chunk):
            cols = slice(t * S5_HALF, (t + 1) * S5_HALF)
            y_ref[pl.ds(t, rows, stride=chunk), :] = _gelu(ys[hf][:, cols] + d * us[hf][:, cols])


def _s5_scratch(chunk):
    width = chunk * S5_HALF
    return [pltpu.VMEM((2, width, 2 * S5_HALF_STATE), BF16), pltpu.VMEM((2, width, width), BF16),
            pltpu.VMEM((2, 2 * S5_HALF_STATE, width), BF16)]


def _s5_prompt(u_lo, u_hi, sm, batch, seq):
    c = sm["lag"].shape[1]
    rows = seq // c
    half = BRANCH_WIDTH // 2
    tok = pl.BlockSpec((seq, half), lambda b: (b, 0))
    y_lo, y_hi, hfin = pl.pallas_call(
        _s5_kernel,
        grid=(batch,),
        in_specs=[tok, tok, _const_spec(sm["lag"].shape), _const_spec(sm["xs"].shape),
                  _const_spec(sm["hy"].shape), _const_spec((2, S5_STATE)), _const_spec((1, BRANCH_WIDTH))],
        out_specs=[tok, tok, pl.BlockSpec((1, 1, 2 * S5_STATE), lambda b: (b, 0, 0))],
        out_shape=[jax.ShapeDtypeStruct((batch * seq, half), F32),
                   jax.ShapeDtypeStruct((batch * seq, half), F32),
                   jax.ShapeDtypeStruct((batch, 1, 2 * S5_STATE), F32)],
        scratch_shapes=_s5_scratch(c) + [pltpu.VMEM((rows, 2 * S5_STATE), F32),
                                         pltpu.VMEM((rows, 2 * S5_STATE), F32)],
        compiler_params=_cparams("arbitrary"),
        name="s5_prompt",
    )(u_lo, u_hi, sm["lag"], sm["xs"], sm["hy"], sm["a_pow"], sm["d_row"])
    return (y_lo, y_hi), hfin.reshape(batch, 2 * S5_STATE)


GLA_ROWS = 256


def _gla_kernel(q_ref, k_ref, v_ref, g_ref, r_ref, tri_ref, ones_ref, bmask_ref, gn_ref,
                y_ref, sfin_ref, st_scr):
    step = pl.program_id(1)
    sub = GLA_SUB
    rows = q_ref.shape[0]
    ns = rows // sub

    @pl.when(step == 0)
    def _():
        st_scr[...] = jnp.zeros_like(st_scr)

    ones = ones_ref[...]
    bmask = bmask_ref[...]
    split = lambda a: a.reshape(ns, sub, BRANCH_WIDTH)
    bc = split(_sel_dot_exact(tri_ref[...], g_ref[...]))
    q = split(q_ref[...] * (HEAD_DIM ** -0.5))
    k = split(k_ref[...])
    v = split(v_ref[...])

    row = lax.broadcasted_iota(jnp.int32, (ns, sub, BRANCH_WIDTH), 1)
    parts = []
    for j in range(sub):
        e = jnp.exp(jnp.minimum(bc - bc[:, j:j + 1, :], 0.0))
        parts.append(jnp.where(row >= j, q * k[:, j:j + 1, :] * e, 0.0))
    att = _dot(jnp.concatenate(parts, axis=1).reshape(ns * sub * sub, BRANCH_WIDTH).astype(BF16), ones)
    att = att.reshape(ns, sub * sub, BRANCH_WIDTH)
    o = att[:, 0:sub] * v[:, 0:1, :]
    for j in range(1, sub):
        o = o + att[:, j * sub:(j + 1) * sub] * v[:, j:j + 1, :]

    last = bc[:, sub - 1:sub, :]
    qt = (q * jnp.exp(bc)).astype(BF16)
    kt = (k * jnp.exp(last - bc)).astype(BF16)
    vb = v.astype(BF16)
    decay = jnp.exp(last)
    outer = [bmask * _dot_tn(vb[s], kt[s]) for s in range(ns)]
    st = st_scr[...]
    inter = []
    for s in range(ns):
        inter.append(_dot_nt(qt[s], st.astype(BF16)))
        st = st * decay[s] + outer[s]
    st_scr[...] = st
    o = o.reshape(rows, BRANCH_WIDTH) + jnp.concatenate(inter, axis=0)
    ms = _dot_sel(o * o, ones) * (1.0 / HEAD_DIM)
    y_ref[...] = o * lax.rsqrt(ms + EPS) * gn_ref[...] * _silu(r_ref[...])

    @pl.when(step == pl.num_programs(1) - 1)
    def _():
        sfin_ref[0] = st


def _sub_tril(rows, sub):
    i = np.arange(rows)
    return jnp.asarray((i[:, None] // sub == i[None, :] // sub) & (i[None, :] <= i[:, None]), BF16)


def _sel_dot_exact(sel, x):
    x1 = x.astype(BF16)
    r1 = x - x1.astype(F32)
    x2 = r1.astype(BF16)
    x3 = (r1 - x2.astype(F32)).astype(BF16)
    return _dot(sel, x1) + _dot(sel, x2) + _dot(sel, x3)


def _unpack_state_t(st):
    b = st.shape[0]
    st = st.reshape(b, HEADS, HEAD_DIM, HEADS, HEAD_DIM)
    diag = jnp.stack([st[:, h, :, h, :] for h in range(HEADS)], axis=1)
    return diag.transpose(0, 1, 3, 2)


def _gla_prompt(q, k, v, g, r, lp, batch, seq):
    rows = min(GLA_ROWS, seq)
    nsteps = seq // rows
    blk = pl.BlockSpec((rows, BRANCH_WIDTH), lambda b, c: (b * nsteps + c, 0))
    hm = np.arange(BRANCH_WIDTH) // HEAD_DIM
    bmask = jnp.asarray(hm[:, None] == hm[None, :], F32)
    y, sfin = pl.pallas_call(
        _gla_kernel,
        grid=(batch, nsteps),
        in_specs=[blk, blk, blk, blk, blk, _const_spec((rows, rows)),
                  _const_spec((BRANCH_WIDTH, BRANCH_WIDTH)), _const_spec((BRANCH_WIDTH, BRANCH_WIDTH)),
                  _const_spec((1, BRANCH_WIDTH))],
        out_specs=[blk, pl.BlockSpec((1, BRANCH_WIDTH, BRANCH_WIDTH), lambda b, c: (b, 0, 0))],
        out_shape=[jax.ShapeDtypeStruct((batch * seq, BRANCH_WIDTH), F32),
                   jax.ShapeDtypeStruct((batch, BRANCH_WIDTH, BRANCH_WIDTH), F32)],
        scratch_shapes=[pltpu.VMEM((BRANCH_WIDTH, BRANCH_WIDTH), F32)],
        compiler_params=_cparams("parallel", "arbitrary"),
        name="gla_prompt",
    )(q, k, v, g, r, _sub_tril(rows, GLA_SUB), _head_ones(), bmask, lp["gla_norm"])
    return y, _unpack_state_t(sfin)


CONV_PAD = 8


GDN_ROWS = 512


def _block_diag(x, ones):
    return jnp.concatenate([x] * HEADS, axis=0) * ones


def _unit_lower_inverses(ns, eye, ones):
    c = ns[0].shape[0]
    every = range(len(ns))
    invs = [eye - n for n in ns]
    pws = list(ns)
    for _ in range(int(math.log2(c)) - 1):
        pbs = [pw.astype(BF16) for pw in pws]
        pws = [_dot(pb, _block_diag(pb, ones)) for pb in pbs]
        invs = [invs[i] + _dot(invs[i].astype(BF16), _block_diag(pws[i].astype(BF16), ones)) for i in every]
    inv_parts = [_split2(inv) for inv in invs]
    n_parts = [_split2(n) for n in ns]
    prods = [_dot(jnp.concatenate(n_parts[i], axis=0), _block_diag(inv_parts[i][0], ones)) for i in every]
    cross = [_dot(n_parts[i][0], _block_diag(inv_parts[i][1], ones)) for i in every]
    resids = [eye - invs[i] - (prods[i][:c] + prods[i][c:] + cross[i]) for i in every]
    return [invs[i] + _dot(inv_parts[i][0], _block_diag(resids[i].astype(BF16), ones)) for i in every]


def _gdn_kernel(x_ref, b_ref, g_ref, z_ref, cw_ref, tri_ref, ones_ref, eye_ref, gn_ref,
                y_ref, sfin_ref, conv_ref, s_scr, buf_scr):
    step = pl.program_id(1)
    rows = x_ref.shape[0]
    c = min(GDN_CHUNK, rows)

    @pl.when(step == 0)
    def _():
        s_scr[...] = jnp.zeros_like(s_scr)
        buf_scr[0:CONV_PAD, :] = jnp.zeros((CONV_PAD, GDN_QKV), F32)

    x = x_ref[...]
    buf_scr[CONV_PAD:CONV_PAD + rows, :] = x
    conv = cw_ref[GDN_CONV - 1:GDN_CONV, :] * x
    for w in range(GDN_CONV - 1):
        lag = GDN_CONV - 1 - w
        conv = conv + cw_ref[w:w + 1, :] * buf_scr[CONV_PAD - lag:CONV_PAD - lag + rows, :]
    tail = buf_scr[rows:rows + CONV_PAD, :]
    buf_scr[0:CONV_PAD, :] = tail
    conv_ref[0] = tail
    qkv = _silu(conv)
    ones = ones_ref[...]
    onesf = ones.astype(F32)
    eye = eye_ref[...]
    q = qkv[:, 0:BRANCH_WIDTH]
    k = qkv[:, BRANCH_WIDTH:2 * BRANCH_WIDTH]
    v = qkv[:, 2 * BRANCH_WIDTH:]
    q = q * lax.rsqrt(_dot_sel(q * q, ones) + EPS) * (HEAD_DIM ** -0.5)
    k = k * lax.rsqrt(_dot_sel(k * k, ones) + EPS)
    beta = b_ref[...]
    gc = _sel_dot_exact(tri_ref[...], g_ref[...])
    gam = jnp.exp(gc)
    ri = lax.broadcasted_iota(jnp.int32, (c, BRANCH_WIDTH), 0)
    cj = lax.broadcasted_iota(jnp.int32, (c, BRANCH_WIDTH), 1) % HEAD_DIM

    every = range(rows // c)
    sls = [slice(n * c, (n + 1) * c) for n in every]
    kcs = [k[rs].astype(BF16) for rs in sls]
    kqs = [_dot_nt(jnp.concatenate([k[rs], q[rs]], axis=0).astype(BF16), _block_diag(kcs[n], ones))
           for n, rs in enumerate(sls)]
    decs = []
    for rs in sls:
        grow = jnp.sum(gc[rs] * eye, axis=0, keepdims=True)
        decs.append(jnp.where(ri >= cj, jnp.exp(jnp.minimum(gc[rs] - grow, 0.0)), 0.0))
    invs = _unit_lower_inverses(
        [jnp.where(ri > cj, beta[rs] * decs[n] * kqs[n][:c], 0.0) for n, rs in enumerate(sls)], eye, ones)
    invbs = [inv.astype(BF16) for inv in invs]
    ws_m = [_dot(invbs[n], _block_diag((beta[rs] * gam[rs] * k[rs]).astype(BF16), ones))
            for n, rs in enumerate(sls)]
    u0s = [_dot(invbs[n], _block_diag((beta[rs] * v[rs]).astype(BF16), ones)) for n, rs in enumerate(sls)]
    aqks = [(decs[n] * kqs[n][c:]).astype(BF16) for n in every]
    lhs = [jnp.concatenate([ws_m[n], gam[rs] * q[rs]], axis=0).astype(BF16) for n, rs in enumerate(sls)]
    glasts = [gc[rs][c - 1:c, :] for rs in sls]
    kds = [(k[rs] * jnp.exp(glasts[n] - gc[rs])).astype(BF16) for n, rs in enumerate(sls)]

    s = s_scr[...]
    outs = []
    for n in every:
        ws = _dot(lhs[n], s.astype(BF16))
        ub = (u0s[n] - ws[:c]).astype(BF16)
        outs.append(ws[c:] + _dot(aqks[n], _block_diag(ub, ones)))
        s = jnp.exp(glasts[n]) * s + onesf * _dot_tn(kds[n], ub)
    s_scr[...] = s
    o = jnp.concatenate(outs, axis=0)
    ms = _dot_sel(o * o, ones) * (1.0 / HEAD_DIM)
    y_ref[...] = o * lax.rsqrt(ms + EPS) * gn_ref[...] * _silu(z_ref[...])

    @pl.when(step == pl.num_programs(1) - 1)
    def _():
        sfin_ref[0] = s


def _unpack_state(st):
    b = st.shape[0]
    st = st.reshape(b, HEADS, HEAD_DIM, HEADS, HEAD_DIM)
    return jnp.stack([st[:, h, :, h, :] for h in range(HEADS)], axis=1)


def _gdn_prompt(x, beta, g, z, lp, batch, seq):
    rows = min(GDN_ROWS, seq)
    c = min(GDN_CHUNK, rows)
    assert c == HEAD_DIM and seq % rows == 0 and rows % c == 0
    nsteps = seq // rows
    blk = lambda w: pl.BlockSpec((rows, w), lambda b, s: (b * nsteps + s, 0))
    eye =jnp.asarray(np.tile(np.eye(c, dtype=np.float32), (1, HEADS)))
    y, sfin, conv = pl.pallas_call(
        _gdn_kernel,
        grid=(batch, nsteps),
        in_specs=[blk(GDN_QKV), blk(256), blk(256), blk(256), _const_spec((CONV_PAD, GDN_QKV)),
                  _const_spec((rows, rows)), _const_spec((256, 256)), _const_spec((c, 256)),
                  _const_spec((1, 256))],
        out_specs=[blk(256),
                   pl.BlockSpec((1, 256, 256), lambda b, s: (b, 0, 0)),
                   pl.BlockSpec((1, CONV_PAD, GDN_QKV), lambda b, s: (b, 0, 0))],
        out_shape=[jax.ShapeDtypeStruct((batch * seq, 256), F32),
                   jax.ShapeDtypeStruct((batch, 256, 256), F32),
                   jax.ShapeDtypeStruct((batch, CONV_PAD, GDN_QKV), F32)],
        scratch_shapes=[pltpu.VMEM((256, 256), F32),
                        pltpu.VMEM((CONV_PAD + rows, GDN_QKV), F32)],
        compiler_params=_cparams("parallel", "arbitrary"),
        name="gdn_prompt",
    )(x, beta, g, z, lp["gdn_conv_w"], _sub_tril(rows, c), _head_ones(), eye, lp["gdn_norm"])
    return y, _unpack_state(sfin), conv[:, CONV_PAD - (GDN_CONV - 1):, :]


HALF_ROT = ROT_DIM // 2


def _rope_tables(pos):
    inv = ROPE_THETA ** (-jnp.arange(HALF_ROT, dtype=F32) / HALF_ROT)
    ang = pos.astype(F32)[:, None] * inv[None, :]
    cos, sin = jnp.cos(ang), jnp.sin(ang)
    n = pos.shape[0]
    rest = HEAD_DIM - ROT_DIM
    head = lambda a, b, fill: jnp.concatenate([a, b, jnp.full((n, rest), fill, F32)], axis=1)
    zero = jnp.zeros_like(sin)
    tabs = [head(cos, cos, 1.0), head(-sin, zero, 0.0), head(zero, sin, 0.0)]
    return jnp.stack([jnp.tile(t, (1, HEADS)) for t in tabs])


def _qk_norm_rope(x, gain, tab_ref, ones):
    y = x * lax.rsqrt(_dot_sel(x * x, ones) * (1.0 / HEAD_DIM) + EPS) * gain
    up = pltpu.roll(y, BRANCH_WIDTH - HALF_ROT, 1)
    down = pltpu.roll(y, HALF_ROT, 1)
    return y * tab_ref[0] + up * tab_ref[1] + down * tab_ref[2]


K_AUG = 2 * HEAD_DIM
V_AUG = HEAD_DIM + 16


def _moba_prep_kernel(q_ref, k_ref, v_ref, tab_ref, qg_ref, kg_ref, ones_ref,
                      kt_ref, vtt_ref, qt_ref, kh_ref, vt_ref, km_ref):
    ones = ones_ref[...]
    rows = q_ref.shape[0]
    nblk = rows // MOBA_BLOCK
    mq = _qk_norm_rope(q_ref[...], qg_ref[...], tab_ref, ones) * (HEAD_DIM ** -0.5)
    mk = _qk_norm_rope(k_ref[...], kg_ref[...], tab_ref, ones)
    kmean = jnp.mean(mk.reshape(nblk, MOBA_BLOCK, BRANCH_WIDTH), axis=1)
    qt = mq.T
    kt = mk.T
    vt = v_ref[...].T
    one_col = (lax.broadcasted_iota(jnp.int32, (MOBA_BLOCK, K_AUG - HEAD_DIM), 1) == 0).astype(BF16)
    one_row = (lax.broadcasted_iota(jnp.int32, (V_AUG - HEAD_DIM, MOBA_BLOCK), 0) == 0).astype(BF16)
    for h in range(HEADS):
        sl = slice(h * HEAD_DIM, (h + 1) * HEAD_DIM)
        qt_ref[0, h] = qt[sl, :]
        km_ref[0, h] = kmean[:, sl]
        kt_ref[0, h] = kt[sl, :]
        vtt_ref[0, h] = vt[sl, :]
        for j in range(nblk):
            rs = slice(j * MOBA_BLOCK, (j + 1) * MOBA_BLOCK)
            kh_ref[0, h, j] = jnp.concatenate([mk[rs, sl].astype(BF16), one_col], axis=1)
            vt_ref[0, h, j] = jnp.concatenate([vt[sl, rs].astype(BF16), one_row], axis=0)


def _moba_prep(q, k, v, tabs, lp, batch, seq):
    rows = min(8 * MOBA_BLOCK, seq)
    nsteps = seq // rows
    nblk = rows // MOBA_BLOCK
    nb = seq // MOBA_BLOCK
    blk = pl.BlockSpec((rows, 256), lambda b, r: (b * nsteps + r, 0))
    cache_rows = pl.BlockSpec((1, HEADS, HEAD_DIM, rows), lambda b, r: (b, 0, 0, r))
    return pl.pallas_call(
        _moba_prep_kernel,
        grid=(batch, nsteps),
        in_specs=[blk, blk, blk, pl.BlockSpec((3, rows, 256), lambda b, r: (0, r, 0)),
                  _const_spec((1, 256)), _const_spec((1, 256)), _const_spec((256, 256))],
        out_specs=[cache_rows, cache_rows, cache_rows,
                   pl.BlockSpec((1, HEADS, nblk, MOBA_BLOCK, K_AUG), lambda b, r: (b, 0, r, 0, 0)),
                   pl.BlockSpec((1, HEADS, nblk, V_AUG, MOBA_BLOCK), lambda b, r: (b, 0, r, 0, 0)),
                   pl.BlockSpec((1, HEADS, nblk, HEAD_DIM), lambda b, r: (b, 0, r, 0))],
        out_shape=[jax.ShapeDtypeStruct((batch, HEADS, HEAD_DIM, seq), F32),
                   jax.ShapeDtypeStruct((batch, HEADS, HEAD_DIM, seq), F32),
                   jax.ShapeDtypeStruct((batch, HEADS, HEAD_DIM, seq), F32),
                   jax.ShapeDtypeStruct((batch, HEADS, nb, MOBA_BLOCK, K_AUG), BF16),
                   jax.ShapeDtypeStruct((batch, HEADS, nb, V_AUG, MOBA_BLOCK), BF16),
                   jax.ShapeDtypeStruct((batch, HEADS, nb, HEAD_DIM), F32)],
        compiler_params=_cparams("parallel", "parallel"),
        name="moba_prep",
    )(q, k, v, tabs, lp["moba_q_norm"], lp["moba_k_norm"], _head_ones())


def _moba_attn_kernel(qt_ref, kh_ref, vt_ref, km_ref, o_ref, bias_scr):
    qb = pl.program_id(1)
    nb = km_ref.shape[2]
    blk = MOBA_BLOCK
    heads = range(HEADS)
    blk_id = lax.broadcasted_iota(jnp.int32, (nb, blk), 0)
    kpos = lax.broadcasted_iota(jnp.int32, (blk, blk), 0)
    qpos = lax.broadcasted_iota(jnp.int32, (blk, blk), 1)
    first_row = lax.broadcasted_iota(jnp.int32, (K_AUG - HEAD_DIM, blk), 0) == 0
    qts = [qt_ref[0, h] for h in heads]
    qtb = [qt.astype(BF16) for qt in qts]

    def scores(j, biases):
        out = []
        for h in heads:
            extra = jnp.where(first_row, biases[h], 0.0).astype(BF16)
            out.append(_dot(kh_ref[0, h, j], jnp.concatenate([qtb[h], extra], axis=0)))
        return out

    gates = [jnp.dot(km_ref[0, h], qts[h], precision=HIGHEST, preferred_element_type=F32) for h in heads]
    own = scores(qb, [jnp.zeros((1, blk), F32)] * HEADS)
    for h in heads:
        gate = jnp.where(blk_id < qb, gates[h], NEG)
        taken = jnp.zeros((nb, blk), jnp.bool_)
        for _ in range(min(MOBA_TOPK, nb)):
            best = jnp.max(gate, axis=0, keepdims=True)
            idx = jnp.min(jnp.where(gate == best, blk_id, nb), axis=0, keepdims=True)
            hit = blk_id == idx
            taken = jnp.logical_or(taken, hit)
            gate = jnp.where(hit, -jnp.inf, gate)
        bias_scr[h] = jnp.where(jnp.logical_and(taken, blk_id < qb), 0.0, NEG)

    ms, ps = [], []
    for h in heads:
        s = jnp.where(kpos <= qpos, own[h], NEG)
        m = jnp.max(s, axis=0, keepdims=True)
        ms.append(m)
        ps.append(jnp.exp(s - m).astype(BF16))
    accs = [_dot(vt_ref[0, h, qb], ps[h]) for h in heads]

    def bias_rows(j):
        return [bias_scr[h, pl.ds(j, 1), :] for h in heads]

    def update(carry, blocks):
        ms, accs = carry
        ss = [scores(j, bias_rows(j)) for j in blocks]
        new_m, new_acc = [], []
        for h in heads:
            m_new = ms[h]
            for s in ss:
                m_new = jnp.maximum(m_new, jnp.max(s[h], axis=0, keepdims=True))
            acc = jnp.exp(ms[h] - m_new) * accs[h]
            for j, s in zip(blocks, ss):
                acc = acc + _dot(vt_ref[0, h, j], jnp.exp(s[h] - m_new).astype(BF16))
            new_m.append(m_new)
            new_acc.append(acc)
        return tuple(new_m), tuple(new_acc)

    carry = (tuple(ms), tuple(accs))
    start = 0
    for width in (4, 2, 1):
        trips = (qb - start) // width
        carry = lax.fori_loop(
            0, trips, lambda i, c, w=width, s=start: update(c, tuple(s + w * i + e for e in range(w))), carry)
        start = start + trips * width
    ms, accs = carry
    outs = [accs[h][:HEAD_DIM] / accs[h][HEAD_DIM:HEAD_DIM + 1] for h in heads]
    o_ref[...] = jnp.concatenate(outs, axis=0).T


def _moba_prompt(qt, kh, vt, km, batch, seq):
    nb = seq // MOBA_BLOCK
    return pl.pallas_call(
        _moba_attn_kernel,
        grid=(batch, nb),
        in_specs=[pl.BlockSpec((1, HEADS, HEAD_DIM, MOBA_BLOCK), lambda b, i: (b, 0, 0, i)),
                  pl.BlockSpec((1, HEADS, nb, MOBA_BLOCK, K_AUG), lambda b, i: (b, 0, 0, 0, 0)),
                  pl.BlockSpec((1, HEADS, nb, V_AUG, MOBA_BLOCK), lambda b, i: (b, 0, 0, 0, 0)),
                  pl.BlockSpec((1, HEADS, nb, HEAD_DIM), lambda b, i: (b, 0, 0, 0))],
        out_specs=pl.BlockSpec((MOBA_BLOCK, 256), lambda b, i: (b * nb + i, 0)),
        out_shape=jax.ShapeDtypeStruct((batch * seq, 256), F32),
        scratch_shapes=[pltpu.VMEM((HEADS, nb, MOBA_BLOCK), F32)],
        compiler_params=_cparams("parallel", "arbitrary"),
        name="moba_prompt",
    )(qt, kh, vt, km)


def _s5_step_kernel(u_ref, h0_ref, lag_ref, xs_ref, hy_ref, apow_ref, d_ref, y_ref, h_ref,
                    wxs_ref, win_ref, why_ref):
    _s5_expand(lag_ref, xs_ref, hy_ref, wxs_ref, win_ref, why_ref)
    u = u_ref[...]
    ubs = [u[:, :S5_HALF].astype(BF16), u[:, S5_HALF:].astype(BF16)]
    h0 = h0_ref[...]
    xs = _s5_state_increment(ubs, wxs_ref)
    a_re = apow_ref[0:1, :]
    a_im = apow_ref[1:2, :]
    hr0 = h0[:, :S5_STATE]
    hi0 = h0[:, S5_STATE:]
    hr = a_re * hr0 - a_im * hi0 + xs[:, :S5_STATE]
    hi = a_re * hi0 + a_im * hr0 + xs[:, S5_STATE:]
    h_ref[...] = jnp.concatenate([hr, hi], axis=-1)
    y = jnp.concatenate(_s5_outputs(ubs, h0.astype(BF16), win_ref, why_ref), axis=1)
    y_ref[...] = _gelu(y + d_ref[...] * u)


def _s5_step(u, h0, sm):
    n = u.shape[0]
    return pl.pallas_call(
        _s5_step_kernel,
        out_shape=[jax.ShapeDtypeStruct((n, BRANCH_WIDTH), F32),
                   jax.ShapeDtypeStruct((n, 2 * S5_STATE), F32)],
        scratch_shapes=_s5_scratch(1),
        compiler_params=pltpu.CompilerParams(vmem_limit_bytes=VMEM_LIMIT),
        name="s5_step",
    )(u, h0, sm["lag"], sm["xs"], sm["hy"], sm["a_pow"], sm["d_row"])


def _expand_mats():
    idx = np.arange(HEAD_DIM * HEAD_DIM)
    rep = (np.arange(HEAD_DIM)[:, None] == idx[None, :] // HEAD_DIM)
    til = (np.arange(HEAD_DIM)[:, None] == idx[None, :] % HEAD_DIM)
    return jnp.asarray(rep, BF16), jnp.asarray(til, BF16), jnp.asarray(til.T, BF16)


def _gla_step_kernel(q_ref, k_ref, v_ref, g_ref, r_ref, s0_ref, rep_ref, til_ref, tilt_ref, gn_ref,
                     y_ref, s_ref):
    rep = rep_ref[...]
    eg = _dot_sel_exact(jnp.exp(g_ref[...]), rep)
    kr = _dot_sel_exact(k_ref[...], rep)
    qr = _dot_sel_exact(q_ref[...] * (HEAD_DIM ** -0.5), rep)
    vt = _dot_sel_exact(v_ref[...], til_ref[...])
    s = eg * s0_ref[...] + kr * vt
    s_ref[...] = s
    o = _dot_sel(qr * s, tilt_ref[...])
    y_ref[...] = _rms_rows(o, gn_ref[...]) * _silu(r_ref[...])


def _gdn_conv_step_kernel(x_ref, c0_ref, cw_ref, qkv_ref, cnew_ref):
    x = x_ref[...]
    c0 = c0_ref[...]
    conv = cw_ref[GDN_CONV - 1:GDN_CONV, :] * x
    for w in range(GDN_CONV - 1):
        conv = conv + cw_ref[w:w + 1, :] * c0[:, w * GDN_QKV:(w + 1) * GDN_QKV]
    qkv_ref[...] = _silu(conv)
    cnew_ref[...] = jnp.concatenate([c0[:, GDN_QKV:], x], axis=-1)


def _gdn_step_kernel(q_ref, k_ref, v_ref, b_ref, g_ref, z_ref, s0_ref, rep_ref, til_ref, tilt_ref,
                     gn_ref, y_ref, s_ref):
    q = q_ref[...]
    k = k_ref[...]
    q = q * lax.rsqrt(jnp.sum(q * q, axis=-1, keepdims=True) + EPS) * (HEAD_DIM ** -0.5)
    k = k * lax.rsqrt(jnp.sum(k * k, axis=-1, keepdims=True) + EPS)
    beta = b_ref[:, 0:1]
    gam = jnp.exp(g_ref[:, 0:1])
    rep = rep_ref[...]
    tilt = tilt_ref[...]
    kr = _dot_sel_exact(k, rep)
    qr = _dot_sel_exact(q, rep)
    s0 = s0_ref[...]
    ks = _dot_sel(kr * s0, tilt)
    qs = _dot_sel(qr * s0, tilt)
    u = v_ref[...] - gam * ks
    qk = jnp.sum(q * k, axis=-1, keepdims=True)
    o = gam * qs + (beta * qk) * u
    s_ref[...] = gam * s0 + kr * _dot_sel_exact(beta * u, til_ref[...])
    y_ref[...] = _rms_rows(o, gn_ref[...]) * _silu(z_ref[...])


def _whole_call(kernel, out_shape, name, *args):
    return pl.pallas_call(kernel, out_shape=out_shape, name=name,
                          compiler_params=pltpu.CompilerParams(vmem_limit_bytes=VMEM_LIMIT))(*args)


def _moba_qk_step_kernel(q_ref, k_ref, tab_ref, qg_ref, kg_ref, ones_ref, mq_ref, mk_ref):
    ones = ones_ref[...]
    mq_ref[...] = _qk_norm_rope(q_ref[...], qg_ref[...], tab_ref, ones) * (HEAD_DIM ** -0.5)
    mk_ref[...] = _qk_norm_rope(k_ref[...], kg_ref[...], tab_ref, ones)


SELECT_PAGES = 64
SEL_ROWS = 8


def _moba_select_kernel(pt_ref, q_ref, *refs):
    pages, sel_ref, gate_scr = refs[:-2], refs[-2], refs[-1]
    per_step = len(pages)
    g = pl.program_id(1)
    ppb = MOBA_BLOCK // PAGE_SIZE
    qcol = jnp.broadcast_to(q_ref[0], (BRANCH_WIDTH, LANES))
    token_ones = jnp.ones((PAGE_SIZE, LANES), BF16)
    blk_lane = lax.broadcasted_iota(jnp.int32, (HEADS, LANES), 1)

    @pl.when(g == 0)
    def _():
        gate_scr[...] = jnp.full((HEADS, LANES), NEG, F32)

    gate = gate_scr[...]
    for n in range(per_step // ppb):
        tile = pages[n * ppb][0, 0]
        for e in range(1, ppb):
            tile = tile + pages[n * ppb + e][0, 0]
        ksum = _dot(tile.reshape(BRANCH_WIDTH, PAGE_SIZE).astype(BF16), token_ones)
        mean = jnp.sum((ksum * qcol).reshape(HEADS, HEAD_DIM, LANES), axis=1) * (1.0 / MOBA_BLOCK)
        gate = jnp.where(blk_lane == g * (per_step // ppb) + n, mean, gate)
    gate_scr[...] = gate

    @pl.when(g == pl.num_programs(1) - 1)
    def _():
        left = gate
        sel = jnp.zeros((HEADS, LANES), jnp.int32)
        for r in range(MOBA_TOPK):
            best = jnp.max(left, axis=-1, keepdims=True)
            idx = jnp.min(jnp.where(left == best, blk_lane, LANES), axis=-1, keepdims=True)
            sel = jnp.where(blk_lane == r, idx, sel)
            left = jnp.where(blk_lane == idx, -jnp.inf, left)
        sel_ref[0] = sel


def _moba_select(mq, page_table, cache_kt, layer):
    nseq, npages = page_table.shape
    per_step = min(SELECT_PAGES, npages)
    assert npages % per_step == 0 and per_step % (MOBA_BLOCK // PAGE_SIZE) == 0

    def page_spec(p):
        return pl.BlockSpec((1, 1, HEADS, HEAD_DIM, PAGE_SIZE),
                            lambda b, g, pt: (layer, pt[b, g * per_step + p], 0, 0, 0))

    grid_spec = pltpu.PrefetchScalarGridSpec(
        num_scalar_prefetch=1,
        grid=(nseq, npages // per_step),
        in_specs=[pl.BlockSpec((1, 256, 1), lambda b, g, pt: (b, 0, 0))]
        + [page_spec(p) for p in range(per_step)],
        out_specs=pl.BlockSpec((1, HEADS, LANES), lambda b, g, pt: (b, 0, 0)),
        scratch_shapes=[pltpu.VMEM((HEADS, LANES), F32)],
    )
    sel = pl.pallas_call(
        _moba_select_kernel,
        grid_spec=grid_spec,
        out_shape=jax.ShapeDtypeStruct((nseq, HEADS, LANES), jnp.int32),
        compiler_params=_cparams("parallel", "arbitrary"),
        name="moba_select",
    )(page_table, mq.reshape(nseq, 256, 1), *([cache_kt] * per_step))
    return sel[:, :, :MOBA_TOPK]


N_SEL_PAGES = MOBA_TOPK * (MOBA_BLOCK // PAGE_SIZE)


def _moba_step_kernel(sel_ref, pt_ref, q_ref, kn_ref, vn_ref, *refs):
    n_pages = HEADS * N_SEL_PAGES
    o_ref = refs[2 * n_pages]
    for h in range(HEADS):
        kp = refs[h * N_SEL_PAGES:(h + 1) * N_SEL_PAGES]
        vp = refs[n_pages + h * N_SEL_PAGES:n_pages + (h + 1) * N_SEL_PAGES]
        q = q_ref[0, h]
        q8 = jnp.broadcast_to(q, (SEL_ROWS, HEAD_DIM)).astype(BF16)
        logits = [_dot(q8, r[0, 0, 0].astype(BF16))[0:1] for r in kp]
        l_self = jnp.sum(q * kn_ref[0, h], axis=-1, keepdims=True)
        m = l_self
        for lg in logits:
            m = jnp.maximum(m, jnp.max(lg, axis=-1, keepdims=True))
        p_self = jnp.exp(l_self - m)
        den = p_self
        num = p_self * vn_ref[0, h]
        for lg, r in zip(logits, vp):
            p = jnp.exp(lg - m)
            den = den + jnp.sum(p, axis=-1, keepdims=True)
            p8 = jnp.broadcast_to(p, (SEL_ROWS, PAGE_SIZE)).astype(BF16)
            num = num + _dot_nt(p8, r[0, 0, 0].astype(BF16))[0:1]
        o_ref[0, h] = num / den


def _moba_step(q, k_new, v_new, sel, page_table, cache_kt, cache_vt, layer):
    nseq, npages = page_table.shape
    ppb = MOBA_BLOCK // PAGE_SIZE

    def page_spec(h, r, e):
        def index(b, sel_ref, pt_ref):
            blk = sel_ref[(b * HEADS + h) * MOBA_TOPK + r]
            return (layer, pt_ref[b * npages + ppb * blk + e], h, 0, 0)
        return pl.BlockSpec((1, 1, 1, HEAD_DIM, PAGE_SIZE), index)

    row = pl.BlockSpec((1, HEADS, 1, HEAD_DIM), lambda b, s, p: (b, 0, 0, 0))
    pages = [page_spec(h, r, e) for h in range(HEADS) for r in range(MOBA_TOPK) for e in range(ppb)]
    grid_spec = pltpu.PrefetchScalarGridSpec(
        num_scalar_prefetch=2,
        grid=(nseq,),
        in_specs=[row, row, row] + pages + pages,
        out_specs=row,
    )
    r4 = lambda a: a.reshape(nseq, HEADS, 1, HEAD_DIM)
    out = pl.pallas_call(
        _moba_step_kernel,
        grid_spec=grid_spec,
        out_shape=jax.ShapeDtypeStruct((nseq, HEADS, 1, HEAD_DIM), F32),
        compiler_params=_cparams("parallel"),
        name="moba_step",
    )(sel.reshape(-1), page_table.reshape(-1), r4(q), r4(k_new), r4(v_new),
      *([cache_kt] * len(pages)), *([cache_vt] * len(pages)))
    return out.reshape(nseq, 256)


def _layer_params(l, w):
    tile4 = lambda a: jnp.tile(a, HEADS)[None]
    rep64 = lambda a: jnp.repeat(a, HEAD_DIM)[None]
    s5p = {k: w[k][l] for k in ("s5_a_re", "s5_a_im", "s5_log_dt", "s5_b_re", "s5_b_im",
                                "s5_c_re", "s5_c_im", "s5_d")}
    return dict(
        ln1_g=w["ln1_g"][l][None], w_in=_regroup_w_in(w["w_in"][l]),
        gla_wg=jnp.pad(w["gla_w_gate"][l], ((0, 128 - GLA_RANK), (0, 0))).astype(BF16),
        gla_bg=w["gla_b_gate"][l][None],
        gdn_alog=rep64(w["gdn_a_log"][l]), gdn_dtb=rep64(w["gdn_dt_bias"][l]),
        s5_prompt=_s5_matrices(s5p, S5_CHUNK), s5_step=_s5_matrices(s5p, 1),
        s5_w_glu=w["s5_w_glu"][l].astype(BF16), s5_b_glu=w["s5_b_glu"][l][None],
        gla_norm=tile4(w["gla_norm"][l]), gla_norm_head=w["gla_norm"][l][None],
        gdn_norm=tile4(w["gdn_norm"][l]), gdn_norm_head=w["gdn_norm"][l][None],
        gdn_conv_w=jnp.pad(w["gdn_conv_w"][l], ((0, CONV_PAD - GDN_CONV), (0, 0))),
        moba_q_norm=tile4(w["moba_q_norm"][l]), moba_k_norm=tile4(w["moba_k_norm"][l]),
        w_br=jnp.stack([w["w_br_s5"][l], w["w_br_gla"][l], w["w_br_gdn"][l], w["w_br_moba"][l]]).astype(BF16),
        ln2_g=w["ln2_g"][l][None],
        layer=l, w_gate=w["w_gate_bf16"], w_out=w["w_out_bf16"], w_ff1=w["w_ff1_bf16"], w_ff2=w["w_ff2_bf16"],
    )


PROMPT_ROWS = 512


def _prompt_layer(x2d, lp, tabs, batch, seq):
    tm = min(PROMPT_ROWS, batch * seq)
    pr = _inproj(x2d, lp, tm)
    ya, s5_fin = _s5_prompt(pr["s5_u_lo"], pr["s5_u_hi"], lp["s5_prompt"], batch, seq)
    yb, gla_s = _gla_prompt(pr["gla_q"], pr["gla_k"], pr["gla_v"], pr["gla_lr"], pr["gla_r"], lp, batch, seq)
    yc, gdn_s, conv = _gdn_prompt(pr["gdn_qkv"], pr["gdn_b"], pr["gdn_a"], pr["gdn_z"], lp, batch, seq)
    kt, vtt, qt, kh, vt, km = _moba_prep(pr["moba_q"], pr["moba_k"], pr["moba_v"], tabs, lp, batch, seq)
    yd = _moba_prompt(qt, kh, vt, km, batch, seq)
    x1 = _merge(x2d, ya, yb, yc, yd, lp, tm)
    x2 = _mlp(x1, lp, tm)
    s5 = lambda a: a.reshape(batch, S5_GROUPS, S5_P)
    states = (kt, vtt, s5(s5_fin[:, :S5_STATE]), s5(s5_fin[:, S5_STATE:]), gla_s, gdn_s, conv)
    return x2, states


def _sample_layer(x2d, lp, tabs, page_table, cache_kt, cache_vt, layer, st):
    s5_re0, s5_im0, gla0, gdn0, conv0 = st
    n = x2d.shape[0]
    pr = _inproj(x2d, lp, n)
    rows = n * HEADS
    per_head = lambda a: a.reshape(rows, HEAD_DIM)
    flat_state = lambda a: a.reshape(rows, HEAD_DIM * HEAD_DIM)
    rep, til, tilt = _expand_mats()
    sds = jax.ShapeDtypeStruct

    h0 = jnp.concatenate([s5_re0.reshape(n, S5_STATE), s5_im0.reshape(n, S5_STATE)], axis=1)
    ya, s5_new = _s5_step(jnp.concatenate([pr["s5_u_lo"], pr["s5_u_hi"]], axis=1), h0, lp["s5_step"])
    ya = (ya[:, :BRANCH_WIDTH // 2], ya[:, BRANCH_WIDTH // 2:])

    yb, gla_s = _whole_call(
        _gla_step_kernel, [sds((rows, HEAD_DIM), F32), sds((rows, HEAD_DIM * HEAD_DIM), F32)], "gla_step",
        per_head(pr["gla_q"]), per_head(pr["gla_k"]), per_head(pr["gla_v"]), per_head(pr["gla_lr"]),
        per_head(pr["gla_r"]), flat_state(gla0), rep, til, tilt, lp["gla_norm_head"])

    qkv, conv_new = _whole_call(
        _gdn_conv_step_kernel, [sds((n, GDN_QKV), F32), sds((n, (GDN_CONV - 1) * GDN_QKV), F32)],
        "gdn_conv_step", pr["gdn_qkv"], conv0.reshape(n, (GDN_CONV - 1) * GDN_QKV), lp["gdn_conv_w"])
    yc, gdn_s = _whole_call(
        _gdn_step_kernel, [sds((rows, HEAD_DIM), F32), sds((rows, HEAD_DIM * HEAD_DIM), F32)], "gdn_step",
        per_head(qkv[:, :256]), per_head(qkv[:, 256:512]), per_head(qkv[:, 512:]),
        per_head(pr["gdn_b"]), per_head(pr["gdn_a"]), per_head(pr["gdn_z"]), flat_state(gdn0),
        rep, til, tilt, lp["gdn_norm_head"])

    mq, mk = _whole_call(
        _moba_qk_step_kernel, [sds((n, 256), F32), sds((n, 256), F32)], "moba_qk_step",
        pr["moba_q"], pr["moba_k"], tabs, lp["moba_q_norm"], lp["moba_k_norm"], _head_ones())
    sel = _moba_select(mq, page_table, cache_kt, layer)
    yd = _moba_step(mq, mk, pr["moba_v"], sel, page_table, cache_kt, cache_vt, layer)

    x1 = _merge(x2d, ya, yb.reshape(n, 256), yc.reshape(n, 256), yd, lp, n)
    x2 = _mlp(x1, lp, n)
    head4 = lambda a: a.reshape(n, 1, HEADS, HEAD_DIM)
    s5 = lambda a: a.reshape(n, S5_GROUPS, S5_P)
    state4 = lambda a: a.reshape(n, HEADS, HEAD_DIM, HEAD_DIM)
    states = (head4(mk), head4(pr["moba_v"]), s5(s5_new[:, :S5_STATE]), s5(s5_new[:, S5_STATE:]),
              state4(gla_s), state4(gdn_s), conv_new.reshape(n, GDN_CONV - 1, GDN_QKV))
    return x2, states


def kernel(x_prompt, x_sample, cache_moba_k, cache_moba_v, page_table, state_s5_re, state_s5_im, state_gla, state_gdn, state_gdn_conv, ln1_g, w_in, s5_a_re, s5_a_im, s5_log_dt, s5_b_re, s5_b_im, s5_c_re, s5_c_im, s5_d, s5_w_glu, s5_b_glu, gla_w_gate, gla_b_gate, gla_norm, gdn_conv_w, gdn_a_log, gdn_dt_bias, gdn_norm, moba_q_norm, moba_k_norm, w_gate, w_br_s5, w_br_gla, w_br_gdn, w_br_moba, w_out, ln2_g, w_ff1, w_ff2):
    weights = dict(ln1_g=ln1_g, w_in=w_in, s5_a_re=s5_a_re, s5_a_im=s5_a_im, s5_log_dt=s5_log_dt,
                   s5_b_re=s5_b_re, s5_b_im=s5_b_im, s5_c_re=s5_c_re, s5_c_im=s5_c_im, s5_d=s5_d,
                   s5_w_glu=s5_w_glu, s5_b_glu=s5_b_glu, gla_w_gate=gla_w_gate, gla_b_gate=gla_b_gate,
                   gla_norm=gla_norm, gdn_conv_w=gdn_conv_w, gdn_a_log=gdn_a_log, gdn_dt_bias=gdn_dt_bias,
                   gdn_norm=gdn_norm, moba_q_norm=moba_q_norm, moba_k_norm=moba_k_norm, w_gate=w_gate,
                   w_br_s5=w_br_s5, w_br_gla=w_br_gla, w_br_gdn=w_br_gdn, w_br_moba=w_br_moba,
                   w_out=w_out, ln2_g=ln2_g, w_ff1=w_ff1, w_ff2=w_ff2)
    for name in ("w_gate", "w_out", "w_ff1", "w_ff2"):
        weights[name + "_bf16"] = weights[name].astype(BF16)
    depth = ln1_g.shape[0]
    layers = [_layer_params(l, weights) for l in range(depth)]
    batch, seq, _ = x_prompt.shape
    nseq = x_sample.shape[0]
    npages = page_table.shape[1]
    past_len = npages * PAGE_SIZE
    assert seq % (S5_CHUNK * 8) == 0 and seq % GLA_ROWS == 0 and seq % MOBA_BLOCK == 0
    assert (batch * seq) % min(PROMPT_ROWS, batch * seq) == 0 and x_sample.shape[1] == 1
    assert past_len % MOBA_BLOCK == 0 and past_len // MOBA_BLOCK >= MOBA_TOPK

    xp = x_prompt.reshape(batch * seq, D_MODEL)
    tabs_p = _rope_tables(jnp.arange(seq, dtype=jnp.int32))
    p_states = []
    for l in range(depth):
        xp, st = _prompt_layer(xp, layers[l], tabs_p, batch, seq)
        p_states.append(st)

    cache_kt = cache_moba_k.transpose(0, 1, 3, 4, 2)
    cache_vt = cache_moba_v.transpose(0, 1, 3, 4, 2)
    tabs_s = _rope_tables(jnp.full((1,), past_len, jnp.int32))
    xs = x_sample.reshape(nseq, D_MODEL)
    s_states = []
    for l in range(depth):
        st0 = (state_s5_re[l], state_s5_im[l], state_gla[l], state_gdn[l], state_gdn_conv[l])
        xs, st = _sample_layer(xs, layers[l], tabs_s, page_table, cache_kt, cache_vt, l, st0)
        s_states.append(st)

    stack = lambda states: [jnp.stack([s[i] for s in states]) for i in range(len(states[0]))]
    p_out = stack(p_states)
    for i in range(2):
        p_out[i] = p_out[i].transpose(0, 1, 4, 2, 3)
    return (xp.reshape(batch, seq, D_MODEL), xs.reshape(nseq, 1, D_MODEL), *p_out, *stack(s_states))
```

```python
import math

import jax
import jax.numpy as jnp
import numpy as np
from jax import lax
from jax.experimental import pallas as pl
from jax.experimental.pallas import tpu as pltpu

F32 = jnp.float32
BF16 = jnp.bfloat16

D_MODEL = 1024
N_BRANCH = 4
BRANCH_WIDTH = D_MODEL // N_BRANCH
HEADS = 4
HEAD_DIM = BRANCH_WIDTH // HEADS
S5_GROUP = 16
S5_GROUPS = BRANCH_WIDTH // S5_GROUP
S5_P = 64
S5_STATE = S5_GROUPS * S5_P
GLA_RANK = 16
GLA_TAU = 16.0
GDN_CONV = 4
GDN_QKV = 3 * BRANCH_WIDTH
MOBA_BLOCK = 256
MOBA_TOPK = 3
ROT_DIM = HEAD_DIM // 4
ROPE_THETA = 500000.0
PAGE_SIZE = 128
D_FF = 4 * D_MODEL
EPS = 1e-6
NEG = -1e30

LANES = 128
S5_CHUNK = 8
GLA_SUB = 16
GDN_CHUNK = 64
VMEM_LIMIT = 56 * 1024 * 1024

HIGHEST = lax.Precision.HIGHEST


def _cparams(*sem):
    return pltpu.CompilerParams(dimension_semantics=sem, vmem_limit_bytes=VMEM_LIMIT)


def _const_spec(shape):
    zeros = (0,) * len(shape)
    return pl.BlockSpec(shape, lambda *_: zeros)


def _layer_spec(shape, layer, **kw):
    zeros = (0,) * len(shape)
    return pl.BlockSpec((None,) + tuple(shape), lambda *_: (layer,) + zeros, **kw)


def _dot(a, b):
    return jnp.dot(a, b, preferred_element_type=F32)


def _dot_nt(a, b):
    return lax.dot_general(a, b, (((1,), (1,)), ((), ())), preferred_element_type=F32)


def _dot_tn(a, b):
    return lax.dot_general(a, b, (((0,), (0,)), ((), ())), preferred_element_type=F32)


def _bdot(a, b):
    return _dot(a.astype(BF16), b.astype(BF16))


def _split2(x):
    hi = x.astype(BF16)
    lo = (x - hi.astype(F32)).astype(BF16)
    return hi, lo


def _dot_sel(x, sel):
    hi, lo = _split2(x)
    return _dot(hi, sel) + _dot(lo, sel)


def _dot_sel_exact(x, sel):
    x1 = x.astype(BF16)
    r1 = x - x1.astype(F32)
    x2 = r1.astype(BF16)
    x3 = (r1 - x2.astype(F32)).astype(BF16)
    return _dot(x1, sel) + _dot(x2, sel) + _dot(x3, sel)


def _rms_rows(x, g):
    return x * lax.rsqrt(jnp.mean(x * x, axis=-1, keepdims=True) + EPS) * g


def _sigmoid(x):
    return 1.0 / (1.0 + jnp.exp(-x))


def _silu(x):
    return x * _sigmoid(x)


def _softplus(x):
    return jnp.maximum(x, 0.0) + jnp.log1p(jnp.exp(-jnp.abs(x)))


def _head_ones():
    r = np.arange(BRANCH_WIDTH) // HEAD_DIM
    return jnp.asarray(r[:, None] == r[None, :], BF16)


IN_OUTS = (("s5_u_lo", 128), ("s5_u_hi", 128), ("gla_q", 256), ("gla_k", 256), ("gla_v", 256), ("gla_r", 256),
           ("gla_lr", 128), ("gdn_qkv", 768), ("gdn_b", 0), ("gdn_a", 0), ("gdn_z", 256),
           ("moba_q", 256), ("moba_k", 256), ("moba_v", 256))
IN_WIDTH = sum(w for _, w in IN_OUTS)
GDN_B_COL = GLA_RANK
GDN_A_COL = GLA_RANK + HEADS


def _regroup_w_in(w_in):
    sizes = (256, 256, 256, 256, GLA_RANK, 256, 256, 256, 256, HEADS, HEADS, 256, 256, 256, 256)
    offs = np.cumsum((0,) + sizes)
    (s5_u, a_q, a_k, a_v, a_lr, a_r, d_q, d_k, d_v, d_b, d_a, d_z, m_q, m_k, m_v) = (
        w_in[:, offs[i]:offs[i + 1]] for i in range(len(sizes)))
    small = jnp.pad(jnp.concatenate([a_lr, d_b, d_a], axis=1), ((0, 0), (0, 128 - GLA_RANK - 2 * HEADS)))
    cols = [s5_u, a_q, a_k, a_v, a_r, small, d_q, d_k, d_v, d_z, m_q, m_k, m_v]
    return jnp.concatenate(cols, axis=1).astype(BF16)


def _head_spread():
    m = np.zeros((128, 2 * BRANCH_WIDTH), np.float32)
    for h in range(HEADS):
        m[GDN_B_COL + h, h * HEAD_DIM:(h + 1) * HEAD_DIM] = 1.0
        m[GDN_A_COL + h, BRANCH_WIDTH + h * HEAD_DIM:BRANCH_WIDTH + (h + 1) * HEAD_DIM] = 1.0
    return jnp.asarray(m, BF16)


def _inproj_kernel(x_ref, g_ref, w_ref, wg_ref, bg_ref, alog_ref, dtb_ref, spread_ref, *outs):
    x = x_ref[...]
    hb = _rms_rows(x, g_ref[...]).astype(BF16)
    vals = {}
    off = 0
    for name, n in IN_OUTS:
        if n:
            vals[name] = _dot(hb, w_ref[:, off:off + n])
            off += n
    small = vals["gla_lr"]
    z = _bdot(small, wg_ref[...]) + bg_ref[...]
    vals["gla_lr"] = -_softplus(-z) * (1.0 / GLA_TAU)
    per_head = _dot_sel_exact(small, spread_ref[...])
    vals["gdn_b"] = _sigmoid(per_head[:, :BRANCH_WIDTH])
    vals["gdn_a"] = -jnp.exp(alog_ref[...]) * _softplus(per_head[:, BRANCH_WIDTH:] + dtb_ref[...])
    for (name, _), o_ref in zip(IN_OUTS, outs):
        o_ref[...] = vals[name]


def _inproj(x2d, lp, tm):
    n = x2d.shape[0]
    out_shape = []
    out_specs = []
    for name, w in IN_OUTS:
        w_out = 256 if name in ("gla_lr", "gdn_b", "gdn_a") else w
        out_shape.append(jax.ShapeDtypeStruct((n, w_out), F32))
        out_specs.append(pl.BlockSpec((tm, w_out), lambda i: (i, 0)))
    res = pl.pallas_call(
        _inproj_kernel,
        grid=(n // tm,),
        in_specs=[pl.BlockSpec((tm, D_MODEL), lambda i: (i, 0)),
                  _const_spec((1, D_MODEL)), _const_spec((D_MODEL, IN_WIDTH)),
                  _const_spec((128, 256)), _const_spec((1, 256)),
                  _const_spec((1, 256)), _const_spec((1, 256)), _const_spec((128, 2 * BRANCH_WIDTH))],
        out_specs=out_specs,
        out_shape=out_shape,
        compiler_params=_cparams("parallel"),
        name="inproj",
    )(x2d, lp["ln1_g"], lp["w_in"], lp["gla_wg"], lp["gla_bg"], lp["gdn_alog"], lp["gdn_dtb"], _head_spread())
    return dict(zip((nm for nm, _ in IN_OUTS), res))


def _merge_kernel(x_ref, ya_lo_ref, ya_hi_ref, yb_ref, yc_ref, yd_ref, g_ref, wgate_ref, wglu_ref, bglu_ref,
                  wbr_ref, wout_ref, o_ref):
    x = x_ref[...]
    hb = _rms_rows(x, g_ref[...]).astype(BF16)
    ya = jnp.concatenate([ya_lo_ref[...], ya_hi_ref[...]], axis=1)
    ya = ya * _sigmoid(_bdot(ya, wglu_ref[...]) + bglu_ref[...])
    merged = None
    for i, y in enumerate((ya, yb_ref[...], yc_ref[...], yd_ref[...])):
        gate = _sigmoid(_dot(hb, wgate_ref[:, i * D_MODEL:(i + 1) * D_MODEL]))
        term = gate * _dot(y.astype(BF16), wbr_ref[i])
        merged = term if merged is None else merged + term
    o_ref[...] = x + _bdot(merged, wout_ref[...])


def _merge(x2d, ya, yb, yc, yd, lp, tm):
    n = x2d.shape[0]
    row = lambda w: pl.BlockSpec((tm, w), lambda i: (i, 0))
    ya_lo, ya_hi = ya
    return pl.pallas_call(
        _merge_kernel,
        grid=(n // tm,),
        in_specs=[row(D_MODEL), row(128), row(128), row(256), row(256), row(256),
                  _const_spec((1, D_MODEL)), _layer_spec((D_MODEL, N_BRANCH * D_MODEL), lp["layer"]),
                  _const_spec((256, 256)), _const_spec((1, 256)),
                  _const_spec((N_BRANCH, 256, D_MODEL)), _layer_spec((D_MODEL, D_MODEL), lp["layer"])],
        out_specs=row(D_MODEL),
        out_shape=jax.ShapeDtypeStruct((n, D_MODEL), F32),
        compiler_params=_cparams("parallel"),
        name="merge",
    )(x2d, ya_lo, ya_hi, yb, yc, yd, lp["ln1_g"], lp["w_gate"], lp["s5_w_glu"], lp["s5_b_glu"],
      lp["w_br"], lp["w_out"])


def _mlp_kernel(x_ref, g_ref, w1_ref, w2_ref, o_ref):
    x = x_ref[...]
    hb = _rms_rows(x, g_ref[...]).astype(BF16)
    z = jnp.maximum(_dot(hb, w1_ref[...]), 0.0)
    o_ref[...] = x + _bdot(z * z, w2_ref[...])


def _mlp(x2d, lp, tm):
    n = x2d.shape[0]
    row = pl.BlockSpec((tm, D_MODEL), lambda i: (i, 0))
    single = pl.Buffered(1)
    return pl.pallas_call(
        _mlp_kernel,
        grid=(n // tm,),
        in_specs=[row, _const_spec((1, D_MODEL)),
                  _layer_spec((D_MODEL, D_FF), lp["layer"], pipeline_mode=single),
                  _layer_spec((D_FF, D_MODEL), lp["layer"], pipeline_mode=single)],
        out_specs=row,
        out_shape=jax.ShapeDtypeStruct((n, D_MODEL), F32),
        compiler_params=_cparams("parallel"),
        name="mlp",
    )(x2d, lp["ln2_g"], lp["w_ff1"], lp["w_ff2"])


def _s5_matrices(p, chunk):
    hp = dict(precision=HIGHEST)
    dt = jnp.exp(p["s5_log_dt"])[:, None]
    ar, ai = p["s5_a_re"], p["s5_a_im"]
    mag = jnp.exp(ar * dt)
    abar_re = mag * jnp.cos(ai * dt)
    abar_im = mag * jnp.sin(ai * dt)
    den = ar * ar + ai * ai
    nr = abar_re - 1.0
    f_re = (nr * ar + abar_im * ai) / den
    f_im = (abar_im * ar - nr * ai) / den
    br, bi = p["s5_b_re"], p["s5_b_im"]
    bbar_re = f_re[..., None] * br - f_im[..., None] * bi
    bbar_im = f_re[..., None] * bi + f_im[..., None] * br
    pw_re = [jnp.ones_like(abar_re)]
    pw_im = [jnp.zeros_like(abar_re)]
    for _ in range(chunk):
        r, i = pw_re[-1], pw_im[-1]
        pw_re.append(r * abar_re - i * abar_im)
        pw_im.append(r * abar_im + i * abar_re)
    pw_re = jnp.stack(pw_re)
    pw_im = jnp.stack(pw_im)
    cr, ci = p["s5_c_re"], p["s5_c_im"]
    ca_re = cr[None] * pw_re[:, :, None, :] - ci[None] * pw_im[:, :, None, :]
    ca_im = cr[None] * pw_im[:, :, None, :] + ci[None] * pw_re[:, :, None, :]
    def table(t):
        lead = t.shape[:-3]
        t = jnp.moveaxis(t, -1, -3)
        return t.reshape(lead + (t.shape[-3], -1))

    kern = (jnp.einsum("tgop,gpi->tgoi", ca_re[:chunk], bbar_re, **hp)
            - jnp.einsum("tgop,gpi->tgoi", ca_im[:chunk], bbar_im, **hp))
    rev_re = pw_re[:chunk][::-1]
    rev_im = pw_im[:chunk][::-1]
    ab_re = rev_re[..., None] * bbar_re[None] - rev_im[..., None] * bbar_im[None]
    ab_im = rev_re[..., None] * bbar_im[None] + rev_im[..., None] * bbar_re[None]
    a_pow = jnp.stack([pw_re[chunk].reshape(-1), pw_im[chunk].reshape(-1)])
    halves = lambda t, n: jnp.stack([t[..., :n], t[..., n:]])
    xs_re, xs_im = halves(table(ab_re), S5_HALF_STATE), halves(table(ab_im), S5_HALF_STATE)
    hy = jnp.stack([table(ca_re[1:]), -table(ca_im[1:])], axis=1)
    return dict(lag=halves(table(kern), S5_HALF),
                xs=jnp.concatenate([xs_re, xs_im], axis=-1),
                hy=halves(hy, S5_HALF),
                a_pow=a_pow, d_row=p["s5_d"][None, :])


S5_HALF = BRANCH_WIDTH // 2
S5_HALF_GROUPS = S5_GROUPS // 2
S5_HALF_STATE = S5_STATE // 2


def _s5_expand(lag_ref, xs_ref, hy_ref, wxs_scr, win_scr, why_scr):
    chunk = lag_ref.shape[1]
    w = S5_HALF

    def group_of(shape, axis, per_group, wrap=None):
        idx = lax.broadcasted_iota(jnp.int32, shape, axis)
        if wrap is not None:
            idx = idx % wrap
        return idx // per_group

    same_ii = group_of((w, w), 0, S5_GROUP) == group_of((w, w), 1, S5_GROUP)
    same_is = (group_of((w, 2 * S5_HALF_STATE), 0, S5_GROUP)
               == group_of((w, 2 * S5_HALF_STATE), 1, S5_P, wrap=S5_HALF_STATE))
    same_si = group_of((S5_HALF_STATE, w), 0, S5_P) == group_of((S5_HALF_STATE, w), 1, S5_GROUP)
    down = lambda t: jnp.concatenate([t] * S5_HALF_GROUPS, axis=0)
    zero = jnp.zeros((w, w), BF16)
    for hf in range(2):
        lags = [jnp.where(same_ii, down(lag_ref[hf, tau]), 0.0).astype(BF16) for tau in range(chunk)]
        for s in range(chunk):
            wxs_scr[hf, s * w:(s + 1) * w, :] = jnp.where(same_is, down(xs_ref[hf, s]), 0.0).astype(BF16)
            win_scr[hf, s * w:(s + 1) * w, :] = jnp.concatenate(
                [lags[t - s] if t >= s else zero for t in range(chunk)], axis=1)
            for part in range(2):
                why_scr[hf, part * S5_HALF_STATE:(part + 1) * S5_HALF_STATE, s * w:(s + 1) * w] = jnp.where(
                    same_si, down(hy_ref[hf, s, part]), 0.0).astype(BF16)


def _s5_state_increment(u_halves, wxs_ref):
    lo = _dot(u_halves[0], wxs_ref[0])
    hi = _dot(u_halves[1], wxs_ref[1])
    n = S5_HALF_STATE
    return jnp.concatenate([lo[:, :n], hi[:, :n], lo[:, n:], hi[:, n:]], axis=1)


def _s5_outputs(u_halves, h_bf16, win_ref, why_ref):
    n = S5_HALF_STATE
    outs = []
    for hf in range(2):
        h_half = jnp.concatenate([h_bf16[:, hf * n:(hf + 1) * n],
                                  h_bf16[:, S5_STATE + hf * n:S5_STATE + (hf + 1) * n]], axis=1)
        outs.append(_dot(u_halves[hf], win_ref[hf]) + _dot(h_half, why_ref[hf]))
    return outs


def _gelu(y):
    c = math.sqrt(2.0 / math.pi)
    return 0.5 * y * (1.0 + jnp.tanh(c * (y + 0.044715 * (y * y * y))))


def _s5_kernel(u_lo_ref, u_hi_ref, lag_ref, xs_ref, hy_ref, apow_ref, d_ref, y_lo_ref, y_hi_ref, hfin_ref,
               wxs_scr, win_scr, why_scr, xs_scr, hs_scr):
    chunk = lag_ref.shape[1]
    rows = u_lo_ref.shape[0] // chunk

    @pl.when(pl.program_id(0) == 0)
    def _():
        _s5_expand(lag_ref, xs_ref, hy_ref, wxs_scr, win_scr, why_scr)

    us = [jnp.concatenate([ref[pl.ds(s, rows, stride=chunk), :] for s in range(chunk)], axis=1)
          for ref in (u_lo_ref, u_hi_ref)]
    ubs = [u.astype(BF16) for u in us]
    xs_scr[...] = _s5_state_increment(ubs, wxs_scr)
    a_re = apow_ref[0:1, :]
    a_im = apow_ref[1:2, :]

    def step(r, carry):
        hr, hi = carry
        hs_scr[pl.ds(r, 1), :] = jnp.concatenate([hr, hi], axis=-1)
        x = xs_scr[pl.ds(r, 1), :]
        nhr = a_re * hr - a_im * hi + x[:, :S5_STATE]
        nhi = a_re * hi + a_im * hr + x[:, S5_STATE:]
        return nhr, nhi

    zero = jnp.zeros((1, S5_STATE), F32)
    hr, hi = lax.fori_loop(0, rows, step, (zero, zero))
    hfin_ref[0] = jnp.concatenate([hr, hi], axis=-1)
    ys = _s5_outputs(ubs, hs_scr[...].astype(BF16), win_scr, why_scr)
    for hf, y_ref in enumerate((y_lo_ref, y_hi_ref)):
        d = d_ref[:, hf * S5_HALF:(hf + 1) * S5_HALF]
        for t in range(chunk):
            cols = slice(t * S5_HALF, (t + 1) * S5_HALF)
            y_ref[pl.ds(t, rows, stride=chunk), :] = _gelu(ys[hf][:, cols] + d * us[hf][:, cols])


def _s5_scratch(chunk):
    width = chunk * S5_HALF
    return [pltpu.VMEM((2, width, 2 * S5_HALF_STATE), BF16), pltpu.VMEM((2, width, width), BF16),
            pltpu.VMEM((2, 2 * S5_HALF_STATE, width), BF16)]


def _s5_prompt(u_lo, u_hi, sm, batch, seq):
    c = sm["lag"].shape[1]
    rows = seq // c
    half = BRANCH_WIDTH // 2
    tok = pl.BlockSpec((seq, half), lambda b: (b, 0))
    y_lo, y_hi, hfin = pl.pallas_call(
        _s5_kernel,
        grid=(batch,),
        in_specs=[tok, tok, _const_spec(sm["lag"].shape), _const_spec(sm["xs"].shape),
                  _const_spec(sm["hy"].shape), _const_spec((2, S5_STATE)), _const_spec((1, BRANCH_WIDTH))],
        out_specs=[tok, tok, pl.BlockSpec((1, 1, 2 * S5_STATE), lambda b: (b, 0, 0))],
        out_shape=[jax.ShapeDtypeStruct((batch * seq, half), F32),
                   jax.ShapeDtypeStruct((batch * seq, half), F32),
                   jax.ShapeDtypeStruct((batch, 1, 2 * S5_STATE), F32)],
        scratch_shapes=_s5_scratch(c) + [pltpu.VMEM((rows, 2 * S5_STATE), F32),
                                         pltpu.VMEM((rows, 2 * S5_STATE), F32)],
        compiler_params=_cparams("arbitrary"),
        name="s5_prompt",
    )(u_lo, u_hi, sm["lag"], sm["xs"], sm["hy"], sm["a_pow"], sm["d_row"])
    return (y_lo, y_hi), hfin.reshape(batch, 2 * S5_STATE)


GLA_ROWS = 256


def _gla_kernel(q_ref, k_ref, v_ref, g_ref, r_ref, tri_ref, ones_ref, bmask_ref, gn_ref,
                y_ref, sfin_ref, st_scr):
    step = pl.program_id(1)
    sub = GLA_SUB
    rows = q_ref.shape[0]
    ns = rows // sub

    @pl.when(step == 0)
    def _():
        st_scr[...] = jnp.zeros_like(st_scr)

    ones = ones_ref[...]
    bmask = bmask_ref[...]
    split = lambda a: a.reshape(ns, sub, BRANCH_WIDTH)
    bc = split(_sel_dot_exact(tri_ref[...], g_ref[...]))
    q = split(q_ref[...] * (HEAD_DIM ** -0.5))
    k = split(k_ref[...])
    v = split(v_ref[...])

    row = lax.broadcasted_iota(jnp.int32, (ns, sub, BRANCH_WIDTH), 1)
    parts = []
    for j in range(sub):
        e = jnp.exp(jnp.minimum(bc - bc[:, j:j + 1, :], 0.0))
        parts.append(jnp.where(row >= j, q * k[:, j:j + 1, :] * e, 0.0))
    att = _dot(jnp.concatenate(parts, axis=1).reshape(ns * sub * sub, BRANCH_WIDTH).astype(BF16), ones)
    att = att.reshape(ns, sub * sub, BRANCH_WIDTH)
    o = att[:, 0:sub] * v[:, 0:1, :]
    for j in range(1, sub):
        o = o + att[:, j * sub:(j + 1) * sub] * v[:, j:j + 1, :]

    last = bc[:, sub - 1:sub, :]
    qt = (q * jnp.exp(bc)).astype(BF16)
    kt = (k * jnp.exp(last - bc)).astype(BF16)
    vb = v.astype(BF16)
    decay = jnp.exp(last)
    outer = [bmask * _dot_tn(vb[s], kt[s]) for s in range(ns)]
    st = st_scr[...]
    inter = []
    for s in range(ns):
        inter.append(_dot_nt(qt[s], st.astype(BF16)))
        st = st * decay[s] + outer[s]
    st_scr[...] = st
    o = o.reshape(rows, BRANCH_WIDTH) + jnp.concatenate(inter, axis=0)
    ms = _dot_sel(o * o, ones) * (1.0 / HEAD_DIM)
    y_ref[...] = o * lax.rsqrt(ms + EPS) * gn_ref[...] * _silu(r_ref[...])

    @pl.when(step == pl.num_programs(1) - 1)
    def _():
        sfin_ref[0] = st


def _sub_tril(rows, sub):
    i = np.arange(rows)
    return jnp.asarray((i[:, None] // sub == i[None, :] // sub) & (i[None, :] <= i[:, None]), BF16)


def _sel_dot_exact(sel, x):
    x1 = x.astype(BF16)
    r1 = x - x1.astype(F32)
    x2 = r1.astype(BF16)
    x3 = (r1 - x2.astype(F32)).astype(BF16)
    return _dot(sel, x1) + _dot(sel, x2) + _dot(sel, x3)


def _unpack_state_t(st):
    b = st.shape[0]
    st = st.reshape(b, HEADS, HEAD_DIM, HEADS, HEAD_DIM)
    diag = jnp.stack([st[:, h, :, h, :] for h in range(HEADS)], axis=1)
    return diag.transpose(0, 1, 3, 2)


def _gla_prompt(q, k, v, g, r, lp, batch, seq):
    rows = min(GLA_ROWS, seq)
    nsteps = seq // rows
    blk = pl.BlockSpec((rows, BRANCH_WIDTH), lambda b, c: (b * nsteps + c, 0))
    hm = np.arange(BRANCH_WIDTH) // HEAD_DIM
    bmask = jnp.asarray(hm[:, None] == hm[None, :], F32)
    y, sfin = pl.pallas_call(
        _gla_kernel,
        grid=(batch, nsteps),
        in_specs=[blk, blk, blk, blk, blk, _const_spec((rows, rows)),
                  _const_spec((BRANCH_WIDTH, BRANCH_WIDTH)), _const_spec((BRANCH_WIDTH, BRANCH_WIDTH)),
                  _const_spec((1, BRANCH_WIDTH))],
        out_specs=[blk, pl.BlockSpec((1, BRANCH_WIDTH, BRANCH_WIDTH), lambda b, c: (b, 0, 0))],
        out_shape=[jax.ShapeDtypeStruct((batch * seq, BRANCH_WIDTH), F32),
                   jax.ShapeDtypeStruct((batch, BRANCH_WIDTH, BRANCH_WIDTH), F32)],
        scratch_shapes=[pltpu.VMEM((BRANCH_WIDTH, BRANCH_WIDTH), F32)],
        compiler_params=_cparams("parallel", "arbitrary"),
        name="gla_prompt",
    )(q, k, v, g, r, _sub_tril(rows, GLA_SUB), _head_ones(), bmask, lp["gla_norm"])
    return y, _unpack_state_t(sfin)


CONV_PAD = 8


GDN_ROWS = 512


def _block_diag(x, ones):
    return jnp.concatenate([x] * HEADS, axis=0) * ones


def _unit_lower_inverses(ns, eye, ones):
    c = ns[0].shape[0]
    every = range(len(ns))
    invs = [eye - n for n in ns]
    pws = list(ns)
    for _ in range(int(math.log2(c)) - 1):
        pbs = [pw.astype(BF16) for pw in pws]
        pws = [_dot(pb, _block_diag(pb, ones)) for pb in pbs]
        invs = [invs[i] + _dot(invs[i].astype(BF16), _block_diag(pws[i].astype(BF16), ones)) for i in every]
    inv_parts = [_split2(inv) for inv in invs]
    n_parts = [_split2(n) for n in ns]
    prods = [_dot(jnp.concatenate(n_parts[i], axis=0), _block_diag(inv_parts[i][0], ones)) for i in every]
    cross = [_dot(n_parts[i][0], _block_diag(inv_parts[i][1], ones)) for i in every]
    resids = [eye - invs[i] - (prods[i][:c] + prods[i][c:] + cross[i]) for i in every]
    return [invs[i] + _dot(inv_parts[i][0], _block_diag(resids[i].astype(BF16), ones)) for i in every]


def _gdn_kernel(x_ref, b_ref, g_ref, z_ref, cw_ref, tri_ref, ones_ref, eye_ref, gn_ref,
                y_ref, sfin_ref, conv_ref, s_scr, buf_scr):
    step = pl.program_id(1)
    rows = x_ref.shape[0]
    c = min(GDN_CHUNK, rows)

    @pl.when(step == 0)
    def _():
        s_scr[...] = jnp.zeros_like(s_scr)
        buf_scr[0:CONV_PAD, :] = jnp.zeros((CONV_PAD, GDN_QKV), F32)

    x = x_ref[...]
    buf_scr[CONV_PAD:CONV_PAD + rows, :] = x
    conv = cw_ref[GDN_CONV - 1:GDN_CONV, :] * x
    for w in range(GDN_CONV - 1):
        lag = GDN_CONV - 1 - w
        conv = conv + cw_ref[w:w + 1, :] * buf_scr[CONV_PAD - lag:CONV_PAD - lag + rows, :]
    tail = buf_scr[rows:rows + CONV_PAD, :]
    buf_scr[0:CONV_PAD, :] = tail
    conv_ref[0] = tail
    qkv = _silu(conv)
    ones = ones_ref[...]
    onesf = ones.astype(F32)
    eye = eye_ref[...]
    q = qkv[:, 0:BRANCH_WIDTH]
    k = qkv[:, BRANCH_WIDTH:2 * BRANCH_WIDTH]
    v = qkv[:, 2 * BRANCH_WIDTH:]
    q = q * lax.rsqrt(_dot_sel(q * q, ones) + EPS) * (HEAD_DIM ** -0.5)
    k = k * lax.rsqrt(_dot_sel(k * k, ones) + EPS)
    beta = b_ref[...]
    gc = _sel_dot_exact(tri_ref[...], g_ref[...])
    gam = jnp.exp(gc)
    ri = lax.broadcasted_iota(jnp.int32, (c, BRANCH_WIDTH), 0)
    cj = lax.broadcasted_iota(jnp.int32, (c, BRANCH_WIDTH), 1) % HEAD_DIM

    every = range(rows // c)
    sls = [slice(n * c, (n + 1) * c) for n in every]
    kcs = [k[rs].astype(BF16) for rs in sls]
    kqs = [_dot_nt(jnp.concatenate([k[rs], q[rs]], axis=0).astype(BF16), _block_diag(kcs[n], ones))
           for n, rs in enumerate(sls)]
    decs = []
    for rs in sls:
        grow = jnp.sum(gc[rs] * eye, axis=0, keepdims=True)
        decs.append(jnp.where(ri >= cj, jnp.exp(jnp.minimum(gc[rs] - grow, 0.0)), 0.0))
    invs = _unit_lower_inverses(
        [jnp.where(ri > cj, beta[rs] * decs[n] * kqs[n][:c], 0.0) for n, rs in enumerate(sls)], eye, ones)
    invbs = [inv.astype(BF16) for inv in invs]
    ws_m = [_dot(invbs[n], _block_diag((beta[rs] * gam[rs] * k[rs]).astype(BF16), ones))
            for n, rs in enumerate(sls)]
    u0s = [_dot(invbs[n], _block_diag((beta[rs] * v[rs]).astype(BF16), ones)) for n, rs in enumerate(sls)]
    aqks = [(decs[n] * kqs[n][c:]).astype(BF16) for n in every]
    lhs = [jnp.concatenate([ws_m[n], gam[rs] * q[rs]], axis=0).astype(BF16) for n, rs in enumerate(sls)]
    glasts = [gc[rs][c - 1:c, :] for rs in sls]
    kds = [(k[rs] * jnp.exp(glasts[n] - gc[rs])).astype(BF16) for n, rs in enumerate(sls)]

    s = s_scr[...]
    outs = []
    for n in every:
        ws = _dot(lhs[n], s.astype(BF16))
        ub = (u0s[n] - ws[:c]).astype(BF16)
        outs.append(ws[c:] + _dot(aqks[n], _block_diag(ub, ones)))
        s = jnp.exp(glasts[n]) * s + onesf * _dot_tn(kds[n], ub)
    s_scr[...] = s
    o = jnp.concatenate(outs, axis=0)
    ms = _dot_sel(o * o, ones) * (1.0 / HEAD_DIM)
    y_ref[...] = o * lax.rsqrt(ms + EPS) * gn_ref[...] * _silu(z_ref[...])

    @pl.when(step == pl.num_programs(1) - 1)
    def _():
        sfin_ref[0] = s


def _unpack_state(st):
    b = st.shape[0]
    st = st.reshape(b, HEADS, HEAD_DIM, HEADS, HEAD_DIM)
    return jnp.stack([st[:, h, :, h, :] for h in range(HEADS)], axis=1)


def _gdn_prompt(x, beta, g, z, lp, batch, seq):
    rows = min(GDN_ROWS, seq)
    c = min(GDN_CHUNK, rows)
    assert c == HEAD_DIM and seq % rows == 0 and rows % c == 0
    nsteps = seq // rows
    blk = lambda w: pl.BlockSpec((rows, w), lambda b, s: (b * nsteps + s, 0))
    eye =jnp.asarray(np.tile(np.eye(c, dtype=np.float32), (1, HEADS)))
    y, sfin, conv = pl.pallas_call(
        _gdn_kernel,
        grid=(batch, nsteps),
        in_specs=[blk(GDN_QKV), blk(256), blk(256), blk(256), _const_spec((CONV_PAD, GDN_QKV)),
                  _const_spec((rows, rows)), _const_spec((256, 256)), _const_spec((c, 256)),
                  _const_spec((1, 256))],
        out_specs=[blk(256),
                   pl.BlockSpec((1, 256, 256), lambda b, s: (b, 0, 0)),
                   pl.BlockSpec((1, CONV_PAD, GDN_QKV), lambda b, s: (b, 0, 0))],
        out_shape=[jax.ShapeDtypeStruct((batch * seq, 256), F32),
                   jax.ShapeDtypeStruct((batch, 256, 256), F32),
                   jax.ShapeDtypeStruct((batch, CONV_PAD, GDN_QKV), F32)],
        scratch_shapes=[pltpu.VMEM((256, 256), F32),
                        pltpu.VMEM((CONV_PAD + rows, GDN_QKV), F32)],
        compiler_params=_cparams("parallel", "arbitrary"),
        name="gdn_prompt",
    )(x, beta, g, z, lp["gdn_conv_w"], _sub_tril(rows, c), _head_ones(), eye, lp["gdn_norm"])
    return y, _unpack_state(sfin), conv[:, CONV_PAD - (GDN_CONV - 1):, :]


HALF_ROT = ROT_DIM // 2


def _rope_tables(pos):
    inv = ROPE_THETA ** (-jnp.arange(HALF_ROT, dtype=F32) / HALF_ROT)
    ang = pos.astype(F32)[:, None] * inv[None, :]
    cos, sin = jnp.cos(ang), jnp.sin(ang)
    n = pos.shape[0]
    rest = HEAD_DIM - ROT_DIM
    head = lambda a, b, fill: jnp.concatenate([a, b, jnp.full((n, rest), fill, F32)], axis=1)
    zero = jnp.zeros_like(sin)
    tabs = [head(cos, cos, 1.0), head(-sin, zero, 0.0), head(zero, sin, 0.0)]
    return jnp.stack([jnp.tile(t, (1, HEADS)) for t in tabs])


def _qk_norm_rope(x, gain, tab_ref, ones):
    y = x * lax.rsqrt(_dot_sel(x * x, ones) * (1.0 / HEAD_DIM) + EPS) * gain
    up = pltpu.roll(y, BRANCH_WIDTH - HALF_ROT, 1)
    down = pltpu.roll(y, HALF_ROT, 1)
    return y * tab_ref[0] + up * tab_ref[1] + down * tab_ref[2]


K_AUG = 2 * HEAD_DIM
V_AUG = HEAD_DIM + 16


def _moba_prep_kernel(q_ref, k_ref, v_ref, tab_ref, qg_ref, kg_ref, ones_ref,
                      kt_ref, vtt_ref, qt_ref, kh_ref, vt_ref, km_ref):
    ones = ones_ref[...]
    rows = q_ref.shape[0]
    nblk = rows // MOBA_BLOCK
    mq = _qk_norm_rope(q_ref[...], qg_ref[...], tab_ref, ones) * (HEAD_DIM ** -0.5)
    mk = _qk_norm_rope(k_ref[...], kg_ref[...], tab_ref, ones)
    kmean = jnp.mean(mk.reshape(nblk, MOBA_BLOCK, BRANCH_WIDTH), axis=1)
    qt = mq.T
    kt = mk.T
    vt = v_ref[...].T
    one_col = (lax.broadcasted_iota(jnp.int32, (MOBA_BLOCK, K_AUG - HEAD_DIM), 1) == 0).astype(BF16)
    one_row = (lax.broadcasted_iota(jnp.int32, (V_AUG - HEAD_DIM, MOBA_BLOCK), 0) == 0).astype(BF16)
    for h in range(HEADS):
        sl = slice(h * HEAD_DIM, (h + 1) * HEAD_DIM)
        qt_ref[0, h] = qt[sl, :]
        km_ref[0, h] = kmean[:, sl]
        kt_ref[0, h] = kt[sl, :]
        vtt_ref[0, h] = vt[sl, :]
        for j in range(nblk):
            rs = slice(j * MOBA_BLOCK, (j + 1) * MOBA_BLOCK)
            kh_ref[0, h, j] = jnp.concatenate([mk[rs, sl].astype(BF16), one_col], axis=1)
            vt_ref[0, h, j] = jnp.concatenate([vt[sl, rs].astype(BF16), one_row], axis=0)


def _moba_prep(q, k, v, tabs, lp, batch, seq):
    rows = min(8 * MOBA_BLOCK, seq)
    nsteps = seq // rows
    nblk = rows // MOBA_BLOCK
    nb = seq // MOBA_BLOCK
    blk = pl.BlockSpec((rows, 256), lambda b, r: (b * nsteps + r, 0))
    cache_rows = pl.BlockSpec((1, HEADS, HEAD_DIM, rows), lambda b, r: (b, 0, 0, r))
    return pl.pallas_call(
        _moba_prep_kernel,
        grid=(batch, nsteps),
        in_specs=[blk, blk, blk, pl.BlockSpec((3, rows, 256), lambda b, r: (0, r, 0)),
                  _const_spec((1, 256)), _const_spec((1, 256)), _const_spec((256, 256))],
        out_specs=[cache_rows, cache_rows, cache_rows,
                   pl.BlockSpec((1, HEADS, nblk, MOBA_BLOCK, K_AUG), lambda b, r: (b, 0, r, 0, 0)),
                   pl.BlockSpec((1, HEADS, nblk, V_AUG, MOBA_BLOCK), lambda b, r: (b, 0, r, 0, 0)),
                   pl.BlockSpec((1, HEADS, nblk, HEAD_DIM), lambda b, r: (b, 0, r, 0))],
        out_shape=[jax.ShapeDtypeStruct((batch, HEADS, HEAD_DIM, seq), F32),
                   jax.ShapeDtypeStruct((batch, HEADS, HEAD_DIM, seq), F32),
                   jax.ShapeDtypeStruct((batch, HEADS, HEAD_DIM, seq), F32),
                   jax.ShapeDtypeStruct((batch, HEADS, nb, MOBA_BLOCK, K_AUG), BF16),
                   jax.ShapeDtypeStruct((batch, HEADS, nb, V_AUG, MOBA_BLOCK), BF16),
                   jax.ShapeDtypeStruct((batch, HEADS, nb, HEAD_DIM), F32)],
        compiler_params=_cparams("parallel", "parallel"),
        name="moba_prep",
    )(q, k, v, tabs, lp["moba_q_norm"], lp["moba_k_norm"], _head_ones())


def _moba_attn_kernel(qt_ref, kh_ref, vt_ref, km_ref, o_ref, bias_scr):
    qb = pl.program_id(1)
    nb = km_ref.shape[2]
    blk = MOBA_BLOCK
    heads = range(HEADS)
    blk_id = lax.broadcasted_iota(jnp.int32, (nb, blk), 0)
    kpos = lax.broadcasted_iota(jnp.int32, (blk, blk), 0)
    qpos = lax.broadcasted_iota(jnp.int32, (blk, blk), 1)
    first_row = lax.broadcasted_iota(jnp.int32, (K_AUG - HEAD_DIM, blk), 0) == 0
    qts = [qt_ref[0, h] for h in heads]
    qtb = [qt.astype(BF16) for qt in qts]

    def scores(j, biases):
        out = []
        for h in heads:
            extra = jnp.where(first_row, biases[h], 0.0).astype(BF16)
            out.append(_dot(kh_ref[0, h, j], jnp.concatenate([qtb[h], extra], axis=0)))
        return out

    gates = [jnp.dot(km_ref[0, h], qts[h], precision=HIGHEST, preferred_element_type=F32) for h in heads]
    own = scores(qb, [jnp.zeros((1, blk), F32)] * HEADS)
    for h in heads:
        gate = jnp.where(blk_id < qb, gates[h], NEG)
        taken = jnp.zeros((nb, blk), jnp.bool_)
        for _ in range(min(MOBA_TOPK, nb)):
            best = jnp.max(gate, axis=0, keepdims=True)
            idx = jnp.min(jnp.where(gate == best, blk_id, nb), axis=0, keepdims=True)
            hit = blk_id == idx
            taken = jnp.logical_or(taken, hit)
            gate = jnp.where(hit, -jnp.inf, gate)
        bias_scr[h] = jnp.where(jnp.logical_and(taken, blk_id < qb), 0.0, NEG)

    ms, ps = [], []
    for h in heads:
        s = jnp.where(kpos <= qpos, own[h], NEG)
        m = jnp.max(s, axis=0, keepdims=True)
        ms.append(m)
        ps.append(jnp.exp(s - m).astype(BF16))
    accs = [_dot(vt_ref[0, h, qb], ps[h]) for h in heads]

    def bias_rows(j):
        return [bias_scr[h, pl.ds(j, 1), :] for h in heads]

    def update(carry, blocks):
        ms, accs = carry
        ss = [scores(j, bias_rows(j)) for j in blocks]
        new_m, new_acc = [], []
        for h in heads:
            m_new = ms[h]
            for s in ss:
                m_new = jnp.maximum(m_new, jnp.max(s[h], axis=0, keepdims=True))
            acc = jnp.exp(ms[h] - m_new) * accs[h]
            for j, s in zip(blocks, ss):
                acc = acc + _dot(vt_ref[0, h, j], jnp.exp(s[h] - m_new).astype(BF16))
            new_m.append(m_new)
            new_acc.append(acc)
        return tuple(new_m), tuple(new_acc)

    carry = (tuple(ms), tuple(accs))
    start = 0
    for width in (8, 4, 2, 1):
        trips = (qb - start) // width
        carry = lax.fori_loop(
            0, trips, lambda i, c, w=width, s=start: update(c, tuple(s + w * i + e for e in range(w))), carry)
        start = start + trips * width
    ms, accs = carry
    outs = [accs[h][:HEAD_DIM] / accs[h][HEAD_DIM:HEAD_DIM + 1] for h in heads]
    o_ref[...] = jnp.concatenate(outs, axis=0).T


def _moba_prompt(qt, kh, vt, km, batch, seq):
    nb = seq // MOBA_BLOCK
    return pl.pallas_call(
        _moba_attn_kernel,
        grid=(batch, nb),
        in_specs=[pl.BlockSpec((1, HEADS, HEAD_DIM, MOBA_BLOCK), lambda b, i: (b, 0, 0, i)),
                  pl.BlockSpec((1, HEADS, nb, MOBA_BLOCK, K_AUG), lambda b, i: (b, 0, 0, 0, 0)),
                  pl.BlockSpec((1, HEADS, nb, V_AUG, MOBA_BLOCK), lambda b, i: (b, 0, 0, 0, 0)),
                  pl.BlockSpec((1, HEADS, nb, HEAD_DIM), lambda b, i: (b, 0, 0, 0))],
        out_specs=pl.BlockSpec((MOBA_BLOCK, 256), lambda b, i: (b * nb + i, 0)),
        out_shape=jax.ShapeDtypeStruct((batch * seq, 256), F32),
        scratch_shapes=[pltpu.VMEM((HEADS, nb, MOBA_BLOCK), F32)],
        compiler_params=_cparams("parallel", "arbitrary"),
        name="moba_prompt",
    )(qt, kh, vt, km)


def _s5_step_kernel(u_ref, h0_ref, lag_ref, xs_ref, hy_ref, apow_ref, d_ref, y_ref, h_ref,
                    wxs_ref, win_ref, why_ref):
    _s5_expand(lag_ref, xs_ref, hy_ref, wxs_ref, win_ref, why_ref)
    u = u_ref[...]
    ubs = [u[:, :S5_HALF].astype(BF16), u[:, S5_HALF:].astype(BF16)]
    h0 = h0_ref[...]
    xs = _s5_state_increment(ubs, wxs_ref)
    a_re = apow_ref[0:1, :]
    a_im = apow_ref[1:2, :]
    hr0 = h0[:, :S5_STATE]
    hi0 = h0[:, S5_STATE:]
    hr = a_re * hr0 - a_im * hi0 + xs[:, :S5_STATE]
    hi = a_re * hi0 + a_im * hr0 + xs[:, S5_STATE:]
    h_ref[...] = jnp.concatenate([hr, hi], axis=-1)
    y = jnp.concatenate(_s5_outputs(ubs, h0.astype(BF16), win_ref, why_ref), axis=1)
    y_ref[...] = _gelu(y + d_ref[...] * u)


def _s5_step(u, h0, sm):
    n = u.shape[0]
    return pl.pallas_call(
        _s5_step_kernel,
        out_shape=[jax.ShapeDtypeStruct((n, BRANCH_WIDTH), F32),
                   jax.ShapeDtypeStruct((n, 2 * S5_STATE), F32)],
        scratch_shapes=_s5_scratch(1),
        compiler_params=pltpu.CompilerParams(vmem_limit_bytes=VMEM_LIMIT),
        name="s5_step",
    )(u, h0, sm["lag"], sm["xs"], sm["hy"], sm["a_pow"], sm["d_row"])


def _expand_mats():
    idx = np.arange(HEAD_DIM * HEAD_DIM)
    rep = (np.arange(HEAD_DIM)[:, None] == idx[None, :] // HEAD_DIM)
    til = (np.arange(HEAD_DIM)[:, None] == idx[None, :] % HEAD_DIM)
    return jnp.asarray(rep, BF16), jnp.asarray(til, BF16), jnp.asarray(til.T, BF16)


def _gla_step_kernel(q_ref, k_ref, v_ref, g_ref, r_ref, s0_ref, rep_ref, til_ref, tilt_ref, gn_ref,
                     y_ref, s_ref):
    rep = rep_ref[...]
    eg = _dot_sel_exact(jnp.exp(g_ref[...]), rep)
    kr = _dot_sel_exact(k_ref[...], rep)
    qr = _dot_sel_exact(q_ref[...] * (HEAD_DIM ** -0.5), rep)
    vt = _dot_sel_exact(v_ref[...], til_ref[...])
    s = eg * s0_ref[...] + kr * vt
    s_ref[...] = s
    o = _dot_sel(qr * s, tilt_ref[...])
    y_ref[...] = _rms_rows(o, gn_ref[...]) * _silu(r_ref[...])


def _gdn_conv_step_kernel(x_ref, c0_ref, cw_ref, qkv_ref, cnew_ref):
    x = x_ref[...]
    c0 = c0_ref[...]
    conv = cw_ref[GDN_CONV - 1:GDN_CONV, :] * x
    for w in range(GDN_CONV - 1):
        conv = conv + cw_ref[w:w + 1, :] * c0[:, w * GDN_QKV:(w + 1) * GDN_QKV]
    qkv_ref[...] = _silu(conv)
    cnew_ref[...] = jnp.concatenate([c0[:, GDN_QKV:], x], axis=-1)


def _gdn_step_kernel(q_ref, k_ref, v_ref, b_ref, g_ref, z_ref, s0_ref, rep_ref, til_ref, tilt_ref,
                     gn_ref, y_ref, s_ref):
    q = q_ref[...]
    k = k_ref[...]
    q = q * lax.rsqrt(jnp.sum(q * q, axis=-1, keepdims=True) + EPS) * (HEAD_DIM ** -0.5)
    k = k * lax.rsqrt(jnp.sum(k * k, axis=-1, keepdims=True) + EPS)
    beta = b_ref[:, 0:1]
    gam = jnp.exp(g_ref[:, 0:1])
    rep = rep_ref[...]
    tilt = tilt_ref[...]
    kr = _dot_sel_exact(k, rep)
    qr = _dot_sel_exact(q, rep)
    s0 = s0_ref[...]
    ks = _dot_sel(kr * s0, tilt)
    qs = _dot_sel(qr * s0, tilt)
    u = v_ref[...] - gam * ks
    qk = jnp.sum(q * k, axis=-1, keepdims=True)
    o = gam * qs + (beta * qk) * u
    s_ref[...] = gam * s0 + kr * _dot_sel_exact(beta * u, til_ref[...])
    y_ref[...] = _rms_rows(o, gn_ref[...]) * _silu(z_ref[...])


def _whole_call(kernel, out_shape, name, *args):
    return pl.pallas_call(kernel, out_shape=out_shape, name=name,
                          compiler_params=pltpu.CompilerParams(vmem_limit_bytes=VMEM_LIMIT))(*args)


def _moba_qk_step_kernel(q_ref, k_ref, tab_ref, qg_ref, kg_ref, ones_ref, mq_ref, mk_ref):
    ones = ones_ref[...]
    mq_ref[...] = _qk_norm_rope(q_ref[...], qg_ref[...], tab_ref, ones) * (HEAD_DIM ** -0.5)
    mk_ref[...] = _qk_norm_rope(k_ref[...], kg_ref[...], tab_ref, ones)


SELECT_PAGES = 64
SEL_ROWS = 8


def _moba_select_kernel(pt_ref, q_ref, *refs):
    pages, sel_ref, gate_scr = refs[:-2], refs[-2], refs[-1]
    per_step = len(pages)
    g = pl.program_id(1)
    ppb = MOBA_BLOCK // PAGE_SIZE
    qcol = jnp.broadcast_to(q_ref[0], (BRANCH_WIDTH, LANES))
    token_ones = jnp.ones((PAGE_SIZE, LANES), BF16)
    blk_lane = lax.broadcasted_iota(jnp.int32, (HEADS, LANES), 1)

    @pl.when(g == 0)
    def _():
        gate_scr[...] = jnp.full((HEADS, LANES), NEG, F32)

    gate = gate_scr[...]
    for n in range(per_step // ppb):
        tile = pages[n * ppb][0, 0]
        for e in range(1, ppb):
            tile = tile + pages[n * ppb + e][0, 0]
        ksum = _dot(tile.reshape(BRANCH_WIDTH, PAGE_SIZE).astype(BF16), token_ones)
        mean = jnp.sum((ksum * qcol).reshape(HEADS, HEAD_DIM, LANES), axis=1) * (1.0 / MOBA_BLOCK)
        gate = jnp.where(blk_lane == g * (per_step // ppb) + n, mean, gate)
    gate_scr[...] = gate

    @pl.when(g == pl.num_programs(1) - 1)
    def _():
        left = gate
        sel = jnp.zeros((HEADS, LANES), jnp.int32)
        for r in range(MOBA_TOPK):
            best = jnp.max(left, axis=-1, keepdims=True)
            idx = jnp.min(jnp.where(left == best, blk_lane, LANES), axis=-1, keepdims=True)
            sel = jnp.where(blk_lane == r, idx, sel)
            left = jnp.where(blk_lane == idx, -jnp.inf, left)
        sel_ref[0] = sel


def _moba_select(mq, page_table, cache_kt, layer):
    nseq, npages = page_table.shape
    per_step = min(SELECT_PAGES, npages)
    assert npages % per_step == 0 and per_step % (MOBA_BLOCK // PAGE_SIZE) == 0

    def page_spec(p):
        return pl.BlockSpec((1, 1, HEADS, HEAD_DIM, PAGE_SIZE),
                            lambda b, g, pt: (layer, pt[b, g * per_step + p], 0, 0, 0))

    grid_spec = pltpu.PrefetchScalarGridSpec(
        num_scalar_prefetch=1,
        grid=(nseq, npages // per_step),
        in_specs=[pl.BlockSpec((1, 256, 1), lambda b, g, pt: (b, 0, 0))]
        + [page_spec(p) for p in range(per_step)],
        out_specs=pl.BlockSpec((1, HEADS, LANES), lambda b, g, pt: (b, 0, 0)),
        scratch_shapes=[pltpu.VMEM((HEADS, LANES), F32)],
    )
    sel = pl.pallas_call(
        _moba_select_kernel,
        grid_spec=grid_spec,
        out_shape=jax.ShapeDtypeStruct((nseq, HEADS, LANES), jnp.int32),
        compiler_params=_cparams("parallel", "arbitrary"),
        name="moba_select",
    )(page_table, mq.reshape(nseq, 256, 1), *([cache_kt] * per_step))
    return sel[:, :, :MOBA_TOPK]


N_SEL_PAGES = MOBA_TOPK * (MOBA_BLOCK // PAGE_SIZE)


def _moba_step_kernel(sel_ref, pt_ref, q_ref, kn_ref, vn_ref, *refs):
    n_pages = HEADS * N_SEL_PAGES
    o_ref = refs[2 * n_pages]
    for h in range(HEADS):
        kp = refs[h * N_SEL_PAGES:(h + 1) * N_SEL_PAGES]
        vp = refs[n_pages + h * N_SEL_PAGES:n_pages + (h + 1) * N_SEL_PAGES]
        q = q_ref[0, h]
        q8 = jnp.broadcast_to(q, (SEL_ROWS, HEAD_DIM)).astype(BF16)
        logits = [_dot(q8, r[0, 0, 0].astype(BF16))[0:1] for r in kp]
        l_self = jnp.sum(q * kn_ref[0, h], axis=-1, keepdims=True)
        m = l_self
        for lg in logits:
            m = jnp.maximum(m, jnp.max(lg, axis=-1, keepdims=True))
        p_self = jnp.exp(l_self - m)
        den = p_self
        num = p_self * vn_ref[0, h]
        for lg, r in zip(logits, vp):
            p = jnp.exp(lg - m)
            den = den + jnp.sum(p, axis=-1, keepdims=True)
            p8 = jnp.broadcast_to(p, (SEL_ROWS, PAGE_SIZE)).astype(BF16)
            num = num + _dot_nt(p8, r[0, 0, 0].astype(BF16))[0:1]
        o_ref[0, h] = num / den


def _moba_step(q, k_new, v_new, sel, page_table, cache_kt, cache_vt, layer):
    nseq, npages = page_table.shape
    ppb = MOBA_BLOCK // PAGE_SIZE

    def page_spec(h, r, e):
        def index(b, sel_ref, pt_ref):
            blk = sel_ref[(b * HEADS + h) * MOBA_TOPK + r]
            return (layer, pt_ref[b * npages + ppb * blk + e], h, 0, 0)
        return pl.BlockSpec((1, 1, 1, HEAD_DIM, PAGE_SIZE), index)

    row = pl.BlockSpec((1, HEADS, 1, HEAD_DIM), lambda b, s, p: (b, 0, 0, 0))
    pages = [page_spec(h, r, e) for h in range(HEADS) for r in range(MOBA_TOPK) for e in range(ppb)]
    grid_spec = pltpu.PrefetchScalarGridSpec(
        num_scalar_prefetch=2,
        grid=(nseq,),
        in_specs=[row, row, row] + pages + pages,
        out_specs=row,
    )
    r4 = lambda a: a.reshape(nseq, HEADS, 1, HEAD_DIM)
    out = pl.pallas_call(
        _moba_step_kernel,
        grid_spec=grid_spec,
        out_shape=jax.ShapeDtypeStruct((nseq, HEADS, 1, HEAD_DIM), F32),
        compiler_params=_cparams("parallel"),
        name="moba_step",
    )(sel.reshape(-1), page_table.reshape(-1), r4(q), r4(k_new), r4(v_new),
      *([cache_kt] * len(pages)), *([cache_vt] * len(pages)))
    return out.reshape(nseq, 256)


def _layer_params(l, w):
    tile4 = lambda a: jnp.tile(a, HEADS)[None]
    rep64 = lambda a: jnp.repeat(a, HEAD_DIM)[None]
    s5p = {k: w[k][l] for k in ("s5_a_re", "s5_a_im", "s5_log_dt", "s5_b_re", "s5_b_im",
                                "s5_c_re", "s5_c_im", "s5_d")}
    return dict(
        ln1_g=w["ln1_g"][l][None], w_in=_regroup_w_in(w["w_in"][l]),
        gla_wg=jnp.pad(w["gla_w_gate"][l], ((0, 128 - GLA_RANK), (0, 0))).astype(BF16),
        gla_bg=w["gla_b_gate"][l][None],
        gdn_alog=rep64(w["gdn_a_log"][l]), gdn_dtb=rep64(w["gdn_dt_bias"][l]),
        s5_prompt=_s5_matrices(s5p, S5_CHUNK), s5_step=_s5_matrices(s5p, 1),
        s5_w_glu=w["s5_w_glu"][l].astype(BF16), s5_b_glu=w["s5_b_glu"][l][None],
        gla_norm=tile4(w["gla_norm"][l]), gla_norm_head=w["gla_norm"][l][None],
        gdn_norm=tile4(w["gdn_norm"][l]), gdn_norm_head=w["gdn_norm"][l][None],
        gdn_conv_w=jnp.pad(w["gdn_conv_w"][l], ((0, CONV_PAD - GDN_CONV), (0, 0))),
        moba_q_norm=tile4(w["moba_q_norm"][l]), moba_k_norm=tile4(w["moba_k_norm"][l]),
        w_br=jnp.stack([w["w_br_s5"][l], w["w_br_gla"][l], w["w_br_gdn"][l], w["w_br_moba"][l]]).astype(BF16),
        ln2_g=w["ln2_g"][l][None],
        layer=l, w_gate=w["w_gate_bf16"], w_out=w["w_out_bf16"], w_ff1=w["w_ff1_bf16"], w_ff2=w["w_ff2_bf16"],
    )


PROMPT_ROWS = 512


def _prompt_layer(x2d, lp, tabs, batch, seq):
    tm = min(PROMPT_ROWS, batch * seq)
    pr = _inproj(x2d, lp, tm)
    ya, s5_fin = _s5_prompt(pr["s5_u_lo"], pr["s5_u_hi"], lp["s5_prompt"], batch, seq)
    yb, gla_s = _gla_prompt(pr["gla_q"], pr["gla_k"], pr["gla_v"], pr["gla_lr"], pr["gla_r"], lp, batch, seq)
    yc, gdn_s, conv = _gdn_prompt(pr["gdn_qkv"], pr["gdn_b"], pr["gdn_a"], pr["gdn_z"], lp, batch, seq)
    kt, vtt, qt, kh, vt, km = _moba_prep(pr["moba_q"], pr["moba_k"], pr["moba_v"], tabs, lp, batch, seq)
    yd = _moba_prompt(qt, kh, vt, km, batch, seq)
    x1 = _merge(x2d, ya, yb, yc, yd, lp, tm)
    x2 = _mlp(x1, lp, tm)
    s5 = lambda a: a.reshape(batch, S5_GROUPS, S5_P)
    states = (kt, vtt, s5(s5_fin[:, :S5_STATE]), s5(s5_fin[:, S5_STATE:]), gla_s, gdn_s, conv)
    return x2, states


def _sample_layer(x2d, lp, tabs, page_table, cache_kt, cache_vt, layer, st):
    s5_re0, s5_im0, gla0, gdn0, conv0 = st
    n = x2d.shape[0]
    pr = _inproj(x2d, lp, n)
    rows = n * HEADS
    per_head = lambda a: a.reshape(rows, HEAD_DIM)
    flat_state = lambda a: a.reshape(rows, HEAD_DIM * HEAD_DIM)
    rep, til, tilt = _expand_mats()
    sds = jax.ShapeDtypeStruct

    h0 = jnp.concatenate([s5_re0.reshape(n, S5_STATE), s5_im0.reshape(n, S5_STATE)], axis=1)
    ya, s5_new = _s5_step(jnp.concatenate([pr["s5_u_lo"], pr["s5_u_hi"]], axis=1), h0, lp["s5_step"])
    ya = (ya[:, :BRANCH_WIDTH // 2], ya[:, BRANCH_WIDTH // 2:])

    yb, gla_s = _whole_call(
        _gla_step_kernel, [sds((rows, HEAD_DIM), F32), sds((rows, HEAD_DIM * HEAD_DIM), F32)], "gla_step",
        per_head(pr["gla_q"]), per_head(pr["gla_k"]), per_head(pr["gla_v"]), per_head(pr["gla_lr"]),
        per_head(pr["gla_r"]), flat_state(gla0), rep, til, tilt, lp["gla_norm_head"])

    qkv, conv_new = _whole_call(
        _gdn_conv_step_kernel, [sds((n, GDN_QKV), F32), sds((n, (GDN_CONV - 1) * GDN_QKV), F32)],
        "gdn_conv_step", pr["gdn_qkv"], conv0.reshape(n, (GDN_CONV - 1) * GDN_QKV), lp["gdn_conv_w"])
    yc, gdn_s = _whole_call(
        _gdn_step_kernel, [sds((rows, HEAD_DIM), F32), sds((rows, HEAD_DIM * HEAD_DIM), F32)], "gdn_step",
        per_head(qkv[:, :256]), per_head(qkv[:, 256:512]), per_head(qkv[:, 512:]),
        per_head(pr["gdn_b"]), per_head(pr["gdn_a"]), per_head(pr["gdn_z"]), flat_state(gdn0),
        rep, til, tilt, lp["gdn_norm_head"])

    mq, mk = _whole_call(
        _moba_qk_step_kernel, [sds((n, 256), F32), sds((n, 256), F32)], "moba_qk_step",
        pr["moba_q"], pr["moba_k"], tabs, lp["moba_q_norm"], lp["moba_k_norm"], _head_ones())
    sel = _moba_select(mq, page_table, cache_kt, layer)
    yd = _moba_step(mq, mk, pr["moba_v"], sel, page_table, cache_kt, cache_vt, layer)

    x1 = _merge(x2d, ya, yb.reshape(n, 256), yc.reshape(n, 256), yd, lp, n)
    x2 = _mlp(x1, lp, n)
    head4 = lambda a: a.reshape(n, 1, HEADS, HEAD_DIM)
    s5 = lambda a: a.reshape(n, S5_GROUPS, S5_P)
    state4 = lambda a: a.reshape(n, HEADS, HEAD_DIM, HEAD_DIM)
    states = (head4(mk), head4(pr["moba_v"]), s5(s5_new[:, :S5_STATE]), s5(s5_new[:, S5_STATE:]),
              state4(gla_s), state4(gdn_s), conv_new.reshape(n, GDN_CONV - 1, GDN_QKV))
    return x2, states


def kernel(x_prompt, x_sample, cache_moba_k, cache_moba_v, page_table, state_s5_re, state_s5_im, state_gla, state_gdn, state_gdn_conv, ln1_g, w_in, s5_a_re, s5_a_im, s5_log_dt, s5_b_re, s5_b_im, s5_c_re, s5_c_im, s5_d, s5_w_glu, s5_b_glu, gla_w_gate, gla_b_gate, gla_norm, gdn_conv_w, gdn_a_log, gdn_dt_bias, gdn_norm, moba_q_norm, moba_k_norm, w_gate, w_br_s5, w_br_gla, w_br_gdn, w_br_moba, w_out, ln2_g, w_ff1, w_ff2):
    weights = dict(ln1_g=ln1_g, w_in=w_in, s5_a_re=s5_a_re, s5_a_im=s5_a_im, s5_log_dt=s5_log_dt,
                   s5_b_re=s5_b_re, s5_b_im=s5_b_im, s5_c_re=s5_c_re, s5_c_im=s5_c_im, s5_d=s5_d,
                   s5_w_glu=s5_w_glu, s5_b_glu=s5_b_glu, gla_w_gate=gla_w_gate, gla_b_gate=gla_b_gate,
                   gla_norm=gla_norm, gdn_conv_w=gdn_conv_w, gdn_a_log=gdn_a_log, gdn_dt_bias=gdn_dt_bias,
                   gdn_norm=gdn_norm, moba_q_norm=moba_q_norm, moba_k_norm=moba_k_norm, w_gate=w_gate,
                   w_br_s5=w_br_s5, w_br_gla=w_br_gla, w_br_gdn=w_br_gdn, w_br_moba=w_br_moba,
                   w_out=w_out, ln2_g=ln2_g, w_ff1=w_ff1, w_ff2=w_ff2)
    for name in ("w_gate", "w_out", "w_ff1", "w_ff2"):
        weights[name + "_bf16"] = weights[name].astype(BF16)
    depth = ln1_g.shape[0]
    layers = [_layer_params(l, weights) for l in range(depth)]
    batch, seq, _ = x_prompt.shape
    nseq = x_sample.shape[0]
    npages = page_table.shape[1]
    past_len = npages * PAGE_SIZE
    assert seq % (S5_CHUNK * 8) == 0 and seq % GLA_ROWS == 0 and seq % MOBA_BLOCK == 0
    assert (batch * seq) % min(PROMPT_ROWS, batch * seq) == 0 and x_sample.shape[1] == 1
    assert past_len % MOBA_BLOCK == 0 and past_len // MOBA_BLOCK >= MOBA_TOPK

    xp = x_prompt.reshape(batch * seq, D_MODEL)
    tabs_p = _rope_tables(jnp.arange(seq, dtype=jnp.int32))
    p_states = []
    for l in range(depth):
        xp, st = _prompt_layer(xp, layers[l], tabs_p, batch, seq)
        p_states.append(st)

    cache_kt = cache_moba_k.transpose(0, 1, 3, 4, 2)
    cache_vt = cache_moba_v.transpose(0, 1, 3, 4, 2)
    tabs_s = _rope_tables(jnp.full((1,), past_len, jnp.int32))
    xs = x_sample.reshape(nseq, D_MODEL)
    s_states = []
    for l in range(depth):
        st0 = (state_s5_re[l], state_s5_im[l], state_gla[l], state_gdn[l], state_gdn_conv[l])
        xs, st = _sample_layer(xs, layers[l], tabs_s, page_table, cache_kt, cache_vt, l, st0)
        s_states.append(st)

    stack = lambda states: [jnp.stack([s[i] for s in states]) for i in range(len(states[0]))]
    p_out = stack(p_states)
    for i in range(2):
        p_out[i] = p_out[i].transpose(0, 1, 4, 2, 3)
    return (xp.reshape(batch, seq, D_MODEL), xs.reshape(nseq, 1, D_MODEL), *p_out, *stack(s_states))
```
